```python
import math
import jax, jax.numpy as jnp
from jax import lax
import numpy as np

D_MODEL = 1024
BATCH = 1
SEQ = 16384
DEPTH = 1

MIX_WIDTH = D_MODEL
HGRN_WIDTH = MIX_WIDTH // 2
HGRN_HEADS = 4
HGRN_HEAD_DIM = HGRN_WIDTH // HGRN_HEADS
HGRN_CHUNK = 64
POOL_WIDTH = MIX_WIDTH - HGRN_WIDTH
POOL_WINDOWS = (2, 4, 8, 16)
POOL_GROUP = POOL_WIDTH // len(POOL_WINDOWS)
IN_PROJ_WIDTH = 4 * HGRN_WIDTH + POOL_WIDTH
N_MEM = 256
XATTN_HEADS = 4
XATTN_HEAD_DIM = D_MODEL // XATTN_HEADS
N_EXPERTS = 32
TOP_K = 4
D_EXPERT = D_MODEL
SWIGLU_LIMIT = 7.0
SWIGLU_ALPHA = 1.702
MOE_BLOCK = 256
DEEPNORM_ALPHA = (2.0 * DEPTH) ** 0.25
DEEPNORM_BETA = (8.0 * DEPTH) ** -0.25
LN_EPS = 1e-5
RMS_EPS = 1e-6

kernel_name = "hgrn2_pool_memxattn_moe_deepnorm"


def layer_norm(x, w, b):
    xf = x.astype(jnp.float32)
    mu = jnp.mean(xf, axis=-1, keepdims=True)
    var = jnp.mean(jnp.square(xf - mu), axis=-1, keepdims=True)
    return ((xf - mu) * lax.rsqrt(var + LN_EPS) * w.astype(jnp.float32) + b.astype(jnp.float32)).astype(x.dtype)


def hgrn2_mixer(q, f_logit, v, g, lb, norm_w):
    B, S, _ = q.shape
    H, Dh, C = HGRN_HEADS, HGRN_HEAD_DIM, HGRN_CHUNK
    nc = S // C
    f32 = jnp.float32
    lb = lb.astype(f32)
    zf = f_logit.astype(f32)
    log_f = jnp.log(lb + (1.0 - lb) * jax.nn.sigmoid(zf))
    key = (1.0 - lb) * jax.nn.sigmoid(-zf)

    def to_chunks(t):
        return t.astype(f32).reshape(B, nc, C, H, Dh).transpose(1, 0, 3, 2, 4)

    causal = jnp.tril(jnp.ones((C, C), dtype=bool))

    def step(state, inp):
        qc, kc, vc, lfc = inp
        b = jnp.cumsum(lfc, axis=-2)
        o_inter = jnp.einsum('bhtk,bhkv->bhtv', qc * jnp.exp(b), state)
        diff = b[:, :, :, None, :] - b[:, :, None, :, :]
        decay = jnp.where(causal[:, :, None], jnp.exp(jnp.minimum(diff, 0.0)), 0.0)
        scores = jnp.einsum('bhtk,bhtsk,bhsk->bhts', qc, decay, kc)
        o_intra = jnp.einsum('bhts,bhsv->bhtv', scores, vc)
        b_last = b[:, :, -1:, :]
        new_state = (jnp.exp(b_last[:, :, 0, :])[..., None] * state
                     + jnp.einsum('bhsk,bhsv->bhkv', kc * jnp.exp(b_last - b), vc))
        return new_state, o_inter + o_intra

    init = jnp.zeros((B, H, Dh, Dh), f32)
    _, o = lax.scan(step, init, (to_chunks(q), to_chunks(key), to_chunks(v), to_chunks(log_f)))
    o = o.transpose(1, 0, 3, 2, 4).reshape(B, S, H, Dh)
    o = o * lax.rsqrt(jnp.mean(jnp.square(o), axis=-1, keepdims=True) + RMS_EPS) \
        * norm_w.astype(f32).reshape(H, Dh)
    gate = jax.nn.silu(g.astype(f32)).reshape(B, S, H, Dh)
    return (o * gate).reshape(B, S, H * Dh)


def pool_mixer(p, w_pool, scale):
    B, S, _ = p.shape
    pf = p.astype(jnp.float32)
    pos = jnp.arange(S)
    outs = []
    for gi, w in enumerate(POOL_WINDOWS):
        pg = pf[..., gi * POOL_GROUP:(gi + 1) * POOL_GROUP]
        cs = jnp.cumsum(pg, axis=1)
        cs_shift = jnp.pad(cs, ((0, 0), (w, 0), (0, 0)))[:, :S]
        count = jnp.minimum(pos + 1, w).astype(jnp.float32)
        mean = (cs - cs_shift) / count[None, :, None]
        outs.append(jnp.einsum('bsc,cd->bsd', mean - pg, w_pool[gi].astype(jnp.float32)))
    return jnp.concatenate(outs, axis=-1) * scale.astype(jnp.float32)


def memory_cross_attention(x, mem, wq, wk, wv, wo):
    B, S, D = x.shape
    M = mem.shape[1]
    q = (x @ wq).reshape(B, S, XATTN_HEADS, XATTN_HEAD_DIM)
    k = (mem @ wk).reshape(B, M, XATTN_HEADS, XATTN_HEAD_DIM)
    v = (mem @ wv).reshape(B, M, XATTN_HEADS, XATTN_HEAD_DIM)
    s = jnp.einsum('bshd,bmhd->bhsm', q, k).astype(jnp.float32) * (XATTN_HEAD_DIM ** -0.5)
    pr = jax.nn.softmax(s, axis=-1).astype(v.dtype)
    o = jnp.einsum('bhsm,bmhd->bshd', pr, v).reshape(B, S, D)
    return o @ wo


def clamped_swiglu(h):
    h_glu, h_lin = h[..., :D_EXPERT], h[..., D_EXPERT:]
    h_glu = jnp.minimum(h_glu, SWIGLU_LIMIT)
    h_lin = jnp.clip(h_lin, -SWIGLU_LIMIT, SWIGLU_LIMIT)
    return h_glu * jax.nn.sigmoid(SWIGLU_ALPHA * h_glu) * (h_lin + 1.0)


def moe_ffn(x, w_r, b_r, w1, b1, w2, b2):
    B, S, D = x.shape
    T = B * S
    TK = T * TOP_K
    xt = x.reshape(T, D)
    logits = (xt @ w_r + b_r).astype(jnp.float32)
    top_v, top_i = lax.top_k(logits, TOP_K)
    gates = jax.nn.softmax(top_v, axis=-1)
    e_flat = top_i.reshape(-1).astype(jnp.int32)
    tok_flat = jnp.repeat(jnp.arange(T, dtype=jnp.int32), TOP_K)
    g_flat = gates.reshape(-1)
    order = jnp.argsort(e_flat)
    e_sorted, tok_sorted, g_sorted = e_flat[order], tok_flat[order], g_flat[order]
    counts = jnp.zeros((N_EXPERTS,), jnp.int32).at[e_flat].add(1)
    padded = ((counts + MOE_BLOCK - 1) // MOE_BLOCK) * MOE_BLOCK
    start = jnp.cumsum(counts) - counts
    pend = jnp.cumsum(padded)
    pstart = pend - padded
    rank = jnp.arange(TK, dtype=jnp.int32) - start[e_sorted]
    dest = pstart[e_sorted] + rank
    n_blocks = -(-(TK + N_EXPERTS * (MOE_BLOCK - 1)) // MOE_BLOCK)
    L = n_blocks * MOE_BLOCK
    buf_tok = jnp.zeros((L,), jnp.int32).at[dest].set(tok_sorted)
    buf_gate = jnp.zeros((L,), jnp.float32).at[dest].set(g_sorted)
    block_start = jnp.arange(n_blocks, dtype=jnp.int32) * MOE_BLOCK
    block_e = jnp.clip(jnp.searchsorted(pend, block_start, side='right'), 0, N_EXPERTS - 1)

    def block_fn(args):
        tok, e = args
        xb = xt[tok]
        h = xb @ w1[e] + b1[e]
        return clamped_swiglu(h) @ w2[e] + b2[e]

    outs = lax.map(block_fn, (buf_tok.reshape(n_blocks, MOE_BLOCK), block_e))
    contrib = outs.reshape(L, D).astype(jnp.float32) * buf_gate[:, None]
    y = jnp.zeros((T, D), jnp.float32).at[buf_tok].add(contrib)
    return y.reshape(B, S, D).astype(x.dtype)


def setup_inputs(seed: int = 0) -> dict:
    key = jax.random.key(seed)
    ks = jax.random.split(key, 24)
    f32 = jnp.float32
    nrm = lambda k, shape, s: jax.random.normal(k, shape, f32) * s
    L, D, E, F = DEPTH, D_MODEL, N_EXPERTS, D_EXPERT
    return {
        "x": nrm(ks[0], (BATCH, SEQ, D), 1.0),
        "mem": nrm(ks[1], (BATCH, N_MEM, D), 1.0),
        "w_in": nrm(ks[2], (L, D, IN_PROJ_WIDTH), D ** -0.5),
        "lb_logits": nrm(ks[3], (L + 1, HGRN_WIDTH), 1.0),
        "hgrn_norm_w": 1.0 + nrm(ks[4], (L, HGRN_WIDTH), 0.02),
        "w_pool": nrm(ks[5], (L, len(POOL_WINDOWS), POOL_GROUP, POOL_GROUP), POOL_GROUP ** -0.5),
        "pool_scale": 1.0 + nrm(ks[6], (L, POOL_WIDTH), 0.1),
        "w_out": nrm(ks[7], (L, MIX_WIDTH, D), MIX_WIDTH ** -0.5 * DEEPNORM_BETA),
        "ln1_w": 1.0 + nrm(ks[8], (L, D), 0.02),
        "ln1_b": nrm(ks[9], (L, D), 0.02),
        "w_xq": nrm(ks[10], (L, D, D), D ** -0.5),
        "w_xk": nrm(ks[11], (L, D, D), D ** -0.5),
        "w_xv": nrm(ks[12], (L, D, D), D ** -0.5 * DEEPNORM_BETA),
        "w_xo": nrm(ks[13], (L, D, D), D ** -0.5 * DEEPNORM_BETA),
        "ln2_w": 1.0 + nrm(ks[14], (L, D), 0.02),
        "ln2_b": nrm(ks[15], (L, D), 0.02),
        "w_router": nrm(ks[16], (L, D, E), D ** -0.5),
        "b_router": nrm(ks[17], (L, E), 0.01),
        "w1": nrm(ks[18], (L, E, D, 2 * F), D ** -0.5),
        "b1": nrm(ks[19], (L, E, 2 * F), 0.02),
        "w2": nrm(ks[20], (L, E, F, D), F ** -0.5 * DEEPNORM_BETA),
        "b2": nrm(ks[21], (L, E, D), 0.02),
        "ln3_w": 1.0 + nrm(ks[22], (L, D), 0.02),
        "ln3_b": nrm(ks[23], (L, D), 0.02),
    }


def reference(x, mem, w_in, lb_logits, hgrn_norm_w, w_pool, pool_scale, w_out, ln1_w, ln1_b,
              w_xq, w_xk, w_xv, w_xo, ln2_w, ln2_b, w_router, b_router, w1, b1, w2, b2,
              ln3_w, ln3_b):
    W = HGRN_WIDTH
    lower_bounds = jnp.cumsum(jax.nn.softmax(lb_logits.astype(jnp.float32), axis=0), axis=0)
    for l in range(DEPTH):
        proj = x @ w_in[l]
        q, f_logit, i_in, g = proj[..., :W], proj[..., W:2 * W], proj[..., 2 * W:3 * W], proj[..., 3 * W:4 * W]
        p_in = proj[..., 4 * W:]
        h_rec = hgrn2_mixer(q, f_logit, i_in, g, lower_bounds[l], hgrn_norm_w[l])
        h_pool = pool_mixer(p_in, w_pool[l], pool_scale[l])
        mix = jnp.concatenate([h_rec, h_pool], axis=-1).astype(x.dtype) @ w_out[l]
        x = layer_norm(DEEPNORM_ALPHA * x + mix, ln1_w[l], ln1_b[l])
        xa = memory_cross_attention(x, mem, w_xq[l], w_xk[l], w_xv[l], w_xo[l])
        x = layer_norm(DEEPNORM_ALPHA * x + xa, ln2_w[l], ln2_b[l])
        y = moe_ffn(x, w_router[l], b_router[l], w1[l], b1[l], w2[l], b2[l])
        x = layer_norm(DEEPNORM_ALPHA * x + y, ln3_w[l], ln3_b[l])
    return x
```

```python
import functools

import numpy as np
import jax
import jax.numpy as jnp
from jax import lax
from jax.experimental import pallas as pl
from jax.experimental.pallas import tpu as pltpu

F32 = jnp.float32
BF16 = jnp.bfloat16

D_MODEL = 1024
HGRN_WIDTH = 512
HGRN_HEADS = 4
HEAD_DIM = 128
POOL_WINDOWS = (2, 4, 8, 16)
POOL_GROUP = 128
POOL_CARRY = 16
IN_PROJ_WIDTH = 4 * HGRN_WIDTH + 512
XATTN_HEADS = 4
XATTN_HEAD_DIM = 256
N_EXPERTS = 32
TOP_K = 4
D_EXPERT = 1024
SWIGLU_LIMIT = 7.0
SWIGLU_ALPHA = 1.702
MOE_BLOCK = 256
DEEPNORM_ALPHA = 2.0 ** 0.25
LN_EPS = 1e-5
RMS_EPS = 1e-6

SEQ_BLOCK = 256
N_LEVELS = 8
VMEM_LIMIT = 48 * 1024 * 1024


def _layer_norm(y, w, b):
    mu = jnp.mean(y, axis=-1, keepdims=True)
    yc = y - mu
    var = jnp.mean(yc * yc, axis=-1, keepdims=True)
    return yc * lax.rsqrt(var + LN_EPS) * w + b


def _dot(a, b):
    return jnp.dot(a, b, preferred_element_type=F32)


def _dot_nt(a, b):
    return lax.dot_general(a, b, (((1,), (1,)), ((), ())), preferred_element_type=F32)


def _decay_tables():
    c = SEQ_BLOCK
    r = np.arange(c)[:, None]
    u = np.arange(c)[None, :]
    mats = [(u <= r)]
    for l in range(N_LEVELS):
        h = 1 << l
        mid = (r // (2 * h)) * (2 * h) + h
        upper = (r >= mid) & (u >= mid) & (u <= r)
        lower = (r < mid) & (u > r) & (u < mid)
        mats.append(upper | lower)
    mall = np.concatenate(mats, axis=0).astype(np.float32)
    x = r ^ u
    lvl = np.where(u < r, np.floor(np.log2(np.maximum(x, 1))).astype(np.int32), -1).astype(np.int32)
    strict_lower = (u < r).astype(np.float32)
    return mall, lvl, strict_lower


def _mixer_kernel(x_ref, win_ref, lbl_ref, nw_ref, wpool_ref, pscale_ref, wout_ref, lnw_ref, lnb_ref,
                  mall_ref, lvl_ref, o_ref, state_ref, carry_ref, ext_ref, mix_ref):
    i = pl.program_id(0)
    c = SEQ_BLOCK

    @pl.when(i == 0)
    def _():
        state_ref[...] = jnp.zeros_like(state_ref)
        carry_ref[...] = jnp.zeros_like(carry_ref)

    x = x_ref[...]
    proj = _dot(x.astype(BF16), win_ref[...])
    w = HGRN_WIDTH
    q = proj[:, 0:w]
    z = proj[:, w:2 * w]
    v = proj[:, 2 * w:3 * w]
    g = proj[:, 3 * w:4 * w]
    p = proj[:, 4 * w:]

    ll = lbl_ref[...]
    ee = jnp.exp(ll - jnp.max(ll, axis=0, keepdims=True))
    lb = ee[0:1] / jnp.sum(ee, axis=0, keepdims=True)

    f = lb + (1.0 - lb) * (1.0 / (1.0 + jnp.exp(-z)))
    nlf = -jnp.log(f)
    key = (1.0 - lb) * (1.0 / (1.0 + jnp.exp(z)))

    hi = nlf.astype(BF16)
    lo = (nlf - hi.astype(F32)).astype(BF16)
    mall = mall_ref[...]
    cum = _dot(mall, hi) + _dot(mall, lo)
    lvl = lvl_ref[...]

    for h in range(HGRN_HEADS):
        hs = slice(h * HEAD_DIM, (h + 1) * HEAD_DIM)
        qh, kh, vh, gh = q[:, hs], key[:, hs], v[:, hs], g[:, hs]
        vb = vh.astype(BF16)
        negb = cum[0:c, hs]
        scores = jnp.zeros((c, c), F32)
        for l in range(N_LEVELS):
            e = jnp.exp(-cum[(l + 1) * c:(l + 2) * c, hs])
            pl_ = _dot_nt((qh * e).astype(BF16), (kh * e).astype(BF16))
            scores = jnp.where(lvl == l, pl_, scores)
        diag = jnp.sum(qh * kh, axis=-1, keepdims=True)
        st = state_ref[h]
        o = (_dot(scores.astype(BF16), vb) + diag * vh
             + _dot_nt((qh * jnp.exp(-negb)).astype(BF16), st.astype(BF16)))
        negb_last = negb[c - 1:c, :]
        kd = (kh * jnp.exp(negb - negb_last)).astype(BF16)
        state_ref[h] = st * jnp.exp(-negb_last) + _dot(vh.T.astype(BF16), kd)
        o = o * lax.rsqrt(jnp.mean(o * o, axis=-1, keepdims=True) + RMS_EPS) * nw_ref[:, hs]
        gate = gh * (1.0 / (1.0 + jnp.exp(-gh)))
        mix_ref[:, hs] = (o * gate).astype(BF16)

    ext_ref[0:POOL_CARRY, :] = carry_ref[...]
    ext_ref[POOL_CARRY:POOL_CARRY + c, :] = p
    carry_ref[...] = p[c - POOL_CARRY:c, :]
    pos = i * c + lax.broadcasted_iota(jnp.int32, (c, 1), 0)
    for gi, win in enumerate(POOL_WINDOWS):
        gs = slice(gi * POOL_GROUP, (gi + 1) * POOL_GROUP)
        pg = p[:, gs]
        tot = pg
        for j in range(1, win):
            tot = tot + ext_ref[POOL_CARRY - j:POOL_CARRY - j + c, gs]
        cnt = jnp.minimum(pos + 1, win).astype(F32)
        pooled = _dot((tot / cnt - pg).astype(BF16), wpool_ref[gi])
        mix_ref[:, w + gi * POOL_GROUP:w + (gi + 1) * POOL_GROUP] = (pooled * pscale_ref[:, gs]).astype(BF16)

    y = DEEPNORM_ALPHA * x + _dot(mix_ref[...], wout_ref[...])
    o_ref[...] = _layer_norm(y, lnw_ref[...], lnb_ref[...])


def _mixer(x, w_in, lb_logits, norm_w, w_pool, pool_scale, w_out, ln_w, ln_b):
    t = x.shape[0]
    c = SEQ_BLOCK
    mall, lvl, _ = _decay_tables()
    full = lambda shape: pl.BlockSpec(shape, lambda i: (0,) * len(shape))
    return pl.pallas_call(
        _mixer_kernel,
        grid=(t // c,),
        in_specs=[
            pl.BlockSpec((c, D_MODEL), lambda i: (i, 0)),
            full((D_MODEL, IN_PROJ_WIDTH)),
            full(lb_logits.shape),
            full((1, HGRN_WIDTH)),
            full((len(POOL_WINDOWS), POOL_GROUP, POOL_GROUP)),
            full((1, 512)),
            full((D_MODEL, D_MODEL)),
            full((1, D_MODEL)),
            full((1, D_MODEL)),
            full(mall.shape),
            full(lvl.shape),
        ],
        out_specs=pl.BlockSpec((c, D_MODEL), lambda i: (i, 0)),
        out_shape=jax.ShapeDtypeStruct((t, D_MODEL), F32),
        scratch_shapes=[
            pltpu.VMEM((HGRN_HEADS, HEAD_DIM, HEAD_DIM), F32),
            pltpu.VMEM((POOL_CARRY, 512), F32),
            pltpu.VMEM((POOL_CARRY + c, 512), F32),
            pltpu.VMEM((c, D_MODEL), BF16),
        ],
        compiler_params=pltpu.CompilerParams(dimension_semantics=("arbitrary",), vmem_limit_bytes=VMEM_LIMIT),
        name="mixer",
    )(x, w_in.astype(BF16), lb_logits, norm_w.reshape(1, -1), w_pool.astype(BF16), pool_scale.reshape(1, -1),
      w_out.astype(BF16), ln_w.reshape(1, -1), ln_b.reshape(1, -1), jnp.asarray(mall, BF16), jnp.asarray(lvl))


def _kv_kernel(mem_ref, wk_ref, wv_ref, k_ref, v_ref):
    m = mem_ref[...].astype(BF16)
    k_ref[...] = _dot(m, wk_ref[...]).astype(BF16)
    v_ref[...] = _dot(m, wv_ref[...]).astype(BF16)


def _kv_proj(mem, wk, wv):
    n = mem.shape[0]
    return pl.pallas_call(
        _kv_kernel,
        out_shape=(jax.ShapeDtypeStruct((n, D_MODEL), BF16), jax.ShapeDtypeStruct((n, D_MODEL), BF16)),
        compiler_params=pltpu.CompilerParams(vmem_limit_bytes=VMEM_LIMIT),
        name="kv_proj",
    )(mem, wk.astype(BF16), wv.astype(BF16))


def _split3(a):
    a0 = a.astype(BF16)
    r1 = a - a0.astype(F32)
    a1 = r1.astype(BF16)
    a2 = (r1 - a1.astype(F32)).astype(BF16)
    return a0, a1, a2


def _xattn_kernel(x_ref, wq_ref, k_ref, v_ref, wo_ref, lnw_ref, lnb_ref, wr_ref, br_ref, lt_ref,
                  x2_ref, topi_ref, gate_ref, rank_ref, cnt_ref, att_ref, run_ref):
    i = pl.program_id(0)
    c = SEQ_BLOCK

    @pl.when(i == 0)
    def _():
        run_ref[...] = jnp.zeros_like(run_ref)

    x = x_ref[...]
    q = _dot(x.astype(BF16), wq_ref[...])
    scale = XATTN_HEAD_DIM ** -0.5
    for h in range(XATTN_HEADS):
        hs = slice(h * XATTN_HEAD_DIM, (h + 1) * XATTN_HEAD_DIM)
        s = _dot_nt(q[:, hs].astype(BF16), k_ref[:, hs]) * scale
        s = s - jnp.max(s, axis=-1, keepdims=True)
        e = jnp.exp(s)
        pr = e / jnp.sum(e, axis=-1, keepdims=True)
        att_ref[:, hs] = _dot(pr.astype(BF16), v_ref[:, hs]).astype(BF16)
    y = DEEPNORM_ALPHA * x + _dot(att_ref[...], wo_ref[...])
    x2 = _layer_norm(y, lnw_ref[...], lnb_ref[...])
    x2_ref[...] = x2

    a0, a1, a2 = _split3(x2)
    w0, w1, w2 = _split3(wr_ref[...])
    logits = (_dot(a2, w0) + _dot(a0, w2) + _dot(a1, w1)) + (_dot(a1, w0) + _dot(a0, w1)) + _dot(a0, w0)
    logits = logits + br_ref[...]

    lane = lax.broadcasted_iota(jnp.int32, (c, N_EXPERTS), 1)
    col = lax.broadcasted_iota(jnp.int32, (c, TOP_K), 1)
    work = logits
    sels, vals = [], []
    topi = jnp.zeros((c, TOP_K), jnp.int32)
    for k in range(TOP_K):
        m = jnp.max(work, axis=-1, keepdims=True)
        idx = jnp.min(jnp.where(work == m, lane, N_EXPERTS), axis=-1, keepdims=True)
        sel = lane == idx
        sels.append(sel)
        vals.append(m)
        topi = jnp.where(col == k, idx, topi)
        work = jnp.where(sel, -jnp.inf, work)
    es = [jnp.exp(vk - vals[0]) for vk in vals]
    denom = es[0] + es[1] + es[2] + es[3]
    gates = jnp.zeros((c, TOP_K), F32)
    for k in range(TOP_K):
        gates = jnp.where(col == k, es[k] / denom, gates)

    onehot = jnp.zeros((c, N_EXPERTS), F32)
    for sel in sels:
        onehot = jnp.where(sel, 1.0, onehot)
    before = _dot(lt_ref[...], onehot.astype(BF16)) + run_ref[...]
    rank = jnp.zeros((c, TOP_K), jnp.int32)
    for k in range(TOP_K):
        rk = jnp.sum(jnp.where(sels[k], before, 0.0), axis=-1, keepdims=True)
        rank = jnp.where(col == k, rk.astype(jnp.int32), rank)
    run_ref[...] = run_ref[...] + jnp.sum(onehot, axis=0, keepdims=True)

    topi_ref[...] = topi
    gate_ref[...] = gates
    rank_ref[...] = rank
    cnt_ref[...] = run_ref[...].astype(jnp.int32)


def _xattn_router(x1, kmem, vmem, wq, wo, ln_w, ln_b, w_r, b_r):
    t = x1.shape[0]
    c = SEQ_BLOCK
    n_mem = kmem.shape[0]
    _, _, strict_lower = _decay_tables()
    full = lambda shape: pl.BlockSpec(shape, lambda i: (0,) * len(shape))
    tok = lambda width: pl.BlockSpec((c, width), lambda i: (i, 0))
    return pl.pallas_call(
        _xattn_kernel,
        grid=(t // c,),
        in_specs=[
            tok(D_MODEL),
            full((D_MODEL, D_MODEL)),
            full((n_mem, D_MODEL)),
            full((n_mem, D_MODEL)),
            full((D_MODEL, D_MODEL)),
            full((1, D_MODEL)),
            full((1, D_MODEL)),
            full((D_MODEL, N_EXPERTS)),
            full((1, N_EXPERTS)),
            full((c, c)),
        ],
        out_specs=[tok(D_MODEL), tok(TOP_K), tok(TOP_K), tok(TOP_K), full((1, N_EXPERTS))],
        out_shape=[
            jax.ShapeDtypeStruct((t, D_MODEL), F32),
            jax.ShapeDtypeStruct((t, TOP_K), jnp.int32),
            jax.ShapeDtypeStruct((t, TOP_K), F32),
            jax.ShapeDtypeStruct((t, TOP_K), jnp.int32),
            jax.ShapeDtypeStruct((1, N_EXPERTS), jnp.int32),
        ],
        scratch_shapes=[pltpu.VMEM((c, D_MODEL), BF16), pltpu.VMEM((1, N_EXPERTS), F32)],
        compiler_params=pltpu.CompilerParams(dimension_semantics=("arbitrary",), vmem_limit_bytes=VMEM_LIMIT),
        name="xattn_router",
    )(x1, wq.astype(BF16), kmem, vmem, wo.astype(BF16), ln_w.reshape(1, -1), ln_b.reshape(1, -1),
      w_r, b_r.reshape(1, -1), jnp.asarray(strict_lower, BF16))


def _expert_kernel(be_ref, nused_ref, tok_ref, x_hbm, gate_ref, w1_ref, b1_ref, w2_ref, b2_ref,
                   o_ref, xbuf, sem, w1b, w2b):
    b = pl.program_id(0)
    valid = b < nused_ref[0]

    @pl.when(valid)
    def _():
        def issue(r, carry):
            pltpu.make_async_copy(x_hbm.at[pl.ds(tok_ref[0, 0, r], 1)], xbuf.at[pl.ds(r, 1)], sem).start()
            return carry
        lax.fori_loop(0, MOE_BLOCK, issue, 0)

        prev = be_ref[jnp.maximum(b - 1, 0)]

        @pl.when((b == 0) | (be_ref[b] != prev))
        def _():
            w1b[...] = w1_ref[0].astype(BF16)
            w2b[...] = w2_ref[0].astype(BF16)

        pltpu.make_async_copy(x_hbm.at[pl.ds(0, MOE_BLOCK)], xbuf, sem).wait()
        h = _dot(xbuf[...].astype(BF16), w1b[...]) + b1_ref[0]
        h_glu = jnp.minimum(h[:, :D_EXPERT], SWIGLU_LIMIT)
        h_lin = jnp.clip(h[:, D_EXPERT:], -SWIGLU_LIMIT, SWIGLU_LIMIT)
        act = h_glu * (1.0 / (1.0 + jnp.exp(-SWIGLU_ALPHA * h_glu))) * (h_lin + 1.0)
        o = _dot(act.astype(BF16), w2b[...]) + b2_ref[0]
        o_ref[...] = o * gate_ref[...]

    @pl.when(jnp.logical_not(valid))
    def _():
        o_ref[...] = jnp.zeros_like(o_ref)


def _experts(x2, block_e, n_used, buf_tok, buf_gate, w1, b1, w2, b2):
    n_blocks = block_e.shape[0]
    grid_spec = pltpu.PrefetchScalarGridSpec(
        num_scalar_prefetch=2,
        grid=(n_blocks,),
        in_specs=[
            pl.BlockSpec((1, 1, MOE_BLOCK), lambda b, be, nu: (b, 0, 0), memory_space=pltpu.SMEM),
            pl.BlockSpec(memory_space=pl.ANY),
            pl.BlockSpec((MOE_BLOCK, 1), lambda b, be, nu: (b, 0)),
            pl.BlockSpec((1, D_MODEL, 2 * D_EXPERT), lambda b, be, nu: (be[b], 0, 0)),
            pl.BlockSpec((1, 1, 2 * D_EXPERT), lambda b, be, nu: (be[b], 0, 0)),
            pl.BlockSpec((1, D_EXPERT, D_MODEL), lambda b, be, nu: (be[b], 0, 0)),
            pl.BlockSpec((1, 1, D_MODEL), lambda b, be, nu: (be[b], 0, 0)),
        ],
        out_specs=pl.BlockSpec((MOE_BLOCK, D_MODEL), lambda b, be, nu: (b, 0)),
        scratch_shapes=[
            pltpu.VMEM((MOE_BLOCK, D_MODEL), F32),
            pltpu.SemaphoreType.DMA,
            pltpu.VMEM((D_MODEL, 2 * D_EXPERT), BF16),
            pltpu.VMEM((D_EXPERT, D_MODEL), BF16),
        ],
    )
    return pl.pallas_call(
        _expert_kernel,
        grid_spec=grid_spec,
        out_shape=jax.ShapeDtypeStruct((n_blocks * MOE_BLOCK, D_MODEL), F32),
        compiler_params=pltpu.CompilerParams(dimension_semantics=("arbitrary",), vmem_limit_bytes=VMEM_LIMIT),
        name="experts",
    )(block_e, n_used, buf_tok.reshape(n_blocks, 1, MOE_BLOCK), x2, buf_gate.reshape(-1, 1),
      w1, b1.reshape(N_EXPERTS, 1, -1), w2, b2.reshape(N_EXPERTS, 1, -1))


def _combine_kernel(dest_ref, ys_hbm, x_ref, lnw_ref, lnb_ref, o_ref, ybuf, sem):
    c = SEQ_BLOCK

    def issue(r, carry):
        for k in range(TOP_K):
            pltpu.make_async_copy(ys_hbm.at[pl.ds(dest_ref[0, 0, r * TOP_K + k], 1)],
                                  ybuf.at[k, pl.ds(r, 1)], sem).start()
        return carry
    lax.fori_loop(0, c, issue, 0)
    for k in range(TOP_K):
        pltpu.make_async_copy(ys_hbm.at[pl.ds(0, c)], ybuf.at[k], sem).wait()
    y = (ybuf[0] + ybuf[1]) + (ybuf[2] + ybuf[3])
    o_ref[...] = _layer_norm(DEEPNORM_ALPHA * x_ref[...] + y, lnw_ref[...], lnb_ref[...])


def _combine(ys, dest, x2, ln_w, ln_b):
    t = x2.shape[0]
    c = SEQ_BLOCK
    return pl.pallas_call(
        _combine_kernel,
        grid=(t // c,),
        in_specs=[
            pl.BlockSpec((1, 1, c * TOP_K), lambda i: (i, 0, 0), memory_space=pltpu.SMEM),
            pl.BlockSpec(memory_space=pl.ANY),
            pl.BlockSpec((c, D_MODEL), lambda i: (i, 0)),
            pl.BlockSpec((1, D_MODEL), lambda i: (0, 0)),
            pl.BlockSpec((1, D_MODEL), lambda i: (0, 0)),
        ],
        out_specs=pl.BlockSpec((c, D_MODEL), lambda i: (i, 0)),
        out_shape=jax.ShapeDtypeStruct((t, D_MODEL), F32),
        scratch_shapes=[pltpu.VMEM((TOP_K, c, D_MODEL), F32), pltpu.SemaphoreType.DMA],
        compiler_params=pltpu.CompilerParams(dimension_semantics=("arbitrary",), vmem_limit_bytes=VMEM_LIMIT),
        name="combine",
    )(dest.reshape(t // c, 1, c * TOP_K), ys, x2, ln_w.reshape(1, -1), ln_b.reshape(1, -1))


def _dispatch_layout(topi, gates, rank, counts):
    t = topi.shape[0]
    tk = t * TOP_K
    counts = counts.reshape(-1)
    padded = ((counts + MOE_BLOCK - 1) // MOE_BLOCK) * MOE_BLOCK
    pend = jnp.cumsum(padded)
    pstart = pend - padded
    dest = pstart[topi] + rank
    n_blocks = -(-(tk + N_EXPERTS * (MOE_BLOCK - 1)) // MOE_BLOCK)
    n_slots = n_blocks * MOE_BLOCK
    tok = jnp.broadcast_to(jnp.arange(t, dtype=jnp.int32)[:, None], (t, TOP_K))
    flat = dest.reshape(-1)
    buf_tok = jnp.zeros((n_slots,), jnp.int32).at[flat].set(tok.reshape(-1))
    buf_gate = jnp.zeros((n_slots,), F32).at[flat].set(gates.reshape(-1))
    n_used = (pend[-1] // MOE_BLOCK).astype(jnp.int32)
    block_start = jnp.arange(n_blocks, dtype=jnp.int32) * MOE_BLOCK
    block_e = jnp.clip(jnp.searchsorted(pend, block_start, side='right'), 0, N_EXPERTS - 1).astype(jnp.int32)
    last_e = block_e[jnp.maximum(n_used - 1, 0)]
    block_e = jnp.where(jnp.arange(n_blocks) < n_used, block_e, last_e)
    return dest.astype(jnp.int32), buf_tok, buf_gate, block_e, n_used.reshape(1)


def kernel(x, mem, w_in, lb_logits, hgrn_norm_w, w_pool, pool_scale, w_out, ln1_w, ln1_b, w_xq, w_xk, w_xv, w_xo,
           ln2_w, ln2_b, w_router, b_router, w1, b1, w2, b2, ln3_w, ln3_b):
    bsz, seq, d = x.shape
    assert bsz == 1 and d == D_MODEL and seq % SEQ_BLOCK == 0 and w_in.shape[0] == 1
    xt = x.reshape(seq, d)
    x1 = _mixer(xt, w_in[0], lb_logits, hgrn_norm_w[0], w_pool[0], pool_scale[0], w_out[0], ln1_w[0], ln1_b[0])
    kmem, vmem = _kv_proj(mem[0], w_xk[0], w_xv[0])
    x2, topi, gates, rank, counts = _xattn_router(x1, kmem, vmem, w_xq[0], w_xo[0], ln2_w[0], ln2_b[0],
                                                  w_router[0], b_router[0])
    dest, buf_tok, buf_gate, block_e, n_used = _dispatch_layout(topi, gates, rank, counts)
    ys = _experts(x2, block_e, n_used, buf_tok, buf_gate, w1[0], b1[0], w2[0], b2[0])
    out = _combine(ys, dest, x2, ln3_w[0], ln3_b[0])
    return out.reshape(bsz, seq, d)
```

```python
import functools

import numpy as np
import jax
import jax.numpy as jnp
from jax import lax
from jax.experimental import pallas as pl
from jax.experimental.pallas import tpu as pltpu

F32 = jnp.float32
BF16 = jnp.bfloat16

D_MODEL = 1024
HGRN_WIDTH = 512
HGRN_HEADS = 4
HEAD_DIM = 128
POOL_WINDOWS = (2, 4, 8, 16)
POOL_GROUP = 128
POOL_CARRY = 16
IN_PROJ_WIDTH = 4 * HGRN_WIDTH + 512
XATTN_HEADS = 4
XATTN_HEAD_DIM = 256
N_EXPERTS = 32
TOP_K = 4
D_EXPERT = 1024
SWIGLU_LIMIT = 7.0
SWIGLU_ALPHA = 1.702
MOE_BLOCK = 256
DEEPNORM_ALPHA = 2.0 ** 0.25
LN_EPS = 1e-5
RMS_EPS = 1e-6

SEQ_BLOCK = 256
N_LEVELS = 8
VMEM_LIMIT = 48 * 1024 * 1024


def _rank_bits(t):
    return max(int(t - 1).bit_length(), 1)


def _layer_norm(y, w, b):
    mu = jnp.mean(y, axis=-1, keepdims=True)
    yc = y - mu
    var = jnp.mean(yc * yc, axis=-1, keepdims=True)
    return yc * lax.rsqrt(var + LN_EPS) * w + b


def _dot(a, b):
    return jnp.dot(a, b, preferred_element_type=F32)


def _dot_nt(a, b):
    return lax.dot_general(a, b, (((1,), (1,)), ((), ())), preferred_element_type=F32)


def _decay_tables():
    c = SEQ_BLOCK
    r = np.arange(c)[:, None]
    u = np.arange(c)[None, :]
    mats = [(u <= r)]
    for l in range(N_LEVELS):
        h = 1 << l
        mid = (r // (2 * h)) * (2 * h) + h
        upper = (r >= mid) & (u >= mid) & (u <= r)
        lower = (r < mid) & (u > r) & (u < mid)
        mats.append(upper | lower)
    mall = np.concatenate(mats, axis=0).astype(np.float32)
    x = r ^ u
    lvl = np.where(u < r, np.floor(np.log2(np.maximum(x, 1))).astype(np.int32), -1).astype(np.int32)
    strict_lower = (u < r).astype(np.float32)
    return mall, lvl, strict_lower


def _mixer_kernel(x_ref, win_ref, lbl_ref, nw_ref, wpool_ref, pscale_ref, wout_ref, lnw_ref, lnb_ref,
                  mall_ref, lvl_ref, o_ref, state_ref, carry_ref, ext_ref, mix_ref):
    i = pl.program_id(0)
    c = SEQ_BLOCK

    @pl.when(i == 0)
    def _():
        state_ref[...] = jnp.zeros_like(state_ref)
        carry_ref[...] = jnp.zeros_like(carry_ref)

    x = x_ref[...]
    proj = _dot(x.astype(BF16), win_ref[...])
    w = HGRN_WIDTH
    q = proj[:, 0:w]
    z = proj[:, w:2 * w]
    v = proj[:, 2 * w:3 * w]
    g = proj[:, 3 * w:4 * w]
    p = proj[:, 4 * w:]

    ll = lbl_ref[...]
    ee = jnp.exp(ll - jnp.max(ll, axis=0, keepdims=True))
    lb = ee[0:1] / jnp.sum(ee, axis=0, keepdims=True)

    f = lb + (1.0 - lb) * (1.0 / (1.0 + jnp.exp(-z)))
    nlf = -jnp.log(f)
    key = (1.0 - lb) * (1.0 / (1.0 + jnp.exp(z)))

    hi = nlf.astype(BF16)
    lo = (nlf - hi.astype(F32)).astype(BF16)
    mall = mall_ref[...]
    cum = _dot(mall, hi) + _dot(mall, lo)
    lvl = lvl_ref[...]

    for h in range(HGRN_HEADS):
        hs = slice(h * HEAD_DIM, (h + 1) * HEAD_DIM)
        qh, kh, vh, gh = q[:, hs], key[:, hs], v[:, hs], g[:, hs]
        vb = vh.astype(BF16)
        negb = cum[0:c, hs]
        scores = jnp.zeros((c, c), F32)
        for l in range(N_LEVELS):
            e = jnp.exp(-cum[(l + 1) * c:(l + 2) * c, hs])
            pl_ = _dot_nt((qh * e).astype(BF16), (kh * e).astype(BF16))
            scores = jnp.where(lvl == l, pl_, scores)
        diag = jnp.sum(qh * kh, axis=-1, keepdims=True)
        st = state_ref[h]
        o = (_dot(scores.astype(BF16), vb) + diag * vh
             + _dot_nt((qh * jnp.exp(-negb)).astype(BF16), st.astype(BF16)))
        negb_last = negb[c - 1:c, :]
        kd = (kh * jnp.exp(negb - negb_last)).astype(BF16)
        state_ref[h] = st * jnp.exp(-negb_last) + _dot(vh.T.astype(BF16), kd)
        o = o * lax.rsqrt(jnp.mean(o * o, axis=-1, keepdims=True) + RMS_EPS) * nw_ref[:, hs]
        gate = gh * (1.0 / (1.0 + jnp.exp(-gh)))
        mix_ref[:, hs] = (o * gate).astype(BF16)

    ext_ref[0:POOL_CARRY, :] = carry_ref[...]
    ext_ref[POOL_CARRY:POOL_CARRY + c, :] = p
    carry_ref[...] = p[c - POOL_CARRY:c, :]
    pos = i * c + lax.broadcasted_iota(jnp.int32, (c, 1), 0)
    for gi, win in enumerate(POOL_WINDOWS):
        gs = slice(gi * POOL_GROUP, (gi + 1) * POOL_GROUP)
        pg = p[:, gs]
        tot = pg
        for j in range(1, win):
            tot = tot + ext_ref[POOL_CARRY - j:POOL_CARRY - j + c, gs]
        cnt = jnp.minimum(pos + 1, win).astype(F32)
        pooled = _dot((tot / cnt - pg).astype(BF16), wpool_ref[gi])
        mix_ref[:, w + gi * POOL_GROUP:w + (gi + 1) * POOL_GROUP] = (pooled * pscale_ref[:, gs]).astype(BF16)

    y = DEEPNORM_ALPHA * x + _dot(mix_ref[...], wout_ref[...])
    o_ref[...] = _layer_norm(y, lnw_ref[...], lnb_ref[...])


def _mixer(x, w_in, lb_logits, norm_w, w_pool, pool_scale, w_out, ln_w, ln_b):
    t = x.shape[0]
    c = SEQ_BLOCK
    mall, lvl, _ = _decay_tables()
    full = lambda shape: pl.BlockSpec(shape, lambda i: (0,) * len(shape))
    return pl.pallas_call(
        _mixer_kernel,
        grid=(t // c,),
        in_specs=[
            pl.BlockSpec((c, D_MODEL), lambda i: (i, 0)),
            full((D_MODEL, IN_PROJ_WIDTH)),
            full(lb_logits.shape),
            full((1, HGRN_WIDTH)),
            full((len(POOL_WINDOWS), POOL_GROUP, POOL_GROUP)),
            full((1, 512)),
            full((D_MODEL, D_MODEL)),
            full((1, D_MODEL)),
            full((1, D_MODEL)),
            full(mall.shape),
            full(lvl.shape),
        ],
        out_specs=pl.BlockSpec((c, D_MODEL), lambda i: (i, 0)),
        out_shape=jax.ShapeDtypeStruct((t, D_MODEL), F32),
        scratch_shapes=[
            pltpu.VMEM((HGRN_HEADS, HEAD_DIM, HEAD_DIM), F32),
            pltpu.VMEM((POOL_CARRY, 512), F32),
            pltpu.VMEM((POOL_CARRY + c, 512), F32),
            pltpu.VMEM((c, D_MODEL), BF16),
        ],
        compiler_params=pltpu.CompilerParams(dimension_semantics=("arbitrary",), vmem_limit_bytes=VMEM_LIMIT),
        name="mixer",
    )(x, w_in.astype(BF16), lb_logits, norm_w.reshape(1, -1), w_pool.astype(BF16), pool_scale.reshape(1, -1),
      w_out.astype(BF16), ln_w.reshape(1, -1), ln_b.reshape(1, -1), jnp.asarray(mall, BF16), jnp.asarray(lvl))


def _kv_kernel(mem_ref, wk_ref, wv_ref, k_ref, v_ref):
    m = mem_ref[...].astype(BF16)
    k_ref[...] = _dot(m, wk_ref[...]).astype(BF16)
    v_ref[...] = _dot(m, wv_ref[...]).astype(BF16)


def _kv_proj(mem, wk, wv):
    n = mem.shape[0]
    return pl.pallas_call(
        _kv_kernel,
        out_shape=(jax.ShapeDtypeStruct((n, D_MODEL), BF16), jax.ShapeDtypeStruct((n, D_MODEL), BF16)),
        compiler_params=pltpu.CompilerParams(vmem_limit_bytes=VMEM_LIMIT),
        name="kv_proj",
    )(mem, wk.astype(BF16), wv.astype(BF16))


def _split3(a):
    a0 = a.astype(BF16)
    r1 = a - a0.astype(F32)
    a1 = r1.astype(BF16)
    a2 = (r1 - a1.astype(F32)).astype(BF16)
    return a0, a1, a2


def _xattn_kernel(x_ref, wq_ref, k_ref, v_ref, wo_ref, lnw_ref, lnb_ref, wr_ref, br_ref, lt_ref,
                  x2_ref, dest_ref, gate_ref, cnt_ref, att_ref, run_ref, *, shift):
    i = pl.program_id(0)
    c = SEQ_BLOCK

    @pl.when(i == 0)
    def _():
        run_ref[...] = jnp.zeros_like(run_ref)

    x = x_ref[...]
    q = _dot(x.astype(BF16), wq_ref[...])
    scale = XATTN_HEAD_DIM ** -0.5
    for h in range(XATTN_HEADS):
        hs = slice(h * XATTN_HEAD_DIM, (h + 1) * XATTN_HEAD_DIM)
        s = _dot_nt(q[:, hs].astype(BF16), k_ref[:, hs]) * scale
        s = s - jnp.max(s, axis=-1, keepdims=True)
        e = jnp.exp(s)
        pr = e / jnp.sum(e, axis=-1, keepdims=True)
        att_ref[:, hs] = _dot(pr.astype(BF16), v_ref[:, hs]).astype(BF16)
    y = DEEPNORM_ALPHA * x + _dot(att_ref[...], wo_ref[...])
    x2 = _layer_norm(y, lnw_ref[...], lnb_ref[...])
    x2_ref[...] = x2

    a0, a1, a2 = _split3(x2)
    w0, w1, w2 = _split3(wr_ref[...])
    logits = (_dot(a2, w0) + _dot(a0, w2) + _dot(a1, w1)) + (_dot(a1, w0) + _dot(a0, w1)) + _dot(a0, w0)
    logits = logits + br_ref[...]

    lane = lax.broadcasted_iota(jnp.int32, (c, N_EXPERTS), 1)
    col = lax.broadcasted_iota(jnp.int32, (c, TOP_K), 1)
    work = logits
    sels, vals = [], []
    topi = jnp.zeros((c, TOP_K), jnp.int32)
    for k in range(TOP_K):
        m = jnp.max(work, axis=-1, keepdims=True)
        idx = jnp.min(jnp.where(work == m, lane, N_EXPERTS), axis=-1, keepdims=True)
        sel = lane == idx
        sels.append(sel)
        vals.append(m)
        topi = jnp.where(col == k, idx, topi)
        work = jnp.where(sel, -jnp.inf, work)
    es = [jnp.exp(vk - vals[0]) for vk in vals]
    denom = es[0] + es[1] + es[2] + es[3]
    gates = jnp.zeros((c, TOP_K), F32)
    for k in range(TOP_K):
        gates = jnp.where(col == k, es[k] / denom, gates)

    onehot = jnp.zeros((c, N_EXPERTS), F32)
    for sel in sels:
        onehot = jnp.where(sel, 1.0, onehot)
    before = _dot(lt_ref[...], onehot.astype(BF16)) + run_ref[...]
    rank = jnp.zeros((c, TOP_K), jnp.int32)
    for k in range(TOP_K):
        rk = jnp.sum(jnp.where(sels[k], before, 0.0), axis=-1, keepdims=True)
        rank = jnp.where(col == k, rk.astype(jnp.int32), rank)
    run_ref[...] = run_ref[...] + jnp.sum(onehot, axis=0, keepdims=True)

    dest_ref[...] = topi * (1 << shift) + rank
    gate_ref[...] = gates
    cnt_ref[...] = run_ref[...].astype(jnp.int32)


def _xattn_router(x1, kmem, vmem, wq, wo, ln_w, ln_b, w_r, b_r):
    t = x1.shape[0]
    c = SEQ_BLOCK
    n_mem = kmem.shape[0]
    _, _, strict_lower = _decay_tables()
    full = lambda shape: pl.BlockSpec(shape, lambda i: (0,) * len(shape))
    tok = lambda width: pl.BlockSpec((c, width), lambda i: (i, 0))
    return pl.pallas_call(
        functools.partial(_xattn_kernel, shift=_rank_bits(t)),
        grid=(t // c,),
        in_specs=[
            tok(D_MODEL),
            full((D_MODEL, D_MODEL)),
            full((n_mem, D_MODEL)),
            full((n_mem, D_MODEL)),
            full((D_MODEL, D_MODEL)),
            full((1, D_MODEL)),
            full((1, D_MODEL)),
            full((D_MODEL, N_EXPERTS)),
            full((1, N_EXPERTS)),
            full((c, c)),
        ],
        out_specs=[tok(D_MODEL), tok(TOP_K), tok(TOP_K), full((1, N_EXPERTS))],
        out_shape=[
            jax.ShapeDtypeStruct((t, D_MODEL), F32),
            jax.ShapeDtypeStruct((t, TOP_K), jnp.int32),
            jax.ShapeDtypeStruct((t, TOP_K), F32),
            jax.ShapeDtypeStruct((1, N_EXPERTS), jnp.int32),
        ],
        scratch_shapes=[pltpu.VMEM((c, D_MODEL), BF16), pltpu.VMEM((1, N_EXPERTS), F32)],
        compiler_params=pltpu.CompilerParams(dimension_semantics=("arbitrary",), vmem_limit_bytes=VMEM_LIMIT),
        name="xattn_router",
    )(x1, wq.astype(BF16), kmem, vmem, wo.astype(BF16), ln_w.reshape(1, -1), ln_b.reshape(1, -1),
      w_r, b_r.reshape(1, -1), jnp.asarray(strict_lower, BF16))


def _slot(packed, pstart_ref, shift):
    return pstart_ref[lax.shift_right_logical(packed, shift)] + (packed & ((1 << shift) - 1))


def _dispatch_kernel(pstart_ref, fill_ref, nused_ref, dest_ref, x_hbm, xs_hbm, zero_ref, sem, zsem, *, shift, n_blocks):
    i = pl.program_id(0)
    c = SEQ_BLOCK

    @pl.when(i == 0)
    def _():
        zero_ref[...] = jnp.zeros_like(zero_ref)

        def tail(e, act):
            s0 = fill_ref[e]
            end = pstart_ref[e + 1]
            n1 = jnp.minimum((-s0) & 7, end - s0)
            s1 = s0 + n1

            def one(r, carry):
                act(pltpu.make_async_copy(zero_ref.at[pl.ds(0, 1)], xs_hbm.at[pl.ds(s0 + r, 1)], zsem))
                return carry

            def eight(j, carry):
                row = pl.multiple_of(s1 + 8 * j, 8)
                act(pltpu.make_async_copy(zero_ref.at[pl.ds(0, 8)], xs_hbm.at[pl.ds(row, 8)], zsem))
                return carry

            lax.fori_loop(0, n1, one, 0)
            lax.fori_loop(0, lax.shift_right_logical(end - s1, 3), eight, 0)

        def spare(b, act):
            row = pl.multiple_of(b * MOE_BLOCK, MOE_BLOCK)
            act(pltpu.make_async_copy(zero_ref, xs_hbm.at[pl.ds(row, MOE_BLOCK)], zsem))

        for act in (lambda cp: cp.start(), lambda cp: cp.wait()):
            lax.fori_loop(0, N_EXPERTS, lambda e, carry: (tail(e, act), carry)[1], 0)
            lax.fori_loop(nused_ref[0], n_blocks, lambda b, carry: (spare(b, act), carry)[1], 0)

    def issue(r, carry):
        src = x_hbm.at[pl.ds(i * c + r, 1)]
        for k in range(TOP_K):
            row = _slot(dest_ref[0, 0, r * TOP_K + k], pstart_ref, shift)
            pltpu.make_async_copy(src, xs_hbm.at[pl.ds(row, 1)], sem).start()
        return carry
    lax.fori_loop(0, c, issue, 0)
    for k in range(TOP_K):
        pltpu.make_async_copy(x_hbm.at[pl.ds(0, c)], xs_hbm.at[pl.ds(0, c)], sem).wait()


def _dispatch(x2, dest, pstart, fill, n_used, n_blocks, shift):
    t = x2.shape[0]
    c = SEQ_BLOCK
    grid_spec = pltpu.PrefetchScalarGridSpec(
        num_scalar_prefetch=3,
        grid=(t // c,),
        in_specs=[
            pl.BlockSpec((1, 1, c * TOP_K), lambda i, *_: (i, 0, 0), memory_space=pltpu.SMEM),
            pl.BlockSpec(memory_space=pl.ANY),
        ],
        out_specs=pl.BlockSpec(memory_space=pl.ANY),
        scratch_shapes=[pltpu.VMEM((MOE_BLOCK, D_MODEL), F32), pltpu.SemaphoreType.DMA, pltpu.SemaphoreType.DMA],
    )
    return pl.pallas_call(
        functools.partial(_dispatch_kernel, shift=shift, n_blocks=n_blocks),
        grid_spec=grid_spec,
        out_shape=jax.ShapeDtypeStruct((n_blocks * MOE_BLOCK, D_MODEL), F32),
        compiler_params=pltpu.CompilerParams(dimension_semantics=("arbitrary",), vmem_limit_bytes=VMEM_LIMIT),
        name="dispatch",
    )(pstart, fill, n_used, dest.reshape(t // c, 1, c * TOP_K), x2)


def _expert_kernel(be_ref, nused_ref, x_ref, w1_ref, b1_ref, w2_ref, b2_ref, o_ref, w1b, w2b):
    b = pl.program_id(0)

    @pl.when(b >= nused_ref[0])
    def _():
        o_ref[...] = jnp.zeros_like(o_ref)

    @pl.when(b < nused_ref[0])
    def _():
        prev = be_ref[jnp.maximum(b - 1, 0)]

        @pl.when((b == 0) | (be_ref[b] != prev))
        def _():
            w1b[...] = w1_ref[0].astype(BF16)
            w2b[...] = w2_ref[0].astype(BF16)

        h = _dot(x_ref[...].astype(BF16), w1b[...]) + b1_ref[0]
        h_glu = jnp.minimum(h[:, :D_EXPERT], SWIGLU_LIMIT)
        h_lin = jnp.clip(h[:, D_EXPERT:], -SWIGLU_LIMIT, SWIGLU_LIMIT)
        act = h_glu * (1.0 / (1.0 + jnp.exp(-SWIGLU_ALPHA * h_glu))) * (h_lin + 1.0)
        o_ref[...] = _dot(act.astype(BF16), w2b[...]) + b2_ref[0]


def _experts(xs, block_e, n_used, w1, b1, w2, b2):
    n_blocks = block_e.shape[0]
    rows = lambda b, be, nu: (b, 0)
    wsel = lambda b, be, nu: (be[b], 0, 0)
    grid_spec = pltpu.PrefetchScalarGridSpec(
        num_scalar_prefetch=2,
        grid=(n_blocks,),
        in_specs=[
            pl.BlockSpec((MOE_BLOCK, D_MODEL), rows),
            pl.BlockSpec((1, D_MODEL, 2 * D_EXPERT), wsel),
            pl.BlockSpec((1, 1, 2 * D_EXPERT), wsel),
            pl.BlockSpec((1, D_EXPERT, D_MODEL), wsel),
            pl.BlockSpec((1, 1, D_MODEL), wsel),
        ],
        out_specs=pl.BlockSpec((MOE_BLOCK, D_MODEL), rows),
        scratch_shapes=[
            pltpu.VMEM((D_MODEL, 2 * D_EXPERT), BF16),
            pltpu.VMEM((D_EXPERT, D_MODEL), BF16),
        ],
    )
    return pl.pallas_call(
        _expert_kernel,
        grid_spec=grid_spec,
        out_shape=jax.ShapeDtypeStruct(xs.shape, F32),
        compiler_params=pltpu.CompilerParams(dimension_semantics=("arbitrary",), vmem_limit_bytes=VMEM_LIMIT),
        name="experts",
    )(block_e, n_used, xs, w1, b1.reshape(N_EXPERTS, 1, -1), w2, b2.reshape(N_EXPERTS, 1, -1))


def _combine_kernel(pstart_ref, dest_ref, ys_hbm, gate_ref, x_ref, lnw_ref, lnb_ref, o_ref, ybuf, sem, *, shift):
    c = SEQ_BLOCK

    def issue(r, carry):
        for k in range(TOP_K):
            row = _slot(dest_ref[0, 0, r * TOP_K + k], pstart_ref, shift)
            pltpu.make_async_copy(ys_hbm.at[pl.ds(row, 1)], ybuf.at[k, pl.ds(r, 1)], sem).start()
        return carry
    lax.fori_loop(0, c, issue, 0)
    for k in range(TOP_K):
        pltpu.make_async_copy(ys_hbm.at[pl.ds(0, c)], ybuf.at[k], sem).wait()
    gates = gate_ref[...]
    y = ((ybuf[0] * gates[:, 0:1] + ybuf[1] * gates[:, 1:2])
         + (ybuf[2] * gates[:, 2:3] + ybuf[3] * gates[:, 3:4]))
    o_ref[...] = _layer_norm(DEEPNORM_ALPHA * x_ref[...] + y, lnw_ref[...], lnb_ref[...])


def _combine(ys, dest, gates, pstart, x2, ln_w, ln_b, shift):
    t = x2.shape[0]
    c = SEQ_BLOCK
    grid_spec = pltpu.PrefetchScalarGridSpec(
        num_scalar_prefetch=1,
        grid=(t // c,),
        in_specs=[
            pl.BlockSpec((1, 1, c * TOP_K), lambda i, *_: (i, 0, 0), memory_space=pltpu.SMEM),
            pl.BlockSpec(memory_space=pl.ANY),
            pl.BlockSpec((c, TOP_K), lambda i, *_: (i, 0)),
            pl.BlockSpec((c, D_MODEL), lambda i, *_: (i, 0)),
            pl.BlockSpec((1, D_MODEL), lambda i, *_: (0, 0)),
            pl.BlockSpec((1, D_MODEL), lambda i, *_: (0, 0)),
        ],
        out_specs=pl.BlockSpec((c, D_MODEL), lambda i, *_: (i, 0)),
        scratch_shapes=[pltpu.VMEM((TOP_K, c, D_MODEL), F32), pltpu.SemaphoreType.DMA],
    )
    return pl.pallas_call(
        functools.partial(_combine_kernel, shift=shift),
        grid_spec=grid_spec,
        out_shape=jax.ShapeDtypeStruct((t, D_MODEL), F32),
        compiler_params=pltpu.CompilerParams(dimension_semantics=("arbitrary",), vmem_limit_bytes=VMEM_LIMIT),
        name="combine",
    )(pstart, dest.reshape(t // c, 1, c * TOP_K), ys, gates, x2, ln_w.reshape(1, -1), ln_b.reshape(1, -1))


def _block_tables(counts, n_blocks):
    counts = counts.reshape(-1)
    nblk = (counts + MOE_BLOCK - 1) // MOE_BLOCK
    ids = jnp.arange(N_EXPERTS)
    bend = jnp.sum(jnp.where(ids[None, :] <= ids[:, None], nblk[None, :], 0), axis=1)
    n_used = bend[N_EXPERTS - 1]
    pstart = jnp.concatenate([jnp.zeros((1,), bend.dtype), bend]) * MOE_BLOCK
    fill = pstart[:N_EXPERTS] + counts
    b = jnp.minimum(jnp.arange(n_blocks), n_used - 1)
    block_e = jnp.sum((bend[None, :] <= b[:, None]).astype(jnp.int32), axis=1)
    i32 = lambda a: a.astype(jnp.int32)
    return i32(pstart), i32(fill), i32(block_e), i32(n_used.reshape(1))


def kernel(x, mem, w_in, lb_logits, hgrn_norm_w, w_pool, pool_scale, w_out, ln1_w, ln1_b, w_xq, w_xk, w_xv, w_xo,
           ln2_w, ln2_b, w_router, b_router, w1, b1, w2, b2, ln3_w, ln3_b):
    bsz, seq, d = x.shape
    assert bsz == 1 and d == D_MODEL and seq % SEQ_BLOCK == 0 and w_in.shape[0] == 1
    xt = x.reshape(seq, d)
    x1 = _mixer(xt, w_in[0], lb_logits, hgrn_norm_w[0], w_pool[0], pool_scale[0], w_out[0], ln1_w[0], ln1_b[0])
    kmem, vmem = _kv_proj(mem[0], w_xk[0], w_xv[0])
    x2, dest, gates, counts = _xattn_router(x1, kmem, vmem, w_xq[0], w_xo[0], ln2_w[0], ln2_b[0],
                                            w_router[0], b_router[0])
    n_blocks = -(-(seq * TOP_K + N_EXPERTS * (MOE_BLOCK - 1)) // MOE_BLOCK)
    shift = _rank_bits(seq)
    pstart, fill, block_e, n_used = _block_tables(counts, n_blocks)
    xs = _dispatch(x2, dest, pstart, fill, n_used, n_blocks, shift)
    ys = _experts(xs, block_e, n_used, w1[0], b1[0], w2[0], b2[0])
    out = _combine(ys, dest, gates, pstart, x2, ln3_w[0], ln3_b[0], shift)
    return out.reshape(bsz, seq, d)
```

```python
import functools

import numpy as np
import jax
import jax.numpy as jnp
from jax import lax
from jax.experimental import pallas as pl
from jax.experimental.pallas import tpu as pltpu

F32 = jnp.float32
BF16 = jnp.bfloat16

D_MODEL = 1024
HGRN_WIDTH = 512
HGRN_HEADS = 4
HEAD_DIM = 128
POOL_WINDOWS = (2, 4, 8, 16)
POOL_GROUP = 128
POOL_CARRY = 16
IN_PROJ_WIDTH = 4 * HGRN_WIDTH + 512
XATTN_HEADS = 4
XATTN_HEAD_DIM = 256
N_EXPERTS = 32
TOP_K = 4
D_EXPERT = 1024
SWIGLU_LIMIT = 7.0
SWIGLU_ALPHA = 1.702
MOE_BLOCK = 256
DEEPNORM_ALPHA = 2.0 ** 0.25
LN_EPS = 1e-5
RMS_EPS = 1e-6

SEQ_BLOCK = 256
N_LEVELS = 8
VMEM_LIMIT = 48 * 1024 * 1024


def _rank_bits(t):
    return max(int(t - 1).bit_length(), 1)


def _layer_norm(y, w, b):
    mu = jnp.mean(y, axis=-1, keepdims=True)
    yc = y - mu
    var = jnp.mean(yc * yc, axis=-1, keepdims=True)
    return yc * lax.rsqrt(var + LN_EPS) * w + b


def _dot(a, b):
    return jnp.dot(a, b, preferred_element_type=F32)


def _dot_nt(a, b):
    return lax.dot_general(a, b, (((1,), (1,)), ((), ())), preferred_element_type=F32)


def _decay_tables():
    c = SEQ_BLOCK
    r = np.arange(c)[:, None]
    u = np.arange(c)[None, :]
    mats = [(u <= r)]
    for l in range(N_LEVELS):
        h = 1 << l
        mid = (r // (2 * h)) * (2 * h) + h
        upper = (r >= mid) & (u >= mid) & (u <= r)
        lower = (r < mid) & (u > r) & (u < mid)
        mats.append(upper | lower)
    mall = np.concatenate(mats, axis=0).astype(np.float32)
    x = r ^ u
    lvl = np.where(u < r, np.floor(np.log2(np.maximum(x, 1))).astype(np.int32), -1).astype(np.int32)
    strict_lower = (u < r).astype(np.float32)
    return mall, lvl, strict_lower


def _mixer_kernel(x_ref, win_ref, lbl_ref, nw_ref, wpool_ref, pscale_ref, wout_ref, lnw_ref, lnb_ref,
                  mall_ref, lvl_ref, o_ref, state_ref, carry_ref, ext_ref, mix_ref):
    i = pl.program_id(0)
    c = SEQ_BLOCK

    @pl.when(i == 0)
    def _():
        state_ref[...] = jnp.zeros_like(state_ref)
        carry_ref[...] = jnp.zeros_like(carry_ref)

    x = x_ref[...]
    proj = _dot(x.astype(BF16), win_ref[...])
    w = HGRN_WIDTH
    q = proj[:, 0:w]
    z = proj[:, w:2 * w]
    v = proj[:, 2 * w:3 * w]
    g = proj[:, 3 * w:4 * w]
    p = proj[:, 4 * w:]

    ll = lbl_ref[...]
    ee = jnp.exp(ll - jnp.max(ll, axis=0, keepdims=True))
    lb = ee[0:1] / jnp.sum(ee, axis=0, keepdims=True)

    f = lb + (1.0 - lb) * (1.0 / (1.0 + jnp.exp(-z)))
    nlf = -jnp.log(f)
    key = (1.0 - lb) * (1.0 / (1.0 + jnp.exp(z)))

    hi = nlf.astype(BF16)
    lo = (nlf - hi.astype(F32)).astype(BF16)
    mall = mall_ref[...]
    cum = _dot(mall, hi) + _dot(mall, lo)
    lvl = lvl_ref[...]

    for h in range(HGRN_HEADS):
        hs = slice(h * HEAD_DIM, (h + 1) * HEAD_DIM)
        qh, kh, vh, gh = q[:, hs], key[:, hs], v[:, hs], g[:, hs]
        vb = vh.astype(BF16)
        negb = cum[0:c, hs]
        scores = jnp.zeros((c, c), F32)
        for l in range(N_LEVELS):
            e = jnp.exp(-cum[(l + 1) * c:(l + 2) * c, hs])
            pl_ = _dot_nt((qh * e).astype(BF16), (kh * e).astype(BF16))
            scores = jnp.where(lvl == l, pl_, scores)
        diag = jnp.sum(qh * kh, axis=-1, keepdims=True)
        st = state_ref[h]
        o = (_dot(scores.astype(BF16), vb) + diag * vh
             + _dot_nt((qh * jnp.exp(-negb)).astype(BF16), st.astype(BF16)))
        negb_last = negb[c - 1:c, :]
        kd = (kh * jnp.exp(negb - negb_last)).astype(BF16)
        state_ref[h] = st * jnp.exp(-negb_last) + _dot(vh.T.astype(BF16), kd)
        o = o * lax.rsqrt(jnp.mean(o * o, axis=-1, keepdims=True) + RMS_EPS) * nw_ref[:, hs]
        gate = gh * (1.0 / (1.0 + jnp.exp(-gh)))
        mix_ref[:, hs] = (o * gate).astype(BF16)

    ext_ref[0:POOL_CARRY, :] = carry_ref[...]
    ext_ref[POOL_CARRY:POOL_CARRY + c, :] = p
    carry_ref[...] = p[c - POOL_CARRY:c, :]
    pos = i * c + lax.broadcasted_iota(jnp.int32, (c, 1), 0)
    for gi, win in enumerate(POOL_WINDOWS):
        gs = slice(gi * POOL_GROUP, (gi + 1) * POOL_GROUP)
        pg = p[:, gs]
        tot = pg
        for j in range(1, win):
            tot = tot + ext_ref[POOL_CARRY - j:POOL_CARRY - j + c, gs]
        cnt = jnp.minimum(pos + 1, win).astype(F32)
        pooled = _dot((tot / cnt - pg).astype(BF16), wpool_ref[gi])
        mix_ref[:, w + gi * POOL_GROUP:w + (gi + 1) * POOL_GROUP] = (pooled * pscale_ref[:, gs]).astype(BF16)

    y = DEEPNORM_ALPHA * x + _dot(mix_ref[...], wout_ref[...])
    o_ref[...] = _layer_norm(y, lnw_ref[...], lnb_ref[...])


def _mixer(x, w_in, lb_logits, norm_w, w_pool, pool_scale, w_out, ln_w, ln_b):
    t = x.shape[0]
    c = SEQ_BLOCK
    mall, lvl, _ = _decay_tables()
    full = lambda shape: pl.BlockSpec(shape, lambda i: (0,) * len(shape))
    return pl.pallas_call(
        _mixer_kernel,
        grid=(t // c,),
        in_specs=[
            pl.BlockSpec((c, D_MODEL), lambda i: (i, 0)),
            full((D_MODEL, IN_PROJ_WIDTH)),
            full(lb_logits.shape),
            full((1, HGRN_WIDTH)),
            full((len(POOL_WINDOWS), POOL_GROUP, POOL_GROUP)),
            full((1, 512)),
            full((D_MODEL, D_MODEL)),
            full((1, D_MODEL)),
            full((1, D_MODEL)),
            full(mall.shape),
            full(lvl.shape),
        ],
        out_specs=pl.BlockSpec((c, D_MODEL), lambda i: (i, 0)),
        out_shape=jax.ShapeDtypeStruct((t, D_MODEL), F32),
        scratch_shapes=[
            pltpu.VMEM((HGRN_HEADS, HEAD_DIM, HEAD_DIM), F32),
            pltpu.VMEM((POOL_CARRY, 512), F32),
            pltpu.VMEM((POOL_CARRY + c, 512), F32),
            pltpu.VMEM((c, D_MODEL), BF16),
        ],
        compiler_params=pltpu.CompilerParams(dimension_semantics=("arbitrary",), vmem_limit_bytes=VMEM_LIMIT),
        name="mixer",
    )(x, w_in.astype(BF16), lb_logits, norm_w.reshape(1, -1), w_pool.astype(BF16), pool_scale.reshape(1, -1),
      w_out.astype(BF16), ln_w.reshape(1, -1), ln_b.reshape(1, -1), jnp.asarray(mall, BF16), jnp.asarray(lvl))


def _kv_kernel(mem_ref, wk_ref, wv_ref, k_ref, v_ref):
    m = mem_ref[...].astype(BF16)
    k_ref[...] = _dot(m, wk_ref[...]).astype(BF16)
    v_ref[...] = _dot(m, wv_ref[...]).astype(BF16)


def _kv_proj(mem, wk, wv):
    n = mem.shape[0]
    return pl.pallas_call(
        _kv_kernel,
        out_shape=(jax.ShapeDtypeStruct((n, D_MODEL), BF16), jax.ShapeDtypeStruct((n, D_MODEL), BF16)),
        compiler_params=pltpu.CompilerParams(vmem_limit_bytes=VMEM_LIMIT),
        name="kv_proj",
    )(mem, wk.astype(BF16), wv.astype(BF16))


def _split3(a):
    a0 = a.astype(BF16)
    r1 = a - a0.astype(F32)
    a1 = r1.astype(BF16)
    a2 = (r1 - a1.astype(F32)).astype(BF16)
    return a0, a1, a2


def _xattn_kernel(x_ref, wq_ref, k_ref, v_ref, wo_ref, lnw_ref, lnb_ref, wr_ref, br_ref, lt_ref,
                  x2_ref, dest_ref, gate_ref, cnt_ref, att_ref, run_ref, *, shift):
    i = pl.program_id(0)
    c = SEQ_BLOCK

    @pl.when(i == 0)
    def _():
        run_ref[...] = jnp.zeros_like(run_ref)

    x = x_ref[...]
    q = _dot(x.astype(BF16), wq_ref[...])
    scale = XATTN_HEAD_DIM ** -0.5
    for h in range(XATTN_HEADS):
        hs = slice(h * XATTN_HEAD_DIM, (h + 1) * XATTN_HEAD_DIM)
        s = _dot_nt(q[:, hs].astype(BF16), k_ref[:, hs]) * scale
        s = s - jnp.max(s, axis=-1, keepdims=True)
        e = jnp.exp(s)
        pr = e / jnp.sum(e, axis=-1, keepdims=True)
        att_ref[:, hs] = _dot(pr.astype(BF16), v_ref[:, hs]).astype(BF16)
    y = DEEPNORM_ALPHA * x + _dot(att_ref[...], wo_ref[...])
    x2 = _layer_norm(y, lnw_ref[...], lnb_ref[...])
    x2_ref[...] = x2

    a0, a1, a2 = _split3(x2)
    w0, w1, w2 = _split3(wr_ref[...])
    logits = (_dot(a2, w0) + _dot(a0, w2) + _dot(a1, w1)) + (_dot(a1, w0) + _dot(a0, w1)) + _dot(a0, w0)
    logits = logits + br_ref[...]

    lane = lax.broadcasted_iota(jnp.int32, (c, N_EXPERTS), 1)
    col = lax.broadcasted_iota(jnp.int32, (c, TOP_K), 1)
    work = logits
    sels, vals = [], []
    topi = jnp.zeros((c, TOP_K), jnp.int32)
    for k in range(TOP_K):
        m = jnp.max(work, axis=-1, keepdims=True)
        idx = jnp.min(jnp.where(work == m, lane, N_EXPERTS), axis=-1, keepdims=True)
        sel = lane == idx
        sels.append(sel)
        vals.append(m)
        topi = jnp.where(col == k, idx, topi)
        work = jnp.where(sel, -jnp.inf, work)
    es = [jnp.exp(vk - vals[0]) for vk in vals]
    denom = es[0] + es[1] + es[2] + es[3]
    gates = jnp.zeros((c, TOP_K), F32)
    for k in range(TOP_K):
        gates = jnp.where(col == k, es[k] / denom, gates)

    onehot = jnp.zeros((c, N_EXPERTS), F32)
    for sel in sels:
        onehot = jnp.where(sel, 1.0, onehot)
    before = _dot(lt_ref[...], onehot.astype(BF16)) + run_ref[...]
    rank = jnp.zeros((c, TOP_K), jnp.int32)
    for k in range(TOP_K):
        rk = jnp.sum(jnp.where(sels[k], before, 0.0), axis=-1, keepdims=True)
        rank = jnp.where(col == k, rk.astype(jnp.int32), rank)
    run_ref[...] = run_ref[...] + jnp.sum(onehot, axis=0, keepdims=True)

    dest_ref[...] = topi * (1 << shift) + rank
    gate_ref[...] = gates
    cnt_ref[...] = run_ref[...].astype(jnp.int32)


def _xattn_router(x1, kmem, vmem, wq, wo, ln_w, ln_b, w_r, b_r):
    t = x1.shape[0]
    c = SEQ_BLOCK
    n_mem = kmem.shape[0]
    _, _, strict_lower = _decay_tables()
    full = lambda shape: pl.BlockSpec(shape, lambda i: (0,) * len(shape))
    tok = lambda width: pl.BlockSpec((c, width), lambda i: (i, 0))
    return pl.pallas_call(
        functools.partial(_xattn_kernel, shift=_rank_bits(t)),
        grid=(t // c,),
        in_specs=[
            tok(D_MODEL),
            full((D_MODEL, D_MODEL)),
            full((n_mem, D_MODEL)),
            full((n_mem, D_MODEL)),
            full((D_MODEL, D_MODEL)),
            full((1, D_MODEL)),
            full((1, D_MODEL)),
            full((D_MODEL, N_EXPERTS)),
            full((1, N_EXPERTS)),
            full((c, c)),
        ],
        out_specs=[tok(D_MODEL), tok(TOP_K), tok(TOP_K), full((1, N_EXPERTS))],
        out_shape=[
            jax.ShapeDtypeStruct((t, D_MODEL), F32),
            jax.ShapeDtypeStruct((t, TOP_K), jnp.int32),
            jax.ShapeDtypeStruct((t, TOP_K), F32),
            jax.ShapeDtypeStruct((1, N_EXPERTS), jnp.int32),
        ],
        scratch_shapes=[pltpu.VMEM((c, D_MODEL), BF16), pltpu.VMEM((1, N_EXPERTS), F32)],
        compiler_params=pltpu.CompilerParams(dimension_semantics=("arbitrary",), vmem_limit_bytes=VMEM_LIMIT),
        name="xattn_router",
    )(x1, wq.astype(BF16), kmem, vmem, wo.astype(BF16), ln_w.reshape(1, -1), ln_b.reshape(1, -1),
      w_r, b_r.reshape(1, -1), jnp.asarray(strict_lower, BF16))


def _slot(packed, pstart_ref, shift):
    return pstart_ref[lax.shift_right_logical(packed, shift)] + (packed & ((1 << shift) - 1))


def _dispatch_kernel(pstart_ref, fill_ref, nused_ref, dest_ref, x_ref, xs_hbm, zero_ref, sem, zsem, *, shift, n_blocks):
    i = pl.program_id(0)
    c = SEQ_BLOCK

    @pl.when(i == 0)
    def _():
        zero_ref[...] = jnp.zeros_like(zero_ref)

        def tail(e, act):
            s0 = fill_ref[e]
            end = pstart_ref[e + 1]
            n1 = jnp.minimum((-s0) & 7, end - s0)
            s1 = s0 + n1

            def one(r, carry):
                act(pltpu.make_async_copy(zero_ref.at[pl.ds(0, 1)], xs_hbm.at[pl.ds(s0 + r, 1)], zsem))
                return carry

            def eight(j, carry):
                row = pl.multiple_of(s1 + 8 * j, 8)
                act(pltpu.make_async_copy(zero_ref.at[pl.ds(0, 8)], xs_hbm.at[pl.ds(row, 8)], zsem))
                return carry

            lax.fori_loop(0, n1, one, 0)
            lax.fori_loop(0, lax.shift_right_logical(end - s1, 3), eight, 0)

        def spare(b, act):
            row = pl.multiple_of(b * MOE_BLOCK, MOE_BLOCK)
            act(pltpu.make_async_copy(zero_ref, xs_hbm.at[pl.ds(row, MOE_BLOCK)], zsem))

        for act in (lambda cp: cp.start(), lambda cp: cp.wait()):
            lax.fori_loop(0, N_EXPERTS, lambda e, carry: (tail(e, act), carry)[1], 0)
            lax.fori_loop(nused_ref[0], n_blocks, lambda b, carry: (spare(b, act), carry)[1], 0)

    def issue(r, carry):
        src = x_ref.at[pl.ds(r, 1)]
        for k in range(TOP_K):
            row = _slot(dest_ref[0, 0, r * TOP_K + k], pstart_ref, shift)
            pltpu.make_async_copy(src, xs_hbm.at[pl.ds(row, 1)], sem).start(priority=k % 2)
        return carry
    lax.fori_loop(0, c, issue, 0)
    for k in range(TOP_K):
        pltpu.make_async_copy(x_ref, xs_hbm.at[pl.ds(0, c)], sem).wait()


def _dispatch(x2, dest, pstart, fill, n_used, n_blocks, shift):
    t = x2.shape[0]
    c = SEQ_BLOCK
    grid_spec = pltpu.PrefetchScalarGridSpec(
        num_scalar_prefetch=3,
        grid=(t // c,),
        in_specs=[
            pl.BlockSpec((1, 1, c * TOP_K), lambda i, *_: (i, 0, 0), memory_space=pltpu.SMEM),
            pl.BlockSpec((c, D_MODEL), lambda i, *_: (i, 0)),
        ],
        out_specs=pl.BlockSpec(memory_space=pl.ANY),
        scratch_shapes=[pltpu.VMEM((MOE_BLOCK, D_MODEL), F32), pltpu.SemaphoreType.DMA, pltpu.SemaphoreType.DMA],
    )
    return pl.pallas_call(
        functools.partial(_dispatch_kernel, shift=shift, n_blocks=n_blocks),
        grid_spec=grid_spec,
        out_shape=jax.ShapeDtypeStruct((n_blocks * MOE_BLOCK, D_MODEL), F32),
        compiler_params=pltpu.CompilerParams(dimension_semantics=("arbitrary",), vmem_limit_bytes=VMEM_LIMIT),
        name="dispatch",
    )(pstart, fill, n_used, dest.reshape(t // c, 1, c * TOP_K), x2)


def _expert_kernel(be_ref, nused_ref, x_ref, w1_ref, b1_ref, w2_ref, b2_ref, o_ref, w1b, w2b):
    b = pl.program_id(0)

    @pl.when(b >= nused_ref[0])
    def _():
        o_ref[...] = jnp.zeros_like(o_ref)

    @pl.when(b < nused_ref[0])
    def _():
        prev = be_ref[jnp.maximum(b - 1, 0)]

        @pl.when((b == 0) | (be_ref[b] != prev))
        def _():
            w1b[...] = w1_ref[0].astype(BF16)
            w2b[...] = w2_ref[0].astype(BF16)

        h = _dot(x_ref[...].astype(BF16), w1b[...]) + b1_ref[0]
        h_glu = jnp.minimum(h[:, :D_EXPERT], SWIGLU_LIMIT)
        h_lin = jnp.clip(h[:, D_EXPERT:], -SWIGLU_LIMIT, SWIGLU_LIMIT)
        act = h_glu * (1.0 / (1.0 + jnp.exp(-SWIGLU_ALPHA * h_glu))) * (h_lin + 1.0)
        o_ref[...] = _dot(act.astype(BF16), w2b[...]) + b2_ref[0]


def _experts(xs, block_e, n_used, w1, b1, w2, b2):
    n_blocks = block_e.shape[0]
    rows = lambda b, be, nu: (b, 0)
    wsel = lambda b, be, nu: (be[b], 0, 0)
    grid_spec = pltpu.PrefetchScalarGridSpec(
        num_scalar_prefetch=2,
        grid=(n_blocks,),
        in_specs=[
            pl.BlockSpec((MOE_BLOCK, D_MODEL), rows),
            pl.BlockSpec((1, D_MODEL, 2 * D_EXPERT), wsel),
            pl.BlockSpec((1, 1, 2 * D_EXPERT), wsel),
            pl.BlockSpec((1, D_EXPERT, D_MODEL), wsel),
            pl.BlockSpec((1, 1, D_MODEL), wsel),
        ],
        out_specs=pl.BlockSpec((MOE_BLOCK, D_MODEL), rows),
        scratch_shapes=[
            pltpu.VMEM((D_MODEL, 2 * D_EXPERT), BF16),
            pltpu.VMEM((D_EXPERT, D_MODEL), BF16),
        ],
    )
    return pl.pallas_call(
        _expert_kernel,
        grid_spec=grid_spec,
        out_shape=jax.ShapeDtypeStruct(xs.shape, F32),
        compiler_params=pltpu.CompilerParams(dimension_semantics=("arbitrary",), vmem_limit_bytes=VMEM_LIMIT),
        name="experts",
    )(block_e, n_used, xs, w1, b1.reshape(N_EXPERTS, 1, -1), w2, b2.reshape(N_EXPERTS, 1, -1))


def _combine_kernel(pstart_ref, dest_ref, ys_hbm, gate_ref, x_ref, lnw_ref, lnb_ref, o_ref, ybuf, sem, *, shift):
    c = SEQ_BLOCK

    def issue(r, carry):
        for k in range(TOP_K):
            row = _slot(dest_ref[0, 0, r * TOP_K + k], pstart_ref, shift)
            pltpu.make_async_copy(ys_hbm.at[pl.ds(row, 1)], ybuf.at[k, pl.ds(r, 1)], sem).start(priority=k % 2)
        return carry
    lax.fori_loop(0, c, issue, 0)
    for k in range(TOP_K):
        pltpu.make_async_copy(ys_hbm.at[pl.ds(0, c)], ybuf.at[k], sem).wait()
    gates = gate_ref[...]
    y = ((ybuf[0] * gates[:, 0:1] + ybuf[1] * gates[:, 1:2])
         + (ybuf[2] * gates[:, 2:3] + ybuf[3] * gates[:, 3:4]))
    o_ref[...] = _layer_norm(DEEPNORM_ALPHA * x_ref[...] + y, lnw_ref[...], lnb_ref[...])


def _combine(ys, dest, gates, pstart, x2, ln_w, ln_b, shift):
    t = x2.shape[0]
    c = SEQ_BLOCK
    grid_spec = pltpu.PrefetchScalarGridSpec(
        num_scalar_prefetch=1,
        grid=(t // c,),
        in_specs=[
            pl.BlockSpec((1, 1, c * TOP_K), lambda i, *_: (i, 0, 0), memory_space=pltpu.SMEM),
            pl.BlockSpec(memory_space=pl.ANY),
            pl.BlockSpec((c, TOP_K), lambda i, *_: (i, 0)),
            pl.BlockSpec((c, D_MODEL), lambda i, *_: (i, 0)),
            pl.BlockSpec((1, D_MODEL), lambda i, *_: (0, 0)),
            pl.BlockSpec((1, D_MODEL), lambda i, *_: (0, 0)),
        ],
        out_specs=pl.BlockSpec((c, D_MODEL), lambda i, *_: (i, 0)),
        scratch_shapes=[pltpu.VMEM((TOP_K, c, D_MODEL), F32), pltpu.SemaphoreType.DMA],
    )
    return pl.pallas_call(
        functools.partial(_combine_kernel, shift=shift),
        grid_spec=grid_spec,
        out_shape=jax.ShapeDtypeStruct((t, D_MODEL), F32),
        compiler_params=pltpu.CompilerParams(dimension_semantics=("arbitrary",), vmem_limit_bytes=VMEM_LIMIT),
        name="combine",
    )(pstart, dest.reshape(t // c, 1, c * TOP_K), ys, gates, x2, ln_w.reshape(1, -1), ln_b.reshape(1, -1))


def _block_tables(counts, n_blocks):
    counts = counts.reshape(-1)
    nblk = (counts + MOE_BLOCK - 1) // MOE_BLOCK
    ids = jnp.arange(N_EXPERTS)
    bend = jnp.sum(jnp.where(ids[None, :] <= ids[:, None], nblk[None, :], 0), axis=1)
    n_used = bend[N_EXPERTS - 1]
    pstart = jnp.concatenate([jnp.zeros((1,), bend.dtype), bend]) * MOE_BLOCK
    fill = pstart[:N_EXPERTS] + counts
    b = jnp.minimum(jnp.arange(n_blocks), n_used - 1)
    block_e = jnp.sum((bend[None, :] <= b[:, None]).astype(jnp.int32), axis=1)
    i32 = lambda a: a.astype(jnp.int32)
    return i32(pstart), i32(fill), i32(block_e), i32(n_used.reshape(1))


def kernel(x, mem, w_in, lb_logits, hgrn_norm_w, w_pool, pool_scale, w_out, ln1_w, ln1_b, w_xq, w_xk, w_xv, w_xo,
           ln2_w, ln2_b, w_router, b_router, w1, b1, w2, b2, ln3_w, ln3_b):
    bsz, seq, d = x.shape
    assert bsz == 1 and d == D_MODEL and seq % SEQ_BLOCK == 0 and w_in.shape[0] == 1
    xt = x.reshape(seq, d)
    x1 = _mixer(xt, w_in[0], lb_logits, hgrn_norm_w[0], w_pool[0], pool_scale[0], w_out[0], ln1_w[0], ln1_b[0])
    kmem, vmem = _kv_proj(mem[0], w_xk[0], w_xv[0])
    x2, dest, gates, counts = _xattn_router(x1, kmem, vmem, w_xq[0], w_xo[0], ln2_w[0], ln2_b[0],
                                            w_router[0], b_router[0])
    n_blocks = -(-(seq * TOP_K + N_EXPERTS * (MOE_BLOCK - 1)) // MOE_BLOCK)
    shift = _rank_bits(seq)
    pstart, fill, block_e, n_used = _block_tables(counts, n_blocks)
    xs = _dispatch(x2, dest, pstart, fill, n_used, n_blocks, shift)
    ys = _experts(xs, block_e, n_used, w1[0], b1[0], w2[0], b2[0])
    out = _combine(ys, dest, gates, pstart, x2, ln3_w[0], ln3_b[0], shift)
    return out.reshape(bsz, seq, d)
```

```python
import functools

import numpy as np
import jax
import jax.numpy as jnp
from jax import lax
from jax.experimental import pallas as pl
from jax.experimental.pallas import tpu as pltpu

F32 = jnp.float32
BF16 = jnp.bfloat16

D_MODEL = 1024
HGRN_WIDTH = 512
HGRN_HEADS = 4
HEAD_DIM = 128
POOL_WINDOWS = (2, 4, 8, 16)
POOL_GROUP = 128
POOL_CARRY = 16
IN_PROJ_WIDTH = 4 * HGRN_WIDTH + 512
XATTN_HEADS = 4
XATTN_HEAD_DIM = 256
N_EXPERTS = 32
TOP_K = 4
D_EXPERT = 1024
SWIGLU_LIMIT = 7.0
SWIGLU_ALPHA = 1.702
MOE_BLOCK = 256
DEEPNORM_ALPHA = 2.0 ** 0.25
LN_EPS = 1e-5
RMS_EPS = 1e-6

SEQ_BLOCK = 256
N_LEVELS = 8
ROW_TILE = (8, 128)
VMEM_LIMIT = 48 * 1024 * 1024


def _rank_bits(t):
    return max(int(t - 1).bit_length(), 1)


def _layer_norm(y, w, b):
    mu = jnp.mean(y, axis=-1, keepdims=True)
    yc = y - mu
    var = jnp.mean(yc * yc, axis=-1, keepdims=True)
    return yc * lax.rsqrt(var + LN_EPS) * w + b


def _tiled(n):
    return (n * ROW_TILE[0], ROW_TILE[1])


def _tiled_spec(n, index_map):
    return pl.BlockSpec(_tiled(n), index_map)


def _tile_of(ref, r):
    return ref.at[pl.ds(pl.multiple_of(r * ROW_TILE[0], ROW_TILE[0]), ROW_TILE[0])]


def _store_row_tiles(ref, val):
    s, l = ROW_TILE
    for j in range(s):
        ref[pl.ds(j, val.shape[0], stride=s), :] = val[:, j * l:(j + 1) * l]


def _load_row_tiles(ref):
    s, _ = ROW_TILE
    return jnp.concatenate([ref[pl.ds(j, ref.shape[0] // s, stride=s), :] for j in range(s)], axis=-1)


def _dot(a, b):
    return jnp.dot(a, b, preferred_element_type=F32)


def _dot_nt(a, b):
    return lax.dot_general(a, b, (((1,), (1,)), ((), ())), preferred_element_type=F32)


def _decay_tables():
    c = SEQ_BLOCK
    r = np.arange(c)[:, None]
    u = np.arange(c)[None, :]
    mats = [(u <= r)]
    for l in range(N_LEVELS):
        h = 1 << l
        mid = (r // (2 * h)) * (2 * h) + h
        upper = (r >= mid) & (u >= mid) & (u <= r)
        lower = (r < mid) & (u > r) & (u < mid)
        mats.append(upper | lower)
    mall = np.concatenate(mats, axis=0).astype(np.float32)
    x = r ^ u
    lvl = np.where(u < r, np.floor(np.log2(np.maximum(x, 1))).astype(np.int32), -1).astype(np.int32)
    strict_lower = (u < r).astype(np.float32)
    return mall, lvl, strict_lower


def _mixer_kernel(x_ref, win_ref, lbl_ref, nw_ref, wpool_ref, pscale_ref, wout_ref, lnw_ref, lnb_ref,
                  mall_ref, lvl_ref, o_ref, state_ref, carry_ref, ext_ref, mix_ref):
    i = pl.program_id(0)
    c = SEQ_BLOCK

    @pl.when(i == 0)
    def _():
        state_ref[...] = jnp.zeros_like(state_ref)
        carry_ref[...] = jnp.zeros_like(carry_ref)

    x = x_ref[...]
    proj = _dot(x.astype(BF16), win_ref[...])
    w = HGRN_WIDTH
    q = proj[:, 0:w]
    z = proj[:, w:2 * w]
    v = proj[:, 2 * w:3 * w]
    g = proj[:, 3 * w:4 * w]
    p = proj[:, 4 * w:]

    ll = lbl_ref[...]
    ee = jnp.exp(ll - jnp.max(ll, axis=0, keepdims=True))
    lb = ee[0:1] / jnp.sum(ee, axis=0, keepdims=True)

    f = lb + (1.0 - lb) * (1.0 / (1.0 + jnp.exp(-z)))
    nlf = -jnp.log(f)
    key = (1.0 - lb) * (1.0 / (1.0 + jnp.exp(z)))

    hi = nlf.astype(BF16)
    lo = (nlf - hi.astype(F32)).astype(BF16)
    mall = mall_ref[...]
    cum = _dot(mall, hi) + _dot(mall, lo)
    lvl = lvl_ref[...]

    for h in range(HGRN_HEADS):
        hs = slice(h * HEAD_DIM, (h + 1) * HEAD_DIM)
        qh, kh, vh, gh = q[:, hs], key[:, hs], v[:, hs], g[:, hs]
        vb = vh.astype(BF16)
        negb = cum[0:c, hs]
        scores = jnp.zeros((c, c), F32)
        for l in range(N_LEVELS):
            e = jnp.exp(-cum[(l + 1) * c:(l + 2) * c, hs])
            pl_ = _dot_nt((qh * e).astype(BF16), (kh * e).astype(BF16))
            scores = jnp.where(lvl == l, pl_, scores)
        diag = jnp.sum(qh * kh, axis=-1, keepdims=True)
        st = state_ref[h]
        o = (_dot(scores.astype(BF16), vb) + diag * vh
             + _dot_nt((qh * jnp.exp(-negb)).astype(BF16), st.astype(BF16)))
        negb_last = negb[c - 1:c, :]
        kd = (kh * jnp.exp(negb - negb_last)).astype(BF16)
        state_ref[h] = st * jnp.exp(-negb_last) + _dot(vh.T.astype(BF16), kd)
        o = o * lax.rsqrt(jnp.mean(o * o, axis=-1, keepdims=True) + RMS_EPS) * nw_ref[:, hs]
        gate = gh * (1.0 / (1.0 + jnp.exp(-gh)))
        mix_ref[:, hs] = (o * gate).astype(BF16)

    ext_ref[0:POOL_CARRY, :] = carry_ref[...]
    ext_ref[POOL_CARRY:POOL_CARRY + c, :] = p
    carry_ref[...] = p[c - POOL_CARRY:c, :]
    pos = i * c + lax.broadcasted_iota(jnp.int32, (c, 1), 0)
    for gi, win in enumerate(POOL_WINDOWS):
        gs = slice(gi * POOL_GROUP, (gi + 1) * POOL_GROUP)
        pg = p[:, gs]
        tot = pg
        for j in range(1, win):
            tot = tot + ext_ref[POOL_CARRY - j:POOL_CARRY - j + c, gs]
        cnt = jnp.minimum(pos + 1, win).astype(F32)
        pooled = _dot((tot / cnt - pg).astype(BF16), wpool_ref[gi])
        mix_ref[:, w + gi * POOL_GROUP:w + (gi + 1) * POOL_GROUP] = (pooled * pscale_ref[:, gs]).astype(BF16)

    y = DEEPNORM_ALPHA * x + _dot(mix_ref[...], wout_ref[...])
    o_ref[...] = _layer_norm(y, lnw_ref[...], lnb_ref[...])


def _mixer(x, w_in, lb_logits, norm_w, w_pool, pool_scale, w_out, ln_w, ln_b):
    t = x.shape[0]
    c = SEQ_BLOCK
    mall, lvl, _ = _decay_tables()
    full = lambda shape: pl.BlockSpec(shape, lambda i: (0,) * len(shape))
    return pl.pallas_call(
        _mixer_kernel,
        grid=(t // c,),
        in_specs=[
            pl.BlockSpec((c, D_MODEL), lambda i: (i, 0)),
            full((D_MODEL, IN_PROJ_WIDTH)),
            full(lb_logits.shape),
            full((1, HGRN_WIDTH)),
            full((len(POOL_WINDOWS), POOL_GROUP, POOL_GROUP)),
            full((1, 512)),
            full((D_MODEL, D_MODEL)),
            full((1, D_MODEL)),
            full((1, D_MODEL)),
            full(mall.shape),
            full(lvl.shape),
        ],
        out_specs=pl.BlockSpec((c, D_MODEL), lambda i: (i, 0)),
        out_shape=jax.ShapeDtypeStruct((t, D_MODEL), F32),
        scratch_shapes=[
            pltpu.VMEM((HGRN_HEADS, HEAD_DIM, HEAD_DIM), F32),
            pltpu.VMEM((POOL_CARRY, 512), F32),
            pltpu.VMEM((POOL_CARRY + c, 512), F32),
            pltpu.VMEM((c, D_MODEL), BF16),
        ],
        compiler_params=pltpu.CompilerParams(dimension_semantics=("arbitrary",), vmem_limit_bytes=VMEM_LIMIT),
        name="mixer",
    )(x, w_in.astype(BF16), lb_logits, norm_w.reshape(1, -1), w_pool.astype(BF16), pool_scale.reshape(1, -1),
      w_out.astype(BF16), ln_w.reshape(1, -1), ln_b.reshape(1, -1), jnp.asarray(mall, BF16), jnp.asarray(lvl))


def _kv_kernel(mem_ref, wk_ref, wv_ref, k_ref, v_ref):
    m = mem_ref[...].astype(BF16)
    k_ref[...] = _dot(m, wk_ref[...]).astype(BF16)
    v_ref[...] = _dot(m, wv_ref[...]).astype(BF16)


def _kv_proj(mem, wk, wv):
    n = mem.shape[0]
    return pl.pallas_call(
        _kv_kernel,
        out_shape=(jax.ShapeDtypeStruct((n, D_MODEL), BF16), jax.ShapeDtypeStruct((n, D_MODEL), BF16)),
        compiler_params=pltpu.CompilerParams(vmem_limit_bytes=VMEM_LIMIT),
        name="kv_proj",
    )(mem, wk.astype(BF16), wv.astype(BF16))


def _split3(a):
    a0 = a.astype(BF16)
    r1 = a - a0.astype(F32)
    a1 = r1.astype(BF16)
    a2 = (r1 - a1.astype(F32)).astype(BF16)
    return a0, a1, a2


def _xattn_kernel(x_ref, wq_ref, k_ref, v_ref, wo_ref, lnw_ref, lnb_ref, wr_ref, br_ref, lt_ref,
                  x2_ref, dest_ref, gate_ref, cnt_ref, att_ref, run_ref, *, shift):
    i = pl.program_id(0)
    c = SEQ_BLOCK

    @pl.when(i == 0)
    def _():
        run_ref[...] = jnp.zeros_like(run_ref)

    x = x_ref[...]
    q = _dot(x.astype(BF16), wq_ref[...])
    scale = XATTN_HEAD_DIM ** -0.5
    for h in range(XATTN_HEADS):
        hs = slice(h * XATTN_HEAD_DIM, (h + 1) * XATTN_HEAD_DIM)
        s = _dot_nt(q[:, hs].astype(BF16), k_ref[:, hs]) * scale
        s = s - jnp.max(s, axis=-1, keepdims=True)
        e = jnp.exp(s)
        pr = e / jnp.sum(e, axis=-1, keepdims=True)
        att_ref[:, hs] = _dot(pr.astype(BF16), v_ref[:, hs]).astype(BF16)
    y = DEEPNORM_ALPHA * x + _dot(att_ref[...], wo_ref[...])
    x2 = _layer_norm(y, lnw_ref[...], lnb_ref[...])
    _store_row_tiles(x2_ref, x2)

    a0, a1, a2 = _split3(x2)
    w0, w1, w2 = _split3(wr_ref[...])
    logits = (_dot(a2, w0) + _dot(a0, w2) + _dot(a1, w1)) + (_dot(a1, w0) + _dot(a0, w1)) + _dot(a0, w0)
    logits = logits + br_ref[...]

    lane = lax.broadcasted_iota(jnp.int32, (c, N_EXPERTS), 1)
    col = lax.broadcasted_iota(jnp.int32, (c, TOP_K), 1)
    work = logits
    sels, vals = [], []
    topi = jnp.zeros((c, TOP_K), jnp.int32)
    for k in range(TOP_K):
        m = jnp.max(work, axis=-1, keepdims=True)
        idx = jnp.min(jnp.where(work == m, lane, N_EXPERTS), axis=-1, keepdims=True)
        sel = lane == idx
        sels.append(sel)
        vals.append(m)
        topi = jnp.where(col == k, idx, topi)
        work = jnp.where(sel, -jnp.inf, work)
    es = [jnp.exp(vk - vals[0]) for vk in vals]
    denom = es[0] + es[1] + es[2] + es[3]
    gates = jnp.zeros((c, TOP_K), F32)
    for k in range(TOP_K):
        gates = jnp.where(col == k, es[k] / denom, gates)

    onehot = jnp.zeros((c, N_EXPERTS), F32)
    for sel in sels:
        onehot = jnp.where(sel, 1.0, onehot)
    before = _dot(lt_ref[...], onehot.astype(BF16)) + run_ref[...]
    rank = jnp.zeros((c, TOP_K), jnp.int32)
    for k in range(TOP_K):
        rk = jnp.sum(jnp.where(sels[k], before, 0.0), axis=-1, keepdims=True)
        rank = jnp.where(col == k, rk.astype(jnp.int32), rank)
    run_ref[...] = run_ref[...] + jnp.sum(onehot, axis=0, keepdims=True)

    dest_ref[...] = topi * (1 << shift) + rank
    gate_ref[...] = gates
    cnt_ref[...] = run_ref[...].astype(jnp.int32)


def _xattn_router(x1, kmem, vmem, wq, wo, ln_w, ln_b, w_r, b_r):
    t = x1.shape[0]
    c = SEQ_BLOCK
    n_mem = kmem.shape[0]
    _, _, strict_lower = _decay_tables()
    full = lambda shape: pl.BlockSpec(shape, lambda i: (0,) * len(shape))
    tok = lambda width: pl.BlockSpec((c, width), lambda i: (i, 0))
    return pl.pallas_call(
        functools.partial(_xattn_kernel, shift=_rank_bits(t)),
        grid=(t // c,),
        in_specs=[
            tok(D_MODEL),
            full((D_MODEL, D_MODEL)),
            full((n_mem, D_MODEL)),
            full((n_mem, D_MODEL)),
            full((D_MODEL, D_MODEL)),
            full((1, D_MODEL)),
            full((1, D_MODEL)),
            full((D_MODEL, N_EXPERTS)),
            full((1, N_EXPERTS)),
            full((c, c)),
        ],
        out_specs=[_tiled_spec(c, lambda i: (i, 0)), tok(TOP_K), tok(TOP_K), full((1, N_EXPERTS))],
        out_shape=[
            jax.ShapeDtypeStruct(_tiled(t), F32),
            jax.ShapeDtypeStruct((t, TOP_K), jnp.int32),
            jax.ShapeDtypeStruct((t, TOP_K), F32),
            jax.ShapeDtypeStruct((1, N_EXPERTS), jnp.int32),
        ],
        scratch_shapes=[pltpu.VMEM((c, D_MODEL), BF16), pltpu.VMEM((1, N_EXPERTS), F32)],
        compiler_params=pltpu.CompilerParams(dimension_semantics=("arbitrary",), vmem_limit_bytes=VMEM_LIMIT),
        name="xattn_router",
    )(x1, wq.astype(BF16), kmem, vmem, wo.astype(BF16), ln_w.reshape(1, -1), ln_b.reshape(1, -1),
      w_r, b_r.reshape(1, -1), jnp.asarray(strict_lower, BF16))


TAIL_RUNS = (128, 64, 32, 16, 8, 4, 2, 1)


def _dispatch_kernel(fill_ref, pend_ref, nused_ref, slot_ref, x_ref, xs_hbm, zero_ref, sem, zsem, *, n_blocks):
    i = pl.program_id(0)
    c = SEQ_BLOCK

    @pl.when(i == 0)
    def _():
        zero_ref[...] = jnp.zeros_like(zero_ref)

        sub = ROW_TILE[0]

        def rows(ref, first, n):
            return ref.at[pl.ds(pl.multiple_of(first * sub, sub), n * sub)]

        def tail(e, act):
            row = fill_ref[e]
            pad = pend_ref[e] - row
            for run in TAIL_RUNS:
                hit = (pad & run) != 0

                @pl.when(hit)
                def _():
                    act(pltpu.make_async_copy(rows(zero_ref, 0, run), rows(xs_hbm, row, run), zsem))
                row = row + jnp.where(hit, run, 0)

        def spare(b, act):
            act(pltpu.make_async_copy(zero_ref, rows(xs_hbm, b * MOE_BLOCK, MOE_BLOCK), zsem))

        for act in (lambda cp: cp.start(), lambda cp: cp.wait()):
            lax.fori_loop(0, N_EXPERTS, lambda e, carry: (tail(e, act), carry)[1], 0)
            lax.fori_loop(nused_ref[0], n_blocks, lambda b, carry: (spare(b, act), carry)[1], 0)

    def issue(r, carry):
        for k in range(TOP_K):
            dst = _tile_of(xs_hbm, slot_ref[0, 0, r * TOP_K + k])
            pltpu.make_async_copy(_tile_of(x_ref, r), dst, sem).start(priority=k % 2)
        return carry
    lax.fori_loop(0, c, issue, 0)
    for k in range(TOP_K):
        pltpu.make_async_copy(x_ref, xs_hbm.at[pl.ds(0, x_ref.shape[0])], sem).wait()


def _dispatch(x2, slots, fill, pend, n_used, n_blocks):
    t = x2.shape[0] // ROW_TILE[0]
    c = SEQ_BLOCK
    grid_spec = pltpu.PrefetchScalarGridSpec(
        num_scalar_prefetch=3,
        grid=(t // c,),
        in_specs=[
            pl.BlockSpec((1, 1, c * TOP_K), lambda i, *_: (i, 0, 0), memory_space=pltpu.SMEM),
            _tiled_spec(c, lambda i, *_: (i, 0)),
        ],
        out_specs=pl.BlockSpec(memory_space=pl.ANY),
        scratch_shapes=[pltpu.VMEM(_tiled(MOE_BLOCK), F32), pltpu.SemaphoreType.DMA, pltpu.SemaphoreType.DMA],
    )
    return pl.pallas_call(
        functools.partial(_dispatch_kernel, n_blocks=n_blocks),
        grid_spec=grid_spec,
        out_shape=jax.ShapeDtypeStruct(_tiled(n_blocks * MOE_BLOCK), F32),
        compiler_params=pltpu.CompilerParams(dimension_semantics=("arbitrary",), vmem_limit_bytes=VMEM_LIMIT),
        name="dispatch",
    )(fill, pend, n_used, slots.reshape(t // c, 1, c * TOP_K), x2)


def _expert_kernel(be_ref, nused_ref, x_ref, w1_ref, b1_ref, w2_ref, b2_ref, o_ref, w1b, w2b):
    b = pl.program_id(0)

    @pl.when(b >= nused_ref[0])
    def _():
        o_ref[...] = jnp.zeros_like(o_ref)

    @pl.when(b < nused_ref[0])
    def _():
        prev = be_ref[jnp.maximum(b - 1, 0)]

        @pl.when((b == 0) | (be_ref[b] != prev))
        def _():
            w1b[...] = w1_ref[0].astype(BF16)
            w2b[...] = w2_ref[0].astype(BF16)

        h = _dot(_load_row_tiles(x_ref).astype(BF16), w1b[...]) + b1_ref[0]
        h_glu = jnp.minimum(h[:, :D_EXPERT], SWIGLU_LIMIT)
        h_lin = jnp.clip(h[:, D_EXPERT:], -SWIGLU_LIMIT, SWIGLU_LIMIT)
        act = h_glu * (1.0 / (1.0 + jnp.exp(-SWIGLU_ALPHA * h_glu))) * (h_lin + 1.0)
        _store_row_tiles(o_ref, _dot(act.astype(BF16), w2b[...]) + b2_ref[0])


def _experts(xs, block_e, n_used, w1, b1, w2, b2):
    n_blocks = block_e.shape[0]
    rows = lambda b, be, nu: (b, 0)
    wsel = lambda b, be, nu: (be[b], 0, 0)
    grid_spec = pltpu.PrefetchScalarGridSpec(
        num_scalar_prefetch=2,
        grid=(n_blocks,),
        in_specs=[
            _tiled_spec(MOE_BLOCK, rows),
            pl.BlockSpec((1, D_MODEL, 2 * D_EXPERT), wsel),
            pl.BlockSpec((1, 1, 2 * D_EXPERT), wsel),
            pl.BlockSpec((1, D_EXPERT, D_MODEL), wsel),
            pl.BlockSpec((1, 1, D_MODEL), wsel),
        ],
        out_specs=_tiled_spec(MOE_BLOCK, rows),
        scratch_shapes=[
            pltpu.VMEM((D_MODEL, 2 * D_EXPERT), BF16),
            pltpu.VMEM((D_EXPERT, D_MODEL), BF16),
        ],
    )
    return pl.pallas_call(
        _expert_kernel,
        grid_spec=grid_spec,
        out_shape=jax.ShapeDtypeStruct(xs.shape, F32),
        compiler_params=pltpu.CompilerParams(dimension_semantics=("arbitrary",), vmem_limit_bytes=VMEM_LIMIT),
        name="experts",
    )(block_e, n_used, xs, w1, b1.reshape(N_EXPERTS, 1, -1), w2, b2.reshape(N_EXPERTS, 1, -1))


def _combine_kernel(slot_ref, ys_hbm, gate_ref, x_ref, lnw_ref, lnb_ref, o_ref, ybuf, sem):
    c = SEQ_BLOCK

    def issue(r, carry):
        for k in range(TOP_K):
            src = _tile_of(ys_hbm, slot_ref[0, 0, r * TOP_K + k])
            pltpu.make_async_copy(src, _tile_of(ybuf.at[k], r), sem).start(priority=k % 2)
        return carry
    lax.fori_loop(0, c, issue, 0)
    for k in range(TOP_K):
        pltpu.make_async_copy(ys_hbm.at[pl.ds(0, ybuf.shape[1])], ybuf.at[k], sem).wait()
    gates = gate_ref[...]
    ys = [_load_row_tiles(ybuf.at[k]) * gates[:, k:k + 1] for k in range(TOP_K)]
    y = (ys[0] + ys[1]) + (ys[2] + ys[3])
    o_ref[...] = _layer_norm(DEEPNORM_ALPHA * _load_row_tiles(x_ref) + y, lnw_ref[...], lnb_ref[...])


def _combine(ys, slots, gates, x2, ln_w, ln_b):
    t = x2.shape[0] // ROW_TILE[0]
    c = SEQ_BLOCK
    return pl.pallas_call(
        _combine_kernel,
        grid=(t // c,),
        in_specs=[
            pl.BlockSpec((1, 1, c * TOP_K), lambda i: (i, 0, 0), memory_space=pltpu.SMEM),
            pl.BlockSpec(memory_space=pl.ANY),
            pl.BlockSpec((c, TOP_K), lambda i: (i, 0)),
            _tiled_spec(c, lambda i: (i, 0)),
            pl.BlockSpec((1, D_MODEL), lambda i: (0, 0)),
            pl.BlockSpec((1, D_MODEL), lambda i: (0, 0)),
        ],
        out_specs=pl.BlockSpec((c, D_MODEL), lambda i: (i, 0)),
        out_shape=jax.ShapeDtypeStruct((t, D_MODEL), F32),
        scratch_shapes=[pltpu.VMEM((TOP_K,) + _tiled(c), F32), pltpu.SemaphoreType.DMA],
        compiler_params=pltpu.CompilerParams(dimension_semantics=("arbitrary",), vmem_limit_bytes=VMEM_LIMIT),
        name="combine",
    )(slots.reshape(t // c, 1, c * TOP_K), ys, gates, x2, ln_w.reshape(1, -1), ln_b.reshape(1, -1))


def _slot_tables(counts, packed, n_blocks, shift):
    counts = counts.reshape(-1)
    nblk = (counts + MOE_BLOCK - 1) // MOE_BLOCK
    ids = jnp.arange(N_EXPERTS)
    bend = jnp.sum(jnp.where(ids[None, :] <= ids[:, None], nblk[None, :], 0), axis=1)
    n_used = bend[N_EXPERTS - 1]
    pend = bend * MOE_BLOCK
    pstart = pend - nblk * MOE_BLOCK
    fill = pstart + counts
    b = jnp.minimum(jnp.arange(n_blocks), n_used - 1)
    block_e = jnp.sum((bend[None, :] <= b[:, None]).astype(jnp.int32), axis=1)
    expert = lax.shift_right_logical(packed, shift)
    rank = packed & ((1 << shift) - 1)
    slots = jnp.sum(jnp.where(expert[..., None] == ids, pstart, 0), axis=-1) + rank
    i32 = lambda a: a.astype(jnp.int32)
    return i32(slots), i32(fill), i32(pend), i32(block_e), i32(n_used.reshape(1))


def kernel(x, mem, w_in, lb_logits, hgrn_norm_w, w_pool, pool_scale, w_out, ln1_w, ln1_b, w_xq, w_xk, w_xv, w_xo,
           ln2_w, ln2_b, w_router, b_router, w1, b1, w2, b2, ln3_w, ln3_b):
    bsz, seq, d = x.shape
    assert bsz == 1 and d == D_MODEL and seq % SEQ_BLOCK == 0 and w_in.shape[0] == 1
    xt = x.reshape(seq, d)
    x1 = _mixer(xt, w_in[0], lb_logits, hgrn_norm_w[0], w_pool[0], pool_scale[0], w_out[0], ln1_w[0], ln1_b[0])
    kmem, vmem = _kv_proj(mem[0], w_xk[0], w_xv[0])
    x2, packed, gates, counts = _xattn_router(x1, kmem, vmem, w_xq[0], w_xo[0], ln2_w[0], ln2_b[0],
                                              w_router[0], b_router[0])
    n_blocks = -(-(seq * TOP_K + N_EXPERTS * (MOE_BLOCK - 1)) // MOE_BLOCK)
    slots, fill, pend, block_e, n_used = _slot_tables(counts, packed, n_blocks, _rank_bits(seq))
    xs = _dispatch(x2, slots, fill, pend, n_used, n_blocks)
    ys = _experts(xs, block_e, n_used, w1[0], b1[0], w2[0], b2[0])
    out = _combine(ys, slots, gates, x2, ln3_w[0], ln3_b[0])
    return out.reshape(bsz, seq, d)
```

```python
import functools

import numpy as np
import jax
import jax.numpy as jnp
from jax import lax
from jax.experimental import pallas as pl
from jax.experimental.pallas import tpu as pltpu

F32 = jnp.float32
BF16 = jnp.bfloat16

D_MODEL = 1024
HGRN_WIDTH = 512
HGRN_HEADS = 4
HEAD_DIM = 128
POOL_WINDOWS = (2, 4, 8, 16)
POOL_GROUP = 128
POOL_CARRY = 16
IN_PROJ_WIDTH = 4 * HGRN_WIDTH + 512
XATTN_HEADS = 4
XATTN_HEAD_DIM = 256
N_EXPERTS = 32
TOP_K = 4
D_EXPERT = 1024
SWIGLU_LIMIT = 7.0
SWIGLU_ALPHA = 1.702
MOE_BLOCK = 256
DEEPNORM_ALPHA = 2.0 ** 0.25
LN_EPS = 1e-5
RMS_EPS = 1e-6

SEQ_BLOCK = 256
N_LEVELS = 8
MXU_LEVELS = 3
ROW_TILE = (8, 128)
VMEM_LIMIT = 48 * 1024 * 1024
EXPERT_VMEM_LIMIT = 56 * 1024 * 1024


def _rank_bits(t):
    return max(int(t - 1).bit_length(), 1)


def _layer_norm(y, w, b):
    mu = jnp.mean(y, axis=-1, keepdims=True)
    yc = y - mu
    var = jnp.mean(yc * yc, axis=-1, keepdims=True)
    return yc * lax.rsqrt(var + LN_EPS) * w + b


def _tiled(n):
    return (n * ROW_TILE[0], ROW_TILE[1])


def _tiled_spec(n, index_map):
    return pl.BlockSpec(_tiled(n), index_map)


def _tile_of(ref, r):
    return ref.at[pl.ds(pl.multiple_of(r * ROW_TILE[0], ROW_TILE[0]), ROW_TILE[0])]


def _store_row_tiles(ref, val):
    s, l = ROW_TILE
    for j in range(s):
        ref[pl.ds(j, val.shape[0], stride=s), :] = val[:, j * l:(j + 1) * l]


def _load_row_tiles(ref):
    s, _ = ROW_TILE
    return jnp.concatenate([ref[pl.ds(j, ref.shape[0] // s, stride=s), :] for j in range(s)], axis=-1)


def _dot(a, b):
    return jnp.dot(a, b, preferred_element_type=F32)


def _dot_nt(a, b):
    return lax.dot_general(a, b, (((1,), (1,)), ((), ())), preferred_element_type=F32)


def _decay_tables():
    c = SEQ_BLOCK
    r = np.arange(c)[:, None]
    u = np.arange(c)[None, :]
    mats = [(u <= r)]
    for l in range(MXU_LEVELS):
        h = 1 << l
        mid = (r // (2 * h)) * (2 * h) + h
        upper = (r >= mid) & (u >= mid) & (u <= r)
        lower = (r < mid) & (u > r) & (u < mid)
        mats.append(upper | lower)
    mall = np.concatenate(mats, axis=0).astype(np.float32)
    x = r ^ u
    lvl = np.where(u < r, np.floor(np.log2(np.maximum(x, 1))).astype(np.int32), -1).astype(np.int32)
    strict_lower = (u < r).astype(np.float32)
    return mall, lvl, strict_lower


def _level_gap(negb, l):
    half = 1 << l
    parts = []
    for base in range(0, negb.shape[0], 2 * half):
        ref_row = negb[base + half - 1:base + half, :]
        parts.append(jnp.abs(negb[base:base + 2 * half, :] - ref_row))
    return jnp.concatenate(parts, axis=0) if len(parts) > 1 else parts[0]


def _mixer_kernel(x_ref, win_ref, lbl_ref, nw_ref, wpool_ref, pscale_ref, wout_ref, lnw_ref, lnb_ref,
                  mall_ref, lvl_ref, o_ref, state_ref, carry_ref, ext_ref, mix_ref):
    i = pl.program_id(0)
    c = SEQ_BLOCK

    @pl.when(i == 0)
    def _():
        state_ref[...] = jnp.zeros_like(state_ref)
        carry_ref[...] = jnp.zeros_like(carry_ref)

    x = x_ref[...]
    proj = _dot(x.astype(BF16), win_ref[...])
    w = HGRN_WIDTH
    q = proj[:, 0:w]
    z = proj[:, w:2 * w]
    v = proj[:, 2 * w:3 * w]
    g = proj[:, 3 * w:4 * w]
    p = proj[:, 4 * w:]

    ll = lbl_ref[...]
    ee = jnp.exp(ll - jnp.max(ll, axis=0, keepdims=True))
    lb = ee[0:1] / jnp.sum(ee, axis=0, keepdims=True)

    f = lb + (1.0 - lb) * (1.0 / (1.0 + jnp.exp(-z)))
    nlf = -jnp.log(f)
    key = (1.0 - lb) * (1.0 / (1.0 + jnp.exp(z)))

    hi = nlf.astype(BF16)
    lo = (nlf - hi.astype(F32)).astype(BF16)
    mall = mall_ref[...]
    cum = _dot(mall, hi) + _dot(mall, lo)
    lvl = lvl_ref[...]

    for h in range(HGRN_HEADS):
        hs = slice(h * HEAD_DIM, (h + 1) * HEAD_DIM)
        qh, kh, vh, gh = q[:, hs], key[:, hs], v[:, hs], g[:, hs]
        vb = vh.astype(BF16)
        negb = cum[0:c, hs]
        scores = jnp.zeros((c, c), F32)
        for l in range(N_LEVELS):
            gap = cum[(l + 1) * c:(l + 2) * c, hs] if l < MXU_LEVELS else _level_gap(negb, l)
            e = jnp.exp(-gap)
            pl_ = _dot_nt((qh * e).astype(BF16), (kh * e).astype(BF16))
            scores = jnp.where(lvl == l, pl_, scores)
        diag = jnp.sum(qh * kh, axis=-1, keepdims=True)
        st = state_ref[h]
        o = (_dot(scores.astype(BF16), vb) + diag * vh
             + _dot_nt((qh * jnp.exp(-negb)).astype(BF16), st.astype(BF16)))
        negb_last = negb[c - 1:c, :]
        kd = (kh * jnp.exp(negb - negb_last)).astype(BF16)
        state_ref[h] = st * jnp.exp(-negb_last) + _dot(vh.T.astype(BF16), kd)
        o = o * lax.rsqrt(jnp.mean(o * o, axis=-1, keepdims=True) + RMS_EPS) * nw_ref[:, hs]
        gate = gh * (1.0 / (1.0 + jnp.exp(-gh)))
        mix_ref[:, hs] = (o * gate).astype(BF16)

    ext_ref[0:POOL_CARRY, :] = carry_ref[...]
    ext_ref[POOL_CARRY:POOL_CARRY + c, :] = p
    carry_ref[...] = p[c - POOL_CARRY:c, :]
    pos = i * c + lax.broadcasted_iota(jnp.int32, (c, 1), 0)
    for gi, win in enumerate(POOL_WINDOWS):
        gs = slice(gi * POOL_GROUP, (gi + 1) * POOL_GROUP)
        pg = p[:, gs]
        tot = pg
        for j in range(1, win):
            tot = tot + ext_ref[POOL_CARRY - j:POOL_CARRY - j + c, gs]
        cnt = jnp.minimum(pos + 1, win).astype(F32)
        pooled = _dot((tot / cnt - pg).astype(BF16), wpool_ref[gi])
        mix_ref[:, w + gi * POOL_GROUP:w + (gi + 1) * POOL_GROUP] = (pooled * pscale_ref[:, gs]).astype(BF16)

    y = DEEPNORM_ALPHA * x + _dot(mix_ref[...], wout_ref[...])
    o_ref[...] = _layer_norm(y, lnw_ref[...], lnb_ref[...])


def _mixer(x, w_in, lb_logits, norm_w, w_pool, pool_scale, w_out, ln_w, ln_b):
    t = x.shape[0]
    c = SEQ_BLOCK
    mall, lvl, _ = _decay_tables()
    full = lambda shape: pl.BlockSpec(shape, lambda i: (0,) * len(shape))
    return pl.pallas_call(
        _mixer_kernel,
        grid=(t // c,),
        in_specs=[
            pl.BlockSpec((c, D_MODEL), lambda i: (i, 0)),
            full((D_MODEL, IN_PROJ_WIDTH)),
            full(lb_logits.shape),
            full((1, HGRN_WIDTH)),
            full((len(POOL_WINDOWS), POOL_GROUP, POOL_GROUP)),
            full((1, 512)),
            full((D_MODEL, D_MODEL)),
            full((1, D_MODEL)),
            full((1, D_MODEL)),
            full(mall.shape),
            full(lvl.shape),
        ],
        out_specs=pl.BlockSpec((c, D_MODEL), lambda i: (i, 0)),
        out_shape=jax.ShapeDtypeStruct((t, D_MODEL), F32),
        scratch_shapes=[
            pltpu.VMEM((HGRN_HEADS, HEAD_DIM, HEAD_DIM), F32),
            pltpu.VMEM((POOL_CARRY, 512), F32),
            pltpu.VMEM((POOL_CARRY + c, 512), F32),
            pltpu.VMEM((c, D_MODEL), BF16),
        ],
        compiler_params=pltpu.CompilerParams(dimension_semantics=("arbitrary",), vmem_limit_bytes=VMEM_LIMIT),
        name="mixer",
    )(x, w_in.astype(BF16), lb_logits, norm_w.reshape(1, -1), w_pool.astype(BF16), pool_scale.reshape(1, -1),
      w_out.astype(BF16), ln_w.reshape(1, -1), ln_b.reshape(1, -1), jnp.asarray(mall, BF16), jnp.asarray(lvl))


def _kv_kernel(mem_ref, wk_ref, wv_ref, k_ref, v_ref):
    m = mem_ref[...].astype(BF16)
    k_ref[...] = _dot(m, wk_ref[...]).astype(BF16)
    v_ref[...] = _dot(m, wv_ref[...]).astype(BF16)


def _kv_proj(mem, wk, wv):
    n = mem.shape[0]
    return pl.pallas_call(
        _kv_kernel,
        out_shape=(jax.ShapeDtypeStruct((n, D_MODEL), BF16), jax.ShapeDtypeStruct((n, D_MODEL), BF16)),
        compiler_params=pltpu.CompilerParams(vmem_limit_bytes=VMEM_LIMIT),
        name="kv_proj",
    )(mem, wk.astype(BF16), wv.astype(BF16))


def _split3(a):
    a0 = a.astype(BF16)
    r1 = a - a0.astype(F32)
    a1 = r1.astype(BF16)
    a2 = (r1 - a1.astype(F32)).astype(BF16)
    return a0, a1, a2


def _xattn_kernel(x_ref, wq_ref, k_ref, v_ref, wo_ref, lnw_ref, lnb_ref, wrt_ref, br_ref, ut_ref,
                  x2_ref, dest_ref, gate_ref, cnt_ref, att_ref, run_ref, *, shift):
    i = pl.program_id(0)
    c = SEQ_BLOCK

    @pl.when(i == 0)
    def _():
        run_ref[...] = jnp.zeros_like(run_ref)

    x = x_ref[...]
    q = _dot(x.astype(BF16), wq_ref[...])
    scale = XATTN_HEAD_DIM ** -0.5
    for h in range(XATTN_HEADS):
        hs = slice(h * XATTN_HEAD_DIM, (h + 1) * XATTN_HEAD_DIM)
        s = _dot_nt(q[:, hs].astype(BF16), k_ref[:, hs]) * scale
        s = s - jnp.max(s, axis=-1, keepdims=True)
        e = jnp.exp(s)
        pr = e / jnp.sum(e, axis=-1, keepdims=True)
        att_ref[:, hs] = _dot(pr.astype(BF16), v_ref[:, hs]).astype(BF16)
    y = DEEPNORM_ALPHA * x + _dot(att_ref[...], wo_ref[...])
    x2 = _layer_norm(y, lnw_ref[...], lnb_ref[...])
    _store_row_tiles(x2_ref, x2)

    n = N_EXPERTS
    a0, a1, a2 = _split3(x2)
    w0, w1, w2 = _split3(wrt_ref[...])
    pa = _dot_nt(jnp.concatenate([w0, w1, w2], axis=0), a0)
    pb = _dot_nt(jnp.concatenate([w0, w1], axis=0), a1)
    pc = _dot_nt(w0, a2)
    logits = ((pc + pa[2 * n:3 * n] + pb[n:2 * n]) + (pb[0:n] + pa[n:2 * n]) + pa[0:n]) + br_ref[...]

    eid = lax.broadcasted_iota(jnp.int32, (n, c), 0)
    work = logits
    sels, vals, idxs = [], [], []
    for k in range(TOP_K):
        m = jnp.max(work, axis=0, keepdims=True)
        idx = jnp.min(jnp.where(work == m, eid, n), axis=0, keepdims=True)
        sel = eid == idx
        sels.append(sel)
        vals.append(m)
        idxs.append(idx)
        work = jnp.where(sel, -jnp.inf, work)
    es = [jnp.exp(vk - vals[0]) for vk in vals]
    denom = es[0] + es[1] + es[2] + es[3]
    gate_ref[0] = jnp.concatenate([ek / denom for ek in es], axis=0)

    onehot = jnp.zeros((n, c), F32)
    for sel in sels:
        onehot = jnp.where(sel, 1.0, onehot)
    before = _dot(onehot.astype(BF16), ut_ref[...]) + run_ref[...]
    ranks = [jnp.sum(jnp.where(sel, before, 0.0), axis=0, keepdims=True).astype(jnp.int32) for sel in sels]
    run_ref[...] = run_ref[...] + jnp.sum(onehot, axis=1, keepdims=True)

    dest_ref[0] = jnp.concatenate([idx * (1 << shift) + rk for idx, rk in zip(idxs, ranks)], axis=0)
    cnt_ref[...] = run_ref[...].astype(jnp.int32)


def _xattn_router(x1, kmem, vmem, wq, wo, ln_w, ln_b, w_r, b_r):
    t = x1.shape[0]
    c = SEQ_BLOCK
    n_mem = kmem.shape[0]
    _, _, strict_lower = _decay_tables()
    full = lambda shape: pl.BlockSpec(shape, lambda i: (0,) * len(shape))
    tok = lambda width: pl.BlockSpec((c, width), lambda i: (i, 0))
    per_k = pl.BlockSpec((1, TOP_K, c), lambda i: (i, 0, 0))
    return pl.pallas_call(
        functools.partial(_xattn_kernel, shift=_rank_bits(t)),
        grid=(t // c,),
        in_specs=[
            tok(D_MODEL),
            full((D_MODEL, D_MODEL)),
            full((n_mem, D_MODEL)),
            full((n_mem, D_MODEL)),
            full((D_MODEL, D_MODEL)),
            full((1, D_MODEL)),
            full((1, D_MODEL)),
            full((N_EXPERTS, D_MODEL)),
            full((N_EXPERTS, 1)),
            full((c, c)),
        ],
        out_specs=[_tiled_spec(c, lambda i: (i, 0)), per_k, per_k, full((N_EXPERTS, 1))],
        out_shape=[
            jax.ShapeDtypeStruct(_tiled(t), F32),
            jax.ShapeDtypeStruct((t // c, TOP_K, c), jnp.int32),
            jax.ShapeDtypeStruct((t // c, TOP_K, c), F32),
            jax.ShapeDtypeStruct((N_EXPERTS, 1), jnp.int32),
        ],
        scratch_shapes=[pltpu.VMEM((c, D_MODEL), BF16), pltpu.VMEM((N_EXPERTS, 1), F32)],
        compiler_params=pltpu.CompilerParams(dimension_semantics=("arbitrary",), vmem_limit_bytes=VMEM_LIMIT),
        name="xattn_router",
    )(x1, wq.astype(BF16), kmem, vmem, wo.astype(BF16), ln_w.reshape(1, -1), ln_b.reshape(1, -1),
      w_r.T, b_r.reshape(-1, 1), jnp.asarray(strict_lower.T, BF16))


TAIL_RUNS = (128, 64, 32, 16, 8, 4, 2, 1)


def _dispatch_kernel(fill_ref, pend_ref, nused_ref, slot_ref, x_ref, xs_hbm, zero_ref, sem, zsem, *, n_blocks):
    i = pl.program_id(0)
    c = SEQ_BLOCK

    @pl.when(i == 0)
    def _():
        zero_ref[...] = jnp.zeros_like(zero_ref)

        sub = ROW_TILE[0]

        def rows(ref, first, n):
            return ref.at[pl.ds(pl.multiple_of(first * sub, sub), n * sub)]

        def tail(e, act):
            row = fill_ref[e]
            pad = pend_ref[e] - row
            for run in TAIL_RUNS:
                hit = (pad & run) != 0

                @pl.when(hit)
                def _():
                    act(pltpu.make_async_copy(rows(zero_ref, 0, run), rows(xs_hbm, row, run), zsem))
                row = row + jnp.where(hit, run, 0)

        def spare(b, act):
            act(pltpu.make_async_copy(zero_ref, rows(xs_hbm, b * MOE_BLOCK, MOE_BLOCK), zsem))

        for act in (lambda cp: cp.start(), lambda cp: cp.wait()):
            lax.fori_loop(0, N_EXPERTS, lambda e, carry: (tail(e, act), carry)[1], 0)
            lax.fori_loop(nused_ref[0], n_blocks, lambda b, carry: (spare(b, act), carry)[1], 0)

    def issue(r, carry):
        for k in range(TOP_K):
            dst = _tile_of(xs_hbm, slot_ref[0, 0, k * c + r])
            pltpu.make_async_copy(_tile_of(x_ref, r), dst, sem).start(priority=k % 2)
        return carry
    lax.fori_loop(0, c, issue, 0)
    for k in range(TOP_K):
        pltpu.make_async_copy(x_ref, xs_hbm.at[pl.ds(0, x_ref.shape[0])], sem).wait()


def _dispatch(x2, slots, fill, pend, n_used, n_blocks):
    t = x2.shape[0] // ROW_TILE[0]
    c = SEQ_BLOCK
    grid_spec = pltpu.PrefetchScalarGridSpec(
        num_scalar_prefetch=3,
        grid=(t // c,),
        in_specs=[
            pl.BlockSpec((1, 1, c * TOP_K), lambda i, *_: (i, 0, 0), memory_space=pltpu.SMEM),
            _tiled_spec(c, lambda i, *_: (i, 0)),
        ],
        out_specs=pl.BlockSpec(memory_space=pl.ANY),
        scratch_shapes=[pltpu.VMEM(_tiled(MOE_BLOCK), F32), pltpu.SemaphoreType.DMA, pltpu.SemaphoreType.DMA],
    )
    return pl.pallas_call(
        functools.partial(_dispatch_kernel, n_blocks=n_blocks),
        grid_spec=grid_spec,
        out_shape=jax.ShapeDtypeStruct(_tiled(n_blocks * MOE_BLOCK), F32),
        compiler_params=pltpu.CompilerParams(dimension_semantics=("arbitrary",), vmem_limit_bytes=VMEM_LIMIT),
        name="dispatch",
    )(fill, pend, n_used, slots.reshape(t // c, 1, c * TOP_K), x2)


def _expert_kernel(be_ref, first_ref, nxt_ref, par_ref, nused_ref, x_ref, w1_hbm, b1_ref, w2_hbm, b2_ref, o_ref,
                   w1f, w2f, w1b, w2b, sem1, sem2):
    b = pl.program_id(0)

    def fetch(e, p):
        return (pltpu.make_async_copy(w1_hbm.at[e], w1f.at[p], sem1.at[p]),
                pltpu.make_async_copy(w2_hbm.at[e], w2f.at[p], sem2.at[p]))

    @pl.when(b == 0)
    def _():
        for cp in fetch(be_ref[0], par_ref[0]):
            cp.start()

    @pl.when(b >= nused_ref[0])
    def _():
        o_ref[...] = jnp.zeros_like(o_ref)

    @pl.when(b < nused_ref[0])
    def _():
        @pl.when(first_ref[b] == 1)
        def _():
            p = par_ref[b]
            for cp in fetch(be_ref[b], p):
                cp.wait()

            @pl.when(nxt_ref[b] >= 0)
            def _():
                for cp in fetch(nxt_ref[b], 1 - p):
                    cp.start()

            w1b[...] = w1f[p].astype(BF16)
            w2b[...] = w2f[p].astype(BF16)

        h = _dot(_load_row_tiles(x_ref).astype(BF16), w1b[...]) + b1_ref[0]
        h_glu = jnp.minimum(h[:, :D_EXPERT], SWIGLU_LIMIT)
        h_lin = jnp.clip(h[:, D_EXPERT:], -SWIGLU_LIMIT, SWIGLU_LIMIT)
        act = h_glu * (1.0 / (1.0 + jnp.exp(-SWIGLU_ALPHA * h_glu))) * (h_lin + 1.0)
        _store_row_tiles(o_ref, _dot(act.astype(BF16), w2b[...]) + b2_ref[0])


def _experts(xs, plan, w1, b1, w2, b2):
    block_e, first, nxt, par, n_used = plan
    n_blocks = block_e.shape[0]
    rows = lambda b, *_: (b, 0)
    bsel = lambda b, be, *_: (be[b], 0, 0)
    grid_spec = pltpu.PrefetchScalarGridSpec(
        num_scalar_prefetch=5,
        grid=(n_blocks,),
        in_specs=[
            _tiled_spec(MOE_BLOCK, rows),
            pl.BlockSpec(memory_space=pl.ANY),
            pl.BlockSpec((1, 1, 2 * D_EXPERT), bsel),
            pl.BlockSpec(memory_space=pl.ANY),
            pl.BlockSpec((1, 1, D_MODEL), bsel),
        ],
        out_specs=_tiled_spec(MOE_BLOCK, rows),
        scratch_shapes=[
            pltpu.VMEM((2, D_MODEL, 2 * D_EXPERT), F32),
            pltpu.VMEM((2, D_EXPERT, D_MODEL), F32),
            pltpu.VMEM((D_MODEL, 2 * D_EXPERT), BF16),
            pltpu.VMEM((D_EXPERT, D_MODEL), BF16),
            pltpu.SemaphoreType.DMA((2,)),
            pltpu.SemaphoreType.DMA((2,)),
        ],
    )
    return pl.pallas_call(
        _expert_kernel,
        grid_spec=grid_spec,
        out_shape=jax.ShapeDtypeStruct(xs.shape, F32),
        compiler_params=pltpu.CompilerParams(dimension_semantics=("arbitrary",), vmem_limit_bytes=EXPERT_VMEM_LIMIT),
        name="experts",
    )(block_e, first, nxt, par, n_used, xs, w1, b1.reshape(N_EXPERTS, 1, -1), w2, b2.reshape(N_EXPERTS, 1, -1))


def _combine_kernel(slot_ref, ys_hbm, gate_ref, x_ref, lnw_ref, lnb_ref, o_ref, ybuf, sem):
    c = SEQ_BLOCK

    def issue(r, carry):
        for k in range(TOP_K):
            src = _tile_of(ys_hbm, slot_ref[0, 0, k * c + r])
            pltpu.make_async_copy(src, _tile_of(ybuf.at[k], r), sem).start(priority=k % 2)
        return carry
    lax.fori_loop(0, c, issue, 0)
    for k in range(TOP_K):
        pltpu.make_async_copy(ys_hbm.at[pl.ds(0, ybuf.shape[1])], ybuf.at[k], sem).wait()
    gates = gate_ref[...]
    ys = [_load_row_tiles(ybuf.at[k]) * gates[:, k:k + 1] for k in range(TOP_K)]
    y = (ys[0] + ys[1]) + (ys[2] + ys[3])
    o_ref[...] = _layer_norm(DEEPNORM_ALPHA * _load_row_tiles(x_ref) + y, lnw_ref[...], lnb_ref[...])


def _combine(ys, slots, gates, x2, ln_w, ln_b):
    t = x2.shape[0] // ROW_TILE[0]
    c = SEQ_BLOCK
    return pl.pallas_call(
        _combine_kernel,
        grid=(t // c,),
        in_specs=[
            pl.BlockSpec((1, 1, c * TOP_K), lambda i: (i, 0, 0), memory_space=pltpu.SMEM),
            pl.BlockSpec(memory_space=pl.ANY),
            pl.BlockSpec((c, TOP_K), lambda i: (i, 0)),
            _tiled_spec(c, lambda i: (i, 0)),
            pl.BlockSpec((1, D_MODEL), lambda i: (0, 0)),
            pl.BlockSpec((1, D_MODEL), lambda i: (0, 0)),
        ],
        out_specs=pl.BlockSpec((c, D_MODEL), lambda i: (i, 0)),
        out_shape=jax.ShapeDtypeStruct((t, D_MODEL), F32),
        scratch_shapes=[pltpu.VMEM((TOP_K,) + _tiled(c), F32), pltpu.SemaphoreType.DMA],
        compiler_params=pltpu.CompilerParams(dimension_semantics=("arbitrary",), vmem_limit_bytes=VMEM_LIMIT),
        name="combine",
    )(slots.reshape(t // c, 1, c * TOP_K), ys, gates, x2, ln_w.reshape(1, -1), ln_b.reshape(1, -1))


def _slot_tables(counts, packed, n_blocks, shift):
    counts = counts.reshape(-1)
    nblk = (counts + MOE_BLOCK - 1) // MOE_BLOCK
    ids = jnp.arange(N_EXPERTS)
    lower = ids[None, :] <= ids[:, None]
    bend = jnp.sum(jnp.where(lower, nblk[None, :], 0), axis=1)
    n_used = bend[N_EXPERTS - 1]
    pend = bend * MOE_BLOCK
    pstart = pend - nblk * MOE_BLOCK
    fill = pstart + counts
    order = jnp.sum(jnp.where(lower, (nblk > 0)[None, :], False), axis=1) - 1
    later = (ids[None, :] > ids[:, None]) & (nblk > 0)[None, :]
    nxt_e = jnp.min(jnp.where(later, ids[None, :], N_EXPERTS), axis=1)
    nxt_e = jnp.where(nxt_e == N_EXPERTS, -1, nxt_e)
    b = jnp.minimum(jnp.arange(n_blocks), n_used - 1)
    hit = bend[None, :] <= b[:, None]
    block_e = jnp.sum(hit.astype(jnp.int32), axis=1)
    of_block = lambda per_expert: jnp.sum(jnp.where(ids[None, :] == block_e[:, None], per_expert[None, :], 0), axis=1)
    first = (b == of_block(bend - nblk)).astype(jnp.int32)
    expert = lax.shift_right_logical(packed, shift)
    rank = packed & ((1 << shift) - 1)
    slots = jnp.sum(jnp.where(expert[..., None] == ids, pstart, 0), axis=-1) + rank
    i32 = lambda a: a.astype(jnp.int32)
    plan = (i32(block_e), first, i32(of_block(nxt_e)), i32(of_block(order & 1)), i32(n_used.reshape(1)))
    return i32(slots), i32(fill), i32(pend), plan


def kernel(x, mem, w_in, lb_logits, hgrn_norm_w, w_pool, pool_scale, w_out, ln1_w, ln1_b, w_xq, w_xk, w_xv, w_xo,
           ln2_w, ln2_b, w_router, b_router, w1, b1, w2, b2, ln3_w, ln3_b):
    bsz, seq, d = x.shape
    assert bsz == 1 and d == D_MODEL and seq % SEQ_BLOCK == 0 and w_in.shape[0] == 1
    xt = x.reshape(seq, d)
    x1 = _mixer(xt, w_in[0], lb_logits, hgrn_norm_w[0], w_pool[0], pool_scale[0], w_out[0], ln1_w[0], ln1_b[0])
    kmem, vmem = _kv_proj(mem[0], w_xk[0], w_xv[0])
    x2, packed, gates, counts = _xattn_router(x1, kmem, vmem, w_xq[0], w_xo[0], ln2_w[0], ln2_b[0],
                                              w_router[0], b_router[0])
    n_blocks = -(-(seq * TOP_K + N_EXPERTS * (MOE_BLOCK - 1)) // MOE_BLOCK)
    slots, fill, pend, plan = _slot_tables(counts, packed, n_blocks, _rank_bits(seq))
    xs = _dispatch(x2, slots, fill, pend, plan[-1], n_blocks)
    ys = _experts(xs, plan, w1[0], b1[0], w2[0], b2[0])
    gates_tk = gates.transpose(0, 2, 1).reshape(seq, TOP_K)
    out = _combine(ys, slots, gates_tk, x2, ln3_w[0], ln3_b[0])
    return out.reshape(bsz, seq, d)
```

```python
import functools

import numpy as np
import jax
import jax.numpy as jnp
from jax import lax
from jax.experimental import pallas as pl
from jax.experimental.pallas import tpu as pltpu

F32 = jnp.float32
BF16 = jnp.bfloat16

D_MODEL = 1024
HGRN_WIDTH = 512
HGRN_HEADS = 4
HEAD_DIM = 128
POOL_WINDOWS = (2, 4, 8, 16)
POOL_GROUP = 128
POOL_CARRY = 16
IN_PROJ_WIDTH = 4 * HGRN_WIDTH + 512
XATTN_HEADS = 4
XATTN_HEAD_DIM = 256
N_EXPERTS = 32
TOP_K = 4
D_EXPERT = 1024
SWIGLU_LIMIT = 7.0
SWIGLU_ALPHA = 1.702
MOE_BLOCK = 256
DEEPNORM_ALPHA = 2.0 ** 0.25
LN_EPS = 1e-5
RMS_EPS = 1e-6

SEQ_BLOCK = 256
N_LEVELS = 8
MXU_LEVELS = 3
ISSUE_UNROLL = 2
ROW_TILE = (8, 128)
VMEM_LIMIT = 48 * 1024 * 1024
EXPERT_VMEM_LIMIT = 56 * 1024 * 1024


def _rank_bits(t):
    return max(int(t - 1).bit_length(), 1)


def _layer_norm(y, w, b):
    mu = jnp.mean(y, axis=-1, keepdims=True)
    yc = y - mu
    var = jnp.mean(yc * yc, axis=-1, keepdims=True)
    return yc * lax.rsqrt(var + LN_EPS) * w + b


def _tiled(n):
    return (n * ROW_TILE[0], ROW_TILE[1])


def _tiled_spec(n, index_map):
    return pl.BlockSpec(_tiled(n), index_map)


def _tile_of(ref, r):
    return ref.at[pl.ds(pl.multiple_of(r * ROW_TILE[0], ROW_TILE[0]), ROW_TILE[0])]


def _store_row_tiles(ref, val):
    s, l = ROW_TILE
    for j in range(s):
        ref[pl.ds(j, val.shape[0], stride=s), :] = val[:, j * l:(j + 1) * l]


def _load_row_tiles(ref):
    s, _ = ROW_TILE
    return jnp.concatenate([ref[pl.ds(j, ref.shape[0] // s, stride=s), :] for j in range(s)], axis=-1)


def _dot(a, b):
    return jnp.dot(a, b, preferred_element_type=F32)


def _dot_nt(a, b):
    return lax.dot_general(a, b, (((1,), (1,)), ((), ())), preferred_element_type=F32)


def _decay_tables():
    c = SEQ_BLOCK
    r = np.arange(c)[:, None]
    u = np.arange(c)[None, :]
    mats = [(u <= r)]
    for l in range(MXU_LEVELS):
        h = 1 << l
        mid = (r // (2 * h)) * (2 * h) + h
        upper = (r >= mid) & (u >= mid) & (u <= r)
        lower = (r < mid) & (u > r) & (u < mid)
        mats.append(upper | lower)
    mall = np.concatenate(mats, axis=0).astype(np.float32)
    x = r ^ u
    lvl = np.where(u < r, np.floor(np.log2(np.maximum(x, 1))).astype(np.int32), -1).astype(np.int32)
    strict_lower = (u < r).astype(np.float32)
    return mall, lvl, strict_lower


def _level_gap(negb, l):
    half = 1 << l
    parts = []
    for base in range(0, negb.shape[0], 2 * half):
        ref_row = negb[base + half - 1:base + half, :]
        parts.append(jnp.abs(negb[base:base + 2 * half, :] - ref_row))
    return jnp.concatenate(parts, axis=0) if len(parts) > 1 else parts[0]


def _mixer_kernel(x_ref, win_ref, lbl_ref, nw_ref, wpool_ref, pscale_ref, wout_ref, lnw_ref, lnb_ref,
                  mall_ref, lvl_ref, o_ref, state_ref, carry_ref, ext_ref, mix_ref):
    i = pl.program_id(0)
    c = SEQ_BLOCK

    @pl.when(i == 0)
    def _():
        state_ref[...] = jnp.zeros_like(state_ref)
        carry_ref[...] = jnp.zeros_like(carry_ref)

    x = x_ref[...]
    proj = _dot(x.astype(BF16), win_ref[...])
    w = HGRN_WIDTH
    q = proj[:, 0:w]
    z = proj[:, w:2 * w]
    v = proj[:, 2 * w:3 * w]
    g = proj[:, 3 * w:4 * w]
    p = proj[:, 4 * w:]

    ll = lbl_ref[...]
    ee = jnp.exp(ll - jnp.max(ll, axis=0, keepdims=True))
    lb = ee[0:1] / jnp.sum(ee, axis=0, keepdims=True)

    f = lb + (1.0 - lb) * (1.0 / (1.0 + jnp.exp(-z)))
    nlf = -jnp.log(f)
    key = (1.0 - lb) * (1.0 / (1.0 + jnp.exp(z)))

    hi = nlf.astype(BF16)
    lo = (nlf - hi.astype(F32)).astype(BF16)
    mall = mall_ref[...]
    cum = _dot(mall, hi) + _dot(mall, lo)
    lvl = lvl_ref[...]

    for h in range(HGRN_HEADS):
        hs = slice(h * HEAD_DIM, (h + 1) * HEAD_DIM)
        qh, kh, vh, gh = q[:, hs], key[:, hs], v[:, hs], g[:, hs]
        vb = vh.astype(BF16)
        negb = cum[0:c, hs]
        scores = jnp.zeros((c, c), F32)
        for l in range(N_LEVELS):
            gap = cum[(l + 1) * c:(l + 2) * c, hs] if l < MXU_LEVELS else _level_gap(negb, l)
            e = jnp.exp(-gap)
            pl_ = _dot_nt((qh * e).astype(BF16), (kh * e).astype(BF16))
            scores = jnp.where(lvl == l, pl_, scores)
        diag = jnp.sum(qh * kh, axis=-1, keepdims=True)
        st = state_ref[h]
        o = (_dot(scores.astype(BF16), vb) + diag * vh
             + _dot_nt((qh * jnp.exp(-negb)).astype(BF16), st.astype(BF16)))
        negb_last = negb[c - 1:c, :]
        kd = (kh * jnp.exp(negb - negb_last)).astype(BF16)
        state_ref[h] = st * jnp.exp(-negb_last) + _dot(vh.T.astype(BF16), kd)
        o = o * lax.rsqrt(jnp.mean(o * o, axis=-1, keepdims=True) + RMS_EPS) * nw_ref[:, hs]
        gate = gh * (1.0 / (1.0 + jnp.exp(-gh)))
        mix_ref[:, hs] = (o * gate).astype(BF16)

    ext_ref[0:POOL_CARRY, :] = carry_ref[...]
    ext_ref[POOL_CARRY:POOL_CARRY + c, :] = p
    carry_ref[...] = p[c - POOL_CARRY:c, :]
    pos = i * c + lax.broadcasted_iota(jnp.int32, (c, 1), 0)
    for gi, win in enumerate(POOL_WINDOWS):
        gs = slice(gi * POOL_GROUP, (gi + 1) * POOL_GROUP)
        pg = p[:, gs]
        tot = pg
        for j in range(1, win):
            tot = tot + ext_ref[POOL_CARRY - j:POOL_CARRY - j + c, gs]
        cnt = jnp.minimum(pos + 1, win).astype(F32)
        pooled = _dot((tot / cnt - pg).astype(BF16), wpool_ref[gi])
        mix_ref[:, w + gi * POOL_GROUP:w + (gi + 1) * POOL_GROUP] = (pooled * pscale_ref[:, gs]).astype(BF16)

    y = DEEPNORM_ALPHA * x + _dot(mix_ref[...], wout_ref[...])
    o_ref[...] = _layer_norm(y, lnw_ref[...], lnb_ref[...])


def _mixer(x, w_in, lb_logits, norm_w, w_pool, pool_scale, w_out, ln_w, ln_b):
    t = x.shape[0]
    c = SEQ_BLOCK
    mall, lvl, _ = _decay_tables()
    full = lambda shape: pl.BlockSpec(shape, lambda i: (0,) * len(shape))
    return pl.pallas_call(
        _mixer_kernel,
        grid=(t // c,),
        in_specs=[
            pl.BlockSpec((c, D_MODEL), lambda i: (i, 0)),
            full((D_MODEL, IN_PROJ_WIDTH)),
            full(lb_logits.shape),
            full((1, HGRN_WIDTH)),
            full((len(POOL_WINDOWS), POOL_GROUP, POOL_GROUP)),
            full((1, 512)),
            full((D_MODEL, D_MODEL)),
            full((1, D_MODEL)),
            full((1, D_MODEL)),
            full(mall.shape),
            full(lvl.shape),
        ],
        out_specs=pl.BlockSpec((c, D_MODEL), lambda i: (i, 0)),
        out_shape=jax.ShapeDtypeStruct((t, D_MODEL), F32),
        scratch_shapes=[
            pltpu.VMEM((HGRN_HEADS, HEAD_DIM, HEAD_DIM), F32),
            pltpu.VMEM((POOL_CARRY, 512), F32),
            pltpu.VMEM((POOL_CARRY + c, 512), F32),
            pltpu.VMEM((c, D_MODEL), BF16),
        ],
        compiler_params=pltpu.CompilerParams(dimension_semantics=("arbitrary",), vmem_limit_bytes=VMEM_LIMIT),
        name="mixer",
    )(x, w_in.astype(BF16), lb_logits, norm_w.reshape(1, -1), w_pool.astype(BF16), pool_scale.reshape(1, -1),
      w_out.astype(BF16), ln_w.reshape(1, -1), ln_b.reshape(1, -1), jnp.asarray(mall, BF16), jnp.asarray(lvl))


def _kv_kernel(mem_ref, wk_ref, wv_ref, k_ref, v_ref):
    m = mem_ref[...].astype(BF16)
    k_ref[...] = _dot(m, wk_ref[...]).astype(BF16)
    v_ref[...] = _dot(m, wv_ref[...]).astype(BF16)


def _kv_proj(mem, wk, wv):
    n = mem.shape[0]
    return pl.pallas_call(
        _kv_kernel,
        out_shape=(jax.ShapeDtypeStruct((n, D_MODEL), BF16), jax.ShapeDtypeStruct((n, D_MODEL), BF16)),
        compiler_params=pltpu.CompilerParams(vmem_limit_bytes=VMEM_LIMIT),
        name="kv_proj",
    )(mem, wk.astype(BF16), wv.astype(BF16))


def _split3(a):
    a0 = a.astype(BF16)
    r1 = a - a0.astype(F32)
    a1 = r1.astype(BF16)
    a2 = (r1 - a1.astype(F32)).astype(BF16)
    return a0, a1, a2


def _xattn_kernel(x_ref, wq_ref, k_ref, v_ref, wo_ref, lnw_ref, lnb_ref, wrt_ref, br_ref, ut_ref,
                  x2_ref, dest_ref, gate_ref, cnt_ref, att_ref, run_ref, *, shift):
    i = pl.program_id(0)
    c = SEQ_BLOCK

    @pl.when(i == 0)
    def _():
        run_ref[...] = jnp.zeros_like(run_ref)

    x = x_ref[...]
    q = _dot(x.astype(BF16), wq_ref[...])
    scale = XATTN_HEAD_DIM ** -0.5
    for h in range(XATTN_HEADS):
        hs = slice(h * XATTN_HEAD_DIM, (h + 1) * XATTN_HEAD_DIM)
        s = _dot_nt(q[:, hs].astype(BF16), k_ref[:, hs]) * scale
        s = s - jnp.max(s, axis=-1, keepdims=True)
        e = jnp.exp(s)
        pr = e / jnp.sum(e, axis=-1, keepdims=True)
        att_ref[:, hs] = _dot(pr.astype(BF16), v_ref[:, hs]).astype(BF16)
    y = DEEPNORM_ALPHA * x + _dot(att_ref[...], wo_ref[...])
    x2 = _layer_norm(y, lnw_ref[...], lnb_ref[...])
    _store_row_tiles(x2_ref, x2)

    n = N_EXPERTS
    a0, a1, a2 = _split3(x2)
    w0, w1, w2 = _split3(wrt_ref[...])
    pa = _dot_nt(jnp.concatenate([w0, w1, w2], axis=0), a0)
    pb = _dot_nt(jnp.concatenate([w0, w1], axis=0), a1)
    pc = _dot_nt(w0, a2)
    logits = ((pc + pa[2 * n:3 * n] + pb[n:2 * n]) + (pb[0:n] + pa[n:2 * n]) + pa[0:n]) + br_ref[...]

    eid = lax.broadcasted_iota(jnp.int32, (n, c), 0)
    work = logits
    sels, vals, idxs = [], [], []
    for k in range(TOP_K):
        m = jnp.max(work, axis=0, keepdims=True)
        idx = jnp.min(jnp.where(work == m, eid, n), axis=0, keepdims=True)
        sel = eid == idx
        sels.append(sel)
        vals.append(m)
        idxs.append(idx)
        work = jnp.where(sel, -jnp.inf, work)
    es = [jnp.exp(vk - vals[0]) for vk in vals]
    denom = es[0] + es[1] + es[2] + es[3]
    gate_ref[0] = jnp.concatenate([ek / denom for ek in es], axis=0)

    onehot = jnp.zeros((n, c), F32)
    for sel in sels:
        onehot = jnp.where(sel, 1.0, onehot)
    before = _dot(onehot.astype(BF16), ut_ref[...]) + run_ref[...]
    ranks = [jnp.sum(jnp.where(sel, before, 0.0), axis=0, keepdims=True).astype(jnp.int32) for sel in sels]
    run_ref[...] = run_ref[...] + jnp.sum(onehot, axis=1, keepdims=True)

    dest_ref[0] = jnp.concatenate([idx * (1 << shift) + rk for idx, rk in zip(idxs, ranks)], axis=0)
    cnt_ref[...] = run_ref[...].astype(jnp.int32)


def _xattn_router(x1, kmem, vmem, wq, wo, ln_w, ln_b, w_r, b_r):
    t = x1.shape[0]
    c = SEQ_BLOCK
    n_mem = kmem.shape[0]
    _, _, strict_lower = _decay_tables()
    full = lambda shape: pl.BlockSpec(shape, lambda i: (0,) * len(shape))
    tok = lambda width: pl.BlockSpec((c, width), lambda i: (i, 0))
    per_k = pl.BlockSpec((1, TOP_K, c), lambda i: (i, 0, 0))
    return pl.pallas_call(
        functools.partial(_xattn_kernel, shift=_rank_bits(t)),
        grid=(t // c,),
        in_specs=[
            tok(D_MODEL),
            full((D_MODEL, D_MODEL)),
            full((n_mem, D_MODEL)),
            full((n_mem, D_MODEL)),
            full((D_MODEL, D_MODEL)),
            full((1, D_MODEL)),
            full((1, D_MODEL)),
            full((N_EXPERTS, D_MODEL)),
            full((N_EXPERTS, 1)),
            full((c, c)),
        ],
        out_specs=[_tiled_spec(c, lambda i: (i, 0)), per_k, per_k, full((N_EXPERTS, 1))],
        out_shape=[
            jax.ShapeDtypeStruct(_tiled(t), F32),
            jax.ShapeDtypeStruct((t // c, TOP_K, c), jnp.int32),
            jax.ShapeDtypeStruct((t // c, TOP_K, c), F32),
            jax.ShapeDtypeStruct((N_EXPERTS, 1), jnp.int32),
        ],
        scratch_shapes=[pltpu.VMEM((c, D_MODEL), BF16), pltpu.VMEM((N_EXPERTS, 1), F32)],
        compiler_params=pltpu.CompilerParams(dimension_semantics=("arbitrary",), vmem_limit_bytes=VMEM_LIMIT),
        name="xattn_router",
    )(x1, wq.astype(BF16), kmem, vmem, wo.astype(BF16), ln_w.reshape(1, -1), ln_b.reshape(1, -1),
      w_r.T, b_r.reshape(-1, 1), jnp.asarray(strict_lower.T, BF16))


TAIL_RUNS = (128, 64, 32, 16, 8, 4, 2, 1)


def _dispatch_kernel(fill_ref, pend_ref, nused_ref, slot_ref, x_ref, xs_hbm, zero_ref, sem, zsem, *, n_blocks):
    i = pl.program_id(0)
    c = SEQ_BLOCK

    @pl.when(i == 0)
    def _():
        zero_ref[...] = jnp.zeros_like(zero_ref)

        sub = ROW_TILE[0]

        def rows(ref, first, n):
            return ref.at[pl.ds(pl.multiple_of(first * sub, sub), n * sub)]

        def tail(e, act):
            row = fill_ref[e]
            pad = pend_ref[e] - row
            for run in TAIL_RUNS:
                hit = (pad & run) != 0

                @pl.when(hit)
                def _():
                    act(pltpu.make_async_copy(rows(zero_ref, 0, run), rows(xs_hbm, row, run), zsem))
                row = row + jnp.where(hit, run, 0)

        def spare(b, act):
            act(pltpu.make_async_copy(zero_ref, rows(xs_hbm, b * MOE_BLOCK, MOE_BLOCK), zsem))

        for act in (lambda cp: cp.start(), lambda cp: cp.wait()):
            lax.fori_loop(0, N_EXPERTS, lambda e, carry: (tail(e, act), carry)[1], 0)
            lax.fori_loop(nused_ref[0], n_blocks, lambda b, carry: (spare(b, act), carry)[1], 0)

    def issue(j, carry):
        for u in range(ISSUE_UNROLL):
            r = j * ISSUE_UNROLL + u
            for k in range(TOP_K):
                dst = _tile_of(xs_hbm, slot_ref[0, 0, k * c + r])
                pltpu.make_async_copy(_tile_of(x_ref, r), dst, sem).start(priority=k % 2)
        return carry
    lax.fori_loop(0, c // ISSUE_UNROLL, issue, 0)
    for k in range(TOP_K):
        pltpu.make_async_copy(x_ref, xs_hbm.at[pl.ds(0, x_ref.shape[0])], sem).wait()


def _dispatch(x2, slots, fill, pend, n_used, n_blocks):
    t = x2.shape[0] // ROW_TILE[0]
    c = SEQ_BLOCK
    grid_spec = pltpu.PrefetchScalarGridSpec(
        num_scalar_prefetch=3,
        grid=(t // c,),
        in_specs=[
            pl.BlockSpec((1, 1, c * TOP_K), lambda i, *_: (i, 0, 0), memory_space=pltpu.SMEM),
            _tiled_spec(c, lambda i, *_: (i, 0)),
        ],
        out_specs=pl.BlockSpec(memory_space=pl.ANY),
        scratch_shapes=[pltpu.VMEM(_tiled(MOE_BLOCK), F32), pltpu.SemaphoreType.DMA, pltpu.SemaphoreType.DMA],
    )
    return pl.pallas_call(
        functools.partial(_dispatch_kernel, n_blocks=n_blocks),
        grid_spec=grid_spec,
        out_shape=jax.ShapeDtypeStruct(_tiled(n_blocks * MOE_BLOCK), F32),
        compiler_params=pltpu.CompilerParams(dimension_semantics=("arbitrary",), vmem_limit_bytes=VMEM_LIMIT),
        name="dispatch",
    )(fill, pend, n_used, slots.reshape(t // c, 1, c * TOP_K), x2)


def _expert_kernel(be_ref, first_ref, nxt_ref, par_ref, nused_ref, x_ref, w1_hbm, b1_ref, w2_hbm, b2_ref, o_ref,
                   w1f, w2f, w1b, w2b, sem1, sem2):
    b = pl.program_id(0)

    def fetch(e, p):
        return (pltpu.make_async_copy(w1_hbm.at[e], w1f.at[p], sem1.at[p]),
                pltpu.make_async_copy(w2_hbm.at[e], w2f.at[p], sem2.at[p]))

    @pl.when(b == 0)
    def _():
        for cp in fetch(be_ref[0], par_ref[0]):
            cp.start()

    @pl.when(b >= nused_ref[0])
    def _():
        o_ref[...] = jnp.zeros_like(o_ref)

    @pl.when(b < nused_ref[0])
    def _():
        @pl.when(first_ref[b] == 1)
        def _():
            p = par_ref[b]
            for cp in fetch(be_ref[b], p):
                cp.wait()

            @pl.when(nxt_ref[b] >= 0)
            def _():
                for cp in fetch(nxt_ref[b], 1 - p):
                    cp.start()

            w1b[...] = w1f[p].astype(BF16)
            w2b[...] = w2f[p].astype(BF16)

        h = _dot(_load_row_tiles(x_ref).astype(BF16), w1b[...]) + b1_ref[0]
        h_glu = jnp.minimum(h[:, :D_EXPERT], SWIGLU_LIMIT)
        h_lin = jnp.clip(h[:, D_EXPERT:], -SWIGLU_LIMIT, SWIGLU_LIMIT)
        act = h_glu * (1.0 / (1.0 + jnp.exp(-SWIGLU_ALPHA * h_glu))) * (h_lin + 1.0)
        _store_row_tiles(o_ref, _dot(act.astype(BF16), w2b[...]) + b2_ref[0])


def _experts(xs, plan, w1, b1, w2, b2):
    block_e, first, nxt, par, n_used = plan
    n_blocks = block_e.shape[0]
    rows = lambda b, *_: (b, 0)
    bsel = lambda b, be, *_: (be[b], 0, 0)
    grid_spec = pltpu.PrefetchScalarGridSpec(
        num_scalar_prefetch=5,
        grid=(n_blocks,),
        in_specs=[
            _tiled_spec(MOE_BLOCK, rows),
            pl.BlockSpec(memory_space=pl.ANY),
            pl.BlockSpec((1, 1, 2 * D_EXPERT), bsel),
            pl.BlockSpec(memory_space=pl.ANY),
            pl.BlockSpec((1, 1, D_MODEL), bsel),
        ],
        out_specs=_tiled_spec(MOE_BLOCK, rows),
        scratch_shapes=[
            pltpu.VMEM((2, D_MODEL, 2 * D_EXPERT), F32),
            pltpu.VMEM((2, D_EXPERT, D_MODEL), F32),
            pltpu.VMEM((D_MODEL, 2 * D_EXPERT), BF16),
            pltpu.VMEM((D_EXPERT, D_MODEL), BF16),
            pltpu.SemaphoreType.DMA((2,)),
            pltpu.SemaphoreType.DMA((2,)),
        ],
    )
    return pl.pallas_call(
        _expert_kernel,
        grid_spec=grid_spec,
        out_shape=jax.ShapeDtypeStruct(xs.shape, F32),
        compiler_params=pltpu.CompilerParams(dimension_semantics=("arbitrary",), vmem_limit_bytes=EXPERT_VMEM_LIMIT),
        name="experts",
    )(block_e, first, nxt, par, n_used, xs, w1, b1.reshape(N_EXPERTS, 1, -1), w2, b2.reshape(N_EXPERTS, 1, -1))


def _combine_kernel(slot_ref, next_slot_ref, ys_hbm, gate_ref, x_ref, lnw_ref, lnb_ref, o_ref, ybuf, sem):
    i = pl.program_id(0)
    c = SEQ_BLOCK

    def gather(slots, p):
        def issue(j, carry):
            for u in range(ISSUE_UNROLL):
                r = j * ISSUE_UNROLL + u
                for k in range(TOP_K):
                    src = _tile_of(ys_hbm, slots[0, 0, k * c + r])
                    pltpu.make_async_copy(src, _tile_of(ybuf.at[p, k], r), sem.at[p]).start(priority=k % 2)
            return carry
        lax.fori_loop(0, c // ISSUE_UNROLL, issue, 0)

    @pl.when(i == 0)
    def _():
        gather(slot_ref, 0)

    @pl.when(i + 1 < pl.num_programs(0))
    def _():
        gather(next_slot_ref, (i + 1) % 2)

    p = i % 2
    for k in range(TOP_K):
        pltpu.make_async_copy(ys_hbm.at[pl.ds(0, ybuf.shape[2])], ybuf.at[p, k], sem.at[p]).wait()
    gates = gate_ref[...]
    ys = [_load_row_tiles(ybuf.at[p, k]) * gates[:, k:k + 1] for k in range(TOP_K)]
    y = (ys[0] + ys[1]) + (ys[2] + ys[3])
    o_ref[...] = _layer_norm(DEEPNORM_ALPHA * _load_row_tiles(x_ref) + y, lnw_ref[...], lnb_ref[...])


def _combine(ys, slots, gates, x2, ln_w, ln_b):
    t = x2.shape[0] // ROW_TILE[0]
    c = SEQ_BLOCK
    n = t // c
    slot_table = slots.reshape(n, 1, c * TOP_K)
    return pl.pallas_call(
        _combine_kernel,
        grid=(n,),
        in_specs=[
            pl.BlockSpec((1, 1, c * TOP_K), lambda i: (i, 0, 0), memory_space=pltpu.SMEM),
            pl.BlockSpec((1, 1, c * TOP_K), lambda i: (jnp.minimum(i + 1, n - 1), 0, 0), memory_space=pltpu.SMEM),
            pl.BlockSpec(memory_space=pl.ANY),
            pl.BlockSpec((c, TOP_K), lambda i: (i, 0)),
            _tiled_spec(c, lambda i: (i, 0)),
            pl.BlockSpec((1, D_MODEL), lambda i: (0, 0)),
            pl.BlockSpec((1, D_MODEL), lambda i: (0, 0)),
        ],
        out_specs=pl.BlockSpec((c, D_MODEL), lambda i: (i, 0)),
        out_shape=jax.ShapeDtypeStruct((t, D_MODEL), F32),
        scratch_shapes=[pltpu.VMEM((2, TOP_K) + _tiled(c), F32), pltpu.SemaphoreType.DMA((2,))],
        compiler_params=pltpu.CompilerParams(dimension_semantics=("arbitrary",), vmem_limit_bytes=VMEM_LIMIT),
        name="combine",
    )(slot_table, slot_table, ys, gates, x2, ln_w.reshape(1, -1), ln_b.reshape(1, -1))


def _slot_tables(counts, packed, n_blocks, shift):
    counts = counts.reshape(-1)
    nblk = (counts + MOE_BLOCK - 1) // MOE_BLOCK
    ids = jnp.arange(N_EXPERTS)
    lower = ids[None, :] <= ids[:, None]
    bend = jnp.sum(jnp.where(lower, nblk[None, :], 0), axis=1)
    n_used = bend[N_EXPERTS - 1]
    pend = bend * MOE_BLOCK
    pstart = pend - nblk * MOE_BLOCK
    fill = pstart + counts
    order = jnp.sum(jnp.where(lower, (nblk > 0)[None, :], False), axis=1) - 1
    later = (ids[None, :] > ids[:, None]) & (nblk > 0)[None, :]
    nxt_e = jnp.min(jnp.where(later, ids[None, :], N_EXPERTS), axis=1)
    nxt_e = jnp.where(nxt_e == N_EXPERTS, -1, nxt_e)
    b = jnp.minimum(jnp.arange(n_blocks), n_used - 1)
    hit = bend[None, :] <= b[:, None]
    block_e = jnp.sum(hit.astype(jnp.int32), axis=1)
    of_block = lambda per_expert: jnp.sum(jnp.where(ids[None, :] == block_e[:, None], per_expert[None, :], 0), axis=1)
    first = (b == of_block(bend - nblk)).astype(jnp.int32)
    expert = lax.shift_right_logical(packed, shift)
    rank = packed & ((1 << shift) - 1)
    slots = jnp.sum(jnp.where(expert[..., None] == ids, pstart, 0), axis=-1) + rank
    i32 = lambda a: a.astype(jnp.int32)
    plan = (i32(block_e), first, i32(of_block(nxt_e)), i32(of_block(order & 1)), i32(n_used.reshape(1)))
    return i32(slots), i32(fill), i32(pend), plan


def kernel(x, mem, w_in, lb_logits, hgrn_norm_w, w_pool, pool_scale, w_out, ln1_w, ln1_b, w_xq, w_xk, w_xv, w_xo,
           ln2_w, ln2_b, w_router, b_router, w1, b1, w2, b2, ln3_w, ln3_b):
    bsz, seq, d = x.shape
    assert bsz == 1 and d == D_MODEL and seq % SEQ_BLOCK == 0 and w_in.shape[0] == 1
    xt = x.reshape(seq, d)
    x1 = _mixer(xt, w_in[0], lb_logits, hgrn_norm_w[0], w_pool[0], pool_scale[0], w_out[0], ln1_w[0], ln1_b[0])
    kmem, vmem = _kv_proj(mem[0], w_xk[0], w_xv[0])
    x2, packed, gates, counts = _xattn_router(x1, kmem, vmem, w_xq[0], w_xo[0], ln2_w[0], ln2_b[0],
                                              w_router[0], b_router[0])
    n_blocks = -(-(seq * TOP_K + N_EXPERTS * (MOE_BLOCK - 1)) // MOE_BLOCK)
    slots, fill, pend, plan = _slot_tables(counts, packed, n_blocks, _rank_bits(seq))
    xs = _dispatch(x2, slots, fill, pend, plan[-1], n_blocks)
    ys = _experts(xs, plan, w1[0], b1[0], w2[0], b2[0])
    gates_tk = gates.transpose(0, 2, 1).reshape(seq, TOP_K)
    out = _combine(ys, slots, gates_tk, x2, ln3_w[0], ln3_b[0])
    return out.reshape(bsz, seq, d)
```

```python
import functools

import numpy as np
import jax
import jax.numpy as jnp
from jax import lax
from jax.experimental import pallas as pl
from jax.experimental.pallas import tpu as pltpu

F32 = jnp.float32
BF16 = jnp.bfloat16

D_MODEL = 1024
HGRN_WIDTH = 512
HGRN_HEADS = 4
HEAD_DIM = 128
POOL_WINDOWS = (2, 4, 8, 16)
POOL_GROUP = 128
POOL_CARRY = 16
IN_PROJ_WIDTH = 4 * HGRN_WIDTH + 512
XATTN_HEADS = 4
XATTN_HEAD_DIM = 256
N_EXPERTS = 32
TOP_K = 4
D_EXPERT = 1024
SWIGLU_LIMIT = 7.0
SWIGLU_ALPHA = 1.702
MOE_BLOCK = 256
DEEPNORM_ALPHA = 2.0 ** 0.25
LN_EPS = 1e-5
RMS_EPS = 1e-6

SEQ_BLOCK = 256
XATTN_SUB = 1
DISPATCH_BLOCK = 1024
N_LEVELS = 8
MXU_LEVELS = 3
ISSUE_UNROLL = 2
ROW_TILE = (8, 128)
VMEM_LIMIT = 48 * 1024 * 1024
EXPERT_VMEM_LIMIT = 56 * 1024 * 1024


def _rank_bits(t):
    return max(int(t - 1).bit_length(), 1)


def _layer_norm(y, w, b):
    mu = jnp.mean(y, axis=-1, keepdims=True)
    yc = y - mu
    var = jnp.mean(yc * yc, axis=-1, keepdims=True)
    return yc * lax.rsqrt(var + LN_EPS) * w + b


def _tiled(n):
    return (n * ROW_TILE[0], ROW_TILE[1])


def _tiled_spec(n, index_map):
    return pl.BlockSpec(_tiled(n), index_map)


def _tile_of(ref, r):
    return ref.at[pl.ds(pl.multiple_of(r * ROW_TILE[0], ROW_TILE[0]), ROW_TILE[0])]


def _store_row_tiles(ref, val):
    s, l = ROW_TILE
    for j in range(s):
        ref[pl.ds(j, val.shape[0], stride=s), :] = val[:, j * l:(j + 1) * l]


def _load_row_tiles(ref):
    s, _ = ROW_TILE
    return jnp.concatenate([ref[pl.ds(j, ref.shape[0] // s, stride=s), :] for j in range(s)], axis=-1)


def _dot(a, b):
    return jnp.dot(a, b, preferred_element_type=F32)


def _dot_nt(a, b):
    return lax.dot_general(a, b, (((1,), (1,)), ((), ())), preferred_element_type=F32)


def _decay_tables():
    c = SEQ_BLOCK
    r = np.arange(c)[:, None]
    u = np.arange(c)[None, :]
    mats = [(u <= r)]
    for l in range(MXU_LEVELS):
        h = 1 << l
        mid = (r // (2 * h)) * (2 * h) + h
        upper = (r >= mid) & (u >= mid) & (u <= r)
        lower = (r < mid) & (u > r) & (u < mid)
        mats.append(upper | lower)
    mall = np.concatenate(mats, axis=0).astype(np.float32)
    x = r ^ u
    lvl = np.where(u < r, np.floor(np.log2(np.maximum(x, 1))).astype(np.int32), -1).astype(np.int32)
    strict_lower = (u < r).astype(np.float32)
    return mall, lvl, strict_lower


def _level_gap(negb, l):
    half = 1 << l
    parts = []
    for base in range(0, negb.shape[0], 2 * half):
        ref_row = negb[base + half - 1:base + half, :]
        parts.append(jnp.abs(negb[base:base + 2 * half, :] - ref_row))
    return jnp.concatenate(parts, axis=0) if len(parts) > 1 else parts[0]


def _mixer_kernel(x_ref, win_ref, lbl_ref, nw_ref, wpool_ref, pscale_ref, wout_ref, lnw_ref, lnb_ref,
                  mall_ref, lvl_ref, o_ref, state_ref, carry_ref, ext_ref, mix_ref):
    i = pl.program_id(0)
    c = SEQ_BLOCK

    @pl.when(i == 0)
    def _():
        state_ref[...] = jnp.zeros_like(state_ref)
        carry_ref[...] = jnp.zeros_like(carry_ref)

    x = x_ref[...]
    proj = _dot(x.astype(BF16), win_ref[...])
    w = HGRN_WIDTH
    q = proj[:, 0:w]
    z = proj[:, w:2 * w]
    v = proj[:, 2 * w:3 * w]
    g = proj[:, 3 * w:4 * w]
    p = proj[:, 4 * w:]

    ll = lbl_ref[...]
    ee = jnp.exp(ll - jnp.max(ll, axis=0, keepdims=True))
    lb = ee[0:1] / jnp.sum(ee, axis=0, keepdims=True)

    f = lb + (1.0 - lb) * (1.0 / (1.0 + jnp.exp(-z)))
    nlf = -jnp.log(f)
    key = (1.0 - lb) * (1.0 / (1.0 + jnp.exp(z)))

    hi = nlf.astype(BF16)
    lo = (nlf - hi.astype(F32)).astype(BF16)
    mall = mall_ref[...]
    cum = _dot(mall, hi) + _dot(mall, lo)
    lvl = lvl_ref[...]

    for h in range(HGRN_HEADS):
        hs = slice(h * HEAD_DIM, (h + 1) * HEAD_DIM)
        qh, kh, vh, gh = q[:, hs], key[:, hs], v[:, hs], g[:, hs]
        vb = vh.astype(BF16)
        negb = cum[0:c, hs]
        scores = jnp.zeros((c, c), F32)
        for l in range(N_LEVELS):
            gap = cum[(l + 1) * c:(l + 2) * c, hs] if l < MXU_LEVELS else _level_gap(negb, l)
            e = jnp.exp(-gap)
            pl_ = _dot_nt((qh * e).astype(BF16), (kh * e).astype(BF16))
            scores = jnp.where(lvl == l, pl_, scores)
        diag = jnp.sum(qh * kh, axis=-1, keepdims=True)
        st = state_ref[h]
        o = (_dot(scores.astype(BF16), vb) + diag * vh
             + _dot_nt((qh * jnp.exp(-negb)).astype(BF16), st.astype(BF16)))
        negb_last = negb[c - 1:c, :]
        kd = (kh * jnp.exp(negb - negb_last)).astype(BF16)
        state_ref[h] = st * jnp.exp(-negb_last) + _dot(vh.T.astype(BF16), kd)
        o = o * lax.rsqrt(jnp.mean(o * o, axis=-1, keepdims=True) + RMS_EPS) * nw_ref[:, hs]
        gate = gh * (1.0 / (1.0 + jnp.exp(-gh)))
        mix_ref[:, hs] = (o * gate).astype(BF16)

    ext_ref[0:POOL_CARRY, :] = carry_ref[...]
    ext_ref[POOL_CARRY:POOL_CARRY + c, :] = p
    carry_ref[...] = p[c - POOL_CARRY:c, :]
    pos = i * c + lax.broadcasted_iota(jnp.int32, (c, 1), 0)
    for gi, win in enumerate(POOL_WINDOWS):
        gs = slice(gi * POOL_GROUP, (gi + 1) * POOL_GROUP)
        pg = p[:, gs]
        tot = pg
        for j in range(1, win):
            tot = tot + ext_ref[POOL_CARRY - j:POOL_CARRY - j + c, gs]
        cnt = jnp.minimum(pos + 1, win).astype(F32)
        pooled = _dot((tot / cnt - pg).astype(BF16), wpool_ref[gi])
        mix_ref[:, w + gi * POOL_GROUP:w + (gi + 1) * POOL_GROUP] = (pooled * pscale_ref[:, gs]).astype(BF16)

    y = DEEPNORM_ALPHA * x + _dot(mix_ref[...], wout_ref[...])
    o_ref[...] = _layer_norm(y, lnw_ref[...], lnb_ref[...])


def _mixer(x, w_in, lb_logits, norm_w, w_pool, pool_scale, w_out, ln_w, ln_b):
    t = x.shape[0]
    c = SEQ_BLOCK
    mall, lvl, _ = _decay_tables()
    full = lambda shape: pl.BlockSpec(shape, lambda i: (0,) * len(shape))
    return pl.pallas_call(
        _mixer_kernel,
        grid=(t // c,),
        in_specs=[
            pl.BlockSpec((c, D_MODEL), lambda i: (i, 0)),
            full((D_MODEL, IN_PROJ_WIDTH)),
            full(lb_logits.shape),
            full((1, HGRN_WIDTH)),
            full((len(POOL_WINDOWS), POOL_GROUP, POOL_GROUP)),
            full((1, 512)),
            full((D_MODEL, D_MODEL)),
            full((1, D_MODEL)),
            full((1, D_MODEL)),
            full(mall.shape),
            full(lvl.shape),
        ],
        out_specs=pl.BlockSpec((c, D_MODEL), lambda i: (i, 0)),
        out_shape=jax.ShapeDtypeStruct((t, D_MODEL), F32),
        scratch_shapes=[
            pltpu.VMEM((HGRN_HEADS, HEAD_DIM, HEAD_DIM), F32),
            pltpu.VMEM((POOL_CARRY, 512), F32),
            pltpu.VMEM((POOL_CARRY + c, 512), F32),
            pltpu.VMEM((c, D_MODEL), BF16),
        ],
        compiler_params=pltpu.CompilerParams(dimension_semantics=("arbitrary",), vmem_limit_bytes=VMEM_LIMIT),
        name="mixer",
    )(x, w_in.astype(BF16), lb_logits, norm_w.reshape(1, -1), w_pool.astype(BF16), pool_scale.reshape(1, -1),
      w_out.astype(BF16), ln_w.reshape(1, -1), ln_b.reshape(1, -1), jnp.asarray(mall, BF16), jnp.asarray(lvl))


def _kv_kernel(mem_ref, wk_ref, wv_ref, k_ref, v_ref):
    m = mem_ref[...].astype(BF16)
    k_ref[...] = _dot(m, wk_ref[...]).astype(BF16)
    v_ref[...] = _dot(m, wv_ref[...]).astype(BF16)


def _kv_proj(mem, wk, wv):
    n = mem.shape[0]
    return pl.pallas_call(
        _kv_kernel,
        out_shape=(jax.ShapeDtypeStruct((n, D_MODEL), BF16), jax.ShapeDtypeStruct((n, D_MODEL), BF16)),
        compiler_params=pltpu.CompilerParams(vmem_limit_bytes=VMEM_LIMIT),
        name="kv_proj",
    )(mem, wk.astype(BF16), wv.astype(BF16))


def _split3(a):
    a0 = a.astype(BF16)
    r1 = a - a0.astype(F32)
    a1 = r1.astype(BF16)
    a2 = (r1 - a1.astype(F32)).astype(BF16)
    return a0, a1, a2


def _xattn_kernel(x_ref, wq_ref, k_ref, v_ref, wo_ref, lnw_ref, lnb_ref, wrt_ref, br_ref, ut_ref,
                  x2_ref, dest_ref, gate_ref, cnt_ref, att_ref, run_ref, *, shift):
    i = pl.program_id(0)
    c = SEQ_BLOCK
    n = N_EXPERTS

    @pl.when(i == 0)
    def _():
        run_ref[...] = jnp.zeros_like(run_ref)

    w0, w1, w2 = _split3(wrt_ref[...])
    eid = lax.broadcasted_iota(jnp.int32, (n, c), 0)
    scale = XATTN_HEAD_DIM ** -0.5
    run = run_ref[...]

    for sb in range(x_ref.shape[0] // c):
        rows = slice(sb * c, (sb + 1) * c)
        x = x_ref[rows, :]
        q = _dot(x.astype(BF16), wq_ref[...])
        for h in range(XATTN_HEADS):
            hs = slice(h * XATTN_HEAD_DIM, (h + 1) * XATTN_HEAD_DIM)
            s = _dot_nt(q[:, hs].astype(BF16), k_ref[:, hs]) * scale
            s = s - jnp.max(s, axis=-1, keepdims=True)
            e = jnp.exp(s)
            pr = e / jnp.sum(e, axis=-1, keepdims=True)
            att_ref[rows, hs] = _dot(pr.astype(BF16), v_ref[:, hs]).astype(BF16)
        y = DEEPNORM_ALPHA * x + _dot(att_ref[rows, :], wo_ref[...])
        x2 = _layer_norm(y, lnw_ref[...], lnb_ref[...])
        _store_row_tiles(x2_ref.at[pl.ds(sb * c * ROW_TILE[0], c * ROW_TILE[0])], x2)

        a0, a1, a2 = _split3(x2)
        pa = _dot_nt(jnp.concatenate([w0, w1, w2], axis=0), a0)
        pb = _dot_nt(jnp.concatenate([w0, w1], axis=0), a1)
        pc = _dot_nt(w0, a2)
        logits = ((pc + pa[2 * n:3 * n] + pb[n:2 * n]) + (pb[0:n] + pa[n:2 * n]) + pa[0:n]) + br_ref[...]

        work = logits
        sels, vals, idxs = [], [], []
        for k in range(TOP_K):
            m = jnp.max(work, axis=0, keepdims=True)
            idx = jnp.min(jnp.where(work == m, eid, n), axis=0, keepdims=True)
            sel = eid == idx
            sels.append(sel)
            vals.append(m)
            idxs.append(idx)
            work = jnp.where(sel, -jnp.inf, work)
        es = [jnp.exp(vk - vals[0]) for vk in vals]
        denom = es[0] + es[1] + es[2] + es[3]
        gate_ref[sb] = jnp.concatenate([ek / denom for ek in es], axis=0)

        onehot = jnp.zeros((n, c), F32)
        for sel in sels:
            onehot = jnp.where(sel, 1.0, onehot)
        before = _dot(onehot.astype(BF16), ut_ref[...]) + run
        ranks = [jnp.sum(jnp.where(sel, before, 0.0), axis=0, keepdims=True).astype(jnp.int32) for sel in sels]
        run = run + jnp.sum(onehot, axis=1, keepdims=True)

        dest_ref[sb] = jnp.concatenate([idx * (1 << shift) + rk for idx, rk in zip(idxs, ranks)], axis=0)

    run_ref[...] = run
    cnt_ref[...] = run.astype(jnp.int32)


def _xattn_router(x1, kmem, vmem, wq, wo, ln_w, ln_b, w_r, b_r):
    t = x1.shape[0]
    c = SEQ_BLOCK
    n_mem = kmem.shape[0]
    _, _, strict_lower = _decay_tables()
    full = lambda shape: pl.BlockSpec(shape, lambda i: (0,) * len(shape))
    sub = XATTN_SUB if t % (XATTN_SUB * c) == 0 else 1
    step = sub * c
    per_k = pl.BlockSpec((sub, TOP_K, c), lambda i: (i, 0, 0))
    return pl.pallas_call(
        functools.partial(_xattn_kernel, shift=_rank_bits(t)),
        grid=(t // step,),
        in_specs=[
            pl.BlockSpec((step, D_MODEL), lambda i: (i, 0)),
            full((D_MODEL, D_MODEL)),
            full((n_mem, D_MODEL)),
            full((n_mem, D_MODEL)),
            full((D_MODEL, D_MODEL)),
            full((1, D_MODEL)),
            full((1, D_MODEL)),
            full((N_EXPERTS, D_MODEL)),
            full((N_EXPERTS, 1)),
            full((c, c)),
        ],
        out_specs=[_tiled_spec(step, lambda i: (i, 0)), per_k, per_k, full((N_EXPERTS, 1))],
        out_shape=[
            jax.ShapeDtypeStruct(_tiled(t), F32),
            jax.ShapeDtypeStruct((t // c, TOP_K, c), jnp.int32),
            jax.ShapeDtypeStruct((t // c, TOP_K, c), F32),
            jax.ShapeDtypeStruct((N_EXPERTS, 1), jnp.int32),
        ],
        scratch_shapes=[pltpu.VMEM((step, D_MODEL), BF16), pltpu.VMEM((N_EXPERTS, 1), F32)],
        compiler_params=pltpu.CompilerParams(dimension_semantics=("arbitrary",), vmem_limit_bytes=VMEM_LIMIT),
        name="xattn_router",
    )(x1, wq.astype(BF16), kmem, vmem, wo.astype(BF16), ln_w.reshape(1, -1), ln_b.reshape(1, -1),
      w_r.T, b_r.reshape(-1, 1), jnp.asarray(strict_lower.T, BF16))


TAIL_RUNS = (128, 64, 32, 16, 8, 4, 2, 1)


def _dispatch_kernel(fill_ref, pend_ref, nused_ref, slot_ref, x_ref, xs_hbm, zero_ref, sem, zsem, *, n_blocks):
    i = pl.program_id(0)
    c = SEQ_BLOCK

    @pl.when(i == 0)
    def _():
        zero_ref[...] = jnp.zeros_like(zero_ref)

        sub = ROW_TILE[0]

        def rows(ref, first, n):
            return ref.at[pl.ds(pl.multiple_of(first * sub, sub), n * sub)]

        def tail(e, act):
            row = fill_ref[e]
            pad = pend_ref[e] - row
            for run in TAIL_RUNS:
                hit = (pad & run) != 0

                @pl.when(hit)
                def _():
                    act(pltpu.make_async_copy(rows(zero_ref, 0, run), rows(xs_hbm, row, run), zsem))
                row = row + jnp.where(hit, run, 0)

        def spare(b, act):
            act(pltpu.make_async_copy(zero_ref, rows(xs_hbm, b * MOE_BLOCK, MOE_BLOCK), zsem))

        for act in (lambda cp: cp.start(), lambda cp: cp.wait()):
            lax.fori_loop(0, N_EXPERTS, lambda e, carry: (tail(e, act), carry)[1], 0)
            lax.fori_loop(nused_ref[0], n_blocks, lambda b, carry: (spare(b, act), carry)[1], 0)

    for q in range(x_ref.shape[0] // (c * ROW_TILE[0])):
        def issue(j, carry, q=q):
            for u in range(ISSUE_UNROLL):
                r = j * ISSUE_UNROLL + u
                for k in range(TOP_K):
                    dst = _tile_of(xs_hbm, slot_ref[0, 0, (q * TOP_K + k) * c + r])
                    pltpu.make_async_copy(_tile_of(x_ref, q * c + r), dst, sem).start(priority=k % 2)
            return carry
        lax.fori_loop(0, c // ISSUE_UNROLL, issue, 0)
    for k in range(TOP_K):
        pltpu.make_async_copy(x_ref, xs_hbm.at[pl.ds(0, x_ref.shape[0])], sem).wait()


def _dispatch(x2, slots, fill, pend, n_used, n_blocks):
    t = x2.shape[0] // ROW_TILE[0]
    c = min(DISPATCH_BLOCK, t)
    assert t % c == 0 and c % SEQ_BLOCK == 0
    grid_spec = pltpu.PrefetchScalarGridSpec(
        num_scalar_prefetch=3,
        grid=(t // c,),
        in_specs=[
            pl.BlockSpec((1, 1, c * TOP_K), lambda i, *_: (i, 0, 0), memory_space=pltpu.SMEM),
            _tiled_spec(c, lambda i, *_: (i, 0)),
        ],
        out_specs=pl.BlockSpec(memory_space=pl.ANY),
        scratch_shapes=[pltpu.VMEM(_tiled(MOE_BLOCK), F32), pltpu.SemaphoreType.DMA, pltpu.SemaphoreType.DMA],
    )
    return pl.pallas_call(
        functools.partial(_dispatch_kernel, n_blocks=n_blocks),
        grid_spec=grid_spec,
        out_shape=jax.ShapeDtypeStruct(_tiled(n_blocks * MOE_BLOCK), F32),
        compiler_params=pltpu.CompilerParams(dimension_semantics=("arbitrary",), vmem_limit_bytes=VMEM_LIMIT),
        name="dispatch",
    )(fill, pend, n_used, slots.reshape(t // c, 1, c * TOP_K), x2)


def _expert_kernel(be_ref, first_ref, nxt_ref, par_ref, nused_ref, x_ref, w1_hbm, b1_ref, w2_hbm, b2_ref, o_ref,
                   w1f, w2f, w1b, w2b, sem1, sem2):
    b = pl.program_id(0)

    def fetch(e, p):
        return (pltpu.make_async_copy(w1_hbm.at[e], w1f.at[p], sem1.at[p]),
                pltpu.make_async_copy(w2_hbm.at[e], w2f.at[p], sem2.at[p]))

    @pl.when(b == 0)
    def _():
        for cp in fetch(be_ref[0], par_ref[0]):
            cp.start()

    @pl.when(b >= nused_ref[0])
    def _():
        o_ref[...] = jnp.zeros_like(o_ref)

    @pl.when(b < nused_ref[0])
    def _():
        @pl.when(first_ref[b] == 1)
        def _():
            p = par_ref[b]
            for cp in fetch(be_ref[b], p):
                cp.wait()

            @pl.when(nxt_ref[b] >= 0)
            def _():
                for cp in fetch(nxt_ref[b], 1 - p):
                    cp.start()

            w1b[...] = w1f[p].astype(BF16)
            w2b[...] = w2f[p].astype(BF16)

        h = _dot(_load_row_tiles(x_ref).astype(BF16), w1b[...]) + b1_ref[0]
        h_glu = jnp.minimum(h[:, :D_EXPERT], SWIGLU_LIMIT)
        h_lin = jnp.clip(h[:, D_EXPERT:], -SWIGLU_LIMIT, SWIGLU_LIMIT)
        act = h_glu * (1.0 / (1.0 + jnp.exp(-SWIGLU_ALPHA * h_glu))) * (h_lin + 1.0)
        _store_row_tiles(o_ref, _dot(act.astype(BF16), w2b[...]) + b2_ref[0])


def _experts(xs, plan, w1, b1, w2, b2):
    block_e, first, nxt, par, n_used = plan
    n_blocks = block_e.shape[0]
    rows = lambda b, *_: (b, 0)
    bsel = lambda b, be, *_: (be[b], 0, 0)
    grid_spec = pltpu.PrefetchScalarGridSpec(
        num_scalar_prefetch=5,
        grid=(n_blocks,),
        in_specs=[
            _tiled_spec(MOE_BLOCK, rows),
            pl.BlockSpec(memory_space=pl.ANY),
            pl.BlockSpec((1, 1, 2 * D_EXPERT), bsel),
            pl.BlockSpec(memory_space=pl.ANY),
            pl.BlockSpec((1, 1, D_MODEL), bsel),
        ],
        out_specs=_tiled_spec(MOE_BLOCK, rows),
        scratch_shapes=[
            pltpu.VMEM((2, D_MODEL, 2 * D_EXPERT), F32),
            pltpu.VMEM((2, D_EXPERT, D_MODEL), F32),
            pltpu.VMEM((D_MODEL, 2 * D_EXPERT), BF16),
            pltpu.VMEM((D_EXPERT, D_MODEL), BF16),
            pltpu.SemaphoreType.DMA((2,)),
            pltpu.SemaphoreType.DMA((2,)),
        ],
    )
    return pl.pallas_call(
        _expert_kernel,
        grid_spec=grid_spec,
        out_shape=jax.ShapeDtypeStruct(xs.shape, F32),
        compiler_params=pltpu.CompilerParams(dimension_semantics=("arbitrary",), vmem_limit_bytes=EXPERT_VMEM_LIMIT),
        name="experts",
    )(block_e, first, nxt, par, n_used, xs, w1, b1.reshape(N_EXPERTS, 1, -1), w2, b2.reshape(N_EXPERTS, 1, -1))


def _combine_kernel(slot_ref, next_slot_ref, ys_hbm, gate_ref, x_ref, lnw_ref, lnb_ref, o_ref, ybuf, sem):
    i = pl.program_id(0)
    c = SEQ_BLOCK

    def gather(slots, p):
        def issue(j, carry):
            for u in range(ISSUE_UNROLL):
                r = j * ISSUE_UNROLL + u
                for k in range(TOP_K):
                    src = _tile_of(ys_hbm, slots[0, 0, k * c + r])
                    pltpu.make_async_copy(src, _tile_of(ybuf.at[p, k], r), sem.at[p]).start(priority=k % 2)
            return carry
        lax.fori_loop(0, c // ISSUE_UNROLL, issue, 0)

    @pl.when(i == 0)
    def _():
        gather(slot_ref, 0)

    @pl.when(i + 1 < pl.num_programs(0))
    def _():
        gather(next_slot_ref, (i + 1) % 2)

    p = i % 2
    for k in range(TOP_K):
        pltpu.make_async_copy(ys_hbm.at[pl.ds(0, ybuf.shape[2])], ybuf.at[p, k], sem.at[p]).wait()
    gates = gate_ref[...]
    ys = [_load_row_tiles(ybuf.at[p, k]) * gates[:, k:k + 1] for k in range(TOP_K)]
    y = (ys[0] + ys[1]) + (ys[2] + ys[3])
    o_ref[...] = _layer_norm(DEEPNORM_ALPHA * _load_row_tiles(x_ref) + y, lnw_ref[...], lnb_ref[...])


def _combine(ys, slots, gates, x2, ln_w, ln_b):
    t = x2.shape[0] // ROW_TILE[0]
    c = SEQ_BLOCK
    n = t // c
    slot_table = slots.reshape(n, 1, c * TOP_K)
    return pl.pallas_call(
        _combine_kernel,
        grid=(n,),
        in_specs=[
            pl.BlockSpec((1, 1, c * TOP_K), lambda i: (i, 0, 0), memory_space=pltpu.SMEM),
            pl.BlockSpec((1, 1, c * TOP_K), lambda i: (jnp.minimum(i + 1, n - 1), 0, 0), memory_space=pltpu.SMEM),
            pl.BlockSpec(memory_space=pl.ANY),
            pl.BlockSpec((c, TOP_K), lambda i: (i, 0)),
            _tiled_spec(c, lambda i: (i, 0)),
            pl.BlockSpec((1, D_MODEL), lambda i: (0, 0)),
            pl.BlockSpec((1, D_MODEL), lambda i: (0, 0)),
        ],
        out_specs=pl.BlockSpec((c, D_MODEL), lambda i: (i, 0)),
        out_shape=jax.ShapeDtypeStruct((t, D_MODEL), F32),
        scratch_shapes=[pltpu.VMEM((2, TOP_K) + _tiled(c), F32), pltpu.SemaphoreType.DMA((2,))],
        compiler_params=pltpu.CompilerParams(dimension_semantics=("arbitrary",), vmem_limit_bytes=VMEM_LIMIT),
        name="combine",
    )(slot_table, slot_table, ys, gates, x2, ln_w.reshape(1, -1), ln_b.reshape(1, -1))


def _slot_tables(counts, packed, n_blocks, shift):
    counts = counts.reshape(-1)
    nblk = (counts + MOE_BLOCK - 1) // MOE_BLOCK
    ids = jnp.arange(N_EXPERTS)
    lower = ids[None, :] <= ids[:, None]
    bend = jnp.sum(jnp.where(lower, nblk[None, :], 0), axis=1)
    n_used = bend[N_EXPERTS - 1]
    pend = bend * MOE_BLOCK
    pstart = pend - nblk * MOE_BLOCK
    fill = pstart + counts
    order = jnp.sum(jnp.where(lower, (nblk > 0)[None, :], False), axis=1) - 1
    later = (ids[None, :] > ids[:, None]) & (nblk > 0)[None, :]
    nxt_e = jnp.min(jnp.where(later, ids[None, :], N_EXPERTS), axis=1)
    nxt_e = jnp.where(nxt_e == N_EXPERTS, -1, nxt_e)
    b = jnp.minimum(jnp.arange(n_blocks), n_used - 1)
    hit = bend[None, :] <= b[:, None]
    block_e = jnp.sum(hit.astype(jnp.int32), axis=1)
    of_block = lambda per_expert: jnp.sum(jnp.where(ids[None, :] == block_e[:, None], per_expert[None, :], 0), axis=1)
    first = (b == of_block(bend - nblk)).astype(jnp.int32)
    expert = lax.shift_right_logical(packed, shift)
    rank = packed & ((1 << shift) - 1)
    slots = jnp.sum(jnp.where(expert[..., None] == ids, pstart, 0), axis=-1) + rank
    i32 = lambda a: a.astype(jnp.int32)
    plan = (i32(block_e), first, i32(of_block(nxt_e)), i32(of_block(order & 1)), i32(n_used.reshape(1)))
    return i32(slots), i32(fill), i32(pend), plan


def kernel(x, mem, w_in, lb_logits, hgrn_norm_w, w_pool, pool_scale, w_out, ln1_w, ln1_b, w_xq, w_xk, w_xv, w_xo,
           ln2_w, ln2_b, w_router, b_router, w1, b1, w2, b2, ln3_w, ln3_b):
    bsz, seq, d = x.shape
    assert bsz == 1 and d == D_MODEL and seq % SEQ_BLOCK == 0 and w_in.shape[0] == 1
    xt = x.reshape(seq, d)
    x1 = _mixer(xt, w_in[0], lb_logits, hgrn_norm_w[0], w_pool[0], pool_scale[0], w_out[0], ln1_w[0], ln1_b[0])
    kmem, vmem = _kv_proj(mem[0], w_xk[0], w_xv[0])
    x2, packed, gates, counts = _xattn_router(x1, kmem, vmem, w_xq[0], w_xo[0], ln2_w[0], ln2_b[0],
                                              w_router[0], b_router[0])
    n_blocks = -(-(seq * TOP_K + N_EXPERTS * (MOE_BLOCK - 1)) // MOE_BLOCK)
    slots, fill, pend, plan = _slot_tables(counts, packed, n_blocks, _rank_bits(seq))
    xs = _dispatch(x2, slots, fill, pend, plan[-1], n_blocks)
    ys = _experts(xs, plan, w1[0], b1[0], w2[0], b2[0])
    gates_tk = gates.transpose(0, 2, 1).reshape(seq, TOP_K)
    out = _combine(ys, slots, gates_tk, x2, ln3_w[0], ln3_b[0])
    return out.reshape(bsz, seq, d)
```

```python
import functools

import numpy as np
import jax
import jax.numpy as jnp
from jax import lax
from jax.experimental import pallas as pl
from jax.experimental.pallas import tpu as pltpu

F32 = jnp.float32
BF16 = jnp.bfloat16

D_MODEL = 1024
HGRN_WIDTH = 512
HGRN_HEADS = 4
HEAD_DIM = 128
POOL_WINDOWS = (2, 4, 8, 16)
POOL_GROUP = 128
POOL_CARRY = 16
IN_PROJ_WIDTH = 4 * HGRN_WIDTH + 512
XATTN_HEADS = 4
XATTN_HEAD_DIM = 256
N_EXPERTS = 32
TOP_K = 4
D_EXPERT = 1024
SWIGLU_LIMIT = 7.0
SWIGLU_ALPHA = 1.702
MOE_BLOCK = 256
DEEPNORM_ALPHA = 2.0 ** 0.25
LN_EPS = 1e-5
RMS_EPS = 1e-6

SEQ_BLOCK = 256
EXPERT_STEP_BLOCKS = 4
XATTN_SUB = 1
DISPATCH_BLOCK = 1024
N_LEVELS = 8
MXU_LEVELS = 3
ISSUE_UNROLL = 2
ROW_TILE = (8, 128)
VMEM_LIMIT = 48 * 1024 * 1024
EXPERT_VMEM_LIMIT = 56 * 1024 * 1024


def _rank_bits(t):
    return max(int(t - 1).bit_length(), 1)


def _layer_norm(y, w, b):
    mu = jnp.mean(y, axis=-1, keepdims=True)
    yc = y - mu
    var = jnp.mean(yc * yc, axis=-1, keepdims=True)
    return yc * lax.rsqrt(var + LN_EPS) * w + b


def _tiled(n):
    return (n * ROW_TILE[0], ROW_TILE[1])


def _tiled_spec(n, index_map):
    return pl.BlockSpec(_tiled(n), index_map)


def _tile_of(ref, r):
    return ref.at[pl.ds(pl.multiple_of(r * ROW_TILE[0], ROW_TILE[0]), ROW_TILE[0])]


def _store_row_tiles(ref, val):
    s, l = ROW_TILE
    for j in range(s):
        ref[pl.ds(j, val.shape[0], stride=s), :] = val[:, j * l:(j + 1) * l]


def _load_row_tiles(ref):
    s, _ = ROW_TILE
    return jnp.concatenate([ref[pl.ds(j, ref.shape[0] // s, stride=s), :] for j in range(s)], axis=-1)


def _dot(a, b):
    return jnp.dot(a, b, preferred_element_type=F32)


def _dot_nt(a, b):
    return lax.dot_general(a, b, (((1,), (1,)), ((), ())), preferred_element_type=F32)


def _decay_tables():
    c = SEQ_BLOCK
    r = np.arange(c)[:, None]
    u = np.arange(c)[None, :]
    mats = [(u <= r)]
    for l in range(MXU_LEVELS):
        h = 1 << l
        mid = (r // (2 * h)) * (2 * h) + h
        upper = (r >= mid) & (u >= mid) & (u <= r)
        lower = (r < mid) & (u > r) & (u < mid)
        mats.append(upper | lower)
    mall = np.concatenate(mats, axis=0).astype(np.float32)
    x = r ^ u
    lvl = np.where(u < r, np.floor(np.log2(np.maximum(x, 1))).astype(np.int32), -1).astype(np.int32)
    strict_lower = (u < r).astype(np.float32)
    return mall, lvl, strict_lower


def _level_gap(negb, l):
    half = 1 << l
    parts = []
    for base in range(0, negb.shape[0], 2 * half):
        ref_row = negb[base + half - 1:base + half, :]
        parts.append(jnp.abs(negb[base:base + 2 * half, :] - ref_row))
    return jnp.concatenate(parts, axis=0) if len(parts) > 1 else parts[0]


def _mixer_kernel(x_ref, win_ref, lbl_ref, nw_ref, wpool_ref, pscale_ref, wout_ref, lnw_ref, lnb_ref,
                  mall_ref, lvl_ref, o_ref, state_ref, carry_ref, ext_ref, mix_ref):
    i = pl.program_id(0)
    c = SEQ_BLOCK

    @pl.when(i == 0)
    def _():
        state_ref[...] = jnp.zeros_like(state_ref)
        carry_ref[...] = jnp.zeros_like(carry_ref)

    x = x_ref[...]
    proj = _dot(x.astype(BF16), win_ref[...])
    w = HGRN_WIDTH
    q = proj[:, 0:w]
    z = proj[:, w:2 * w]
    v = proj[:, 2 * w:3 * w]
    g = proj[:, 3 * w:4 * w]
    p = proj[:, 4 * w:]

    ll = lbl_ref[...]
    ee = jnp.exp(ll - jnp.max(ll, axis=0, keepdims=True))
    lb = ee[0:1] / jnp.sum(ee, axis=0, keepdims=True)

    f = lb + (1.0 - lb) * (1.0 / (1.0 + jnp.exp(-z)))
    nlf = -jnp.log(f)
    key = (1.0 - lb) * (1.0 / (1.0 + jnp.exp(z)))

    hi = nlf.astype(BF16)
    lo = (nlf - hi.astype(F32)).astype(BF16)
    mall = mall_ref[...]
    cum = _dot(mall, hi) + _dot(mall, lo)
    lvl = lvl_ref[...]

    for h in range(HGRN_HEADS):
        hs = slice(h * HEAD_DIM, (h + 1) * HEAD_DIM)
        qh, kh, vh, gh = q[:, hs], key[:, hs], v[:, hs], g[:, hs]
        vb = vh.astype(BF16)
        negb = cum[0:c, hs]
        scores = jnp.zeros((c, c), F32)
        for l in range(N_LEVELS):
            gap = cum[(l + 1) * c:(l + 2) * c, hs] if l < MXU_LEVELS else _level_gap(negb, l)
            e = jnp.exp(-gap)
            pl_ = _dot_nt((qh * e).astype(BF16), (kh * e).astype(BF16))
            scores = jnp.where(lvl == l, pl_, scores)
        diag = jnp.sum(qh * kh, axis=-1, keepdims=True)
        st = state_ref[h]
        o = (_dot(scores.astype(BF16), vb) + diag * vh
             + _dot_nt((qh * jnp.exp(-negb)).astype(BF16), st.astype(BF16)))
        negb_last = negb[c - 1:c, :]
        kd = (kh * jnp.exp(negb - negb_last)).astype(BF16)
        state_ref[h] = st * jnp.exp(-negb_last) + _dot(vh.T.astype(BF16), kd)
        o = o * lax.rsqrt(jnp.mean(o * o, axis=-1, keepdims=True) + RMS_EPS) * nw_ref[:, hs]
        gate = gh * (1.0 / (1.0 + jnp.exp(-gh)))
        mix_ref[:, hs] = (o * gate).astype(BF16)

    ext_ref[0:POOL_CARRY, :] = carry_ref[...]
    ext_ref[POOL_CARRY:POOL_CARRY + c, :] = p
    carry_ref[...] = p[c - POOL_CARRY:c, :]
    pos = i * c + lax.broadcasted_iota(jnp.int32, (c, 1), 0)
    for gi, win in enumerate(POOL_WINDOWS):
        gs = slice(gi * POOL_GROUP, (gi + 1) * POOL_GROUP)
        pg = p[:, gs]
        tot = pg
        for j in range(1, win):
            tot = tot + ext_ref[POOL_CARRY - j:POOL_CARRY - j + c, gs]
        cnt = jnp.minimum(pos + 1, win).astype(F32)
        pooled = _dot((tot / cnt - pg).astype(BF16), wpool_ref[gi])
        mix_ref[:, w + gi * POOL_GROUP:w + (gi + 1) * POOL_GROUP] = (pooled * pscale_ref[:, gs]).astype(BF16)

    y = DEEPNORM_ALPHA * x + _dot(mix_ref[...], wout_ref[...])
    o_ref[...] = _layer_norm(y, lnw_ref[...], lnb_ref[...])


def _mixer(x, w_in, lb_logits, norm_w, w_pool, pool_scale, w_out, ln_w, ln_b):
    t = x.shape[0]
    c = SEQ_BLOCK
    mall, lvl, _ = _decay_tables()
    full = lambda shape: pl.BlockSpec(shape, lambda i: (0,) * len(shape))
    return pl.pallas_call(
        _mixer_kernel,
        grid=(t // c,),
        in_specs=[
            pl.BlockSpec((c, D_MODEL), lambda i: (i, 0)),
            full((D_MODEL, IN_PROJ_WIDTH)),
            full(lb_logits.shape),
            full((1, HGRN_WIDTH)),
            full((len(POOL_WINDOWS), POOL_GROUP, POOL_GROUP)),
            full((1, 512)),
            full((D_MODEL, D_MODEL)),
            full((1, D_MODEL)),
            full((1, D_MODEL)),
            full(mall.shape),
            full(lvl.shape),
        ],
        out_specs=pl.BlockSpec((c, D_MODEL), lambda i: (i, 0)),
        out_shape=jax.ShapeDtypeStruct((t, D_MODEL), F32),
        scratch_shapes=[
            pltpu.VMEM((HGRN_HEADS, HEAD_DIM, HEAD_DIM), F32),
            pltpu.VMEM((POOL_CARRY, 512), F32),
            pltpu.VMEM((POOL_CARRY + c, 512), F32),
            pltpu.VMEM((c, D_MODEL), BF16),
        ],
        compiler_params=pltpu.CompilerParams(dimension_semantics=("arbitrary",), vmem_limit_bytes=VMEM_LIMIT),
        name="mixer",
    )(x, w_in.astype(BF16), lb_logits, norm_w.reshape(1, -1), w_pool.astype(BF16), pool_scale.reshape(1, -1),
      w_out.astype(BF16), ln_w.reshape(1, -1), ln_b.reshape(1, -1), jnp.asarray(mall, BF16), jnp.asarray(lvl))


def _kv_kernel(mem_ref, wk_ref, wv_ref, k_ref, v_ref):
    m = mem_ref[...].astype(BF16)
    k_ref[...] = _dot(m, wk_ref[...]).astype(BF16)
    v_ref[...] = _dot(m, wv_ref[...]).astype(BF16)


def _kv_proj(mem, wk, wv):
    n = mem.shape[0]
    return pl.pallas_call(
        _kv_kernel,
        out_shape=(jax.ShapeDtypeStruct((n, D_MODEL), BF16), jax.ShapeDtypeStruct((n, D_MODEL), BF16)),
        compiler_params=pltpu.CompilerParams(vmem_limit_bytes=VMEM_LIMIT),
        name="kv_proj",
    )(mem, wk.astype(BF16), wv.astype(BF16))


def _split3(a):
    a0 = a.astype(BF16)
    r1 = a - a0.astype(F32)
    a1 = r1.astype(BF16)
    a2 = (r1 - a1.astype(F32)).astype(BF16)
    return a0, a1, a2


def _xattn_kernel(x_ref, wq_ref, k_ref, v_ref, wo_ref, lnw_ref, lnb_ref, wrt_ref, br_ref, ut_ref,
                  x2_ref, dest_ref, gate_ref, cnt_ref, att_ref, run_ref, *, shift):
    i = pl.program_id(0)
    c = SEQ_BLOCK
    n = N_EXPERTS

    @pl.when(i == 0)
    def _():
        run_ref[...] = jnp.zeros_like(run_ref)

    w0, w1, w2 = _split3(wrt_ref[...])
    eid = lax.broadcasted_iota(jnp.int32, (n, c), 0)
    scale = XATTN_HEAD_DIM ** -0.5
    run = run_ref[...]

    for sb in range(x_ref.shape[0] // c):
        rows = slice(sb * c, (sb + 1) * c)
        x = x_ref[rows, :]
        q = _dot(x.astype(BF16), wq_ref[...])
        for h in range(XATTN_HEADS):
            hs = slice(h * XATTN_HEAD_DIM, (h + 1) * XATTN_HEAD_DIM)
            s = _dot_nt(q[:, hs].astype(BF16), k_ref[:, hs]) * scale
            s = s - jnp.max(s, axis=-1, keepdims=True)
            e = jnp.exp(s)
            pr = e / jnp.sum(e, axis=-1, keepdims=True)
            att_ref[rows, hs] = _dot(pr.astype(BF16), v_ref[:, hs]).astype(BF16)
        y = DEEPNORM_ALPHA * x + _dot(att_ref[rows, :], wo_ref[...])
        x2 = _layer_norm(y, lnw_ref[...], lnb_ref[...])
        _store_row_tiles(x2_ref.at[pl.ds(sb * c * ROW_TILE[0], c * ROW_TILE[0])], x2)

        a0, a1, a2 = _split3(x2)
        pa = _dot_nt(jnp.concatenate([w0, w1, w2], axis=0), a0)
        pb = _dot_nt(jnp.concatenate([w0, w1], axis=0), a1)
        pc = _dot_nt(w0, a2)
        logits = ((pc + pa[2 * n:3 * n] + pb[n:2 * n]) + (pb[0:n] + pa[n:2 * n]) + pa[0:n]) + br_ref[...]

        work = logits
        sels, vals, idxs = [], [], []
        for k in range(TOP_K):
            m = jnp.max(work, axis=0, keepdims=True)
            idx = jnp.min(jnp.where(work == m, eid, n), axis=0, keepdims=True)
            sel = eid == idx
            sels.append(sel)
            vals.append(m)
            idxs.append(idx)
            work = jnp.where(sel, -jnp.inf, work)
        es = [jnp.exp(vk - vals[0]) for vk in vals]
        denom = es[0] + es[1] + es[2] + es[3]
        gate_ref[sb] = jnp.concatenate([ek / denom for ek in es], axis=0)

        onehot = jnp.zeros((n, c), F32)
        for sel in sels:
            onehot = jnp.where(sel, 1.0, onehot)
        before = _dot(onehot.astype(BF16), ut_ref[...]) + run
        ranks = [jnp.sum(jnp.where(sel, before, 0.0), axis=0, keepdims=True).astype(jnp.int32) for sel in sels]
        run = run + jnp.sum(onehot, axis=1, keepdims=True)

        dest_ref[sb] = jnp.concatenate([idx * (1 << shift) + rk for idx, rk in zip(idxs, ranks)], axis=0)

    run_ref[...] = run
    cnt_ref[...] = run.astype(jnp.int32)


def _xattn_router(x1, kmem, vmem, wq, wo, ln_w, ln_b, w_r, b_r):
    t = x1.shape[0]
    c = SEQ_BLOCK
    n_mem = kmem.shape[0]
    _, _, strict_lower = _decay_tables()
    full = lambda shape: pl.BlockSpec(shape, lambda i: (0,) * len(shape))
    sub = XATTN_SUB if t % (XATTN_SUB * c) == 0 else 1
    step = sub * c
    per_k = pl.BlockSpec((sub, TOP_K, c), lambda i: (i, 0, 0))
    return pl.pallas_call(
        functools.partial(_xattn_kernel, shift=_rank_bits(t)),
        grid=(t // step,),
        in_specs=[
            pl.BlockSpec((step, D_MODEL), lambda i: (i, 0)),
            full((D_MODEL, D_MODEL)),
            full((n_mem, D_MODEL)),
            full((n_mem, D_MODEL)),
            full((D_MODEL, D_MODEL)),
            full((1, D_MODEL)),
            full((1, D_MODEL)),
            full((N_EXPERTS, D_MODEL)),
            full((N_EXPERTS, 1)),
            full((c, c)),
        ],
        out_specs=[_tiled_spec(step, lambda i: (i, 0)), per_k, per_k, full((N_EXPERTS, 1))],
        out_shape=[
            jax.ShapeDtypeStruct(_tiled(t), F32),
            jax.ShapeDtypeStruct((t // c, TOP_K, c), jnp.int32),
            jax.ShapeDtypeStruct((t // c, TOP_K, c), F32),
            jax.ShapeDtypeStruct((N_EXPERTS, 1), jnp.int32),
        ],
        scratch_shapes=[pltpu.VMEM((step, D_MODEL), BF16), pltpu.VMEM((N_EXPERTS, 1), F32)],
        compiler_params=pltpu.CompilerParams(dimension_semantics=("arbitrary",), vmem_limit_bytes=VMEM_LIMIT),
        name="xattn_router",
    )(x1, wq.astype(BF16), kmem, vmem, wo.astype(BF16), ln_w.reshape(1, -1), ln_b.reshape(1, -1),
      w_r.T, b_r.reshape(-1, 1), jnp.asarray(strict_lower.T, BF16))


TAIL_RUNS = (128, 64, 32, 16, 8, 4, 2, 1)


def _dispatch_kernel(fill_ref, pend_ref, nused_ref, slot_ref, x_ref, xs_hbm, zero_ref, sem, zsem, *, n_blocks):
    i = pl.program_id(0)
    c = SEQ_BLOCK

    @pl.when(i == 0)
    def _():
        zero_ref[...] = jnp.zeros_like(zero_ref)

        sub = ROW_TILE[0]

        def rows(ref, first, n):
            return ref.at[pl.ds(pl.multiple_of(first * sub, sub), n * sub)]

        def tail(e, act):
            row = fill_ref[e]
            pad = pend_ref[e] - row
            for run in TAIL_RUNS:
                hit = (pad & run) != 0

                @pl.when(hit)
                def _():
                    act(pltpu.make_async_copy(rows(zero_ref, 0, run), rows(xs_hbm, row, run), zsem))
                row = row + jnp.where(hit, run, 0)

        def spare(b, act):
            act(pltpu.make_async_copy(zero_ref, rows(xs_hbm, b * MOE_BLOCK, MOE_BLOCK), zsem))

        for act in (lambda cp: cp.start(), lambda cp: cp.wait()):
            lax.fori_loop(0, N_EXPERTS, lambda e, carry: (tail(e, act), carry)[1], 0)
            lax.fori_loop(nused_ref[0], n_blocks, lambda b, carry: (spare(b, act), carry)[1], 0)

    for q in range(x_ref.shape[0] // (c * ROW_TILE[0])):
        def issue(j, carry, q=q):
            for u in range(ISSUE_UNROLL):
                r = j * ISSUE_UNROLL + u
                for k in range(TOP_K):
                    dst = _tile_of(xs_hbm, slot_ref[0, 0, (q * TOP_K + k) * c + r])
                    pltpu.make_async_copy(_tile_of(x_ref, q * c + r), dst, sem).start(priority=k % 2)
            return carry
        lax.fori_loop(0, c // ISSUE_UNROLL, issue, 0)
    for k in range(TOP_K):
        pltpu.make_async_copy(x_ref, xs_hbm.at[pl.ds(0, x_ref.shape[0])], sem).wait()


def _dispatch(x2, slots, fill, pend, n_used, n_blocks):
    t = x2.shape[0] // ROW_TILE[0]
    c = min(DISPATCH_BLOCK, t)
    assert t % c == 0 and c % SEQ_BLOCK == 0
    grid_spec = pltpu.PrefetchScalarGridSpec(
        num_scalar_prefetch=3,
        grid=(t // c,),
        in_specs=[
            pl.BlockSpec((1, 1, c * TOP_K), lambda i, *_: (i, 0, 0), memory_space=pltpu.SMEM),
            _tiled_spec(c, lambda i, *_: (i, 0)),
        ],
        out_specs=pl.BlockSpec(memory_space=pl.ANY),
        scratch_shapes=[pltpu.VMEM(_tiled(MOE_BLOCK), F32), pltpu.SemaphoreType.DMA, pltpu.SemaphoreType.DMA],
    )
    return pl.pallas_call(
        functools.partial(_dispatch_kernel, n_blocks=n_blocks),
        grid_spec=grid_spec,
        out_shape=jax.ShapeDtypeStruct(_tiled(n_blocks * MOE_BLOCK), F32),
        compiler_params=pltpu.CompilerParams(dimension_semantics=("arbitrary",), vmem_limit_bytes=VMEM_LIMIT),
        name="dispatch",
    )(fill, pend, n_used, slots.reshape(t // c, 1, c * TOP_K), x2)


def _expert_kernel(be_ref, first_ref, nxt_ref, par_ref, nused_ref, x_ref, w1_hbm, b1_ref, w2_hbm, b2_ref, o_ref,
                   w1f, w2f, w1b, w2b, sem1, sem2):
    step = pl.program_id(0)
    rows = MOE_BLOCK * ROW_TILE[0]

    def fetch(e, p):
        return (pltpu.make_async_copy(w1_hbm.at[e], w1f.at[p], sem1.at[p]),
                pltpu.make_async_copy(w2_hbm.at[e], w2f.at[p], sem2.at[p]))

    @pl.when(step == 0)
    def _():
        for cp in fetch(be_ref[0], par_ref[0]):
            cp.start()

    def block(sb, carry):
        b = step * EXPERT_STEP_BLOCKS + sb
        x_blk = x_ref.at[pl.ds(pl.multiple_of(sb * rows, rows), rows)]
        o_blk = o_ref.at[pl.ds(pl.multiple_of(sb * rows, rows), rows)]

        @pl.when(b >= nused_ref[0])
        def _():
            o_blk[...] = jnp.zeros(o_blk.shape, o_blk.dtype)

        @pl.when(b < nused_ref[0])
        def _():
            e = be_ref[b]

            @pl.when(first_ref[b] == 1)
            def _():
                p = par_ref[b]
                for cp in fetch(e, p):
                    cp.wait()

                @pl.when(nxt_ref[b] >= 0)
                def _():
                    for cp in fetch(nxt_ref[b], 1 - p):
                        cp.start()

                w1b[...] = w1f[p].astype(BF16)
                w2b[...] = w2f[p].astype(BF16)

            h = _dot(_load_row_tiles(x_blk).astype(BF16), w1b[...]) + b1_ref[e]
            h_glu = jnp.minimum(h[:, :D_EXPERT], SWIGLU_LIMIT)
            h_lin = jnp.clip(h[:, D_EXPERT:], -SWIGLU_LIMIT, SWIGLU_LIMIT)
            act = h_glu * (1.0 / (1.0 + jnp.exp(-SWIGLU_ALPHA * h_glu))) * (h_lin + 1.0)
            _store_row_tiles(o_blk, _dot(act.astype(BF16), w2b[...]) + b2_ref[e])
        return carry

    lax.fori_loop(0, EXPERT_STEP_BLOCKS, block, 0)


def _experts(xs, plan, w1, b1, w2, b2):
    block_e, first, nxt, par, n_used = plan
    n_blocks = block_e.shape[0]
    assert n_blocks % EXPERT_STEP_BLOCKS == 0
    rows = lambda s, *_: (s, 0)
    whole = lambda s, *_: (0, 0, 0)
    grid_spec = pltpu.PrefetchScalarGridSpec(
        num_scalar_prefetch=5,
        grid=(n_blocks // EXPERT_STEP_BLOCKS,),
        in_specs=[
            _tiled_spec(EXPERT_STEP_BLOCKS * MOE_BLOCK, rows),
            pl.BlockSpec(memory_space=pl.ANY),
            pl.BlockSpec((N_EXPERTS, 1, 2 * D_EXPERT), whole),
            pl.BlockSpec(memory_space=pl.ANY),
            pl.BlockSpec((N_EXPERTS, 1, D_MODEL), whole),
        ],
        out_specs=_tiled_spec(EXPERT_STEP_BLOCKS * MOE_BLOCK, rows),
        scratch_shapes=[
            pltpu.VMEM((2, D_MODEL, 2 * D_EXPERT), F32),
            pltpu.VMEM((2, D_EXPERT, D_MODEL), F32),
            pltpu.VMEM((D_MODEL, 2 * D_EXPERT), BF16),
            pltpu.VMEM((D_EXPERT, D_MODEL), BF16),
            pltpu.SemaphoreType.DMA((2,)),
            pltpu.SemaphoreType.DMA((2,)),
        ],
    )
    return pl.pallas_call(
        _expert_kernel,
        grid_spec=grid_spec,
        out_shape=jax.ShapeDtypeStruct(xs.shape, F32),
        compiler_params=pltpu.CompilerParams(dimension_semantics=("arbitrary",), vmem_limit_bytes=EXPERT_VMEM_LIMIT),
        name="experts",
    )(block_e, first, nxt, par, n_used, xs, w1, b1.reshape(N_EXPERTS, 1, -1), w2, b2.reshape(N_EXPERTS, 1, -1))


def _combine_kernel(slot_ref, next_slot_ref, ys_hbm, gate_ref, x_ref, lnw_ref, lnb_ref, o_ref, ybuf, sem):
    i = pl.program_id(0)
    c = SEQ_BLOCK

    def gather(slots, p):
        def issue(j, carry):
            for u in range(ISSUE_UNROLL):
                r = j * ISSUE_UNROLL + u
                for k in range(TOP_K):
                    src = _tile_of(ys_hbm, slots[0, 0, k * c + r])
                    pltpu.make_async_copy(src, _tile_of(ybuf.at[p, k], r), sem.at[p]).start(priority=k % 2)
            return carry
        lax.fori_loop(0, c // ISSUE_UNROLL, issue, 0)

    @pl.when(i == 0)
    def _():
        gather(slot_ref, 0)

    @pl.when(i + 1 < pl.num_programs(0))
    def _():
        gather(next_slot_ref, (i + 1) % 2)

    p = i % 2
    for k in range(TOP_K):
        pltpu.make_async_copy(ys_hbm.at[pl.ds(0, ybuf.shape[2])], ybuf.at[p, k], sem.at[p]).wait()
    gates = gate_ref[...]
    ys = [_load_row_tiles(ybuf.at[p, k]) * gates[:, k:k + 1] for k in range(TOP_K)]
    y = (ys[0] + ys[1]) + (ys[2] + ys[3])
    o_ref[...] = _layer_norm(DEEPNORM_ALPHA * _load_row_tiles(x_ref) + y, lnw_ref[...], lnb_ref[...])


def _combine(ys, slots, gates, x2, ln_w, ln_b):
    t = x2.shape[0] // ROW_TILE[0]
    c = SEQ_BLOCK
    n = t // c
    slot_table = slots.reshape(n, 1, c * TOP_K)
    return pl.pallas_call(
        _combine_kernel,
        grid=(n,),
        in_specs=[
            pl.BlockSpec((1, 1, c * TOP_K), lambda i: (i, 0, 0), memory_space=pltpu.SMEM),
            pl.BlockSpec((1, 1, c * TOP_K), lambda i: (jnp.minimum(i + 1, n - 1), 0, 0), memory_space=pltpu.SMEM),
            pl.BlockSpec(memory_space=pl.ANY),
            pl.BlockSpec((c, TOP_K), lambda i: (i, 0)),
            _tiled_spec(c, lambda i: (i, 0)),
            pl.BlockSpec((1, D_MODEL), lambda i: (0, 0)),
            pl.BlockSpec((1, D_MODEL), lambda i: (0, 0)),
        ],
        out_specs=pl.BlockSpec((c, D_MODEL), lambda i: (i, 0)),
        out_shape=jax.ShapeDtypeStruct((t, D_MODEL), F32),
        scratch_shapes=[pltpu.VMEM((2, TOP_K) + _tiled(c), F32), pltpu.SemaphoreType.DMA((2,))],
        compiler_params=pltpu.CompilerParams(dimension_semantics=("arbitrary",), vmem_limit_bytes=VMEM_LIMIT),
        name="combine",
    )(slot_table, slot_table, ys, gates, x2, ln_w.reshape(1, -1), ln_b.reshape(1, -1))


def _slot_tables(counts, packed, n_blocks, shift):
    counts = counts.reshape(-1)
    nblk = (counts + MOE_BLOCK - 1) // MOE_BLOCK
    ids = jnp.arange(N_EXPERTS)
    lower = ids[None, :] <= ids[:, None]
    bend = jnp.sum(jnp.where(lower, nblk[None, :], 0), axis=1)
    n_used = bend[N_EXPERTS - 1]
    pend = bend * MOE_BLOCK
    pstart = pend - nblk * MOE_BLOCK
    fill = pstart + counts
    order = jnp.sum(jnp.where(lower, (nblk > 0)[None, :], False), axis=1) - 1
    later = (ids[None, :] > ids[:, None]) & (nblk > 0)[None, :]
    nxt_e = jnp.min(jnp.where(later, ids[None, :], N_EXPERTS), axis=1)
    nxt_e = jnp.where(nxt_e == N_EXPERTS, -1, nxt_e)
    b = jnp.minimum(jnp.arange(n_blocks), n_used - 1)
    hit = bend[None, :] <= b[:, None]
    block_e = jnp.sum(hit.astype(jnp.int32), axis=1)
    of_block = lambda per_expert: jnp.sum(jnp.where(ids[None, :] == block_e[:, None], per_expert[None, :], 0), axis=1)
    first = (b == of_block(bend - nblk)).astype(jnp.int32)
    expert = lax.shift_right_logical(packed, shift)
    rank = packed & ((1 << shift) - 1)
    slots = jnp.sum(jnp.where(expert[..., None] == ids, pstart, 0), axis=-1) + rank
    i32 = lambda a: a.astype(jnp.int32)
    plan = (i32(block_e), first, i32(of_block(nxt_e)), i32(of_block(order & 1)), i32(n_used.reshape(1)))
    return i32(slots), i32(fill), i32(pend), plan


def kernel(x, mem, w_in, lb_logits, hgrn_norm_w, w_pool, pool_scale, w_out, ln1_w, ln1_b, w_xq, w_xk, w_xv, w_xo,
           ln2_w, ln2_b, w_router, b_router, w1, b1, w2, b2, ln3_w, ln3_b):
    bsz, seq, d = x.shape
    assert bsz == 1 and d == D_MODEL and seq % SEQ_BLOCK == 0 and w_in.shape[0] == 1
    xt = x.reshape(seq, d)
    x1 = _mixer(xt, w_in[0], lb_logits, hgrn_norm_w[0], w_pool[0], pool_scale[0], w_out[0], ln1_w[0], ln1_b[0])
    kmem, vmem = _kv_proj(mem[0], w_xk[0], w_xv[0])
    x2, packed, gates, counts = _xattn_router(x1, kmem, vmem, w_xq[0], w_xo[0], ln2_w[0], ln2_b[0],
                                              w_router[0], b_router[0])
    n_blocks = -(-(seq * TOP_K + N_EXPERTS * (MOE_BLOCK - 1)) // MOE_BLOCK)
    slots, fill, pend, plan = _slot_tables(counts, packed, n_blocks, _rank_bits(seq))
    xs = _dispatch(x2, slots, fill, pend, plan[-1], n_blocks)
    ys = _experts(xs, plan, w1[0], b1[0], w2[0], b2[0])
    gates_tk = gates.transpose(0, 2, 1).reshape(seq, TOP_K)
    out = _combine(ys, slots, gates_tk, x2, ln3_w[0], ln3_b[0])
    return out.reshape(bsz, seq, d)
```

```python
import functools

import numpy as np
import jax
import jax.numpy as jnp
from jax import lax
from jax.experimental import pallas as pl
from jax.experimental.pallas import tpu as pltpu

F32 = jnp.float32
BF16 = jnp.bfloat16

D_MODEL = 1024
HGRN_WIDTH = 512
HGRN_HEADS = 4
HEAD_DIM = 128
POOL_WINDOWS = (2, 4, 8, 16)
POOL_GROUP = 128
POOL_CARRY = 16
IN_PROJ_WIDTH = 4 * HGRN_WIDTH + 512
XATTN_HEADS = 4
XATTN_HEAD_DIM = 256
N_EXPERTS = 32
TOP_K = 4
D_EXPERT = 1024
SWIGLU_LIMIT = 7.0
SWIGLU_ALPHA = 1.702
MOE_BLOCK = 256
DEEPNORM_ALPHA = 2.0 ** 0.25
LN_EPS = 1e-5
RMS_EPS = 1e-6

SEQ_BLOCK = 256
STEP_BLOCKS = 4
EXPERT_STEP_BLOCKS = 4
DISPATCH_BLOCK = 1024
N_LEVELS = 8
MXU_LEVELS = 3
ISSUE_UNROLL = 2
ROW_TILE = (8, 128)
VMEM_LIMIT = 48 * 1024 * 1024
EXPERT_VMEM_LIMIT = 56 * 1024 * 1024


def _rank_bits(t):
    return max(int(t - 1).bit_length(), 1)


def _step_rows(t):
    rows = STEP_BLOCKS * SEQ_BLOCK
    return rows if t % rows == 0 else SEQ_BLOCK


def _layer_norm(y, w, b):
    mu = jnp.mean(y, axis=-1, keepdims=True)
    yc = y - mu
    var = jnp.mean(yc * yc, axis=-1, keepdims=True)
    return yc * lax.rsqrt(var + LN_EPS) * w + b


def _tiled(n):
    return (n * ROW_TILE[0], ROW_TILE[1])


def _tiled_spec(n, index_map):
    return pl.BlockSpec(_tiled(n), index_map)


def _tile_of(ref, r):
    return ref.at[pl.ds(pl.multiple_of(r * ROW_TILE[0], ROW_TILE[0]), ROW_TILE[0])]


def _store_row_tiles(ref, val):
    s, l = ROW_TILE
    for j in range(s):
        ref[pl.ds(j, val.shape[0], stride=s), :] = val[:, j * l:(j + 1) * l]


def _load_row_tiles(ref):
    s, _ = ROW_TILE
    return jnp.concatenate([ref[pl.ds(j, ref.shape[0] // s, stride=s), :] for j in range(s)], axis=-1)


def _dot(a, b):
    return jnp.dot(a, b, preferred_element_type=F32)


def _dot_nt(a, b):
    return lax.dot_general(a, b, (((1,), (1,)), ((), ())), preferred_element_type=F32)


def _decay_tables():
    c = SEQ_BLOCK
    r = np.arange(c)[:, None]
    u = np.arange(c)[None, :]
    mats = [(u <= r)]
    for l in range(MXU_LEVELS):
        h = 1 << l
        mid = (r // (2 * h)) * (2 * h) + h
        upper = (r >= mid) & (u >= mid) & (u <= r)
        lower = (r < mid) & (u > r) & (u < mid)
        mats.append(upper | lower)
    mall = np.concatenate(mats, axis=0).astype(np.float32)
    x = r ^ u
    lvl = np.where(u < r, np.floor(np.log2(np.maximum(x, 1))).astype(np.int32), -1).astype(np.int32)
    strict_lower = (u < r).astype(np.float32)
    return mall, lvl, strict_lower


def _level_gap(negb, l):
    half = 1 << l
    parts = []
    for base in range(0, negb.shape[0], 2 * half):
        ref_row = negb[base + half - 1:base + half, :]
        parts.append(jnp.abs(negb[base:base + 2 * half, :] - ref_row))
    return jnp.concatenate(parts, axis=0) if len(parts) > 1 else parts[0]


def _mixer_kernel(x_ref, *refs):
    o_ref, state_ref, carry_ref = refs[-5], refs[-4], refs[-3]
    step = pl.program_id(0)
    c = SEQ_BLOCK

    @pl.when(step == 0)
    def _():
        state_ref[...] = jnp.zeros_like(state_ref)
        carry_ref[...] = jnp.zeros_like(carry_ref)

    def block(sb, carry):
        rows = pl.ds(pl.multiple_of(sb * c, c), c)
        _mixer_block(step * (x_ref.shape[0] // c) + sb, x_ref.at[rows], *refs[:-5], o_ref.at[rows], *refs[-4:])
        return carry

    lax.fori_loop(0, x_ref.shape[0] // c, block, 0)


def _mixer_block(i, x_ref, win_ref, lbl_ref, nw_ref, wpool_ref, pscale_ref, wout_ref, lnw_ref, lnb_ref,
                 mall_ref, lvl_ref, o_ref, state_ref, carry_ref, ext_ref, mix_ref):
    c = SEQ_BLOCK
    x = x_ref[...]
    proj = _dot(x.astype(BF16), win_ref[...])
    w = HGRN_WIDTH
    q = proj[:, 0:w]
    z = proj[:, w:2 * w]
    v = proj[:, 2 * w:3 * w]
    g = proj[:, 3 * w:4 * w]
    p = proj[:, 4 * w:]

    ll = lbl_ref[...]
    ee = jnp.exp(ll - jnp.max(ll, axis=0, keepdims=True))
    lb = ee[0:1] / jnp.sum(ee, axis=0, keepdims=True)

    f = lb + (1.0 - lb) * (1.0 / (1.0 + jnp.exp(-z)))
    nlf = -jnp.log(f)
    key = (1.0 - lb) * (1.0 / (1.0 + jnp.exp(z)))

    hi = nlf.astype(BF16)
    lo = (nlf - hi.astype(F32)).astype(BF16)
    mall = mall_ref[...]
    cum = _dot(mall, hi) + _dot(mall, lo)
    lvl = lvl_ref[...]

    for h in range(HGRN_HEADS):
        hs = slice(h * HEAD_DIM, (h + 1) * HEAD_DIM)
        qh, kh, vh, gh = q[:, hs], key[:, hs], v[:, hs], g[:, hs]
        vb = vh.astype(BF16)
        negb = cum[0:c, hs]
        scores = jnp.zeros((c, c), F32)
        for l in range(N_LEVELS):
            gap = cum[(l + 1) * c:(l + 2) * c, hs] if l < MXU_LEVELS else _level_gap(negb, l)
            e = jnp.exp(-gap)
            pl_ = _dot_nt((qh * e).astype(BF16), (kh * e).astype(BF16))
            scores = jnp.where(lvl == l, pl_, scores)
        diag = jnp.sum(qh * kh, axis=-1, keepdims=True)
        st = state_ref[h]
        o = (_dot(scores.astype(BF16), vb) + diag * vh
             + _dot_nt((qh * jnp.exp(-negb)).astype(BF16), st.astype(BF16)))
        negb_last = negb[c - 1:c, :]
        kd = (kh * jnp.exp(negb - negb_last)).astype(BF16)
        state_ref[h] = st * jnp.exp(-negb_last) + _dot(vh.T.astype(BF16), kd)
        o = o * lax.rsqrt(jnp.mean(o * o, axis=-1, keepdims=True) + RMS_EPS) * nw_ref[:, hs]
        gate = gh * (1.0 / (1.0 + jnp.exp(-gh)))
        mix_ref[:, hs] = (o * gate).astype(BF16)

    ext_ref[0:POOL_CARRY, :] = carry_ref[...]
    ext_ref[POOL_CARRY:POOL_CARRY + c, :] = p
    carry_ref[...] = p[c - POOL_CARRY:c, :]
    pos = i * c + lax.broadcasted_iota(jnp.int32, (c, 1), 0)
    for gi, win in enumerate(POOL_WINDOWS):
        gs = slice(gi * POOL_GROUP, (gi + 1) * POOL_GROUP)
        pg = p[:, gs]
        tot = pg
        for j in range(1, win):
            tot = tot + ext_ref[POOL_CARRY - j:POOL_CARRY - j + c, gs]
        cnt = jnp.minimum(pos + 1, win).astype(F32)
        pooled = _dot((tot / cnt - pg).astype(BF16), wpool_ref[gi])
        mix_ref[:, w + gi * POOL_GROUP:w + (gi + 1) * POOL_GROUP] = (pooled * pscale_ref[:, gs]).astype(BF16)

    y = DEEPNORM_ALPHA * x + _dot(mix_ref[...], wout_ref[...])
    o_ref[...] = _layer_norm(y, lnw_ref[...], lnb_ref[...])


def _mixer(x, w_in, lb_logits, norm_w, w_pool, pool_scale, w_out, ln_w, ln_b):
    t = x.shape[0]
    c = SEQ_BLOCK
    mall, lvl, _ = _decay_tables()
    full = lambda shape: pl.BlockSpec(shape, lambda i: (0,) * len(shape))
    step = _step_rows(t)
    return pl.pallas_call(
        _mixer_kernel,
        grid=(t // step,),
        in_specs=[
            pl.BlockSpec((step, D_MODEL), lambda i: (i, 0)),
            full((D_MODEL, IN_PROJ_WIDTH)),
            full(lb_logits.shape),
            full((1, HGRN_WIDTH)),
            full((len(POOL_WINDOWS), POOL_GROUP, POOL_GROUP)),
            full((1, 512)),
            full((D_MODEL, D_MODEL)),
            full((1, D_MODEL)),
            full((1, D_MODEL)),
            full(mall.shape),
            full(lvl.shape),
        ],
        out_specs=pl.BlockSpec((step, D_MODEL), lambda i: (i, 0)),
        out_shape=jax.ShapeDtypeStruct((t, D_MODEL), F32),
        scratch_shapes=[
            pltpu.VMEM((HGRN_HEADS, HEAD_DIM, HEAD_DIM), F32),
            pltpu.VMEM((POOL_CARRY, 512), F32),
            pltpu.VMEM((POOL_CARRY + c, 512), F32),
            pltpu.VMEM((c, D_MODEL), BF16),
        ],
        compiler_params=pltpu.CompilerParams(dimension_semantics=("arbitrary",), vmem_limit_bytes=VMEM_LIMIT),
        name="mixer",
    )(x, w_in.astype(BF16), lb_logits, norm_w.reshape(1, -1), w_pool.astype(BF16), pool_scale.reshape(1, -1),
      w_out.astype(BF16), ln_w.reshape(1, -1), ln_b.reshape(1, -1), jnp.asarray(mall, BF16), jnp.asarray(lvl))


def _kv_kernel(mem_ref, wk_ref, wv_ref, k_ref, v_ref):
    m = mem_ref[...].astype(BF16)
    k_ref[...] = _dot(m, wk_ref[...]).astype(BF16)
    v_ref[...] = _dot(m, wv_ref[...]).astype(BF16)


def _kv_proj(mem, wk, wv):
    n = mem.shape[0]
    return pl.pallas_call(
        _kv_kernel,
        out_shape=(jax.ShapeDtypeStruct((n, D_MODEL), BF16), jax.ShapeDtypeStruct((n, D_MODEL), BF16)),
        compiler_params=pltpu.CompilerParams(vmem_limit_bytes=VMEM_LIMIT),
        name="kv_proj",
    )(mem, wk.astype(BF16), wv.astype(BF16))


def _split3(a):
    a0 = a.astype(BF16)
    r1 = a - a0.astype(F32)
    a1 = r1.astype(BF16)
    a2 = (r1 - a1.astype(F32)).astype(BF16)
    return a0, a1, a2


def _xattn_kernel(x_ref, wq_ref, k_ref, v_ref, wo_ref, lnw_ref, lnb_ref, wrt_ref, br_ref, ut_ref,
                  x2_ref, dest_ref, gate_ref, cnt_ref, att_ref, run_ref, *, shift):
    i = pl.program_id(0)
    c = SEQ_BLOCK
    n = N_EXPERTS

    @pl.when(i == 0)
    def _():
        run_ref[...] = jnp.zeros_like(run_ref)

    scale = XATTN_HEAD_DIM ** -0.5

    def block(sb, carry):
        w0, w1, w2 = _split3(wrt_ref[...])
        eid = lax.broadcasted_iota(jnp.int32, (n, c), 0)
        run = run_ref[...]
        rows = pl.ds(pl.multiple_of(sb * c, c), c)
        x = x_ref[rows, :]
        q = _dot(x.astype(BF16), wq_ref[...])
        for h in range(XATTN_HEADS):
            hs = slice(h * XATTN_HEAD_DIM, (h + 1) * XATTN_HEAD_DIM)
            s = _dot_nt(q[:, hs].astype(BF16), k_ref[:, hs]) * scale
            s = s - jnp.max(s, axis=-1, keepdims=True)
            e = jnp.exp(s)
            pr = e / jnp.sum(e, axis=-1, keepdims=True)
            att_ref[rows, hs] = _dot(pr.astype(BF16), v_ref[:, hs]).astype(BF16)
        y = DEEPNORM_ALPHA * x + _dot(att_ref[rows, :], wo_ref[...])
        x2 = _layer_norm(y, lnw_ref[...], lnb_ref[...])
        tile_rows = c * ROW_TILE[0]
        _store_row_tiles(x2_ref.at[pl.ds(pl.multiple_of(sb * tile_rows, tile_rows), tile_rows)], x2)

        a0, a1, a2 = _split3(x2)
        pa = _dot_nt(jnp.concatenate([w0, w1, w2], axis=0), a0)
        pb = _dot_nt(jnp.concatenate([w0, w1], axis=0), a1)
        pc = _dot_nt(w0, a2)
        logits = ((pc + pa[2 * n:3 * n] + pb[n:2 * n]) + (pb[0:n] + pa[n:2 * n]) + pa[0:n]) + br_ref[...]

        work = logits
        sels, vals, idxs = [], [], []
        for k in range(TOP_K):
            m = jnp.max(work, axis=0, keepdims=True)
            idx = jnp.min(jnp.where(work == m, eid, n), axis=0, keepdims=True)
            sel = eid == idx
            sels.append(sel)
            vals.append(m)
            idxs.append(idx)
            work = jnp.where(sel, -jnp.inf, work)
        es = [jnp.exp(vk - vals[0]) for vk in vals]
        denom = es[0] + es[1] + es[2] + es[3]
        gate_ref[sb] = jnp.concatenate([ek / denom for ek in es], axis=0)

        onehot = jnp.zeros((n, c), F32)
        for sel in sels:
            onehot = jnp.where(sel, 1.0, onehot)
        before = _dot(onehot.astype(BF16), ut_ref[...]) + run
        ranks = [jnp.sum(jnp.where(sel, before, 0.0), axis=0, keepdims=True).astype(jnp.int32) for sel in sels]
        run_ref[...] = run + jnp.sum(onehot, axis=1, keepdims=True)

        dest_ref[sb] = jnp.concatenate([idx * (1 << shift) + rk for idx, rk in zip(idxs, ranks)], axis=0)
        return carry

    lax.fori_loop(0, x_ref.shape[0] // c, block, 0)
    cnt_ref[...] = run_ref[...].astype(jnp.int32)


def _xattn_router(x1, kmem, vmem, wq, wo, ln_w, ln_b, w_r, b_r):
    t = x1.shape[0]
    c = SEQ_BLOCK
    n_mem = kmem.shape[0]
    _, _, strict_lower = _decay_tables()
    full = lambda shape: pl.BlockSpec(shape, lambda i: (0,) * len(shape))
    step = _step_rows(t)
    per_k = pl.BlockSpec((step // c, TOP_K, c), lambda i: (i, 0, 0))
    return pl.pallas_call(
        functools.partial(_xattn_kernel, shift=_rank_bits(t)),
        grid=(t // step,),
        in_specs=[
            pl.BlockSpec((step, D_MODEL), lambda i: (i, 0)),
            full((D_MODEL, D_MODEL)),
            full((n_mem, D_MODEL)),
            full((n_mem, D_MODEL)),
            full((D_MODEL, D_MODEL)),
            full((1, D_MODEL)),
            full((1, D_MODEL)),
            full((N_EXPERTS, D_MODEL)),
            full((N_EXPERTS, 1)),
            full((c, c)),
        ],
        out_specs=[_tiled_spec(step, lambda i: (i, 0)), per_k, per_k, full((N_EXPERTS, 1))],
        out_shape=[
            jax.ShapeDtypeStruct(_tiled(t), F32),
            jax.ShapeDtypeStruct((t // c, TOP_K, c), jnp.int32),
            jax.ShapeDtypeStruct((t // c, TOP_K, c), F32),
            jax.ShapeDtypeStruct((N_EXPERTS, 1), jnp.int32),
        ],
        scratch_shapes=[pltpu.VMEM((step, D_MODEL), BF16), pltpu.VMEM((N_EXPERTS, 1), F32)],
        compiler_params=pltpu.CompilerParams(dimension_semantics=("arbitrary",), vmem_limit_bytes=VMEM_LIMIT),
        name="xattn_router",
    )(x1, wq.astype(BF16), kmem, vmem, wo.astype(BF16), ln_w.reshape(1, -1), ln_b.reshape(1, -1),
      w_r.T, b_r.reshape(-1, 1), jnp.asarray(strict_lower.T, BF16))


TAIL_RUNS = (128, 64, 32, 16, 8, 4, 2, 1)


def _dispatch_kernel(fill_ref, pend_ref, nused_ref, slot_ref, x_ref, xs_hbm, zero_ref, sem, zsem, *, n_blocks):
    i = pl.program_id(0)
    c = SEQ_BLOCK

    @pl.when(i == 0)
    def _():
        zero_ref[...] = jnp.zeros_like(zero_ref)

        sub = ROW_TILE[0]

        def rows(ref, first, n):
            return ref.at[pl.ds(pl.multiple_of(first * sub, sub), n * sub)]

        def tail(e, act):
            row = fill_ref[e]
            pad = pend_ref[e] - row
            for run in TAIL_RUNS:
                hit = (pad & run) != 0

                @pl.when(hit)
                def _():
                    act(pltpu.make_async_copy(rows(zero_ref, 0, run), rows(xs_hbm, row, run), zsem))
                row = row + jnp.where(hit, run, 0)

        def spare(b, act):
            act(pltpu.make_async_copy(zero_ref, rows(xs_hbm, b * MOE_BLOCK, MOE_BLOCK), zsem))

        for act in (lambda cp: cp.start(), lambda cp: cp.wait()):
            lax.fori_loop(0, N_EXPERTS, lambda e, carry: (tail(e, act), carry)[1], 0)
            lax.fori_loop(nused_ref[0], n_blocks, lambda b, carry: (spare(b, act), carry)[1], 0)

    for q in range(x_ref.shape[0] // (c * ROW_TILE[0])):
        def issue(j, carry, q=q):
            for u in range(ISSUE_UNROLL):
                r = j * ISSUE_UNROLL + u
                for k in range(TOP_K):
                    dst = _tile_of(xs_hbm, slot_ref[0, 0, (q * TOP_K + k) * c + r])
                    pltpu.make_async_copy(_tile_of(x_ref, q * c + r), dst, sem).start(priority=k % 2)
            return carry
        lax.fori_loop(0, c // ISSUE_UNROLL, issue, 0)
    for k in range(TOP_K):
        pltpu.make_async_copy(x_ref, xs_hbm.at[pl.ds(0, x_ref.shape[0])], sem).wait()


def _dispatch(x2, slots, fill, pend, n_used, n_blocks):
    t = x2.shape[0] // ROW_TILE[0]
    c = min(DISPATCH_BLOCK, t)
    assert t % c == 0 and c % SEQ_BLOCK == 0
    grid_spec = pltpu.PrefetchScalarGridSpec(
        num_scalar_prefetch=3,
        grid=(t // c,),
        in_specs=[
            pl.BlockSpec((1, 1, c * TOP_K), lambda i, *_: (i, 0, 0), memory_space=pltpu.SMEM),
            _tiled_spec(c, lambda i, *_: (i, 0)),
        ],
        out_specs=pl.BlockSpec(memory_space=pl.ANY),
        scratch_shapes=[pltpu.VMEM(_tiled(MOE_BLOCK), F32), pltpu.SemaphoreType.DMA, pltpu.SemaphoreType.DMA],
    )
    return pl.pallas_call(
        functools.partial(_dispatch_kernel, n_blocks=n_blocks),
        grid_spec=grid_spec,
        out_shape=jax.ShapeDtypeStruct(_tiled(n_blocks * MOE_BLOCK), F32),
        compiler_params=pltpu.CompilerParams(dimension_semantics=("arbitrary",), vmem_limit_bytes=VMEM_LIMIT),
        name="dispatch",
    )(fill, pend, n_used, slots.reshape(t // c, 1, c * TOP_K), x2)


def _expert_kernel(be_ref, first_ref, nxt_ref, par_ref, nused_ref, x_ref, w1_hbm, b1_ref, w2_hbm, b2_ref, o_ref,
                   w1f, w2f, w1b, w2b, sem1, sem2):
    step = pl.program_id(0)
    rows = MOE_BLOCK * ROW_TILE[0]

    def fetch(e, p):
        return (pltpu.make_async_copy(w1_hbm.at[e], w1f.at[p], sem1.at[p]),
                pltpu.make_async_copy(w2_hbm.at[e], w2f.at[p], sem2.at[p]))

    @pl.when(step == 0)
    def _():
        for cp in fetch(be_ref[0], par_ref[0]):
            cp.start()

    def block(sb, carry):
        b = step * EXPERT_STEP_BLOCKS + sb
        x_blk = x_ref.at[pl.ds(pl.multiple_of(sb * rows, rows), rows)]
        o_blk = o_ref.at[pl.ds(pl.multiple_of(sb * rows, rows), rows)]

        @pl.when(b >= nused_ref[0])
        def _():
            o_blk[...] = jnp.zeros(o_blk.shape, o_blk.dtype)

        @pl.when(b < nused_ref[0])
        def _():
            e = be_ref[b]

            @pl.when(first_ref[b] == 1)
            def _():
                p = par_ref[b]
                for cp in fetch(e, p):
                    cp.wait()

                @pl.when(nxt_ref[b] >= 0)
                def _():
                    for cp in fetch(nxt_ref[b], 1 - p):
                        cp.start()

                w1b[...] = w1f[p].astype(BF16)
                w2b[...] = w2f[p].astype(BF16)

            h = _dot(_load_row_tiles(x_blk).astype(BF16), w1b[...]) + b1_ref[e]
            h_glu = jnp.minimum(h[:, :D_EXPERT], SWIGLU_LIMIT)
            h_lin = jnp.clip(h[:, D_EXPERT:], -SWIGLU_LIMIT, SWIGLU_LIMIT)
            act = h_glu * (1.0 / (1.0 + jnp.exp(-SWIGLU_ALPHA * h_glu))) * (h_lin + 1.0)
            _store_row_tiles(o_blk, _dot(act.astype(BF16), w2b[...]) + b2_ref[e])
        return carry

    lax.fori_loop(0, EXPERT_STEP_BLOCKS, block, 0)


def _experts(xs, plan, w1, b1, w2, b2):
    block_e, first, nxt, par, n_used = plan
    n_blocks = block_e.shape[0]
    assert n_blocks % EXPERT_STEP_BLOCKS == 0
    rows = lambda s, *_: (s, 0)
    whole = lambda s, *_: (0, 0, 0)
    grid_spec = pltpu.PrefetchScalarGridSpec(
        num_scalar_prefetch=5,
        grid=(n_blocks // EXPERT_STEP_BLOCKS,),
        in_specs=[
            _tiled_spec(EXPERT_STEP_BLOCKS * MOE_BLOCK, rows),
            pl.BlockSpec(memory_space=pl.ANY),
            pl.BlockSpec((N_EXPERTS, 1, 2 * D_EXPERT), whole),
            pl.BlockSpec(memory_space=pl.ANY),
            pl.BlockSpec((N_EXPERTS, 1, D_MODEL), whole),
        ],
        out_specs=_tiled_spec(EXPERT_STEP_BLOCKS * MOE_BLOCK, rows),
        scratch_shapes=[
            pltpu.VMEM((2, D_MODEL, 2 * D_EXPERT), F32),
            pltpu.VMEM((2, D_EXPERT, D_MODEL), F32),
            pltpu.VMEM((D_MODEL, 2 * D_EXPERT), BF16),
            pltpu.VMEM((D_EXPERT, D_MODEL), BF16),
            pltpu.SemaphoreType.DMA((2,)),
            pltpu.SemaphoreType.DMA((2,)),
        ],
    )
    return pl.pallas_call(
        _expert_kernel,
        grid_spec=grid_spec,
        out_shape=jax.ShapeDtypeStruct(xs.shape, F32),
        compiler_params=pltpu.CompilerParams(dimension_semantics=("arbitrary",), vmem_limit_bytes=EXPERT_VMEM_LIMIT),
        name="experts",
    )(block_e, first, nxt, par, n_used, xs, w1, b1.reshape(N_EXPERTS, 1, -1), w2, b2.reshape(N_EXPERTS, 1, -1))


def _combine_kernel(slot_ref, next_slot_ref, ys_hbm, gate_ref, x_ref, lnw_ref, lnb_ref, o_ref, ybuf, sem):
    i = pl.program_id(0)
    c = SEQ_BLOCK

    def gather(slots, p):
        def issue(j, carry):
            for u in range(ISSUE_UNROLL):
                r = j * ISSUE_UNROLL + u
                for k in range(TOP_K):
                    src = _tile_of(ys_hbm, slots[0, 0, k * c + r])
                    pltpu.make_async_copy(src, _tile_of(ybuf.at[p, k], r), sem.at[p]).start(priority=k % 2)
            return carry
        lax.fori_loop(0, c // ISSUE_UNROLL, issue, 0)

    @pl.when(i == 0)
    def _():
        gather(slot_ref, 0)

    @pl.when(i + 1 < pl.num_programs(0))
    def _():
        gather(next_slot_ref, (i + 1) % 2)

    p = i % 2
    for k in range(TOP_K):
        pltpu.make_async_copy(ys_hbm.at[pl.ds(0, ybuf.shape[2])], ybuf.at[p, k], sem.at[p]).wait()
    gates = gate_ref[...]
    ys = [_load_row_tiles(ybuf.at[p, k]) * gates[:, k:k + 1] for k in range(TOP_K)]
    y = (ys[0] + ys[1]) + (ys[2] + ys[3])
    o_ref[...] = _layer_norm(DEEPNORM_ALPHA * _load_row_tiles(x_ref) + y, lnw_ref[...], lnb_ref[...])


def _combine(ys, slots, gates, x2, ln_w, ln_b):
    t = x2.shape[0] // ROW_TILE[0]
    c = SEQ_BLOCK
    n = t // c
    slot_table = slots.reshape(n, 1, c * TOP_K)
    return pl.pallas_call(
        _combine_kernel,
        grid=(n,),
        in_specs=[
            pl.BlockSpec((1, 1, c * TOP_K), lambda i: (i, 0, 0), memory_space=pltpu.SMEM),
            pl.BlockSpec((1, 1, c * TOP_K), lambda i: (jnp.minimum(i + 1, n - 1), 0, 0), memory_space=pltpu.SMEM),
            pl.BlockSpec(memory_space=pl.ANY),
            pl.BlockSpec((c, TOP_K), lambda i: (i, 0)),
            _tiled_spec(c, lambda i: (i, 0)),
            pl.BlockSpec((1, D_MODEL), lambda i: (0, 0)),
            pl.BlockSpec((1, D_MODEL), lambda i: (0, 0)),
        ],
        out_specs=pl.BlockSpec((c, D_MODEL), lambda i: (i, 0)),
        out_shape=jax.ShapeDtypeStruct((t, D_MODEL), F32),
        scratch_shapes=[pltpu.VMEM((2, TOP_K) + _tiled(c), F32), pltpu.SemaphoreType.DMA((2,))],
        compiler_params=pltpu.CompilerParams(dimension_semantics=("arbitrary",), vmem_limit_bytes=VMEM_LIMIT),
        name="combine",
    )(slot_table, slot_table, ys, gates, x2, ln_w.reshape(1, -1), ln_b.reshape(1, -1))


def _slot_tables(counts, packed, n_blocks, shift):
    counts = counts.reshape(-1)
    nblk = (counts + MOE_BLOCK - 1) // MOE_BLOCK
    ids = jnp.arange(N_EXPERTS)
    lower = ids[None, :] <= ids[:, None]
    bend = jnp.sum(jnp.where(lower, nblk[None, :], 0), axis=1)
    n_used = bend[N_EXPERTS - 1]
    pend = bend * MOE_BLOCK
    pstart = pend - nblk * MOE_BLOCK
    fill = pstart + counts
    order = jnp.sum(jnp.where(lower, (nblk > 0)[None, :], False), axis=1) - 1
    later = (ids[None, :] > ids[:, None]) & (nblk > 0)[None, :]
    nxt_e = jnp.min(jnp.where(later, ids[None, :], N_EXPERTS), axis=1)
    nxt_e = jnp.where(nxt_e == N_EXPERTS, -1, nxt_e)
    b = jnp.minimum(jnp.arange(n_blocks), n_used - 1)
    hit = bend[None, :] <= b[:, None]
    block_e = jnp.sum(hit.astype(jnp.int32), axis=1)
    of_block = lambda per_expert: jnp.sum(jnp.where(ids[None, :] == block_e[:, None], per_expert[None, :], 0), axis=1)
    first = (b == of_block(bend - nblk)).astype(jnp.int32)
    expert = lax.shift_right_logical(packed, shift)
    rank = packed & ((1 << shift) - 1)
    slots = jnp.sum(jnp.where(expert[..., None] == ids, pstart, 0), axis=-1) + rank
    i32 = lambda a: a.astype(jnp.int32)
    plan = (i32(block_e), first, i32(of_block(nxt_e)), i32(of_block(order & 1)), i32(n_used.reshape(1)))
    return i32(slots), i32(fill), i32(pend), plan


def kernel(x, mem, w_in, lb_logits, hgrn_norm_w, w_pool, pool_scale, w_out, ln1_w, ln1_b, w_xq, w_xk, w_xv, w_xo,
           ln2_w, ln2_b, w_router, b_router, w1, b1, w2, b2, ln3_w, ln3_b):
    bsz, seq, d = x.shape
    assert bsz == 1 and d == D_MODEL and seq % SEQ_BLOCK == 0 and w_in.shape[0] == 1
    xt = x.reshape(seq, d)
    x1 = _mixer(xt, w_in[0], lb_logits, hgrn_norm_w[0], w_pool[0], pool_scale[0], w_out[0], ln1_w[0], ln1_b[0])
    kmem, vmem = _kv_proj(mem[0], w_xk[0], w_xv[0])
    x2, packed, gates, counts = _xattn_router(x1, kmem, vmem, w_xq[0], w_xo[0], ln2_w[0], ln2_b[0],
                                              w_router[0], b_router[0])
    n_blocks = -(-(seq * TOP_K + N_EXPERTS * (MOE_BLOCK - 1)) // MOE_BLOCK)
    slots, fill, pend, plan = _slot_tables(counts, packed, n_blocks, _rank_bits(seq))
    xs = _dispatch(x2, slots, fill, pend, plan[-1], n_blocks)
    ys = _experts(xs, plan, w1[0], b1[0], w2[0], b2[0])
    gates_tk = gates.transpose(0, 2, 1).reshape(seq, TOP_K)
    out = _combine(ys, slots, gates_tk, x2, ln3_w[0], ln3_b[0])
    return out.reshape(bsz, seq, d)
```

```python
import functools

import numpy as np
import jax
import jax.numpy as jnp
from jax import lax
from jax.experimental import pallas as pl
from jax.experimental.pallas import tpu as pltpu

F32 = jnp.float32
BF16 = jnp.bfloat16

D_MODEL = 1024
HGRN_WIDTH = 512
HGRN_HEADS = 4
HEAD_DIM = 128
POOL_WINDOWS = (2, 4, 8, 16)
POOL_GROUP = 128
POOL_CARRY = 16
IN_PROJ_WIDTH = 4 * HGRN_WIDTH + 512
XATTN_HEADS = 4
XATTN_HEAD_DIM = 256
N_EXPERTS = 32
TOP_K = 4
D_EXPERT = 1024
SWIGLU_LIMIT = 7.0
SWIGLU_ALPHA = 1.702
MOE_BLOCK = 256
DEEPNORM_ALPHA = 2.0 ** 0.25
LN_EPS = 1e-5
RMS_EPS = 1e-6

SEQ_BLOCK = 256
STEP_BLOCKS = 4
EXPERT_STEP_BLOCKS = 4
DISPATCH_BLOCK = 1024
N_LEVELS = 8
MXU_LEVELS = 3
ISSUE_UNROLL = 2
ROW_TILE = (8, 128)
VMEM_LIMIT = 48 * 1024 * 1024
EXPERT_VMEM_LIMIT = 56 * 1024 * 1024


def _rank_bits(t):
    return max(int(t - 1).bit_length(), 1)


def _step_rows(t):
    rows = STEP_BLOCKS * SEQ_BLOCK
    return rows if t % rows == 0 else SEQ_BLOCK


def _layer_norm(y, w, b):
    mu = jnp.mean(y, axis=-1, keepdims=True)
    yc = y - mu
    var = jnp.mean(yc * yc, axis=-1, keepdims=True)
    return yc * lax.rsqrt(var + LN_EPS) * w + b


def _tiled(n):
    return (n * ROW_TILE[0], ROW_TILE[1])


def _tiled_spec(n, index_map):
    return pl.BlockSpec(_tiled(n), index_map)


def _tile_of(ref, r):
    return ref.at[pl.ds(pl.multiple_of(r * ROW_TILE[0], ROW_TILE[0]), ROW_TILE[0])]


def _store_row_tiles(ref, val):
    s, l = ROW_TILE
    for j in range(s):
        ref[pl.ds(j, val.shape[0], stride=s), :] = val[:, j * l:(j + 1) * l]


def _load_row_tiles(ref):
    s, _ = ROW_TILE
    return jnp.concatenate([ref[pl.ds(j, ref.shape[0] // s, stride=s), :] for j in range(s)], axis=-1)


def _dot(a, b):
    return jnp.dot(a, b, preferred_element_type=F32)


def _dot_nt(a, b):
    return lax.dot_general(a, b, (((1,), (1,)), ((), ())), preferred_element_type=F32)


def _decay_tables():
    c = SEQ_BLOCK
    r = np.arange(c)[:, None]
    u = np.arange(c)[None, :]
    mats = [(u <= r)]
    for l in range(MXU_LEVELS):
        h = 1 << l
        mid = (r // (2 * h)) * (2 * h) + h
        upper = (r >= mid) & (u >= mid) & (u <= r)
        lower = (r < mid) & (u > r) & (u < mid)
        mats.append(upper | lower)
    mall = np.concatenate(mats, axis=0).astype(np.float32)
    x = r ^ u
    lvl = np.where(u < r, np.floor(np.log2(np.maximum(x, 1))).astype(np.int32), -1).astype(np.int32)
    strict_lower = (u < r).astype(np.float32)
    return mall, lvl, strict_lower


def _level_gap(negb, l):
    half = 1 << l
    parts = []
    for base in range(0, negb.shape[0], 2 * half):
        ref_row = negb[base + half - 1:base + half, :]
        parts.append(jnp.abs(negb[base:base + 2 * half, :] - ref_row))
    return jnp.concatenate(parts, axis=0) if len(parts) > 1 else parts[0]


def _mixer_kernel(x_ref, *refs):
    o_ref, state_ref, carry_ref = refs[-5], refs[-4], refs[-3]
    step = pl.program_id(0)
    c = SEQ_BLOCK

    @pl.when(step == 0)
    def _():
        state_ref[...] = jnp.zeros_like(state_ref)
        carry_ref[...] = jnp.zeros_like(carry_ref)

    def block(sb, carry):
        rows = pl.ds(pl.multiple_of(sb * c, c), c)
        _mixer_block(step * (x_ref.shape[0] // c) + sb, x_ref.at[rows], *refs[:-5], o_ref.at[rows], *refs[-4:])
        return carry

    lax.fori_loop(0, x_ref.shape[0] // c, block, 0)


def _mixer_block(i, x_ref, win_ref, lbl_ref, nw_ref, wpool_ref, pscale_ref, wout_ref, lnw_ref, lnb_ref,
                 mall_ref, lvl_ref, o_ref, state_ref, carry_ref, ext_ref, mix_ref):
    c = SEQ_BLOCK
    x = x_ref[...]
    proj = _dot(x.astype(BF16), win_ref[...])
    w = HGRN_WIDTH
    q = proj[:, 0:w]
    z = proj[:, w:2 * w]
    v = proj[:, 2 * w:3 * w]
    g = proj[:, 3 * w:4 * w]
    p = proj[:, 4 * w:]

    ll = lbl_ref[...]
    ee = jnp.exp(ll - jnp.max(ll, axis=0, keepdims=True))
    lb = ee[0:1] / jnp.sum(ee, axis=0, keepdims=True)

    f = lb + (1.0 - lb) * (1.0 / (1.0 + jnp.exp(-z)))
    nlf = -jnp.log(f)
    key = (1.0 - lb) * (1.0 / (1.0 + jnp.exp(z)))

    hi = nlf.astype(BF16)
    lo = (nlf - hi.astype(F32)).astype(BF16)
    mall = mall_ref[...]
    cum = _dot(mall, hi) + _dot(mall, lo)
    lvl = lvl_ref[...]

    for h in range(HGRN_HEADS):
        hs = slice(h * HEAD_DIM, (h + 1) * HEAD_DIM)
        qh, kh, vh, gh = q[:, hs], key[:, hs], v[:, hs], g[:, hs]
        vb = vh.astype(BF16)
        negb = cum[0:c, hs]
        scores = jnp.zeros((c, c), F32)
        for l in range(N_LEVELS):
            gap = cum[(l + 1) * c:(l + 2) * c, hs] if l < MXU_LEVELS else _level_gap(negb, l)
            e = jnp.exp(-gap)
            pl_ = _dot_nt((qh * e).astype(BF16), (kh * e).astype(BF16))
            scores = jnp.where(lvl == l, pl_, scores)
        diag = jnp.sum(qh * kh, axis=-1, keepdims=True)
        st = state_ref[h]
        o = (_dot(scores.astype(BF16), vb) + diag * vh
             + _dot_nt((qh * jnp.exp(-negb)).astype(BF16), st.astype(BF16)))
        negb_last = negb[c - 1:c, :]
        kd = (kh * jnp.exp(negb - negb_last)).astype(BF16)
        state_ref[h] = st * jnp.exp(-negb_last) + _dot(vh.T.astype(BF16), kd)
        o = o * lax.rsqrt(jnp.mean(o * o, axis=-1, keepdims=True) + RMS_EPS) * nw_ref[:, hs]
        gate = gh * (1.0 / (1.0 + jnp.exp(-gh)))
        mix_ref[:, hs] = (o * gate).astype(BF16)

    ext_ref[0:POOL_CARRY, :] = carry_ref[...]
    ext_ref[POOL_CARRY:POOL_CARRY + c, :] = p
    carry_ref[...] = p[c - POOL_CARRY:c, :]
    pos = i * c + lax.broadcasted_iota(jnp.int32, (c, 1), 0)
    for gi, win in enumerate(POOL_WINDOWS):
        gs = slice(gi * POOL_GROUP, (gi + 1) * POOL_GROUP)
        pg = p[:, gs]
        tot = pg
        for j in range(1, win):
            tot = tot + ext_ref[POOL_CARRY - j:POOL_CARRY - j + c, gs]
        cnt = jnp.minimum(pos + 1, win).astype(F32)
        pooled = _dot((tot / cnt - pg).astype(BF16), wpool_ref[gi])
        mix_ref[:, w + gi * POOL_GROUP:w + (gi + 1) * POOL_GROUP] = (pooled * pscale_ref[:, gs]).astype(BF16)

    y = DEEPNORM_ALPHA * x + _dot(mix_ref[...], wout_ref[...])
    o_ref[...] = _layer_norm(y, lnw_ref[...], lnb_ref[...])


def _mixer(x, w_in, lb_logits, norm_w, w_pool, pool_scale, w_out, ln_w, ln_b):
    t = x.shape[0]
    c = SEQ_BLOCK
    mall, lvl, _ = _decay_tables()
    full = lambda shape: pl.BlockSpec(shape, lambda i: (0,) * len(shape))
    step = _step_rows(t)
    return pl.pallas_call(
        _mixer_kernel,
        grid=(t // step,),
        in_specs=[
            pl.BlockSpec((step, D_MODEL), lambda i: (i, 0)),
            full((D_MODEL, IN_PROJ_WIDTH)),
            full(lb_logits.shape),
            full((1, HGRN_WIDTH)),
            full((len(POOL_WINDOWS), POOL_GROUP, POOL_GROUP)),
            full((1, 512)),
            full((D_MODEL, D_MODEL)),
            full((1, D_MODEL)),
            full((1, D_MODEL)),
            full(mall.shape),
            full(lvl.shape),
        ],
        out_specs=pl.BlockSpec((step, D_MODEL), lambda i: (i, 0)),
        out_shape=jax.ShapeDtypeStruct((t, D_MODEL), F32),
        scratch_shapes=[
            pltpu.VMEM((HGRN_HEADS, HEAD_DIM, HEAD_DIM), F32),
            pltpu.VMEM((POOL_CARRY, 512), F32),
            pltpu.VMEM((POOL_CARRY + c, 512), F32),
            pltpu.VMEM((c, D_MODEL), BF16),
        ],
        compiler_params=pltpu.CompilerParams(dimension_semantics=("arbitrary",), vmem_limit_bytes=VMEM_LIMIT),
        name="mixer",
    )(x, w_in.astype(BF16), lb_logits, norm_w.reshape(1, -1), w_pool.astype(BF16), pool_scale.reshape(1, -1),
      w_out.astype(BF16), ln_w.reshape(1, -1), ln_b.reshape(1, -1), jnp.asarray(mall, BF16), jnp.asarray(lvl))


def _kv_kernel(mem_ref, wk_ref, wv_ref, wq_ref, wo_ref, a_ref, b_ref):
    n_mem = mem_ref.shape[0]
    m = mem_ref[...].astype(BF16)
    k = _dot(m, wk_ref[...]).astype(BF16)
    v = _dot(m, wv_ref[...]).astype(BF16)
    scale = XATTN_HEAD_DIM ** -0.5
    for h in range(XATTN_HEADS):
        hs = slice(h * XATTN_HEAD_DIM, (h + 1) * XATTN_HEAD_DIM)
        ms = slice(h * n_mem, (h + 1) * n_mem)
        a_ref[:, ms] = (_dot_nt(wq_ref[:, hs], k[:, hs]) * scale).astype(BF16)
        b_ref[ms, :] = _dot(v[:, hs], wo_ref[hs, :]).astype(BF16)


def _kv_proj(mem, wk, wv, wq, wo):
    n = mem.shape[0]
    return pl.pallas_call(
        _kv_kernel,
        out_shape=(jax.ShapeDtypeStruct((D_MODEL, XATTN_HEADS * n), BF16),
                   jax.ShapeDtypeStruct((XATTN_HEADS * n, D_MODEL), BF16)),
        compiler_params=pltpu.CompilerParams(vmem_limit_bytes=VMEM_LIMIT),
        name="kv_proj",
    )(mem, wk.astype(BF16), wv.astype(BF16), wq.astype(BF16), wo.astype(BF16))


def _split3(a):
    a0 = a.astype(BF16)
    r1 = a - a0.astype(F32)
    a1 = r1.astype(BF16)
    a2 = (r1 - a1.astype(F32)).astype(BF16)
    return a0, a1, a2


def _xattn_kernel(x_ref, a_ref, b_ref, lnw_ref, lnb_ref, wrt_ref, br_ref, ut_ref,
                  x2_ref, dest_ref, gate_ref, cnt_ref, att_ref, run_ref, *, shift):
    i = pl.program_id(0)
    c = SEQ_BLOCK
    n = N_EXPERTS

    @pl.when(i == 0)
    def _():
        run_ref[...] = jnp.zeros_like(run_ref)

    def block(sb, carry):
        w0, w1, w2 = _split3(wrt_ref[...])
        eid = lax.broadcasted_iota(jnp.int32, (n, c), 0)
        run = run_ref[...]
        rows = pl.ds(pl.multiple_of(sb * c, c), c)
        x = x_ref[rows, :]
        scores = _dot(x.astype(BF16), a_ref[...])
        n_mem = a_ref.shape[1] // XATTN_HEADS
        for h in range(XATTN_HEADS):
            ms = slice(h * n_mem, (h + 1) * n_mem)
            s = scores[:, ms]
            e = jnp.exp(s - jnp.max(s, axis=-1, keepdims=True))
            att_ref[rows, ms] = (e / jnp.sum(e, axis=-1, keepdims=True)).astype(BF16)
        y = DEEPNORM_ALPHA * x + _dot(att_ref[rows, :], b_ref[...])
        x2 = _layer_norm(y, lnw_ref[...], lnb_ref[...])
        tile_rows = c * ROW_TILE[0]
        _store_row_tiles(x2_ref.at[pl.ds(pl.multiple_of(sb * tile_rows, tile_rows), tile_rows)], x2)

        a0, a1, a2 = _split3(x2)
        pa = _dot_nt(jnp.concatenate([w0, w1, w2], axis=0), a0)
        pb = _dot_nt(jnp.concatenate([w0, w1], axis=0), a1)
        pc = _dot_nt(w0, a2)
        logits = ((pc + pa[2 * n:3 * n] + pb[n:2 * n]) + (pb[0:n] + pa[n:2 * n]) + pa[0:n]) + br_ref[...]

        work = logits
        sels, vals, idxs = [], [], []
        for k in range(TOP_K):
            m = jnp.max(work, axis=0, keepdims=True)
            idx = jnp.min(jnp.where(work == m, eid, n), axis=0, keepdims=True)
            sel = eid == idx
            sels.append(sel)
            vals.append(m)
            idxs.append(idx)
            work = jnp.where(sel, -jnp.inf, work)
        es = [jnp.exp(vk - vals[0]) for vk in vals]
        denom = es[0] + es[1] + es[2] + es[3]
        gate_ref[sb] = jnp.concatenate([ek / denom for ek in es], axis=0)

        onehot = jnp.zeros((n, c), F32)
        for sel in sels:
            onehot = jnp.where(sel, 1.0, onehot)
        before = _dot(onehot.astype(BF16), ut_ref[...]) + run
        ranks = [jnp.sum(jnp.where(sel, before, 0.0), axis=0, keepdims=True).astype(jnp.int32) for sel in sels]
        run_ref[...] = run + jnp.sum(onehot, axis=1, keepdims=True)

        dest_ref[sb] = jnp.concatenate([idx * (1 << shift) + rk for idx, rk in zip(idxs, ranks)], axis=0)
        return carry

    lax.fori_loop(0, x_ref.shape[0] // c, block, 0)
    cnt_ref[...] = run_ref[...].astype(jnp.int32)


def _xattn_router(x1, score_w, out_w, ln_w, ln_b, w_r, b_r):
    t = x1.shape[0]
    c = SEQ_BLOCK
    _, _, strict_lower = _decay_tables()
    full = lambda shape: pl.BlockSpec(shape, lambda i: (0,) * len(shape))
    step = _step_rows(t)
    per_k = pl.BlockSpec((step // c, TOP_K, c), lambda i: (i, 0, 0))
    return pl.pallas_call(
        functools.partial(_xattn_kernel, shift=_rank_bits(t)),
        grid=(t // step,),
        in_specs=[
            pl.BlockSpec((step, D_MODEL), lambda i: (i, 0)),
            full(score_w.shape),
            full(out_w.shape),
            full((1, D_MODEL)),
            full((1, D_MODEL)),
            full((N_EXPERTS, D_MODEL)),
            full((N_EXPERTS, 1)),
            full((c, c)),
        ],
        out_specs=[_tiled_spec(step, lambda i: (i, 0)), per_k, per_k, full((N_EXPERTS, 1))],
        out_shape=[
            jax.ShapeDtypeStruct(_tiled(t), F32),
            jax.ShapeDtypeStruct((t // c, TOP_K, c), jnp.int32),
            jax.ShapeDtypeStruct((t // c, TOP_K, c), F32),
            jax.ShapeDtypeStruct((N_EXPERTS, 1), jnp.int32),
        ],
        scratch_shapes=[pltpu.VMEM((step, score_w.shape[1]), BF16), pltpu.VMEM((N_EXPERTS, 1), F32)],
        compiler_params=pltpu.CompilerParams(dimension_semantics=("arbitrary",), vmem_limit_bytes=VMEM_LIMIT),
        name="xattn_router",
    )(x1, score_w, out_w, ln_w.reshape(1, -1), ln_b.reshape(1, -1),
      w_r.T, b_r.reshape(-1, 1), jnp.asarray(strict_lower.T, BF16))


TAIL_RUNS = (128, 64, 32, 16, 8, 4, 2, 1)


def _dispatch_kernel(fill_ref, pend_ref, nused_ref, slot_ref, x_ref, xs_hbm, zero_ref, sem, zsem, *, n_blocks):
    i = pl.program_id(0)
    c = SEQ_BLOCK

    @pl.when(i == 0)
    def _():
        zero_ref[...] = jnp.zeros_like(zero_ref)

        sub = ROW_TILE[0]

        def rows(ref, first, n):
            return ref.at[pl.ds(pl.multiple_of(first * sub, sub), n * sub)]

        def tail(e, act):
            row = fill_ref[e]
            pad = pend_ref[e] - row
            for run in TAIL_RUNS:
                hit = (pad & run) != 0

                @pl.when(hit)
                def _():
                    act(pltpu.make_async_copy(rows(zero_ref, 0, run), rows(xs_hbm, row, run), zsem))
                row = row + jnp.where(hit, run, 0)

        def spare(b, act):
            act(pltpu.make_async_copy(zero_ref, rows(xs_hbm, b * MOE_BLOCK, MOE_BLOCK), zsem))

        for act in (lambda cp: cp.start(), lambda cp: cp.wait()):
            lax.fori_loop(0, N_EXPERTS, lambda e, carry: (tail(e, act), carry)[1], 0)
            lax.fori_loop(nused_ref[0], n_blocks, lambda b, carry: (spare(b, act), carry)[1], 0)

    for q in range(x_ref.shape[0] // (c * ROW_TILE[0])):
        def issue(j, carry, q=q):
            for u in range(ISSUE_UNROLL):
                r = j * ISSUE_UNROLL + u
                for k in range(TOP_K):
                    dst = _tile_of(xs_hbm, slot_ref[0, 0, (q * TOP_K + k) * c + r])
                    pltpu.make_async_copy(_tile_of(x_ref, q * c + r), dst, sem).start(priority=k % 2)
            return carry
        lax.fori_loop(0, c // ISSUE_UNROLL, issue, 0)
    for k in range(TOP_K):
        pltpu.make_async_copy(x_ref, xs_hbm.at[pl.ds(0, x_ref.shape[0])], sem).wait()


def _dispatch(x2, slots, fill, pend, n_used, n_blocks):
    t = x2.shape[0] // ROW_TILE[0]
    c = min(DISPATCH_BLOCK, t)
    assert t % c == 0 and c % SEQ_BLOCK == 0
    grid_spec = pltpu.PrefetchScalarGridSpec(
        num_scalar_prefetch=3,
        grid=(t // c,),
        in_specs=[
            pl.BlockSpec((1, 1, c * TOP_K), lambda i, *_: (i, 0, 0), memory_space=pltpu.SMEM),
            _tiled_spec(c, lambda i, *_: (i, 0)),
        ],
        out_specs=pl.BlockSpec(memory_space=pl.ANY),
        scratch_shapes=[pltpu.VMEM(_tiled(MOE_BLOCK), F32), pltpu.SemaphoreType.DMA, pltpu.SemaphoreType.DMA],
    )
    return pl.pallas_call(
        functools.partial(_dispatch_kernel, n_blocks=n_blocks),
        grid_spec=grid_spec,
        out_shape=jax.ShapeDtypeStruct(_tiled(n_blocks * MOE_BLOCK), F32),
        compiler_params=pltpu.CompilerParams(dimension_semantics=("arbitrary",), vmem_limit_bytes=VMEM_LIMIT),
        name="dispatch",
    )(fill, pend, n_used, slots.reshape(t // c, 1, c * TOP_K), x2)


def _expert_kernel(be_ref, first_ref, nxt_ref, par_ref, nused_ref, x_ref, w1_hbm, b1_ref, w2_hbm, b2_ref, o_ref,
                   w1f, w2f, w1b, w2b, sem1, sem2):
    step = pl.program_id(0)
    rows = MOE_BLOCK * ROW_TILE[0]

    def fetch(e, p):
        return (pltpu.make_async_copy(w1_hbm.at[e], w1f.at[p], sem1.at[p]),
                pltpu.make_async_copy(w2_hbm.at[e], w2f.at[p], sem2.at[p]))

    @pl.when(step == 0)
    def _():
        for cp in fetch(be_ref[0], par_ref[0]):
            cp.start()

    def block(sb, carry):
        b = step * EXPERT_STEP_BLOCKS + sb
        x_blk = x_ref.at[pl.ds(pl.multiple_of(sb * rows, rows), rows)]
        o_blk = o_ref.at[pl.ds(pl.multiple_of(sb * rows, rows), rows)]

        @pl.when(b >= nused_ref[0])
        def _():
            o_blk[...] = jnp.zeros(o_blk.shape, o_blk.dtype)

        @pl.when(b < nused_ref[0])
        def _():
            e = be_ref[b]

            @pl.when(first_ref[b] == 1)
            def _():
                p = par_ref[b]
                for cp in fetch(e, p):
                    cp.wait()

                @pl.when(nxt_ref[b] >= 0)
                def _():
                    for cp in fetch(nxt_ref[b], 1 - p):
                        cp.start()

                w1b[...] = w1f[p].astype(BF16)
                w2b[...] = w2f[p].astype(BF16)

            h = _dot(_load_row_tiles(x_blk).astype(BF16), w1b[...]) + b1_ref[e]
            h_glu = jnp.minimum(h[:, :D_EXPERT], SWIGLU_LIMIT)
            h_lin = jnp.clip(h[:, D_EXPERT:], -SWIGLU_LIMIT, SWIGLU_LIMIT)
            act = h_glu * (1.0 / (1.0 + jnp.exp(-SWIGLU_ALPHA * h_glu))) * (h_lin + 1.0)
            _store_row_tiles(o_blk, _dot(act.astype(BF16), w2b[...]) + b2_ref[e])
        return carry

    lax.fori_loop(0, EXPERT_STEP_BLOCKS, block, 0)


def _experts(xs, plan, w1, b1, w2, b2):
    block_e, first, nxt, par, n_used = plan
    n_blocks = block_e.shape[0]
    assert n_blocks % EXPERT_STEP_BLOCKS == 0
    rows = lambda s, *_: (s, 0)
    whole = lambda s, *_: (0, 0, 0)
    grid_spec = pltpu.PrefetchScalarGridSpec(
        num_scalar_prefetch=5,
        grid=(n_blocks // EXPERT_STEP_BLOCKS,),
        in_specs=[
            _tiled_spec(EXPERT_STEP_BLOCKS * MOE_BLOCK, rows),
            pl.BlockSpec(memory_space=pl.ANY),
            pl.BlockSpec((N_EXPERTS, 1, 2 * D_EXPERT), whole),
            pl.BlockSpec(memory_space=pl.ANY),
            pl.BlockSpec((N_EXPERTS, 1, D_MODEL), whole),
        ],
        out_specs=_tiled_spec(EXPERT_STEP_BLOCKS * MOE_BLOCK, rows),
        scratch_shapes=[
            pltpu.VMEM((2, D_MODEL, 2 * D_EXPERT), F32),
            pltpu.VMEM((2, D_EXPERT, D_MODEL), F32),
            pltpu.VMEM((D_MODEL, 2 * D_EXPERT), BF16),
            pltpu.VMEM((D_EXPERT, D_MODEL), BF16),
            pltpu.SemaphoreType.DMA((2,)),
            pltpu.SemaphoreType.DMA((2,)),
        ],
    )
    return pl.pallas_call(
        _expert_kernel,
        grid_spec=grid_spec,
        out_shape=jax.ShapeDtypeStruct(xs.shape, F32),
        compiler_params=pltpu.CompilerParams(dimension_semantics=("arbitrary",), vmem_limit_bytes=EXPERT_VMEM_LIMIT),
        name="experts",
    )(block_e, first, nxt, par, n_used, xs, w1, b1.reshape(N_EXPERTS, 1, -1), w2, b2.reshape(N_EXPERTS, 1, -1))


def _combine_kernel(slot_ref, next_slot_ref, ys_hbm, gate_ref, x_ref, lnw_ref, lnb_ref, o_ref, ybuf, sem):
    i = pl.program_id(0)
    c = SEQ_BLOCK

    def gather(slots, p):
        def issue(j, carry):
            for u in range(ISSUE_UNROLL):
                r = j * ISSUE_UNROLL + u
                for k in range(TOP_K):
                    src = _tile_of(ys_hbm, slots[0, 0, k * c + r])
                    pltpu.make_async_copy(src, _tile_of(ybuf.at[p, k], r), sem.at[p]).start(priority=k % 2)
            return carry
        lax.fori_loop(0, c // ISSUE_UNROLL, issue, 0)

    @pl.when(i == 0)
    def _():
        gather(slot_ref, 0)

    @pl.when(i + 1 < pl.num_programs(0))
    def _():
        gather(next_slot_ref, (i + 1) % 2)

    p = i % 2
    for k in range(TOP_K):
        pltpu.make_async_copy(ys_hbm.at[pl.ds(0, ybuf.shape[2])], ybuf.at[p, k], sem.at[p]).wait()
    gates = gate_ref[...]
    ys = [_load_row_tiles(ybuf.at[p, k]) * gates[:, k:k + 1] for k in range(TOP_K)]
    y = (ys[0] + ys[1]) + (ys[2] + ys[3])
    o_ref[...] = _layer_norm(DEEPNORM_ALPHA * _load_row_tiles(x_ref) + y, lnw_ref[...], lnb_ref[...])


def _combine(ys, slots, gates, x2, ln_w, ln_b):
    t = x2.shape[0] // ROW_TILE[0]
    c = SEQ_BLOCK
    n = t // c
    slot_table = slots.reshape(n, 1, c * TOP_K)
    return pl.pallas_call(
        _combine_kernel,
        grid=(n,),
        in_specs=[
            pl.BlockSpec((1, 1, c * TOP_K), lambda i: (i, 0, 0), memory_space=pltpu.SMEM),
            pl.BlockSpec((1, 1, c * TOP_K), lambda i: (jnp.minimum(i + 1, n - 1), 0, 0), memory_space=pltpu.SMEM),
            pl.BlockSpec(memory_space=pl.ANY),
            pl.BlockSpec((c, TOP_K), lambda i: (i, 0)),
            _tiled_spec(c, lambda i: (i, 0)),
            pl.BlockSpec((1, D_MODEL), lambda i: (0, 0)),
            pl.BlockSpec((1, D_MODEL), lambda i: (0, 0)),
        ],
        out_specs=pl.BlockSpec((c, D_MODEL), lambda i: (i, 0)),
        out_shape=jax.ShapeDtypeStruct((t, D_MODEL), F32),
        scratch_shapes=[pltpu.VMEM((2, TOP_K) + _tiled(c), F32), pltpu.SemaphoreType.DMA((2,))],
        compiler_params=pltpu.CompilerParams(dimension_semantics=("arbitrary",), vmem_limit_bytes=VMEM_LIMIT),
        name="combine",
    )(slot_table, slot_table, ys, gates, x2, ln_w.reshape(1, -1), ln_b.reshape(1, -1))


def _slot_tables(counts, packed, n_blocks, shift):
    counts = counts.reshape(-1)
    nblk = (counts + MOE_BLOCK - 1) // MOE_BLOCK
    ids = jnp.arange(N_EXPERTS)
    lower = ids[None, :] <= ids[:, None]
    bend = jnp.sum(jnp.where(lower, nblk[None, :], 0), axis=1)
    n_used = bend[N_EXPERTS - 1]
    pend = bend * MOE_BLOCK
    pstart = pend - nblk * MOE_BLOCK
    fill = pstart + counts
    order = jnp.sum(jnp.where(lower, (nblk > 0)[None, :], False), axis=1) - 1
    later = (ids[None, :] > ids[:, None]) & (nblk > 0)[None, :]
    nxt_e = jnp.min(jnp.where(later, ids[None, :], N_EXPERTS), axis=1)
    nxt_e = jnp.where(nxt_e == N_EXPERTS, -1, nxt_e)
    b = jnp.minimum(jnp.arange(n_blocks), n_used - 1)
    hit = bend[None, :] <= b[:, None]
    block_e = jnp.sum(hit.astype(jnp.int32), axis=1)
    of_block = lambda per_expert: jnp.sum(jnp.where(ids[None, :] == block_e[:, None], per_expert[None, :], 0), axis=1)
    first = (b == of_block(bend - nblk)).astype(jnp.int32)
    expert = lax.shift_right_logical(packed, shift)
    rank = packed & ((1 << shift) - 1)
    slots = jnp.sum(jnp.where(expert[..., None] == ids, pstart, 0), axis=-1) + rank
    i32 = lambda a: a.astype(jnp.int32)
    plan = (i32(block_e), first, i32(of_block(nxt_e)), i32(of_block(order & 1)), i32(n_used.reshape(1)))
    return i32(slots), i32(fill), i32(pend), plan


def kernel(x, mem, w_in, lb_logits, hgrn_norm_w, w_pool, pool_scale, w_out, ln1_w, ln1_b, w_xq, w_xk, w_xv, w_xo,
           ln2_w, ln2_b, w_router, b_router, w1, b1, w2, b2, ln3_w, ln3_b):
    bsz, seq, d = x.shape
    assert bsz == 1 and d == D_MODEL and seq % SEQ_BLOCK == 0 and w_in.shape[0] == 1
    xt = x.reshape(seq, d)
    x1 = _mixer(xt, w_in[0], lb_logits, hgrn_norm_w[0], w_pool[0], pool_scale[0], w_out[0], ln1_w[0], ln1_b[0])
    score_w, out_w = _kv_proj(mem[0], w_xk[0], w_xv[0], w_xq[0], w_xo[0])
    x2, packed, gates, counts = _xattn_router(x1, score_w, out_w, ln2_w[0], ln2_b[0], w_router[0], b_router[0])
    n_blocks = -(-(seq * TOP_K + N_EXPERTS * (MOE_BLOCK - 1)) // MOE_BLOCK)
    slots, fill, pend, plan = _slot_tables(counts, packed, n_blocks, _rank_bits(seq))
    xs = _dispatch(x2, slots, fill, pend, plan[-1], n_blocks)
    ys = _experts(xs, plan, w1[0], b1[0], w2[0], b2[0])
    gates_tk = gates.transpose(0, 2, 1).reshape(seq, TOP_K)
    out = _combine(ys, slots, gates_tk, x2, ln3_w[0], ln3_b[0])
    return out.reshape(bsz, seq, d)
```

```python
import functools

import numpy as np
import jax
import jax.numpy as jnp
from jax import lax
from jax.experimental import pallas as pl
from jax.experimental.pallas import tpu as pltpu

F32 = jnp.float32
BF16 = jnp.bfloat16

D_MODEL = 1024
HGRN_WIDTH = 512
HGRN_HEADS = 4
HEAD_DIM = 128
POOL_WINDOWS = (2, 4, 8, 16)
POOL_GROUP = 128
POOL_CARRY = 16
IN_PROJ_WIDTH = 4 * HGRN_WIDTH + 512
XATTN_HEADS = 4
XATTN_HEAD_DIM = 256
N_EXPERTS = 32
TOP_K = 4
D_EXPERT = 1024
SWIGLU_LIMIT = 7.0
SWIGLU_ALPHA = 1.702
MOE_BLOCK = 256
DEEPNORM_ALPHA = 2.0 ** 0.25
LN_EPS = 1e-5
RMS_EPS = 1e-6

SEQ_BLOCK = 256
EXPERT_STEP_BLOCKS = 4
DISPATCH_BLOCK = 2048
N_LEVELS = 8
MXU_LEVELS = 3
ISSUE_UNROLL = 2
ROW_TILE = (8, 128)
VMEM_LIMIT = 48 * 1024 * 1024
EXPERT_VMEM_LIMIT = 56 * 1024 * 1024


def _rank_bits(t):
    return max(int(t - 1).bit_length(), 1)


def _layer_norm(y, w, b):
    mu = jnp.mean(y, axis=-1, keepdims=True)
    yc = y - mu
    var = jnp.mean(yc * yc, axis=-1, keepdims=True)
    return yc * lax.rsqrt(var + LN_EPS) * w + b


def _tiled(n):
    return (n * ROW_TILE[0], ROW_TILE[1])


def _tiled_spec(n, index_map):
    return pl.BlockSpec(_tiled(n), index_map)


def _tile_of(ref, r):
    return ref.at[pl.ds(pl.multiple_of(r * ROW_TILE[0], ROW_TILE[0]), ROW_TILE[0])]


def _store_row_tiles(ref, val):
    s, l = ROW_TILE
    for j in range(s):
        ref[pl.ds(j, val.shape[0], stride=s), :] = val[:, j * l:(j + 1) * l]


def _load_row_tiles(ref):
    s, _ = ROW_TILE
    return jnp.concatenate([ref[pl.ds(j, ref.shape[0] // s, stride=s), :] for j in range(s)], axis=-1)


def _dot(a, b):
    return jnp.dot(a, b, preferred_element_type=F32)


def _dot_nt(a, b):
    return lax.dot_general(a, b, (((1,), (1,)), ((), ())), preferred_element_type=F32)


def _decay_tables():
    c = SEQ_BLOCK
    r = np.arange(c)[:, None]
    u = np.arange(c)[None, :]
    mats = [(u <= r)]
    for l in range(MXU_LEVELS):
        h = 1 << l
        mid = (r // (2 * h)) * (2 * h) + h
        upper = (r >= mid) & (u >= mid) & (u <= r)
        lower = (r < mid) & (u > r) & (u < mid)
        mats.append(upper | lower)
    mall = np.concatenate(mats, axis=0).astype(np.float32)
    x = r ^ u
    lvl = np.where(u < r, np.floor(np.log2(np.maximum(x, 1))).astype(np.int32), -1).astype(np.int32)
    strict_lower = (u < r).astype(np.float32)
    return mall, lvl, strict_lower


def _level_gap(negb, l):
    half = 1 << l
    parts = []
    for base in range(0, negb.shape[0], 2 * half):
        ref_row = negb[base + half - 1:base + half, :]
        parts.append(jnp.abs(negb[base:base + 2 * half, :] - ref_row))
    return jnp.concatenate(parts, axis=0) if len(parts) > 1 else parts[0]


def _mixer_kernel(x_ref, win_ref, lbl_ref, nw_ref, wpool_ref, pscale_ref, wout_ref, lnw_ref, lnb_ref,
                  mall_ref, lvl_ref, o_ref, state_ref, carry_ref, ext_ref, mix_ref):
    i = pl.program_id(0)
    c = SEQ_BLOCK

    @pl.when(i == 0)
    def _():
        state_ref[...] = jnp.zeros_like(state_ref)
        carry_ref[...] = jnp.zeros_like(carry_ref)

    x = x_ref[...]
    proj = _dot(x.astype(BF16), win_ref[...])
    w = HGRN_WIDTH
    q = proj[:, 0:w]
    z = proj[:, w:2 * w]
    v = proj[:, 2 * w:3 * w]
    g = proj[:, 3 * w:4 * w]
    p = proj[:, 4 * w:]

    ll = lbl_ref[...]
    ee = jnp.exp(ll - jnp.max(ll, axis=0, keepdims=True))
    lb = ee[0:1] / jnp.sum(ee, axis=0, keepdims=True)

    f = lb + (1.0 - lb) * (1.0 / (1.0 + jnp.exp(-z)))
    nlf = -jnp.log(f)
    key = (1.0 - lb) * (1.0 / (1.0 + jnp.exp(z)))

    hi = nlf.astype(BF16)
    lo = (nlf - hi.astype(F32)).astype(BF16)
    mall = mall_ref[...]
    cum = _dot(mall, hi) + _dot(mall, lo)
    lvl = lvl_ref[...]

    for h in range(HGRN_HEADS):
        hs = slice(h * HEAD_DIM, (h + 1) * HEAD_DIM)
        qh, kh, vh, gh = q[:, hs], key[:, hs], v[:, hs], g[:, hs]
        vb = vh.astype(BF16)
        negb = cum[0:c, hs]
        scores = jnp.zeros((c, c), F32)
        for l in range(N_LEVELS):
            gap = cum[(l + 1) * c:(l + 2) * c, hs] if l < MXU_LEVELS else _level_gap(negb, l)
            e = jnp.exp(-gap)
            pl_ = _dot_nt((qh * e).astype(BF16), (kh * e).astype(BF16))
            scores = jnp.where(lvl == l, pl_, scores)
        diag = jnp.sum(qh * kh, axis=-1, keepdims=True)
        st = state_ref[h]
        o = (_dot(scores.astype(BF16), vb) + diag * vh
             + _dot_nt((qh * jnp.exp(-negb)).astype(BF16), st.astype(BF16)))
        negb_last = negb[c - 1:c, :]
        kd = (kh * jnp.exp(negb - negb_last)).astype(BF16)
        state_ref[h] = st * jnp.exp(-negb_last) + _dot(vh.T.astype(BF16), kd)
        o = o * lax.rsqrt(jnp.mean(o * o, axis=-1, keepdims=True) + RMS_EPS) * nw_ref[:, hs]
        gate = gh * (1.0 / (1.0 + jnp.exp(-gh)))
        mix_ref[:, hs] = (o * gate).astype(BF16)

    ext_ref[0:POOL_CARRY, :] = carry_ref[...]
    ext_ref[POOL_CARRY:POOL_CARRY + c, :] = p
    carry_ref[...] = p[c - POOL_CARRY:c, :]
    pos = i * c + lax.broadcasted_iota(jnp.int32, (c, 1), 0)
    for gi, win in enumerate(POOL_WINDOWS):
        gs = slice(gi * POOL_GROUP, (gi + 1) * POOL_GROUP)
        pg = p[:, gs]
        tot = pg
        for j in range(1, win):
            tot = tot + ext_ref[POOL_CARRY - j:POOL_CARRY - j + c, gs]
        cnt = jnp.minimum(pos + 1, win).astype(F32)
        pooled = _dot((tot / cnt - pg).astype(BF16), wpool_ref[gi])
        mix_ref[:, w + gi * POOL_GROUP:w + (gi + 1) * POOL_GROUP] = (pooled * pscale_ref[:, gs]).astype(BF16)

    y = DEEPNORM_ALPHA * x + _dot(mix_ref[...], wout_ref[...])
    o_ref[...] = _layer_norm(y, lnw_ref[...], lnb_ref[...])


def _mixer(x, w_in, lb_logits, norm_w, w_pool, pool_scale, w_out, ln_w, ln_b):
    t = x.shape[0]
    c = SEQ_BLOCK
    mall, lvl, _ = _decay_tables()
    full = lambda shape: pl.BlockSpec(shape, lambda i: (0,) * len(shape))
    return pl.pallas_call(
        _mixer_kernel,
        grid=(t // c,),
        in_specs=[
            pl.BlockSpec((c, D_MODEL), lambda i: (i, 0)),
            full((D_MODEL, IN_PROJ_WIDTH)),
            full(lb_logits.shape),
            full((1, HGRN_WIDTH)),
            full((len(POOL_WINDOWS), POOL_GROUP, POOL_GROUP)),
            full((1, 512)),
            full((D_MODEL, D_MODEL)),
            full((1, D_MODEL)),
            full((1, D_MODEL)),
            full(mall.shape),
            full(lvl.shape),
        ],
        out_specs=pl.BlockSpec((c, D_MODEL), lambda i: (i, 0)),
        out_shape=jax.ShapeDtypeStruct((t, D_MODEL), F32),
        scratch_shapes=[
            pltpu.VMEM((HGRN_HEADS, HEAD_DIM, HEAD_DIM), F32),
            pltpu.VMEM((POOL_CARRY, 512), F32),
            pltpu.VMEM((POOL_CARRY + c, 512), F32),
            pltpu.VMEM((c, D_MODEL), BF16),
        ],
        compiler_params=pltpu.CompilerParams(dimension_semantics=("arbitrary",), vmem_limit_bytes=VMEM_LIMIT),
        name="mixer",
    )(x, w_in.astype(BF16), lb_logits, norm_w.reshape(1, -1), w_pool.astype(BF16), pool_scale.reshape(1, -1),
      w_out.astype(BF16), ln_w.reshape(1, -1), ln_b.reshape(1, -1), jnp.asarray(mall, BF16), jnp.asarray(lvl))


def _kv_kernel(mem_ref, wk_ref, wv_ref, wq_ref, wo_ref, a_ref, b_ref):
    n_mem = mem_ref.shape[0]
    m = mem_ref[...].astype(BF16)
    k = _dot(m, wk_ref[...]).astype(BF16)
    v = _dot(m, wv_ref[...]).astype(BF16)
    scale = XATTN_HEAD_DIM ** -0.5
    for h in range(XATTN_HEADS):
        hs = slice(h * XATTN_HEAD_DIM, (h + 1) * XATTN_HEAD_DIM)
        ms = slice(h * n_mem, (h + 1) * n_mem)
        a_ref[:, ms] = (_dot_nt(wq_ref[:, hs], k[:, hs]) * scale).astype(BF16)
        b_ref[ms, :] = _dot(v[:, hs], wo_ref[hs, :]).astype(BF16)


def _kv_proj(mem, wk, wv, wq, wo):
    n = mem.shape[0]
    return pl.pallas_call(
        _kv_kernel,
        out_shape=(jax.ShapeDtypeStruct((D_MODEL, XATTN_HEADS * n), BF16),
                   jax.ShapeDtypeStruct((XATTN_HEADS * n, D_MODEL), BF16)),
        compiler_params=pltpu.CompilerParams(vmem_limit_bytes=VMEM_LIMIT),
        name="kv_proj",
    )(mem, wk.astype(BF16), wv.astype(BF16), wq.astype(BF16), wo.astype(BF16))


def _split3(a):
    a0 = a.astype(BF16)
    r1 = a - a0.astype(F32)
    a1 = r1.astype(BF16)
    a2 = (r1 - a1.astype(F32)).astype(BF16)
    return a0, a1, a2


def _xattn_kernel(x_ref, a_ref, b_ref, lnw_ref, lnb_ref, wrt_ref, br_ref, ut_ref,
                  x2_ref, dest_ref, gate_ref, cnt_ref, att_ref, run_ref, *, shift):
    i = pl.program_id(0)
    c = SEQ_BLOCK
    n = N_EXPERTS

    @pl.when(i == 0)
    def _():
        run_ref[...] = jnp.zeros_like(run_ref)

    x = x_ref[...]
    scores = _dot(x.astype(BF16), a_ref[...])
    n_mem = a_ref.shape[1] // XATTN_HEADS
    for h in range(XATTN_HEADS):
        ms = slice(h * n_mem, (h + 1) * n_mem)
        s = scores[:, ms]
        e = jnp.exp(s - jnp.max(s, axis=-1, keepdims=True))
        att_ref[:, ms] = (e / jnp.sum(e, axis=-1, keepdims=True)).astype(BF16)
    y = DEEPNORM_ALPHA * x + _dot(att_ref[...], b_ref[...])
    x2 = _layer_norm(y, lnw_ref[...], lnb_ref[...])
    _store_row_tiles(x2_ref, x2)

    a0, a1, a2 = _split3(x2)
    w0, w1, w2 = _split3(wrt_ref[...])
    pa = _dot_nt(jnp.concatenate([w0, w1, w2], axis=0), a0)
    pb = _dot_nt(jnp.concatenate([w0, w1], axis=0), a1)
    pc = _dot_nt(w0, a2)
    logits = ((pc + pa[2 * n:3 * n] + pb[n:2 * n]) + (pb[0:n] + pa[n:2 * n]) + pa[0:n]) + br_ref[...]

    eid = lax.broadcasted_iota(jnp.int32, (n, c), 0)
    work = logits
    sels, vals, idxs = [], [], []
    for k in range(TOP_K):
        m = jnp.max(work, axis=0, keepdims=True)
        idx = jnp.min(jnp.where(work == m, eid, n), axis=0, keepdims=True)
        sel = eid == idx
        sels.append(sel)
        vals.append(m)
        idxs.append(idx)
        work = jnp.where(sel, -jnp.inf, work)
    es = [jnp.exp(vk - vals[0]) for vk in vals]
    denom = es[0] + es[1] + es[2] + es[3]
    gate_ref[0] = jnp.concatenate([ek / denom for ek in es], axis=0)

    onehot = jnp.zeros((n, c), F32)
    for sel in sels:
        onehot = jnp.where(sel, 1.0, onehot)
    before = _dot(onehot.astype(BF16), ut_ref[...]) + run_ref[...]
    ranks = [jnp.sum(jnp.where(sel, before, 0.0), axis=0, keepdims=True).astype(jnp.int32) for sel in sels]
    run_ref[...] = run_ref[...] + jnp.sum(onehot, axis=1, keepdims=True)

    dest_ref[0] = jnp.concatenate([idx * (1 << shift) + rk for idx, rk in zip(idxs, ranks)], axis=0)
    cnt_ref[...] = run_ref[...].astype(jnp.int32)


def _xattn_router(x1, score_w, out_w, ln_w, ln_b, w_r, b_r):
    t = x1.shape[0]
    c = SEQ_BLOCK
    _, _, strict_lower = _decay_tables()
    full = lambda shape: pl.BlockSpec(shape, lambda i: (0,) * len(shape))
    per_k = pl.BlockSpec((1, TOP_K, c), lambda i: (i, 0, 0))
    return pl.pallas_call(
        functools.partial(_xattn_kernel, shift=_rank_bits(t)),
        grid=(t // c,),
        in_specs=[
            pl.BlockSpec((c, D_MODEL), lambda i: (i, 0)),
            full(score_w.shape),
            full(out_w.shape),
            full((1, D_MODEL)),
            full((1, D_MODEL)),
            full((N_EXPERTS, D_MODEL)),
            full((N_EXPERTS, 1)),
            full((c, c)),
        ],
        out_specs=[_tiled_spec(c, lambda i: (i, 0)), per_k, per_k, full((N_EXPERTS, 1))],
        out_shape=[
            jax.ShapeDtypeStruct(_tiled(t), F32),
            jax.ShapeDtypeStruct((t // c, TOP_K, c), jnp.int32),
            jax.ShapeDtypeStruct((t // c, TOP_K, c), F32),
            jax.ShapeDtypeStruct((N_EXPERTS, 1), jnp.int32),
        ],
        scratch_shapes=[pltpu.VMEM((c, score_w.shape[1]), BF16), pltpu.VMEM((N_EXPERTS, 1), F32)],
        compiler_params=pltpu.CompilerParams(dimension_semantics=("arbitrary",), vmem_limit_bytes=VMEM_LIMIT),
        name="xattn_router",
    )(x1, score_w, out_w, ln_w.reshape(1, -1), ln_b.reshape(1, -1),
      w_r.T, b_r.reshape(-1, 1), jnp.asarray(strict_lower.T, BF16))


TAIL_RUNS = (128, 64, 32, 16, 8, 4, 2, 1)


def _dispatch_kernel(fill_ref, pend_ref, nused_ref, slot_ref, x_ref, xs_hbm, zero_ref, sem, zsem, *, n_blocks):
    i = pl.program_id(0)
    c = SEQ_BLOCK

    @pl.when(i == 0)
    def _():
        zero_ref[...] = jnp.zeros_like(zero_ref)

        sub = ROW_TILE[0]

        def rows(ref, first, n):
            return ref.at[pl.ds(pl.multiple_of(first * sub, sub), n * sub)]

        def tail(e, act):
            row = fill_ref[e]
            pad = pend_ref[e] - row
            for run in TAIL_RUNS:
                hit = (pad & run) != 0

                @pl.when(hit)
                def _():
                    act(pltpu.make_async_copy(rows(zero_ref, 0, run), rows(xs_hbm, row, run), zsem))
                row = row + jnp.where(hit, run, 0)

        def spare(b, act):
            act(pltpu.make_async_copy(zero_ref, rows(xs_hbm, b * MOE_BLOCK, MOE_BLOCK), zsem))

        for act in (lambda cp: cp.start(), lambda cp: cp.wait()):
            lax.fori_loop(0, N_EXPERTS, lambda e, carry: (tail(e, act), carry)[1], 0)
            lax.fori_loop(nused_ref[0], n_blocks, lambda b, carry: (spare(b, act), carry)[1], 0)

    for q in range(x_ref.shape[0] // (c * ROW_TILE[0])):
        def issue(j, carry, q=q):
            for u in range(ISSUE_UNROLL):
                r = j * ISSUE_UNROLL + u
                for k in range(TOP_K):
                    dst = _tile_of(xs_hbm, slot_ref[0, 0, (q * TOP_K + k) * c + r])
                    pltpu.make_async_copy(_tile_of(x_ref, q * c + r), dst, sem).start(priority=k % 2)
            return carry
        lax.fori_loop(0, c // ISSUE_UNROLL, issue, 0)
    for k in range(TOP_K):
        pltpu.make_async_copy(x_ref, xs_hbm.at[pl.ds(0, x_ref.shape[0])], sem).wait()


def _dispatch(x2, slots, fill, pend, n_used, n_blocks):
    t = x2.shape[0] // ROW_TILE[0]
    c = min(DISPATCH_BLOCK, t)
    assert t % c == 0 and c % SEQ_BLOCK == 0
    grid_spec = pltpu.PrefetchScalarGridSpec(
        num_scalar_prefetch=3,
        grid=(t // c,),
        in_specs=[
            pl.BlockSpec((1, 1, c * TOP_K), lambda i, *_: (i, 0, 0), memory_space=pltpu.SMEM),
            _tiled_spec(c, lambda i, *_: (i, 0)),
        ],
        out_specs=pl.BlockSpec(memory_space=pl.ANY),
        scratch_shapes=[pltpu.VMEM(_tiled(MOE_BLOCK), F32), pltpu.SemaphoreType.DMA, pltpu.SemaphoreType.DMA],
    )
    return pl.pallas_call(
        functools.partial(_dispatch_kernel, n_blocks=n_blocks),
        grid_spec=grid_spec,
        out_shape=jax.ShapeDtypeStruct(_tiled(n_blocks * MOE_BLOCK), F32),
        compiler_params=pltpu.CompilerParams(dimension_semantics=("arbitrary",), vmem_limit_bytes=VMEM_LIMIT),
        name="dispatch",
    )(fill, pend, n_used, slots.reshape(t // c, 1, c * TOP_K), x2)


def _expert_kernel(be_ref, first_ref, nxt_ref, par_ref, nused_ref, x_ref, w1_hbm, b1_ref, w2_hbm, b2_ref, o_ref,
                   w1f, w2f, w1b, w2b, sem1, sem2):
    step = pl.program_id(0)
    rows = MOE_BLOCK * ROW_TILE[0]

    def fetch(e, p):
        return (pltpu.make_async_copy(w1_hbm.at[e], w1f.at[p], sem1.at[p]),
                pltpu.make_async_copy(w2_hbm.at[e], w2f.at[p], sem2.at[p]))

    @pl.when(step == 0)
    def _():
        for cp in fetch(be_ref[0], par_ref[0]):
            cp.start()

    def block(sb, carry):
        b = step * EXPERT_STEP_BLOCKS + sb
        x_blk = x_ref.at[pl.ds(pl.multiple_of(sb * rows, rows), rows)]
        o_blk = o_ref.at[pl.ds(pl.multiple_of(sb * rows, rows), rows)]

        @pl.when(b >= nused_ref[0])
        def _():
            o_blk[...] = jnp.zeros(o_blk.shape, o_blk.dtype)

        @pl.when(b < nused_ref[0])
        def _():
            e = be_ref[b]

            @pl.when(first_ref[b] == 1)
            def _():
                p = par_ref[b]
                for cp in fetch(e, p):
                    cp.wait()

                @pl.when(nxt_ref[b] >= 0)
                def _():
                    for cp in fetch(nxt_ref[b], 1 - p):
                        cp.start()

                w1b[...] = w1f[p].astype(BF16)
                w2b[...] = w2f[p].astype(BF16)

            h = _dot(_load_row_tiles(x_blk).astype(BF16), w1b[...]) + b1_ref[e]
            h_glu = jnp.minimum(h[:, :D_EXPERT], SWIGLU_LIMIT)
            h_lin = jnp.clip(h[:, D_EXPERT:], -SWIGLU_LIMIT, SWIGLU_LIMIT)
            act = h_glu * (1.0 / (1.0 + jnp.exp(-SWIGLU_ALPHA * h_glu))) * (h_lin + 1.0)
            _store_row_tiles(o_blk, _dot(act.astype(BF16), w2b[...]) + b2_ref[e])
        return carry

    lax.fori_loop(0, EXPERT_STEP_BLOCKS, block, 0)


def _experts(xs, plan, w1, b1, w2, b2):
    block_e, first, nxt, par, n_used = plan
    n_blocks = block_e.shape[0]
    assert n_blocks % EXPERT_STEP_BLOCKS == 0
    rows = lambda s, *_: (s, 0)
    whole = lambda s, *_: (0, 0, 0)
    grid_spec = pltpu.PrefetchScalarGridSpec(
        num_scalar_prefetch=5,
        grid=(n_blocks // EXPERT_STEP_BLOCKS,),
        in_specs=[
            _tiled_spec(EXPERT_STEP_BLOCKS * MOE_BLOCK, rows),
            pl.BlockSpec(memory_space=pl.ANY),
            pl.BlockSpec((N_EXPERTS, 1, 2 * D_EXPERT), whole),
            pl.BlockSpec(memory_space=pl.ANY),
            pl.BlockSpec((N_EXPERTS, 1, D_MODEL), whole),
        ],
        out_specs=_tiled_spec(EXPERT_STEP_BLOCKS * MOE_BLOCK, rows),
        scratch_shapes=[
            pltpu.VMEM((2, D_MODEL, 2 * D_EXPERT), F32),
            pltpu.VMEM((2, D_EXPERT, D_MODEL), F32),
            pltpu.VMEM((D_MODEL, 2 * D_EXPERT), BF16),
            pltpu.VMEM((D_EXPERT, D_MODEL), BF16),
            pltpu.SemaphoreType.DMA((2,)),
            pltpu.SemaphoreType.DMA((2,)),
        ],
    )
    return pl.pallas_call(
        _expert_kernel,
        grid_spec=grid_spec,
        out_shape=jax.ShapeDtypeStruct(xs.shape, F32),
        compiler_params=pltpu.CompilerParams(dimension_semantics=("arbitrary",), vmem_limit_bytes=EXPERT_VMEM_LIMIT),
        name="experts",
    )(block_e, first, nxt, par, n_used, xs, w1, b1.reshape(N_EXPERTS, 1, -1), w2, b2.reshape(N_EXPERTS, 1, -1))


def _combine_kernel(slot_ref, next_slot_ref, ys_hbm, gate_ref, x_ref, lnw_ref, lnb_ref, o_ref, ybuf, sem):
    i = pl.program_id(0)
    c = SEQ_BLOCK

    def gather(slots, p):
        def issue(j, carry):
            for u in range(ISSUE_UNROLL):
                r = j * ISSUE_UNROLL + u
                for k in range(TOP_K):
                    src = _tile_of(ys_hbm, slots[0, 0, k * c + r])
                    pltpu.make_async_copy(src, _tile_of(ybuf.at[p, k], r), sem.at[p]).start(priority=k % 2)
            return carry
        lax.fori_loop(0, c // ISSUE_UNROLL, issue, 0)

    @pl.when(i == 0)
    def _():
        gather(slot_ref, 0)

    @pl.when(i + 1 < pl.num_programs(0))
    def _():
        gather(next_slot_ref, (i + 1) % 2)

    p = i % 2
    for k in range(TOP_K):
        pltpu.make_async_copy(ys_hbm.at[pl.ds(0, ybuf.shape[2])], ybuf.at[p, k], sem.at[p]).wait()
    gates = gate_ref[...]
    ys = [_load_row_tiles(ybuf.at[p, k]) * gates[:, k:k + 1] for k in range(TOP_K)]
    y = (ys[0] + ys[1]) + (ys[2] + ys[3])
    o_ref[...] = _layer_norm(DEEPNORM_ALPHA * _load_row_tiles(x_ref) + y, lnw_ref[...], lnb_ref[...])


def _combine(ys, slots, gates, x2, ln_w, ln_b):
    t = x2.shape[0] // ROW_TILE[0]
    c = SEQ_BLOCK
    n = t // c
    slot_table = slots.reshape(n, 1, c * TOP_K)
    return pl.pallas_call(
        _combine_kernel,
        grid=(n,),
        in_specs=[
            pl.BlockSpec((1, 1, c * TOP_K), lambda i: (i, 0, 0), memory_space=pltpu.SMEM),
            pl.BlockSpec((1, 1, c * TOP_K), lambda i: (jnp.minimum(i + 1, n - 1), 0, 0), memory_space=pltpu.SMEM),
            pl.BlockSpec(memory_space=pl.ANY),
            pl.BlockSpec((c, TOP_K), lambda i: (i, 0)),
            _tiled_spec(c, lambda i: (i, 0)),
            pl.BlockSpec((1, D_MODEL), lambda i: (0, 0)),
            pl.BlockSpec((1, D_MODEL), lambda i: (0, 0)),
        ],
        out_specs=pl.BlockSpec((c, D_MODEL), lambda i: (i, 0)),
        out_shape=jax.ShapeDtypeStruct((t, D_MODEL), F32),
        scratch_shapes=[pltpu.VMEM((2, TOP_K) + _tiled(c), F32), pltpu.SemaphoreType.DMA((2,))],
        compiler_params=pltpu.CompilerParams(dimension_semantics=("arbitrary",), vmem_limit_bytes=VMEM_LIMIT),
        name="combine",
    )(slot_table, slot_table, ys, gates, x2, ln_w.reshape(1, -1), ln_b.reshape(1, -1))


def _slot_tables(counts, packed, n_blocks, shift):
    counts = counts.reshape(-1)
    nblk = (counts + MOE_BLOCK - 1) // MOE_BLOCK
    ids = jnp.arange(N_EXPERTS)
    lower = ids[None, :] <= ids[:, None]
    bend = jnp.sum(jnp.where(lower, nblk[None, :], 0), axis=1)
    n_used = bend[N_EXPERTS - 1]
    pend = bend * MOE_BLOCK
    pstart = pend - nblk * MOE_BLOCK
    fill = pstart + counts
    order = jnp.sum(jnp.where(lower, (nblk > 0)[None, :], False), axis=1) - 1
    later = (ids[None, :] > ids[:, None]) & (nblk > 0)[None, :]
    nxt_e = jnp.min(jnp.where(later, ids[None, :], N_EXPERTS), axis=1)
    nxt_e = jnp.where(nxt_e == N_EXPERTS, -1, nxt_e)
    b = jnp.minimum(jnp.arange(n_blocks), n_used - 1)
    hit = bend[None, :] <= b[:, None]
    block_e = jnp.sum(hit.astype(jnp.int32), axis=1)
    of_block = lambda per_expert: jnp.sum(jnp.where(ids[None, :] == block_e[:, None], per_expert[None, :], 0), axis=1)
    first = (b == of_block(bend - nblk)).astype(jnp.int32)
    expert = lax.shift_right_logical(packed, shift)
    rank = packed & ((1 << shift) - 1)
    slots = jnp.sum(jnp.where(expert[..., None] == ids, pstart, 0), axis=-1) + rank
    i32 = lambda a: a.astype(jnp.int32)
    plan = (i32(block_e), first, i32(of_block(nxt_e)), i32(of_block(order & 1)), i32(n_used.reshape(1)))
    return i32(slots), i32(fill), i32(pend), plan


def kernel(x, mem, w_in, lb_logits, hgrn_norm_w, w_pool, pool_scale, w_out, ln1_w, ln1_b, w_xq, w_xk, w_xv, w_xo,
           ln2_w, ln2_b, w_router, b_router, w1, b1, w2, b2, ln3_w, ln3_b):
    bsz, seq, d = x.shape
    assert bsz == 1 and d == D_MODEL and seq % SEQ_BLOCK == 0 and w_in.shape[0] == 1
    xt = x.reshape(seq, d)
    x1 = _mixer(xt, w_in[0], lb_logits, hgrn_norm_w[0], w_pool[0], pool_scale[0], w_out[0], ln1_w[0], ln1_b[0])
    score_w, out_w = _kv_proj(mem[0], w_xk[0], w_xv[0], w_xq[0], w_xo[0])
    x2, packed, gates, counts = _xattn_router(x1, score_w, out_w, ln2_w[0], ln2_b[0], w_router[0], b_router[0])
    n_blocks = -(-(seq * TOP_K + N_EXPERTS * (MOE_BLOCK - 1)) // MOE_BLOCK)
    slots, fill, pend, plan = _slot_tables(counts, packed, n_blocks, _rank_bits(seq))
    xs = _dispatch(x2, slots, fill, pend, plan[-1], n_blocks)
    ys = _experts(xs, plan, w1[0], b1[0], w2[0], b2[0])
    gates_tk = gates.transpose(0, 2, 1).reshape(seq, TOP_K)
    out = _combine(ys, slots, gates_tk, x2, ln3_w[0], ln3_b[0])
    return out.reshape(bsz, seq, d)
```

```python
import functools

import numpy as np
import jax
import jax.numpy as jnp
from jax import lax
from jax.experimental import pallas as pl
from jax.experimental.pallas import tpu as pltpu

F32 = jnp.float32
BF16 = jnp.bfloat16

D_MODEL = 1024
HGRN_WIDTH = 512
HGRN_HEADS = 4
HEAD_DIM = 128
POOL_WINDOWS = (2, 4, 8, 16)
POOL_GROUP = 128
POOL_WIDTH = POOL_GROUP * len(POOL_WINDOWS)
POOL_CARRY = 24
POOL_HEAD = 8
IN_PROJ_WIDTH = 4 * HGRN_WIDTH + POOL_WIDTH
XATTN_HEADS = 4
XATTN_HEAD_DIM = 256
N_EXPERTS = 32
TOP_K = 4
D_EXPERT = 1024
SWIGLU_LIMIT = 7.0
SWIGLU_ALPHA = 1.702
MOE_BLOCK = 256
DEEPNORM_ALPHA = 2.0 ** 0.25
LN_EPS = 1e-5
RMS_EPS = 1e-6

SEQ_BLOCK = 256
EXPERT_STEP_BLOCKS = 4
DISPATCH_BLOCK = 2048
N_LEVELS = 8
MXU_LEVELS = 3
ISSUE_UNROLL = 2
ROW_TILE = (8, 128)
VMEM_LIMIT = 48 * 1024 * 1024
EXPERT_VMEM_LIMIT = 56 * 1024 * 1024


def _rank_bits(t):
    return max(int(t - 1).bit_length(), 1)


def _layer_norm(y, w, b):
    mu = jnp.mean(y, axis=-1, keepdims=True)
    yc = y - mu
    var = jnp.mean(yc * yc, axis=-1, keepdims=True)
    return yc * lax.rsqrt(var + LN_EPS) * w + b


def _tiled(n):
    return (n * ROW_TILE[0], ROW_TILE[1])


def _tiled_spec(n, index_map):
    return pl.BlockSpec(_tiled(n), index_map)


def _tile_of(ref, r):
    return ref.at[pl.ds(pl.multiple_of(r * ROW_TILE[0], ROW_TILE[0]), ROW_TILE[0])]


def _store_row_tiles(ref, val):
    s, l = ROW_TILE
    for j in range(s):
        ref[pl.ds(j, val.shape[0], stride=s), :] = val[:, j * l:(j + 1) * l]


def _load_row_tiles(ref):
    s, _ = ROW_TILE
    return jnp.concatenate([ref[pl.ds(j, ref.shape[0] // s, stride=s), :] for j in range(s)], axis=-1)


def _dot(a, b):
    return jnp.dot(a, b, preferred_element_type=F32)


def _dot_nt(a, b):
    return lax.dot_general(a, b, (((1,), (1,)), ((), ())), preferred_element_type=F32)


def _decay_tables():
    c = SEQ_BLOCK
    r = np.arange(c)[:, None]
    u = np.arange(c)[None, :]
    mats = [(u <= r)]
    for l in range(MXU_LEVELS):
        h = 1 << l
        mid = (r // (2 * h)) * (2 * h) + h
        upper = (r >= mid) & (u >= mid) & (u <= r)
        lower = (r < mid) & (u > r) & (u < mid)
        mats.append(upper | lower)
    mall = np.concatenate(mats, axis=0).astype(np.float32)
    x = r ^ u
    lvl = np.where(u < r, np.floor(np.log2(np.maximum(x, 1))).astype(np.int32), -1).astype(np.int32)
    strict_lower = (u < r).astype(np.float32)
    return mall, lvl, strict_lower


def _level_gap(negb, l):
    half = 1 << l
    parts = []
    for base in range(0, negb.shape[0], 2 * half):
        ref_row = negb[base + half - 1:base + half, :]
        parts.append(jnp.abs(negb[base:base + 2 * half, :] - ref_row))
    return jnp.concatenate(parts, axis=0) if len(parts) > 1 else parts[0]


def _mixer_kernel(x_ref, win_ref, lbl_ref, nw_ref, wpool_ref, pscale_ref, wout_ref, lnw_ref, lnb_ref,
                  mall_ref, lvl_ref, o_ref, state_ref, carry_ref, ext_ref, mix_ref, sum_ref):
    i = pl.program_id(0)
    c = SEQ_BLOCK

    @pl.when(i == 0)
    def _():
        state_ref[...] = jnp.zeros_like(state_ref)
        carry_ref[...] = jnp.zeros_like(carry_ref)
        sum_ref[:, 0:POOL_HEAD, :] = jnp.zeros((2, POOL_HEAD, POOL_WIDTH), F32)

    x = x_ref[...]
    proj = _dot(x.astype(BF16), win_ref[...])
    w = HGRN_WIDTH
    q = proj[:, 0:w]
    z = proj[:, w:2 * w]
    v = proj[:, 2 * w:3 * w]
    g = proj[:, 3 * w:4 * w]
    p = proj[:, 4 * w:]

    ll = lbl_ref[...]
    ee = jnp.exp(ll - jnp.max(ll, axis=0, keepdims=True))
    lb = ee[0:1] / jnp.sum(ee, axis=0, keepdims=True)

    f = lb + (1.0 - lb) * (1.0 / (1.0 + jnp.exp(-z)))
    nlf = -jnp.log(f)
    key = (1.0 - lb) * (1.0 / (1.0 + jnp.exp(z)))

    hi = nlf.astype(BF16)
    lo = (nlf - hi.astype(F32)).astype(BF16)
    mall = mall_ref[...]
    cum = _dot(mall, hi) + _dot(mall, lo)
    lvl = lvl_ref[...]

    for h in range(HGRN_HEADS):
        hs = slice(h * HEAD_DIM, (h + 1) * HEAD_DIM)
        qh, kh, vh, gh = q[:, hs], key[:, hs], v[:, hs], g[:, hs]
        vb = vh.astype(BF16)
        negb = cum[0:c, hs]
        scores = jnp.zeros((c, c), F32)
        for l in range(N_LEVELS):
            gap = cum[(l + 1) * c:(l + 2) * c, hs] if l < MXU_LEVELS else _level_gap(negb, l)
            e = jnp.exp(-gap)
            pl_ = _dot_nt((qh * e).astype(BF16), (kh * e).astype(BF16))
            scores = jnp.where(lvl == l, pl_, scores)
        diag = jnp.sum(qh * kh, axis=-1, keepdims=True)
        st = state_ref[h]
        o = (_dot(scores.astype(BF16), vb) + diag * vh
             + _dot_nt((qh * jnp.exp(-negb)).astype(BF16), st.astype(BF16)))
        negb_last = negb[c - 1:c, :]
        kd = (kh * jnp.exp(negb - negb_last)).astype(BF16)
        state_ref[h] = st * jnp.exp(-negb_last) + _dot(vh.T.astype(BF16), kd)
        o = o * lax.rsqrt(jnp.mean(o * o, axis=-1, keepdims=True) + RMS_EPS) * nw_ref[:, hs]
        gate = gh * (1.0 / (1.0 + jnp.exp(-gh)))
        mix_ref[:, hs] = (o * gate).astype(BF16)

    top = POOL_CARRY + c
    ext_ref[0:POOL_CARRY, :] = carry_ref[...]
    ext_ref[POOL_CARRY:top, :] = p
    carry_ref[...] = p[c - POOL_CARRY:c, :]
    pos = i * c + lax.broadcasted_iota(jnp.int32, (c, 1), 0)
    src = ext_ref
    for gi, win in enumerate(POOL_WINDOWS):
        gs = slice(gi * POOL_GROUP, (gi + 1) * POOL_GROUP)
        rest = slice(gi * POOL_GROUP, POOL_WIDTH)
        half = win // 2
        dst = sum_ref.at[gi % 2]
        dst[POOL_HEAD:top, rest] = src[POOL_HEAD:top, rest] + src[POOL_HEAD - half:top - half, rest]
        src = dst
        pg = p[:, gs]
        tot = dst[POOL_CARRY:top, gs]
        cnt = jnp.minimum(pos + 1, win).astype(F32)
        pooled = _dot((tot / cnt - pg).astype(BF16), wpool_ref[gi])
        mix_ref[:, w + gi * POOL_GROUP:w + (gi + 1) * POOL_GROUP] = (pooled * pscale_ref[:, gs]).astype(BF16)

    y = DEEPNORM_ALPHA * x + _dot(mix_ref[...], wout_ref[...])
    o_ref[...] = _layer_norm(y, lnw_ref[...], lnb_ref[...])


def _mixer(x, w_in, lb_logits, norm_w, w_pool, pool_scale, w_out, ln_w, ln_b):
    t = x.shape[0]
    c = SEQ_BLOCK
    mall, lvl, _ = _decay_tables()
    full = lambda shape: pl.BlockSpec(shape, lambda i: (0,) * len(shape))
    return pl.pallas_call(
        _mixer_kernel,
        grid=(t // c,),
        in_specs=[
            pl.BlockSpec((c, D_MODEL), lambda i: (i, 0)),
            full((D_MODEL, IN_PROJ_WIDTH)),
            full(lb_logits.shape),
            full((1, HGRN_WIDTH)),
            full((len(POOL_WINDOWS), POOL_GROUP, POOL_GROUP)),
            full((1, POOL_WIDTH)),
            full((D_MODEL, D_MODEL)),
            full((1, D_MODEL)),
            full((1, D_MODEL)),
            full(mall.shape),
            full(lvl.shape),
        ],
        out_specs=pl.BlockSpec((c, D_MODEL), lambda i: (i, 0)),
        out_shape=jax.ShapeDtypeStruct((t, D_MODEL), F32),
        scratch_shapes=[
            pltpu.VMEM((HGRN_HEADS, HEAD_DIM, HEAD_DIM), F32),
            pltpu.VMEM((POOL_CARRY, POOL_WIDTH), F32),
            pltpu.VMEM((POOL_CARRY + c, POOL_WIDTH), F32),
            pltpu.VMEM((c, D_MODEL), BF16),
            pltpu.VMEM((2, POOL_CARRY + c, POOL_WIDTH), F32),
        ],
        compiler_params=pltpu.CompilerParams(dimension_semantics=("arbitrary",), vmem_limit_bytes=VMEM_LIMIT),
        name="mixer",
    )(x, w_in.astype(BF16), lb_logits, norm_w.reshape(1, -1), w_pool.astype(BF16), pool_scale.reshape(1, -1),
      w_out.astype(BF16), ln_w.reshape(1, -1), ln_b.reshape(1, -1), jnp.asarray(mall, BF16), jnp.asarray(lvl))


def _kv_kernel(mem_ref, wk_ref, wv_ref, wq_ref, wo_ref, a_ref, b_ref):
    n_mem = mem_ref.shape[0]
    m = mem_ref[...].astype(BF16)
    k = _dot(m, wk_ref[...]).astype(BF16)
    v = _dot(m, wv_ref[...]).astype(BF16)
    scale = XATTN_HEAD_DIM ** -0.5
    for h in range(XATTN_HEADS):
        hs = slice(h * XATTN_HEAD_DIM, (h + 1) * XATTN_HEAD_DIM)
        ms = slice(h * n_mem, (h + 1) * n_mem)
        a_ref[:, ms] = (_dot_nt(wq_ref[:, hs], k[:, hs]) * scale).astype(BF16)
        b_ref[ms, :] = _dot(v[:, hs], wo_ref[hs, :]).astype(BF16)


def _kv_proj(mem, wk, wv, wq, wo):
    n = mem.shape[0]
    return pl.pallas_call(
        _kv_kernel,
        out_shape=(jax.ShapeDtypeStruct((D_MODEL, XATTN_HEADS * n), BF16),
                   jax.ShapeDtypeStruct((XATTN_HEADS * n, D_MODEL), BF16)),
        compiler_params=pltpu.CompilerParams(vmem_limit_bytes=VMEM_LIMIT),
        name="kv_proj",
    )(mem, wk.astype(BF16), wv.astype(BF16), wq.astype(BF16), wo.astype(BF16))


def _split3(a):
    a0 = a.astype(BF16)
    r1 = a - a0.astype(F32)
    a1 = r1.astype(BF16)
    a2 = (r1 - a1.astype(F32)).astype(BF16)
    return a0, a1, a2


def _xattn_kernel(x_ref, a_ref, b_ref, lnw_ref, lnb_ref, wrt_ref, br_ref, ut_ref,
                  x2_ref, dest_ref, gate_ref, cnt_ref, att_ref, run_ref, *, shift):
    i = pl.program_id(0)
    c = SEQ_BLOCK
    n = N_EXPERTS

    @pl.when(i == 0)
    def _():
        run_ref[...] = jnp.zeros_like(run_ref)

    x = x_ref[...]
    scores = _dot(x.astype(BF16), a_ref[...])
    n_mem = a_ref.shape[1] // XATTN_HEADS
    for h in range(XATTN_HEADS):
        ms = slice(h * n_mem, (h + 1) * n_mem)
        s = scores[:, ms]
        e = jnp.exp(s - jnp.max(s, axis=-1, keepdims=True))
        att_ref[:, ms] = (e / jnp.sum(e, axis=-1, keepdims=True)).astype(BF16)
    y = DEEPNORM_ALPHA * x + _dot(att_ref[...], b_ref[...])
    x2 = _layer_norm(y, lnw_ref[...], lnb_ref[...])
    _store_row_tiles(x2_ref, x2)

    a0, a1, _ = _split3(x2)
    w0, w1, w2 = _split3(wrt_ref[...])
    pa = _dot_nt(jnp.concatenate([w0, w1, w2], axis=0), a0)
    pb = _dot_nt(jnp.concatenate([w0, w1], axis=0), a1)
    logits = ((pa[2 * n:3 * n] + pb[n:2 * n]) + (pb[0:n] + pa[n:2 * n]) + pa[0:n]) + br_ref[...]

    eid = lax.broadcasted_iota(jnp.int32, (n, c), 0)
    work = logits
    sels, vals, idxs = [], [], []
    for k in range(TOP_K):
        m = jnp.max(work, axis=0, keepdims=True)
        idx = jnp.min(jnp.where(work == m, eid, n), axis=0, keepdims=True)
        sel = eid == idx
        sels.append(sel)
        vals.append(m)
        idxs.append(idx)
        work = jnp.where(sel, -jnp.inf, work)
    es = [jnp.exp(vk - vals[0]) for vk in vals]
    denom = es[0] + es[1] + es[2] + es[3]
    gate_ref[0] = jnp.concatenate([ek / denom for ek in es], axis=0)

    onehot = jnp.zeros((n, c), F32)
    for sel in sels:
        onehot = jnp.where(sel, 1.0, onehot)
    before = _dot(onehot.astype(BF16), ut_ref[...]) + run_ref[...]
    ranks = [jnp.sum(jnp.where(sel, before, 0.0), axis=0, keepdims=True).astype(jnp.int32) for sel in sels]
    run_ref[...] = run_ref[...] + jnp.sum(onehot, axis=1, keepdims=True)

    dest_ref[0] = jnp.concatenate([idx * (1 << shift) + rk for idx, rk in zip(idxs, ranks)], axis=0)
    cnt_ref[...] = run_ref[...].astype(jnp.int32)


def _xattn_router(x1, score_w, out_w, ln_w, ln_b, w_r, b_r):
    t = x1.shape[0]
    c = SEQ_BLOCK
    _, _, strict_lower = _decay_tables()
    full = lambda shape: pl.BlockSpec(shape, lambda i: (0,) * len(shape))
    per_k = pl.BlockSpec((1, TOP_K, c), lambda i: (i, 0, 0))
    return pl.pallas_call(
        functools.partial(_xattn_kernel, shift=_rank_bits(t)),
        grid=(t // c,),
        in_specs=[
            pl.BlockSpec((c, D_MODEL), lambda i: (i, 0)),
            full(score_w.shape),
            full(out_w.shape),
            full((1, D_MODEL)),
            full((1, D_MODEL)),
            full((N_EXPERTS, D_MODEL)),
            full((N_EXPERTS, 1)),
            full((c, c)),
        ],
        out_specs=[_tiled_spec(c, lambda i: (i, 0)), per_k, per_k, full((N_EXPERTS, 1))],
        out_shape=[
            jax.ShapeDtypeStruct(_tiled(t), F32),
            jax.ShapeDtypeStruct((t // c, TOP_K, c), jnp.int32),
            jax.ShapeDtypeStruct((t // c, TOP_K, c), F32),
            jax.ShapeDtypeStruct((N_EXPERTS, 1), jnp.int32),
        ],
        scratch_shapes=[pltpu.VMEM((c, score_w.shape[1]), BF16), pltpu.VMEM((N_EXPERTS, 1), F32)],
        compiler_params=pltpu.CompilerParams(dimension_semantics=("arbitrary",), vmem_limit_bytes=VMEM_LIMIT),
        name="xattn_router",
    )(x1, score_w, out_w, ln_w.reshape(1, -1), ln_b.reshape(1, -1),
      w_r.T, b_r.reshape(-1, 1), jnp.asarray(strict_lower.T, BF16))


TAIL_RUNS = (128, 64, 32, 16, 8, 4, 2, 1)


def _dispatch_kernel(fill_ref, pend_ref, nused_ref, slot_ref, x_ref, xs_hbm, zero_ref, sem, zsem, *, n_blocks):
    i = pl.program_id(0)
    c = SEQ_BLOCK

    @pl.when(i == 0)
    def _():
        zero_ref[...] = jnp.zeros_like(zero_ref)

        sub = ROW_TILE[0]

        def rows(ref, first, n):
            return ref.at[pl.ds(pl.multiple_of(first * sub, sub), n * sub)]

        def tail(e, act):
            row = fill_ref[e]
            pad = pend_ref[e] - row
            for run in TAIL_RUNS:
                hit = (pad & run) != 0

                @pl.when(hit)
                def _():
                    act(pltpu.make_async_copy(rows(zero_ref, 0, run), rows(xs_hbm, row, run), zsem))
                row = row + jnp.where(hit, run, 0)

        def spare(b, act):
            act(pltpu.make_async_copy(zero_ref, rows(xs_hbm, b * MOE_BLOCK, MOE_BLOCK), zsem))

        for act in (lambda cp: cp.start(), lambda cp: cp.wait()):
            lax.fori_loop(0, N_EXPERTS, lambda e, carry: (tail(e, act), carry)[1], 0)
            lax.fori_loop(nused_ref[0], n_blocks, lambda b, carry: (spare(b, act), carry)[1], 0)

    for q in range(x_ref.shape[0] // (c * ROW_TILE[0])):
        def issue(j, carry, q=q):
            for u in range(ISSUE_UNROLL):
                r = j * ISSUE_UNROLL + u
                for k in range(TOP_K):
                    dst = _tile_of(xs_hbm, slot_ref[0, 0, (q * TOP_K + k) * c + r])
                    pltpu.make_async_copy(_tile_of(x_ref, q * c + r), dst, sem).start(priority=k % 2)
            return carry
        lax.fori_loop(0, c // ISSUE_UNROLL, issue, 0)
    for k in range(TOP_K):
        pltpu.make_async_copy(x_ref, xs_hbm.at[pl.ds(0, x_ref.shape[0])], sem).wait()


def _dispatch(x2, slots, fill, pend, n_used, n_blocks):
    t = x2.shape[0] // ROW_TILE[0]
    c = min(DISPATCH_BLOCK, t)
    assert t % c == 0 and c % SEQ_BLOCK == 0
    grid_spec = pltpu.PrefetchScalarGridSpec(
        num_scalar_prefetch=3,
        grid=(t // c,),
        in_specs=[
            pl.BlockSpec((1, 1, c * TOP_K), lambda i, *_: (i, 0, 0), memory_space=pltpu.SMEM),
            _tiled_spec(c, lambda i, *_: (i, 0)),
        ],
        out_specs=pl.BlockSpec(memory_space=pl.ANY),
        scratch_shapes=[pltpu.VMEM(_tiled(MOE_BLOCK), F32), pltpu.SemaphoreType.DMA, pltpu.SemaphoreType.DMA],
    )
    return pl.pallas_call(
        functools.partial(_dispatch_kernel, n_blocks=n_blocks),
        grid_spec=grid_spec,
        out_shape=jax.ShapeDtypeStruct(_tiled(n_blocks * MOE_BLOCK), F32),
        compiler_params=pltpu.CompilerParams(dimension_semantics=("arbitrary",), vmem_limit_bytes=VMEM_LIMIT),
        name="dispatch",
    )(fill, pend, n_used, slots.reshape(t // c, 1, c * TOP_K), x2)


def _expert_kernel(be_ref, first_ref, nxt_ref, par_ref, nused_ref, x_ref, w1_hbm, b1_ref, w2_hbm, b2_ref, o_ref,
                   w1f, w2f, w1b, w2b, sem1, sem2):
    step = pl.program_id(0)
    rows = MOE_BLOCK * ROW_TILE[0]

    def fetch(e, p):
        return (pltpu.make_async_copy(w1_hbm.at[e], w1f.at[p], sem1.at[p]),
                pltpu.make_async_copy(w2_hbm.at[e], w2f.at[p], sem2.at[p]))

    @pl.when(step == 0)
    def _():
        for cp in fetch(be_ref[0], par_ref[0]):
            cp.start()

    def block(sb, carry):
        b = step * EXPERT_STEP_BLOCKS + sb
        x_blk = x_ref.at[pl.ds(pl.multiple_of(sb * rows, rows), rows)]
        o_blk = o_ref.at[pl.ds(pl.multiple_of(sb * rows, rows), rows)]

        @pl.when(b >= nused_ref[0])
        def _():
            o_blk[...] = jnp.zeros(o_blk.shape, o_blk.dtype)

        @pl.when(b < nused_ref[0])
        def _():
            e = be_ref[b]

            @pl.when(first_ref[b] == 1)
            def _():
                p = par_ref[b]
                for cp in fetch(e, p):
                    cp.wait()

                @pl.when(nxt_ref[b] >= 0)
                def _():
                    for cp in fetch(nxt_ref[b], 1 - p):
                        cp.start()

                w1b[...] = w1f[p].astype(BF16)
                w2b[...] = w2f[p].astype(BF16)

            h = _dot(_load_row_tiles(x_blk).astype(BF16), w1b[...]) + b1_ref[e]
            h_glu = jnp.minimum(h[:, :D_EXPERT], SWIGLU_LIMIT)
            h_lin = jnp.clip(h[:, D_EXPERT:], -SWIGLU_LIMIT, SWIGLU_LIMIT)
            act = h_glu * (1.0 / (1.0 + jnp.exp(-SWIGLU_ALPHA * h_glu))) * (h_lin + 1.0)
            _store_row_tiles(o_blk, _dot(act.astype(BF16), w2b[...]) + b2_ref[e])
        return carry

    lax.fori_loop(0, EXPERT_STEP_BLOCKS, block, 0)


def _experts(xs, plan, w1, b1, w2, b2):
    block_e, first, nxt, par, n_used = plan
    n_blocks = block_e.shape[0]
    assert n_blocks % EXPERT_STEP_BLOCKS == 0
    rows = lambda s, *_: (s, 0)
    whole = lambda s, *_: (0, 0, 0)
    grid_spec = pltpu.PrefetchScalarGridSpec(
        num_scalar_prefetch=5,
        grid=(n_blocks // EXPERT_STEP_BLOCKS,),
        in_specs=[
            _tiled_spec(EXPERT_STEP_BLOCKS * MOE_BLOCK, rows),
            pl.BlockSpec(memory_space=pl.ANY),
            pl.BlockSpec((N_EXPERTS, 1, 2 * D_EXPERT), whole),
            pl.BlockSpec(memory_space=pl.ANY),
            pl.BlockSpec((N_EXPERTS, 1, D_MODEL), whole),
        ],
        out_specs=_tiled_spec(EXPERT_STEP_BLOCKS * MOE_BLOCK, rows),
        scratch_shapes=[
            pltpu.VMEM((2, D_MODEL, 2 * D_EXPERT), F32),
            pltpu.VMEM((2, D_EXPERT, D_MODEL), F32),
            pltpu.VMEM((D_MODEL, 2 * D_EXPERT), BF16),
            pltpu.VMEM((D_EXPERT, D_MODEL), BF16),
            pltpu.SemaphoreType.DMA((2,)),
            pltpu.SemaphoreType.DMA((2,)),
        ],
    )
    return pl.pallas_call(
        _expert_kernel,
        grid_spec=grid_spec,
        out_shape=jax.ShapeDtypeStruct(xs.shape, F32),
        compiler_params=pltpu.CompilerParams(dimension_semantics=("arbitrary",), vmem_limit_bytes=EXPERT_VMEM_LIMIT),
        name="experts",
    )(block_e, first, nxt, par, n_used, xs, w1, b1.reshape(N_EXPERTS, 1, -1), w2, b2.reshape(N_EXPERTS, 1, -1))


def _combine_kernel(slot_ref, next_slot_ref, ys_hbm, gate_ref, x_ref, lnw_ref, lnb_ref, o_ref, ybuf, sem):
    i = pl.program_id(0)
    c = SEQ_BLOCK

    def gather(slots, p):
        def issue(j, carry):
            for u in range(ISSUE_UNROLL):
                r = j * ISSUE_UNROLL + u
                for k in range(TOP_K):
                    src = _tile_of(ys_hbm, slots[0, 0, k * c + r])
                    pltpu.make_async_copy(src, _tile_of(ybuf.at[p, k], r), sem.at[p]).start(priority=k % 2)
            return carry
        lax.fori_loop(0, c // ISSUE_UNROLL, issue, 0)

    @pl.when(i == 0)
    def _():
        gather(slot_ref, 0)

    @pl.when(i + 1 < pl.num_programs(0))
    def _():
        gather(next_slot_ref, (i + 1) % 2)

    p = i % 2
    for k in range(TOP_K):
        pltpu.make_async_copy(ys_hbm.at[pl.ds(0, ybuf.shape[2])], ybuf.at[p, k], sem.at[p]).wait()
    gates = gate_ref[...]
    ys = [_load_row_tiles(ybuf.at[p, k]) * gates[:, k:k + 1] for k in range(TOP_K)]
    y = (ys[0] + ys[1]) + (ys[2] + ys[3])
    o_ref[...] = _layer_norm(DEEPNORM_ALPHA * _load_row_tiles(x_ref) + y, lnw_ref[...], lnb_ref[...])


def _combine(ys, slots, gates, x2, ln_w, ln_b):
    t = x2.shape[0] // ROW_TILE[0]
    c = SEQ_BLOCK
    n = t // c
    slot_table = slots.reshape(n, 1, c * TOP_K)
    return pl.pallas_call(
        _combine_kernel,
        grid=(n,),
        in_specs=[
            pl.BlockSpec((1, 1, c * TOP_K), lambda i: (i, 0, 0), memory_space=pltpu.SMEM),
            pl.BlockSpec((1, 1, c * TOP_K), lambda i: (jnp.minimum(i + 1, n - 1), 0, 0), memory_space=pltpu.SMEM),
            pl.BlockSpec(memory_space=pl.ANY),
            pl.BlockSpec((c, TOP_K), lambda i: (i, 0)),
            _tiled_spec(c, lambda i: (i, 0)),
            pl.BlockSpec((1, D_MODEL), lambda i: (0, 0)),
            pl.BlockSpec((1, D_MODEL), lambda i: (0, 0)),
        ],
        out_specs=pl.BlockSpec((c, D_MODEL), lambda i: (i, 0)),
        out_shape=jax.ShapeDtypeStruct((t, D_MODEL), F32),
        scratch_shapes=[pltpu.VMEM((2, TOP_K) + _tiled(c), F32), pltpu.SemaphoreType.DMA((2,))],
        compiler_params=pltpu.CompilerParams(dimension_semantics=("arbitrary",), vmem_limit_bytes=VMEM_LIMIT),
        name="combine",
    )(slot_table, slot_table, ys, gates, x2, ln_w.reshape(1, -1), ln_b.reshape(1, -1))


def _slot_tables(counts, packed, n_blocks, shift):
    counts = counts.reshape(-1)
    nblk = (counts + MOE_BLOCK - 1) // MOE_BLOCK
    ids = jnp.arange(N_EXPERTS)
    lower = ids[None, :] <= ids[:, None]
    bend = jnp.sum(jnp.where(lower, nblk[None, :], 0), axis=1)
    n_used = bend[N_EXPERTS - 1]
    pend = bend * MOE_BLOCK
    pstart = pend - nblk * MOE_BLOCK
    fill = pstart + counts
    order = jnp.sum(jnp.where(lower, (nblk > 0)[None, :], False), axis=1) - 1
    later = (ids[None, :] > ids[:, None]) & (nblk > 0)[None, :]
    nxt_e = jnp.min(jnp.where(later, ids[None, :], N_EXPERTS), axis=1)
    nxt_e = jnp.where(nxt_e == N_EXPERTS, -1, nxt_e)
    b = jnp.minimum(jnp.arange(n_blocks), n_used - 1)
    hit = bend[None, :] <= b[:, None]
    block_e = jnp.sum(hit.astype(jnp.int32), axis=1)
    of_block = lambda per_expert: jnp.sum(jnp.where(ids[None, :] == block_e[:, None], per_expert[None, :], 0), axis=1)
    first = (b == of_block(bend - nblk)).astype(jnp.int32)
    expert = lax.shift_right_logical(packed, shift)
    rank = packed & ((1 << shift) - 1)
    slots = jnp.sum(jnp.where(expert[..., None] == ids, pstart, 0), axis=-1) + rank
    i32 = lambda a: a.astype(jnp.int32)
    plan = (i32(block_e), first, i32(of_block(nxt_e)), i32(of_block(order & 1)), i32(n_used.reshape(1)))
    return i32(slots), i32(fill), i32(pend), plan


def kernel(x, mem, w_in, lb_logits, hgrn_norm_w, w_pool, pool_scale, w_out, ln1_w, ln1_b, w_xq, w_xk, w_xv, w_xo,
           ln2_w, ln2_b, w_router, b_router, w1, b1, w2, b2, ln3_w, ln3_b):
    bsz, seq, d = x.shape
    assert bsz == 1 and d == D_MODEL and seq % SEQ_BLOCK == 0 and w_in.shape[0] == 1
    xt = x.reshape(seq, d)
    x1 = _mixer(xt, w_in[0], lb_logits, hgrn_norm_w[0], w_pool[0], pool_scale[0], w_out[0], ln1_w[0], ln1_b[0])
    score_w, out_w = _kv_proj(mem[0], w_xk[0], w_xv[0], w_xq[0], w_xo[0])
    x2, packed, gates, counts = _xattn_router(x1, score_w, out_w, ln2_w[0], ln2_b[0], w_router[0], b_router[0])
    n_blocks = -(-(seq * TOP_K + N_EXPERTS * (MOE_BLOCK - 1)) // MOE_BLOCK)
    slots, fill, pend, plan = _slot_tables(counts, packed, n_blocks, _rank_bits(seq))
    xs = _dispatch(x2, slots, fill, pend, plan[-1], n_blocks)
    ys = _experts(xs, plan, w1[0], b1[0], w2[0], b2[0])
    gates_tk = gates.transpose(0, 2, 1).reshape(seq, TOP_K)
    out = _combine(ys, slots, gates_tk, x2, ln3_w[0], ln3_b[0])
    return out.reshape(bsz, seq, d)
```

```python
import functools

import numpy as np
import jax
import jax.numpy as jnp
from jax import lax
from jax.experimental import pallas as pl
from jax.experimental.pallas import tpu as pltpu

F32 = jnp.float32
BF16 = jnp.bfloat16

D_MODEL = 1024
HGRN_WIDTH = 512
HGRN_HEADS = 4
HEAD_DIM = 128
POOL_WINDOWS = (2, 4, 8, 16)
POOL_GROUP = 128
POOL_WIDTH = POOL_GROUP * len(POOL_WINDOWS)
POOL_CARRY = 24
POOL_HEAD = 8
IN_PROJ_WIDTH = 4 * HGRN_WIDTH + POOL_WIDTH
XATTN_HEADS = 4
XATTN_HEAD_DIM = 256
N_EXPERTS = 32
TOP_K = 4
D_EXPERT = 1024
SWIGLU_LIMIT = 7.0
SWIGLU_ALPHA = 1.702
MOE_BLOCK = 256
DEEPNORM_ALPHA = 2.0 ** 0.25
LN_EPS = 1e-5
RMS_EPS = 1e-6

SEQ_BLOCK = 256
EXPERT_STEP_BLOCKS = 4
XATTN_BLOCK = 1024
DISPATCH_BLOCK = 2048
N_LEVELS = 8
MXU_LEVELS = 3
ISSUE_UNROLL = 2
ROW_TILE = (8, 128)
VMEM_LIMIT = 48 * 1024 * 1024
EXPERT_VMEM_LIMIT = 56 * 1024 * 1024


def _rank_bits(t):
    return max(int(t - 1).bit_length(), 1)


def _layer_norm(y, w, b):
    mu = jnp.mean(y, axis=-1, keepdims=True)
    yc = y - mu
    var = jnp.mean(yc * yc, axis=-1, keepdims=True)
    return yc * lax.rsqrt(var + LN_EPS) * w + b


def _tiled(n):
    return (n * ROW_TILE[0], ROW_TILE[1])


def _tiled_spec(n, index_map):
    return pl.BlockSpec(_tiled(n), index_map)


def _tile_of(ref, r):
    return ref.at[pl.ds(pl.multiple_of(r * ROW_TILE[0], ROW_TILE[0]), ROW_TILE[0])]


def _store_row_tiles(ref, val):
    s, l = ROW_TILE
    for j in range(s):
        ref[pl.ds(j, val.shape[0], stride=s), :] = val[:, j * l:(j + 1) * l]


def _load_row_tiles(ref):
    s, _ = ROW_TILE
    return jnp.concatenate([ref[pl.ds(j, ref.shape[0] // s, stride=s), :] for j in range(s)], axis=-1)


def _dot(a, b):
    return jnp.dot(a, b, preferred_element_type=F32)


def _dot_nt(a, b):
    return lax.dot_general(a, b, (((1,), (1,)), ((), ())), preferred_element_type=F32)


def _decay_tables():
    c = SEQ_BLOCK
    r = np.arange(c)[:, None]
    u = np.arange(c)[None, :]
    mats = [(u <= r)]
    for l in range(MXU_LEVELS):
        h = 1 << l
        mid = (r // (2 * h)) * (2 * h) + h
        upper = (r >= mid) & (u >= mid) & (u <= r)
        lower = (r < mid) & (u > r) & (u < mid)
        mats.append(upper | lower)
    mall = np.concatenate(mats, axis=0).astype(np.float32)
    x = r ^ u
    lvl = np.where(u < r, np.floor(np.log2(np.maximum(x, 1))).astype(np.int32), -1).astype(np.int32)
    strict_lower = (u < r).astype(np.float32)
    return mall, lvl, strict_lower


def _level_gap(negb, l):
    half = 1 << l
    parts = []
    for base in range(0, negb.shape[0], 2 * half):
        ref_row = negb[base + half - 1:base + half, :]
        parts.append(jnp.abs(negb[base:base + 2 * half, :] - ref_row))
    return jnp.concatenate(parts, axis=0) if len(parts) > 1 else parts[0]


def _mixer_kernel(x_ref, win_ref, lbl_ref, nw_ref, wpool_ref, pscale_ref, wout_ref, lnw_ref, lnb_ref,
                  mall_ref, lvl_ref, o_ref, state_ref, carry_ref, ext_ref, mix_ref, sum_ref):
    i = pl.program_id(0)
    c = SEQ_BLOCK

    @pl.when(i == 0)
    def _():
        state_ref[...] = jnp.zeros_like(state_ref)
        carry_ref[...] = jnp.zeros_like(carry_ref)
        sum_ref[:, 0:POOL_HEAD, :] = jnp.zeros((2, POOL_HEAD, POOL_WIDTH), F32)

    x = x_ref[...]
    proj = _dot(x.astype(BF16), win_ref[...])
    w = HGRN_WIDTH
    q = proj[:, 0:w]
    z = proj[:, w:2 * w]
    v = proj[:, 2 * w:3 * w]
    g = proj[:, 3 * w:4 * w]
    p = proj[:, 4 * w:]

    ll = lbl_ref[...]
    ee = jnp.exp(ll - jnp.max(ll, axis=0, keepdims=True))
    lb = ee[0:1] / jnp.sum(ee, axis=0, keepdims=True)

    f = lb + (1.0 - lb) * (1.0 / (1.0 + jnp.exp(-z)))
    nlf = -jnp.log(f)
    key = (1.0 - lb) * (1.0 / (1.0 + jnp.exp(z)))

    hi = nlf.astype(BF16)
    lo = (nlf - hi.astype(F32)).astype(BF16)
    mall = mall_ref[...]
    cum = _dot(mall, hi) + _dot(mall, lo)
    lvl = lvl_ref[...]

    for h in range(HGRN_HEADS):
        hs = slice(h * HEAD_DIM, (h + 1) * HEAD_DIM)
        qh, kh, vh, gh = q[:, hs], key[:, hs], v[:, hs], g[:, hs]
        vb = vh.astype(BF16)
        negb = cum[0:c, hs]
        scores = jnp.zeros((c, c), F32)
        for l in range(N_LEVELS):
            gap = cum[(l + 1) * c:(l + 2) * c, hs] if l < MXU_LEVELS else _level_gap(negb, l)
            e = jnp.exp(-gap)
            pl_ = _dot_nt((qh * e).astype(BF16), (kh * e).astype(BF16))
            scores = jnp.where(lvl == l, pl_, scores)
        diag = jnp.sum(qh * kh, axis=-1, keepdims=True)
        st = state_ref[h]
        o = (_dot(scores.astype(BF16), vb) + diag * vh
             + _dot_nt((qh * jnp.exp(-negb)).astype(BF16), st.astype(BF16)))
        negb_last = negb[c - 1:c, :]
        kd = (kh * jnp.exp(negb - negb_last)).astype(BF16)
        state_ref[h] = st * jnp.exp(-negb_last) + _dot(vh.T.astype(BF16), kd)
        o = o * lax.rsqrt(jnp.mean(o * o, axis=-1, keepdims=True) + RMS_EPS) * nw_ref[:, hs]
        gate = gh * (1.0 / (1.0 + jnp.exp(-gh)))
        mix_ref[:, hs] = (o * gate).astype(BF16)

    top = POOL_CARRY + c
    ext_ref[0:POOL_CARRY, :] = carry_ref[...]
    ext_ref[POOL_CARRY:top, :] = p
    carry_ref[...] = p[c - POOL_CARRY:c, :]
    pos = i * c + lax.broadcasted_iota(jnp.int32, (c, 1), 0)
    src = ext_ref
    for gi, win in enumerate(POOL_WINDOWS):
        gs = slice(gi * POOL_GROUP, (gi + 1) * POOL_GROUP)
        rest = slice(gi * POOL_GROUP, POOL_WIDTH)
        half = win // 2
        dst = sum_ref.at[gi % 2]
        dst[POOL_HEAD:top, rest] = src[POOL_HEAD:top, rest] + src[POOL_HEAD - half:top - half, rest]
        src = dst
        pg = p[:, gs]
        tot = dst[POOL_CARRY:top, gs]
        cnt = jnp.minimum(pos + 1, win).astype(F32)
        pooled = _dot((tot / cnt - pg).astype(BF16), wpool_ref[gi])
        mix_ref[:, w + gi * POOL_GROUP:w + (gi + 1) * POOL_GROUP] = (pooled * pscale_ref[:, gs]).astype(BF16)

    y = DEEPNORM_ALPHA * x + _dot(mix_ref[...], wout_ref[...])
    o_ref[...] = _layer_norm(y, lnw_ref[...], lnb_ref[...])


def _mixer(x, w_in, lb_logits, norm_w, w_pool, pool_scale, w_out, ln_w, ln_b):
    t = x.shape[0]
    c = SEQ_BLOCK
    mall, lvl, _ = _decay_tables()
    full = lambda shape: pl.BlockSpec(shape, lambda i: (0,) * len(shape))
    return pl.pallas_call(
        _mixer_kernel,
        grid=(t // c,),
        in_specs=[
            pl.BlockSpec((c, D_MODEL), lambda i: (i, 0)),
            full((D_MODEL, IN_PROJ_WIDTH)),
            full(lb_logits.shape),
            full((1, HGRN_WIDTH)),
            full((len(POOL_WINDOWS), POOL_GROUP, POOL_GROUP)),
            full((1, POOL_WIDTH)),
            full((D_MODEL, D_MODEL)),
            full((1, D_MODEL)),
            full((1, D_MODEL)),
            full(mall.shape),
            full(lvl.shape),
        ],
        out_specs=pl.BlockSpec((c, D_MODEL), lambda i: (i, 0)),
        out_shape=jax.ShapeDtypeStruct((t, D_MODEL), F32),
        scratch_shapes=[
            pltpu.VMEM((HGRN_HEADS, HEAD_DIM, HEAD_DIM), F32),
            pltpu.VMEM((POOL_CARRY, POOL_WIDTH), F32),
            pltpu.VMEM((POOL_CARRY + c, POOL_WIDTH), F32),
            pltpu.VMEM((c, D_MODEL), BF16),
            pltpu.VMEM((2, POOL_CARRY + c, POOL_WIDTH), F32),
        ],
        compiler_params=pltpu.CompilerParams(dimension_semantics=("arbitrary",), vmem_limit_bytes=VMEM_LIMIT),
        name="mixer",
    )(x, w_in.astype(BF16), lb_logits, norm_w.reshape(1, -1), w_pool.astype(BF16), pool_scale.reshape(1, -1),
      w_out.astype(BF16), ln_w.reshape(1, -1), ln_b.reshape(1, -1), jnp.asarray(mall, BF16), jnp.asarray(lvl))


def _kv_kernel(mem_ref, wk_ref, wv_ref, wq_ref, wo_ref, a_ref, b_ref):
    n_mem = mem_ref.shape[0]
    m = mem_ref[...].astype(BF16)
    k = _dot(m, wk_ref[...]).astype(BF16)
    v = _dot(m, wv_ref[...]).astype(BF16)
    scale = XATTN_HEAD_DIM ** -0.5
    for h in range(XATTN_HEADS):
        hs = slice(h * XATTN_HEAD_DIM, (h + 1) * XATTN_HEAD_DIM)
        ms = slice(h * n_mem, (h + 1) * n_mem)
        a_ref[:, ms] = (_dot_nt(wq_ref[:, hs], k[:, hs]) * scale).astype(BF16)
        b_ref[ms, :] = _dot(v[:, hs], wo_ref[hs, :]).astype(BF16)


def _kv_proj(mem, wk, wv, wq, wo):
    n = mem.shape[0]
    return pl.pallas_call(
        _kv_kernel,
        out_shape=(jax.ShapeDtypeStruct((D_MODEL, XATTN_HEADS * n), BF16),
                   jax.ShapeDtypeStruct((XATTN_HEADS * n, D_MODEL), BF16)),
        compiler_params=pltpu.CompilerParams(vmem_limit_bytes=VMEM_LIMIT),
        name="kv_proj",
    )(mem, wk.astype(BF16), wv.astype(BF16), wq.astype(BF16), wo.astype(BF16))


def _split3(a):
    a0 = a.astype(BF16)
    r1 = a - a0.astype(F32)
    a1 = r1.astype(BF16)
    a2 = (r1 - a1.astype(F32)).astype(BF16)
    return a0, a1, a2


def _xattn_kernel(x_ref, a_ref, b_ref, lnw_ref, lnb_ref, wrt_ref, br_ref, ut_ref,
                  x2_ref, dest_ref, gate_ref, cnt_ref, att_ref, run_ref, *, shift):
    i = pl.program_id(0)
    c = x_ref.shape[0]
    n = N_EXPERTS

    @pl.when(i == 0)
    def _():
        run_ref[...] = jnp.zeros_like(run_ref)

    x = x_ref[...]
    scores = _dot(x.astype(BF16), a_ref[...])
    n_mem = a_ref.shape[1] // XATTN_HEADS
    for h in range(XATTN_HEADS):
        ms = slice(h * n_mem, (h + 1) * n_mem)
        s = scores[:, ms]
        e = jnp.exp(s - jnp.max(s, axis=-1, keepdims=True))
        att_ref[:, ms] = (e / jnp.sum(e, axis=-1, keepdims=True)).astype(BF16)
    y = DEEPNORM_ALPHA * x + _dot(att_ref[...], b_ref[...])
    x2 = _layer_norm(y, lnw_ref[...], lnb_ref[...])
    _store_row_tiles(x2_ref, x2)

    a0, a1, _ = _split3(x2)
    w0, w1, w2 = _split3(wrt_ref[...])
    pa = _dot_nt(jnp.concatenate([w0, w1, w2], axis=0), a0)
    pb = _dot_nt(jnp.concatenate([w0, w1], axis=0), a1)
    logits = ((pa[2 * n:3 * n] + pb[n:2 * n]) + (pb[0:n] + pa[n:2 * n]) + pa[0:n]) + br_ref[...]

    eid = lax.broadcasted_iota(jnp.int32, (n, c), 0)
    work = logits
    sels, vals, idxs = [], [], []
    for k in range(TOP_K):
        m = jnp.max(work, axis=0, keepdims=True)
        idx = jnp.min(jnp.where(work == m, eid, n), axis=0, keepdims=True)
        sel = eid == idx
        sels.append(sel)
        vals.append(m)
        idxs.append(idx)
        work = jnp.where(sel, -jnp.inf, work)
    es = [jnp.exp(vk - vals[0]) for vk in vals]
    denom = es[0] + es[1] + es[2] + es[3]
    gate_ref[0] = jnp.concatenate([ek / denom for ek in es], axis=0)

    onehot = jnp.zeros((n, c), F32)
    for sel in sels:
        onehot = jnp.where(sel, 1.0, onehot)
    before = _dot(onehot.astype(BF16), ut_ref[...]) + run_ref[...]
    ranks = [jnp.sum(jnp.where(sel, before, 0.0), axis=0, keepdims=True).astype(jnp.int32) for sel in sels]
    run_ref[...] = run_ref[...] + jnp.sum(onehot, axis=1, keepdims=True)

    dest_ref[0] = jnp.concatenate([idx * (1 << shift) + rk for idx, rk in zip(idxs, ranks)], axis=0)
    cnt_ref[...] = run_ref[...].astype(jnp.int32)


def _xattn_router(x1, score_w, out_w, ln_w, ln_b, w_r, b_r):
    t = x1.shape[0]
    c = XATTN_BLOCK if t % XATTN_BLOCK == 0 else SEQ_BLOCK
    idx = np.arange(c)
    strict_upper = (idx[:, None] < idx[None, :]).astype(np.float32)
    full = lambda shape: pl.BlockSpec(shape, lambda i: (0,) * len(shape))
    per_k = pl.BlockSpec((1, TOP_K, c), lambda i: (i, 0, 0))
    return pl.pallas_call(
        functools.partial(_xattn_kernel, shift=_rank_bits(t)),
        grid=(t // c,),
        in_specs=[
            pl.BlockSpec((c, D_MODEL), lambda i: (i, 0)),
            full(score_w.shape),
            full(out_w.shape),
            full((1, D_MODEL)),
            full((1, D_MODEL)),
            full((N_EXPERTS, D_MODEL)),
            full((N_EXPERTS, 1)),
            full((c, c)),
        ],
        out_specs=[_tiled_spec(c, lambda i: (i, 0)), per_k, per_k, full((N_EXPERTS, 1))],
        out_shape=[
            jax.ShapeDtypeStruct(_tiled(t), F32),
            jax.ShapeDtypeStruct((t // c, TOP_K, c), jnp.int32),
            jax.ShapeDtypeStruct((t // c, TOP_K, c), F32),
            jax.ShapeDtypeStruct((N_EXPERTS, 1), jnp.int32),
        ],
        scratch_shapes=[pltpu.VMEM((c, score_w.shape[1]), BF16), pltpu.VMEM((N_EXPERTS, 1), F32)],
        compiler_params=pltpu.CompilerParams(dimension_semantics=("arbitrary",), vmem_limit_bytes=VMEM_LIMIT),
        name="xattn_router",
    )(x1, score_w, out_w, ln_w.reshape(1, -1), ln_b.reshape(1, -1),
      w_r.T, b_r.reshape(-1, 1), jnp.asarray(strict_upper, BF16))


TAIL_RUNS = (128, 64, 32, 16, 8, 4, 2, 1)


def _dispatch_kernel(fill_ref, pend_ref, nused_ref, slot_ref, x_ref, xs_hbm, zero_ref, sem, zsem, *, n_blocks):
    i = pl.program_id(0)
    c = SEQ_BLOCK

    @pl.when(i == 0)
    def _():
        zero_ref[...] = jnp.zeros_like(zero_ref)

        sub = ROW_TILE[0]

        def rows(ref, first, n):
            return ref.at[pl.ds(pl.multiple_of(first * sub, sub), n * sub)]

        def tail(e, act):
            row = fill_ref[e]
            pad = pend_ref[e] - row
            for run in TAIL_RUNS:
                hit = (pad & run) != 0

                @pl.when(hit)
                def _():
                    act(pltpu.make_async_copy(rows(zero_ref, 0, run), rows(xs_hbm, row, run), zsem))
                row = row + jnp.where(hit, run, 0)

        def spare(b, act):
            act(pltpu.make_async_copy(zero_ref, rows(xs_hbm, b * MOE_BLOCK, MOE_BLOCK), zsem))

        for act in (lambda cp: cp.start(), lambda cp: cp.wait()):
            lax.fori_loop(0, N_EXPERTS, lambda e, carry: (tail(e, act), carry)[1], 0)
            lax.fori_loop(nused_ref[0], n_blocks, lambda b, carry: (spare(b, act), carry)[1], 0)

    for q in range(x_ref.shape[0] // (c * ROW_TILE[0])):
        def issue(j, carry, q=q):
            for u in range(ISSUE_UNROLL):
                r = j * ISSUE_UNROLL + u
                for k in range(TOP_K):
                    dst = _tile_of(xs_hbm, slot_ref[0, 0, (q * TOP_K + k) * c + r])
                    pltpu.make_async_copy(_tile_of(x_ref, q * c + r), dst, sem).start(priority=k % 2)
            return carry
        lax.fori_loop(0, c // ISSUE_UNROLL, issue, 0)
    for k in range(TOP_K):
        pltpu.make_async_copy(x_ref, xs_hbm.at[pl.ds(0, x_ref.shape[0])], sem).wait()


def _dispatch(x2, slots, fill, pend, n_used, n_blocks):
    t = x2.shape[0] // ROW_TILE[0]
    c = min(DISPATCH_BLOCK, t)
    assert t % c == 0 and c % SEQ_BLOCK == 0
    grid_spec = pltpu.PrefetchScalarGridSpec(
        num_scalar_prefetch=3,
        grid=(t // c,),
        in_specs=[
            pl.BlockSpec((1, 1, c * TOP_K), lambda i, *_: (i, 0, 0), memory_space=pltpu.SMEM),
            _tiled_spec(c, lambda i, *_: (i, 0)),
        ],
        out_specs=pl.BlockSpec(memory_space=pl.ANY),
        scratch_shapes=[pltpu.VMEM(_tiled(MOE_BLOCK), F32), pltpu.SemaphoreType.DMA, pltpu.SemaphoreType.DMA],
    )
    return pl.pallas_call(
        functools.partial(_dispatch_kernel, n_blocks=n_blocks),
        grid_spec=grid_spec,
        out_shape=jax.ShapeDtypeStruct(_tiled(n_blocks * MOE_BLOCK), F32),
        compiler_params=pltpu.CompilerParams(dimension_semantics=("arbitrary",), vmem_limit_bytes=VMEM_LIMIT),
        name="dispatch",
    )(fill, pend, n_used, slots.reshape(t // c, 1, c * TOP_K), x2)


def _expert_kernel(be_ref, first_ref, nxt_ref, par_ref, nused_ref, x_ref, w1_hbm, b1_ref, w2_hbm, b2_ref, o_ref,
                   w1f, w2f, w1b, w2b, sem1, sem2):
    step = pl.program_id(0)
    rows = MOE_BLOCK * ROW_TILE[0]

    def fetch(e, p):
        return (pltpu.make_async_copy(w1_hbm.at[e], w1f.at[p], sem1.at[p]),
                pltpu.make_async_copy(w2_hbm.at[e], w2f.at[p], sem2.at[p]))

    @pl.when(step == 0)
    def _():
        for cp in fetch(be_ref[0], par_ref[0]):
            cp.start()

    def block(sb, carry):
        b = step * EXPERT_STEP_BLOCKS + sb
        x_blk = x_ref.at[pl.ds(pl.multiple_of(sb * rows, rows), rows)]
        o_blk = o_ref.at[pl.ds(pl.multiple_of(sb * rows, rows), rows)]

        @pl.when(b >= nused_ref[0])
        def _():
            o_blk[...] = jnp.zeros(o_blk.shape, o_blk.dtype)

        @pl.when(b < nused_ref[0])
        def _():
            e = be_ref[b]

            @pl.when(first_ref[b] == 1)
            def _():
                p = par_ref[b]
                for cp in fetch(e, p):
                    cp.wait()

                @pl.when(nxt_ref[b] >= 0)
                def _():
                    for cp in fetch(nxt_ref[b], 1 - p):
                        cp.start()

                w1b[...] = w1f[p].astype(BF16)
                w2b[...] = w2f[p].astype(BF16)

            h = _dot(_load_row_tiles(x_blk).astype(BF16), w1b[...]) + b1_ref[e]
            h_glu = jnp.minimum(h[:, :D_EXPERT], SWIGLU_LIMIT)
            h_lin = jnp.clip(h[:, D_EXPERT:], -SWIGLU_LIMIT, SWIGLU_LIMIT)
            act = h_glu * (1.0 / (1.0 + jnp.exp(-SWIGLU_ALPHA * h_glu))) * (h_lin + 1.0)
            _store_row_tiles(o_blk, _dot(act.astype(BF16), w2b[...]) + b2_ref[e])
        return carry

    lax.fori_loop(0, EXPERT_STEP_BLOCKS, block, 0)


def _experts(xs, plan, w1, b1, w2, b2):
    block_e, first, nxt, par, n_used = plan
    n_blocks = block_e.shape[0]
    assert n_blocks % EXPERT_STEP_BLOCKS == 0
    rows = lambda s, *_: (s, 0)
    whole = lambda s, *_: (0, 0, 0)
    grid_spec = pltpu.PrefetchScalarGridSpec(
        num_scalar_prefetch=5,
        grid=(n_blocks // EXPERT_STEP_BLOCKS,),
        in_specs=[
            _tiled_spec(EXPERT_STEP_BLOCKS * MOE_BLOCK, rows),
            pl.BlockSpec(memory_space=pl.ANY),
            pl.BlockSpec((N_EXPERTS, 1, 2 * D_EXPERT), whole),
            pl.BlockSpec(memory_space=pl.ANY),
            pl.BlockSpec((N_EXPERTS, 1, D_MODEL), whole),
        ],
        out_specs=_tiled_spec(EXPERT_STEP_BLOCKS * MOE_BLOCK, rows),
        scratch_shapes=[
            pltpu.VMEM((2, D_MODEL, 2 * D_EXPERT), F32),
            pltpu.VMEM((2, D_EXPERT, D_MODEL), F32),
            pltpu.VMEM((D_MODEL, 2 * D_EXPERT), BF16),
            pltpu.VMEM((D_EXPERT, D_MODEL), BF16),
            pltpu.SemaphoreType.DMA((2,)),
            pltpu.SemaphoreType.DMA((2,)),
        ],
    )
    return pl.pallas_call(
        _expert_kernel,
        grid_spec=grid_spec,
        out_shape=jax.ShapeDtypeStruct(xs.shape, F32),
        compiler_params=pltpu.CompilerParams(dimension_semantics=("arbitrary",), vmem_limit_bytes=EXPERT_VMEM_LIMIT),
        name="experts",
    )(block_e, first, nxt, par, n_used, xs, w1, b1.reshape(N_EXPERTS, 1, -1), w2, b2.reshape(N_EXPERTS, 1, -1))


def _combine_kernel(slot_ref, next_slot_ref, ys_hbm, gate_ref, x_ref, lnw_ref, lnb_ref, o_ref, ybuf, sem):
    i = pl.program_id(0)
    c = SEQ_BLOCK

    def gather(slots, p):
        def issue(j, carry):
            for u in range(ISSUE_UNROLL):
                r = j * ISSUE_UNROLL + u
                for k in range(TOP_K):
                    src = _tile_of(ys_hbm, slots[0, 0, k * c + r])
                    pltpu.make_async_copy(src, _tile_of(ybuf.at[p, k], r), sem.at[p]).start(priority=k % 2)
            return carry
        lax.fori_loop(0, c // ISSUE_UNROLL, issue, 0)

    @pl.when(i == 0)
    def _():
        gather(slot_ref, 0)

    @pl.when(i + 1 < pl.num_programs(0))
    def _():
        gather(next_slot_ref, (i + 1) % 2)

    p = i % 2
    for k in range(TOP_K):
        pltpu.make_async_copy(ys_hbm.at[pl.ds(0, ybuf.shape[2])], ybuf.at[p, k], sem.at[p]).wait()
    gates = gate_ref[...]
    ys = [_load_row_tiles(ybuf.at[p, k]) * gates[:, k:k + 1] for k in range(TOP_K)]
    y = (ys[0] + ys[1]) + (ys[2] + ys[3])
    o_ref[...] = _layer_norm(DEEPNORM_ALPHA * _load_row_tiles(x_ref) + y, lnw_ref[...], lnb_ref[...])


def _combine(ys, slots, gates, x2, ln_w, ln_b):
    t = x2.shape[0] // ROW_TILE[0]
    c = SEQ_BLOCK
    n = t // c
    slot_table = slots.reshape(n, 1, c * TOP_K)
    return pl.pallas_call(
        _combine_kernel,
        grid=(n,),
        in_specs=[
            pl.BlockSpec((1, 1, c * TOP_K), lambda i: (i, 0, 0), memory_space=pltpu.SMEM),
            pl.BlockSpec((1, 1, c * TOP_K), lambda i: (jnp.minimum(i + 1, n - 1), 0, 0), memory_space=pltpu.SMEM),
            pl.BlockSpec(memory_space=pl.ANY),
            pl.BlockSpec((c, TOP_K), lambda i: (i, 0)),
            _tiled_spec(c, lambda i: (i, 0)),
            pl.BlockSpec((1, D_MODEL), lambda i: (0, 0)),
            pl.BlockSpec((1, D_MODEL), lambda i: (0, 0)),
        ],
        out_specs=pl.BlockSpec((c, D_MODEL), lambda i: (i, 0)),
        out_shape=jax.ShapeDtypeStruct((t, D_MODEL), F32),
        scratch_shapes=[pltpu.VMEM((2, TOP_K) + _tiled(c), F32), pltpu.SemaphoreType.DMA((2,))],
        compiler_params=pltpu.CompilerParams(dimension_semantics=("arbitrary",), vmem_limit_bytes=VMEM_LIMIT),
        name="combine",
    )(slot_table, slot_table, ys, gates, x2, ln_w.reshape(1, -1), ln_b.reshape(1, -1))


def _slot_tables(counts, packed, n_blocks, shift):
    counts = counts.reshape(-1)
    nblk = (counts + MOE_BLOCK - 1) // MOE_BLOCK
    ids = jnp.arange(N_EXPERTS)
    lower = ids[None, :] <= ids[:, None]
    bend = jnp.sum(jnp.where(lower, nblk[None, :], 0), axis=1)
    n_used = bend[N_EXPERTS - 1]
    pend = bend * MOE_BLOCK
    pstart = pend - nblk * MOE_BLOCK
    fill = pstart + counts
    order = jnp.sum(jnp.where(lower, (nblk > 0)[None, :], False), axis=1) - 1
    later = (ids[None, :] > ids[:, None]) & (nblk > 0)[None, :]
    nxt_e = jnp.min(jnp.where(later, ids[None, :], N_EXPERTS), axis=1)
    nxt_e = jnp.where(nxt_e == N_EXPERTS, -1, nxt_e)
    b = jnp.minimum(jnp.arange(n_blocks), n_used - 1)
    hit = bend[None, :] <= b[:, None]
    block_e = jnp.sum(hit.astype(jnp.int32), axis=1)
    of_block = lambda per_expert: jnp.sum(jnp.where(ids[None, :] == block_e[:, None], per_expert[None, :], 0), axis=1)
    first = (b == of_block(bend - nblk)).astype(jnp.int32)
    expert = lax.shift_right_logical(packed, shift)
    rank = packed & ((1 << shift) - 1)
    slots = jnp.sum(jnp.where(expert[..., None] == ids, pstart, 0), axis=-1) + rank
    i32 = lambda a: a.astype(jnp.int32)
    plan = (i32(block_e), first, i32(of_block(nxt_e)), i32(of_block(order & 1)), i32(n_used.reshape(1)))
    return i32(slots), i32(fill), i32(pend), plan


def kernel(x, mem, w_in, lb_logits, hgrn_norm_w, w_pool, pool_scale, w_out, ln1_w, ln1_b, w_xq, w_xk, w_xv, w_xo,
           ln2_w, ln2_b, w_router, b_router, w1, b1, w2, b2, ln3_w, ln3_b):
    bsz, seq, d = x.shape
    assert bsz == 1 and d == D_MODEL and seq % SEQ_BLOCK == 0 and w_in.shape[0] == 1
    xt = x.reshape(seq, d)
    x1 = _mixer(xt, w_in[0], lb_logits, hgrn_norm_w[0], w_pool[0], pool_scale[0], w_out[0], ln1_w[0], ln1_b[0])
    score_w, out_w = _kv_proj(mem[0], w_xk[0], w_xv[0], w_xq[0], w_xo[0])
    x2, packed, gates, counts = _xattn_router(x1, score_w, out_w, ln2_w[0], ln2_b[0], w_router[0], b_router[0])
    n_blocks = -(-(seq * TOP_K + N_EXPERTS * (MOE_BLOCK - 1)) // MOE_BLOCK)
    slots, fill, pend, plan = _slot_tables(counts, packed, n_blocks, _rank_bits(seq))
    nb, _, cx = slots.shape
    slots = slots.reshape(nb, TOP_K, cx // SEQ_BLOCK, SEQ_BLOCK).transpose(0, 2, 1, 3).reshape(-1, TOP_K, SEQ_BLOCK)
    xs = _dispatch(x2, slots, fill, pend, plan[-1], n_blocks)
    ys = _experts(xs, plan, w1[0], b1[0], w2[0], b2[0])
    gates_tk = gates.transpose(0, 2, 1).reshape(seq, TOP_K)
    out = _combine(ys, slots, gates_tk, x2, ln3_w[0], ln3_b[0])
    return out.reshape(bsz, seq, d)
```

```python
import functools

import numpy as np
import jax
import jax.numpy as jnp
from jax import lax
from jax.experimental import pallas as pl
from jax.experimental.pallas import tpu as pltpu

F32 = jnp.float32
BF16 = jnp.bfloat16

D_MODEL = 1024
HGRN_WIDTH = 512
HGRN_HEADS = 4
HEAD_DIM = 128
POOL_WINDOWS = (2, 4, 8, 16)
POOL_GROUP = 128
POOL_WIDTH = POOL_GROUP * len(POOL_WINDOWS)
POOL_CARRY = 24
POOL_HEAD = 8
IN_PROJ_WIDTH = 4 * HGRN_WIDTH + POOL_WIDTH
XATTN_HEADS = 4
XATTN_HEAD_DIM = 256
N_EXPERTS = 32
TOP_K = 4
D_EXPERT = 1024
SWIGLU_LIMIT = 7.0
SWIGLU_ALPHA = 1.702
MOE_BLOCK = 256
DEEPNORM_ALPHA = 2.0 ** 0.25
LN_EPS = 1e-5
RMS_EPS = 1e-6

SEQ_BLOCK = 256
EXPERT_STEP_BLOCKS = 4
XATTN_BLOCK = 1024
DISPATCH_BLOCK = 2048
N_LEVELS = 8
MXU_LEVELS = 3
ISSUE_UNROLL = 2
GATHER_UNROLL = 8
ROW_TILE = (8, 128)
VMEM_LIMIT = 48 * 1024 * 1024
EXPERT_VMEM_LIMIT = 56 * 1024 * 1024


def _rank_bits(t):
    return max(int(t - 1).bit_length(), 1)


def _layer_norm(y, w, b):
    mu = jnp.mean(y, axis=-1, keepdims=True)
    yc = y - mu
    var = jnp.mean(yc * yc, axis=-1, keepdims=True)
    return yc * lax.rsqrt(var + LN_EPS) * w + b


def _tiled(n):
    return (n * ROW_TILE[0], ROW_TILE[1])


def _tiled_spec(n, index_map):
    return pl.BlockSpec(_tiled(n), index_map)


def _tile_of(ref, r):
    return ref.at[pl.ds(pl.multiple_of(r * ROW_TILE[0], ROW_TILE[0]), ROW_TILE[0])]


def _store_row_tiles(ref, val):
    s, l = ROW_TILE
    for j in range(s):
        ref[pl.ds(j, val.shape[0], stride=s), :] = val[:, j * l:(j + 1) * l]


def _load_row_tiles(ref):
    s, _ = ROW_TILE
    return jnp.concatenate([ref[pl.ds(j, ref.shape[0] // s, stride=s), :] for j in range(s)], axis=-1)


def _dot(a, b):
    return jnp.dot(a, b, preferred_element_type=F32)


def _dot_nt(a, b):
    return lax.dot_general(a, b, (((1,), (1,)), ((), ())), preferred_element_type=F32)


def _decay_tables():
    c = SEQ_BLOCK
    r = np.arange(c)[:, None]
    u = np.arange(c)[None, :]
    mats = [(u <= r)]
    for l in range(MXU_LEVELS):
        h = 1 << l
        mid = (r // (2 * h)) * (2 * h) + h
        upper = (r >= mid) & (u >= mid) & (u <= r)
        lower = (r < mid) & (u > r) & (u < mid)
        mats.append(upper | lower)
    mall = np.concatenate(mats, axis=0).astype(np.float32)
    x = r ^ u
    lvl = np.where(u < r, np.floor(np.log2(np.maximum(x, 1))).astype(np.int32), -1).astype(np.int32)
    strict_lower = (u < r).astype(np.float32)
    return mall, lvl, strict_lower


def _level_gap(negb, l):
    half = 1 << l
    parts = []
    for base in range(0, negb.shape[0], 2 * half):
        ref_row = negb[base + half - 1:base + half, :]
        parts.append(jnp.abs(negb[base:base + 2 * half, :] - ref_row))
    return jnp.concatenate(parts, axis=0) if len(parts) > 1 else parts[0]


def _mixer_kernel(x_ref, win_ref, lbl_ref, nw_ref, wpool_ref, pscale_ref, wout_ref, lnw_ref, lnb_ref,
                  mall_ref, lvl_ref, o_ref, state_ref, carry_ref, ext_ref, mix_ref, sum_ref):
    i = pl.program_id(0)
    c = SEQ_BLOCK

    @pl.when(i == 0)
    def _():
        state_ref[...] = jnp.zeros_like(state_ref)
        carry_ref[...] = jnp.zeros_like(carry_ref)
        sum_ref[:, 0:POOL_HEAD, :] = jnp.zeros((2, POOL_HEAD, POOL_WIDTH), F32)

    x = x_ref[...]
    proj = _dot(x.astype(BF16), win_ref[...])
    w = HGRN_WIDTH
    q = proj[:, 0:w]
    z = proj[:, w:2 * w]
    v = proj[:, 2 * w:3 * w]
    g = proj[:, 3 * w:4 * w]
    p = proj[:, 4 * w:]

    ll = lbl_ref[...]
    ee = jnp.exp(ll - jnp.max(ll, axis=0, keepdims=True))
    lb = ee[0:1] / jnp.sum(ee, axis=0, keepdims=True)

    f = lb + (1.0 - lb) * (1.0 / (1.0 + jnp.exp(-z)))
    nlf = -jnp.log(f)
    key = (1.0 - lb) * (1.0 / (1.0 + jnp.exp(z)))

    hi = nlf.astype(BF16)
    lo = (nlf - hi.astype(F32)).astype(BF16)
    mall = mall_ref[...]
    cum = _dot(mall, hi) + _dot(mall, lo)
    lvl = lvl_ref[...]

    for h in range(HGRN_HEADS):
        hs = slice(h * HEAD_DIM, (h + 1) * HEAD_DIM)
        qh, kh, vh, gh = q[:, hs], key[:, hs], v[:, hs], g[:, hs]
        vb = vh.astype(BF16)
        negb = cum[0:c, hs]
        scores = jnp.zeros((c, c), F32)
        for l in range(N_LEVELS):
            gap = cum[(l + 1) * c:(l + 2) * c, hs] if l < MXU_LEVELS else _level_gap(negb, l)
            e = jnp.exp(-gap)
            pl_ = _dot_nt((qh * e).astype(BF16), (kh * e).astype(BF16))
            scores = jnp.where(lvl == l, pl_, scores)
        diag = jnp.sum(qh * kh, axis=-1, keepdims=True)
        st = state_ref[h]
        o = (_dot(scores.astype(BF16), vb) + diag * vh
             + _dot_nt((qh * jnp.exp(-negb)).astype(BF16), st.astype(BF16)))
        negb_last = negb[c - 1:c, :]
        kd = (kh * jnp.exp(negb - negb_last)).astype(BF16)
        state_ref[h] = st * jnp.exp(-negb_last) + _dot(vh.T.astype(BF16), kd)
        o = o * lax.rsqrt(jnp.mean(o * o, axis=-1, keepdims=True) + RMS_EPS) * nw_ref[:, hs]
        gate = gh * (1.0 / (1.0 + jnp.exp(-gh)))
        mix_ref[:, hs] = (o * gate).astype(BF16)

    top = POOL_CARRY + c
    ext_ref[0:POOL_CARRY, :] = carry_ref[...]
    ext_ref[POOL_CARRY:top, :] = p
    carry_ref[...] = p[c - POOL_CARRY:c, :]
    pos = i * c + lax.broadcasted_iota(jnp.int32, (c, 1), 0)
    src = ext_ref
    for gi, win in enumerate(POOL_WINDOWS):
        gs = slice(gi * POOL_GROUP, (gi + 1) * POOL_GROUP)
        rest = slice(gi * POOL_GROUP, POOL_WIDTH)
        half = win // 2
        dst = sum_ref.at[gi % 2]
        dst[POOL_HEAD:top, rest] = src[POOL_HEAD:top, rest] + src[POOL_HEAD - half:top - half, rest]
        src = dst
        pg = p[:, gs]
        tot = dst[POOL_CARRY:top, gs]
        cnt = jnp.minimum(pos + 1, win).astype(F32)
        pooled = _dot((tot / cnt - pg).astype(BF16), wpool_ref[gi])
        mix_ref[:, w + gi * POOL_GROUP:w + (gi + 1) * POOL_GROUP] = (pooled * pscale_ref[:, gs]).astype(BF16)

    y = DEEPNORM_ALPHA * x + _dot(mix_ref[...], wout_ref[...])
    o_ref[...] = _layer_norm(y, lnw_ref[...], lnb_ref[...])


def _mixer(x, w_in, lb_logits, norm_w, w_pool, pool_scale, w_out, ln_w, ln_b):
    t = x.shape[0]
    c = SEQ_BLOCK
    mall, lvl, _ = _decay_tables()
    full = lambda shape: pl.BlockSpec(shape, lambda i: (0,) * len(shape))
    return pl.pallas_call(
        _mixer_kernel,
        grid=(t // c,),
        in_specs=[
            pl.BlockSpec((c, D_MODEL), lambda i: (i, 0)),
            full((D_MODEL, IN_PROJ_WIDTH)),
            full(lb_logits.shape),
            full((1, HGRN_WIDTH)),
            full((len(POOL_WINDOWS), POOL_GROUP, POOL_GROUP)),
            full((1, POOL_WIDTH)),
            full((D_MODEL, D_MODEL)),
            full((1, D_MODEL)),
            full((1, D_MODEL)),
            full(mall.shape),
            full(lvl.shape),
        ],
        out_specs=pl.BlockSpec((c, D_MODEL), lambda i: (i, 0)),
        out_shape=jax.ShapeDtypeStruct((t, D_MODEL), F32),
        scratch_shapes=[
            pltpu.VMEM((HGRN_HEADS, HEAD_DIM, HEAD_DIM), F32),
            pltpu.VMEM((POOL_CARRY, POOL_WIDTH), F32),
            pltpu.VMEM((POOL_CARRY + c, POOL_WIDTH), F32),
            pltpu.VMEM((c, D_MODEL), BF16),
            pltpu.VMEM((2, POOL_CARRY + c, POOL_WIDTH), F32),
        ],
        compiler_params=pltpu.CompilerParams(dimension_semantics=("arbitrary",), vmem_limit_bytes=VMEM_LIMIT),
        name="mixer",
    )(x, w_in.astype(BF16), lb_logits, norm_w.reshape(1, -1), w_pool.astype(BF16), pool_scale.reshape(1, -1),
      w_out.astype(BF16), ln_w.reshape(1, -1), ln_b.reshape(1, -1), jnp.asarray(mall, BF16), jnp.asarray(lvl))


def _kv_kernel(mem_ref, wk_ref, wv_ref, wq_ref, wo_ref, a_ref, b_ref):
    n_mem = mem_ref.shape[0]
    m = mem_ref[...].astype(BF16)
    k = _dot(m, wk_ref[...]).astype(BF16)
    v = _dot(m, wv_ref[...]).astype(BF16)
    scale = XATTN_HEAD_DIM ** -0.5
    for h in range(XATTN_HEADS):
        hs = slice(h * XATTN_HEAD_DIM, (h + 1) * XATTN_HEAD_DIM)
        ms = slice(h * n_mem, (h + 1) * n_mem)
        a_ref[:, ms] = (_dot_nt(wq_ref[:, hs], k[:, hs]) * scale).astype(BF16)
        b_ref[ms, :] = _dot(v[:, hs], wo_ref[hs, :]).astype(BF16)


def _kv_proj(mem, wk, wv, wq, wo):
    n = mem.shape[0]
    return pl.pallas_call(
        _kv_kernel,
        out_shape=(jax.ShapeDtypeStruct((D_MODEL, XATTN_HEADS * n), BF16),
                   jax.ShapeDtypeStruct((XATTN_HEADS * n, D_MODEL), BF16)),
        compiler_params=pltpu.CompilerParams(vmem_limit_bytes=VMEM_LIMIT),
        name="kv_proj",
    )(mem, wk.astype(BF16), wv.astype(BF16), wq.astype(BF16), wo.astype(BF16))


def _split3(a):
    a0 = a.astype(BF16)
    r1 = a - a0.astype(F32)
    a1 = r1.astype(BF16)
    a2 = (r1 - a1.astype(F32)).astype(BF16)
    return a0, a1, a2


def _xattn_kernel(x_ref, a_ref, b_ref, lnw_ref, lnb_ref, wrt_ref, br_ref, ut_ref,
                  x2_ref, dest_ref, gate_ref, cnt_ref, att_ref, run_ref, *, shift):
    i = pl.program_id(0)
    c = x_ref.shape[0]
    n = N_EXPERTS

    @pl.when(i == 0)
    def _():
        run_ref[...] = jnp.zeros_like(run_ref)

    x = x_ref[...]
    scores = _dot(x.astype(BF16), a_ref[...])
    n_mem = a_ref.shape[1] // XATTN_HEADS
    for h in range(XATTN_HEADS):
        ms = slice(h * n_mem, (h + 1) * n_mem)
        s = scores[:, ms]
        e = jnp.exp(s - jnp.max(s, axis=-1, keepdims=True))
        att_ref[:, ms] = (e / jnp.sum(e, axis=-1, keepdims=True)).astype(BF16)
    y = DEEPNORM_ALPHA * x + _dot(att_ref[...], b_ref[...])
    x2 = _layer_norm(y, lnw_ref[...], lnb_ref[...])
    _store_row_tiles(x2_ref, x2)

    a0, a1, _ = _split3(x2)
    w0, w1, w2 = _split3(wrt_ref[...])
    pa = _dot_nt(jnp.concatenate([w0, w1, w2], axis=0), a0)
    pb = _dot_nt(jnp.concatenate([w0, w1], axis=0), a1)
    logits = ((pa[2 * n:3 * n] + pb[n:2 * n]) + (pb[0:n] + pa[n:2 * n]) + pa[0:n]) + br_ref[...]

    eid = lax.broadcasted_iota(jnp.int32, (n, c), 0)
    work = logits
    sels, vals, idxs = [], [], []
    for k in range(TOP_K):
        m = jnp.max(work, axis=0, keepdims=True)
        idx = jnp.min(jnp.where(work == m, eid, n), axis=0, keepdims=True)
        sel = eid == idx
        sels.append(sel)
        vals.append(m)
        idxs.append(idx)
        work = jnp.where(sel, -jnp.inf, work)
    es = [jnp.exp(vk - vals[0]) for vk in vals]
    denom = es[0] + es[1] + es[2] + es[3]
    gate_ref[0] = jnp.concatenate([ek / denom for ek in es], axis=0)

    onehot = jnp.zeros((n, c), F32)
    for sel in sels:
        onehot = jnp.where(sel, 1.0, onehot)
    before = _dot(onehot.astype(BF16), ut_ref[...]) + run_ref[...]
    ranks = [jnp.sum(jnp.where(sel, before, 0.0), axis=0, keepdims=True).astype(jnp.int32) for sel in sels]
    run_ref[...] = run_ref[...] + jnp.sum(onehot, axis=1, keepdims=True)

    dest_ref[0] = jnp.concatenate([idx * (1 << shift) + rk for idx, rk in zip(idxs, ranks)], axis=0)
    cnt_ref[...] = run_ref[...].astype(jnp.int32)


def _xattn_router(x1, score_w, out_w, ln_w, ln_b, w_r, b_r):
    t = x1.shape[0]
    c = XATTN_BLOCK if t % XATTN_BLOCK == 0 else SEQ_BLOCK
    idx = np.arange(c)
    strict_upper = (idx[:, None] < idx[None, :]).astype(np.float32)
    full = lambda shape: pl.BlockSpec(shape, lambda i: (0,) * len(shape))
    per_k = pl.BlockSpec((1, TOP_K, c), lambda i: (i, 0, 0))
    return pl.pallas_call(
        functools.partial(_xattn_kernel, shift=_rank_bits(t)),
        grid=(t // c,),
        in_specs=[
            pl.BlockSpec((c, D_MODEL), lambda i: (i, 0)),
            full(score_w.shape),
            full(out_w.shape),
            full((1, D_MODEL)),
            full((1, D_MODEL)),
            full((N_EXPERTS, D_MODEL)),
            full((N_EXPERTS, 1)),
            full((c, c)),
        ],
        out_specs=[_tiled_spec(c, lambda i: (i, 0)), per_k, per_k, full((N_EXPERTS, 1))],
        out_shape=[
            jax.ShapeDtypeStruct(_tiled(t), F32),
            jax.ShapeDtypeStruct((t // c, TOP_K, c), jnp.int32),
            jax.ShapeDtypeStruct((t // c, TOP_K, c), F32),
            jax.ShapeDtypeStruct((N_EXPERTS, 1), jnp.int32),
        ],
        scratch_shapes=[pltpu.VMEM((c, score_w.shape[1]), BF16), pltpu.VMEM((N_EXPERTS, 1), F32)],
        compiler_params=pltpu.CompilerParams(dimension_semantics=("arbitrary",), vmem_limit_bytes=VMEM_LIMIT),
        name="xattn_router",
    )(x1, score_w, out_w, ln_w.reshape(1, -1), ln_b.reshape(1, -1),
      w_r.T, b_r.reshape(-1, 1), jnp.asarray(strict_upper, BF16))


TAIL_RUNS = (128, 64, 32, 16, 8, 4, 2, 1)


def _dispatch_kernel(fill_ref, pend_ref, nused_ref, slot_ref, x_ref, xs_hbm, zero_ref, sem, zsem, *, n_blocks):
    i = pl.program_id(0)
    c = SEQ_BLOCK

    @pl.when(i == 0)
    def _():
        zero_ref[...] = jnp.zeros_like(zero_ref)

        sub = ROW_TILE[0]

        def rows(ref, first, n):
            return ref.at[pl.ds(pl.multiple_of(first * sub, sub), n * sub)]

        def tail(e, act):
            row = fill_ref[e]
            pad = pend_ref[e] - row
            for run in TAIL_RUNS:
                hit = (pad & run) != 0

                @pl.when(hit)
                def _():
                    act(pltpu.make_async_copy(rows(zero_ref, 0, run), rows(xs_hbm, row, run), zsem))
                row = row + jnp.where(hit, run, 0)

        def spare(b, act):
            act(pltpu.make_async_copy(zero_ref, rows(xs_hbm, b * MOE_BLOCK, MOE_BLOCK), zsem))

        for act in (lambda cp: cp.start(), lambda cp: cp.wait()):
            lax.fori_loop(0, N_EXPERTS, lambda e, carry: (tail(e, act), carry)[1], 0)
            lax.fori_loop(nused_ref[0], n_blocks, lambda b, carry: (spare(b, act), carry)[1], 0)

    for q in range(x_ref.shape[0] // (c * ROW_TILE[0])):
        def issue(j, carry, q=q):
            for u in range(ISSUE_UNROLL):
                r = j * ISSUE_UNROLL + u
                for k in range(TOP_K):
                    dst = _tile_of(xs_hbm, slot_ref[0, 0, (q * TOP_K + k) * c + r])
                    pltpu.make_async_copy(_tile_of(x_ref, q * c + r), dst, sem).start(priority=k % 2)
            return carry
        lax.fori_loop(0, c // ISSUE_UNROLL, issue, 0)
    for k in range(TOP_K):
        pltpu.make_async_copy(x_ref, xs_hbm.at[pl.ds(0, x_ref.shape[0])], sem).wait()


def _dispatch(x2, slots, fill, pend, n_used, n_blocks):
    t = x2.shape[0] // ROW_TILE[0]
    c = min(DISPATCH_BLOCK, t)
    assert t % c == 0 and c % SEQ_BLOCK == 0
    grid_spec = pltpu.PrefetchScalarGridSpec(
        num_scalar_prefetch=3,
        grid=(t // c,),
        in_specs=[
            pl.BlockSpec((1, 1, c * TOP_K), lambda i, *_: (i, 0, 0), memory_space=pltpu.SMEM),
            _tiled_spec(c, lambda i, *_: (i, 0)),
        ],
        out_specs=pl.BlockSpec(memory_space=pl.ANY),
        scratch_shapes=[pltpu.VMEM(_tiled(MOE_BLOCK), F32), pltpu.SemaphoreType.DMA, pltpu.SemaphoreType.DMA],
    )
    return pl.pallas_call(
        functools.partial(_dispatch_kernel, n_blocks=n_blocks),
        grid_spec=grid_spec,
        out_shape=jax.ShapeDtypeStruct(_tiled(n_blocks * MOE_BLOCK), F32),
        compiler_params=pltpu.CompilerParams(dimension_semantics=("arbitrary",), vmem_limit_bytes=VMEM_LIMIT),
        name="dispatch",
    )(fill, pend, n_used, slots.reshape(t // c, 1, c * TOP_K), x2)


def _expert_kernel(be_ref, first_ref, nxt_ref, par_ref, nused_ref, x_ref, w1_hbm, b1_ref, w2_hbm, b2_ref, o_ref,
                   w1f, w2f, w1b, w2b, sem1, sem2):
    step = pl.program_id(0)
    rows = MOE_BLOCK * ROW_TILE[0]

    def fetch(e, p):
        return (pltpu.make_async_copy(w1_hbm.at[e], w1f.at[p], sem1.at[p]),
                pltpu.make_async_copy(w2_hbm.at[e], w2f.at[p], sem2.at[p]))

    @pl.when(step == 0)
    def _():
        for cp in fetch(be_ref[0], par_ref[0]):
            cp.start()

    def block(sb, carry):
        b = step * EXPERT_STEP_BLOCKS + sb
        x_blk = x_ref.at[pl.ds(pl.multiple_of(sb * rows, rows), rows)]
        o_blk = o_ref.at[pl.ds(pl.multiple_of(sb * rows, rows), rows)]

        @pl.when(b >= nused_ref[0])
        def _():
            o_blk[...] = jnp.zeros(o_blk.shape, o_blk.dtype)

        @pl.when(b < nused_ref[0])
        def _():
            e = be_ref[b]

            @pl.when(first_ref[b] == 1)
            def _():
                p = par_ref[b]
                for cp in fetch(e, p):
                    cp.wait()

                @pl.when(nxt_ref[b] >= 0)
                def _():
                    for cp in fetch(nxt_ref[b], 1 - p):
                        cp.start()

                w1b[...] = w1f[p].astype(BF16)
                w2b[...] = w2f[p].astype(BF16)

            h = _dot(_load_row_tiles(x_blk).astype(BF16), w1b[...]) + b1_ref[e]
            h_glu = jnp.minimum(h[:, :D_EXPERT], SWIGLU_LIMIT)
            h_lin = jnp.clip(h[:, D_EXPERT:], -SWIGLU_LIMIT, SWIGLU_LIMIT)
            act = h_glu * (1.0 / (1.0 + jnp.exp(-SWIGLU_ALPHA * h_glu))) * (h_lin + 1.0)
            _store_row_tiles(o_blk, _dot(act.astype(BF16), w2b[...]) + b2_ref[e])
        return carry

    lax.fori_loop(0, EXPERT_STEP_BLOCKS, block, 0)


def _experts(xs, plan, w1, b1, w2, b2):
    block_e, first, nxt, par, n_used = plan
    n_blocks = block_e.shape[0]
    assert n_blocks % EXPERT_STEP_BLOCKS == 0
    rows = lambda s, *_: (s, 0)
    whole = lambda s, *_: (0, 0, 0)
    grid_spec = pltpu.PrefetchScalarGridSpec(
        num_scalar_prefetch=5,
        grid=(n_blocks // EXPERT_STEP_BLOCKS,),
        in_specs=[
            _tiled_spec(EXPERT_STEP_BLOCKS * MOE_BLOCK, rows),
            pl.BlockSpec(memory_space=pl.ANY),
            pl.BlockSpec((N_EXPERTS, 1, 2 * D_EXPERT), whole),
            pl.BlockSpec(memory_space=pl.ANY),
            pl.BlockSpec((N_EXPERTS, 1, D_MODEL), whole),
        ],
        out_specs=_tiled_spec(EXPERT_STEP_BLOCKS * MOE_BLOCK, rows),
        scratch_shapes=[
            pltpu.VMEM((2, D_MODEL, 2 * D_EXPERT), F32),
            pltpu.VMEM((2, D_EXPERT, D_MODEL), F32),
            pltpu.VMEM((D_MODEL, 2 * D_EXPERT), BF16),
            pltpu.VMEM((D_EXPERT, D_MODEL), BF16),
            pltpu.SemaphoreType.DMA((2,)),
            pltpu.SemaphoreType.DMA((2,)),
        ],
    )
    return pl.pallas_call(
        _expert_kernel,
        grid_spec=grid_spec,
        out_shape=jax.ShapeDtypeStruct(xs.shape, F32),
        compiler_params=pltpu.CompilerParams(dimension_semantics=("arbitrary",), vmem_limit_bytes=EXPERT_VMEM_LIMIT),
        name="experts",
    )(block_e, first, nxt, par, n_used, xs, w1, b1.reshape(N_EXPERTS, 1, -1), w2, b2.reshape(N_EXPERTS, 1, -1))


def _combine_kernel(slot_ref, next_slot_ref, ys_hbm, gate_ref, x_ref, lnw_ref, lnb_ref, o_ref, ybuf, sem):
    i = pl.program_id(0)
    c = SEQ_BLOCK

    def gather(slots, p):
        def issue(j, carry):
            for u in range(GATHER_UNROLL):
                r = j * GATHER_UNROLL + u
                for k in range(TOP_K):
                    src = _tile_of(ys_hbm, slots[0, 0, k * c + r])
                    pltpu.make_async_copy(src, _tile_of(ybuf.at[p, k], r), sem.at[p]).start(priority=k % 2)
            return carry
        lax.fori_loop(0, c // GATHER_UNROLL, issue, 0)

    @pl.when(i == 0)
    def _():
        gather(slot_ref, 0)

    @pl.when(i + 1 < pl.num_programs(0))
    def _():
        gather(next_slot_ref, (i + 1) % 2)

    p = i % 2
    for k in range(TOP_K):
        pltpu.make_async_copy(ys_hbm.at[pl.ds(0, ybuf.shape[2])], ybuf.at[p, k], sem.at[p]).wait()
    gates = gate_ref[...]
    ys = [_load_row_tiles(ybuf.at[p, k]) * gates[:, k:k + 1] for k in range(TOP_K)]
    y = (ys[0] + ys[1]) + (ys[2] + ys[3])
    o_ref[...] = _layer_norm(DEEPNORM_ALPHA * _load_row_tiles(x_ref) + y, lnw_ref[...], lnb_ref[...])


def _combine(ys, slots, gates, x2, ln_w, ln_b):
    t = x2.shape[0] // ROW_TILE[0]
    c = SEQ_BLOCK
    n = t // c
    slot_table = slots.reshape(n, 1, c * TOP_K)
    return pl.pallas_call(
        _combine_kernel,
        grid=(n,),
        in_specs=[
            pl.BlockSpec((1, 1, c * TOP_K), lambda i: (i, 0, 0), memory_space=pltpu.SMEM),
            pl.BlockSpec((1, 1, c * TOP_K), lambda i: (jnp.minimum(i + 1, n - 1), 0, 0), memory_space=pltpu.SMEM),
            pl.BlockSpec(memory_space=pl.ANY),
            pl.BlockSpec((c, TOP_K), lambda i: (i, 0)),
            _tiled_spec(c, lambda i: (i, 0)),
            pl.BlockSpec((1, D_MODEL), lambda i: (0, 0)),
            pl.BlockSpec((1, D_MODEL), lambda i: (0, 0)),
        ],
        out_specs=pl.BlockSpec((c, D_MODEL), lambda i: (i, 0)),
        out_shape=jax.ShapeDtypeStruct((t, D_MODEL), F32),
        scratch_shapes=[pltpu.VMEM((2, TOP_K) + _tiled(c), F32), pltpu.SemaphoreType.DMA((2,))],
        compiler_params=pltpu.CompilerParams(dimension_semantics=("arbitrary",), vmem_limit_bytes=VMEM_LIMIT),
        name="combine",
    )(slot_table, slot_table, ys, gates, x2, ln_w.reshape(1, -1), ln_b.reshape(1, -1))


def _slot_tables(counts, packed, n_blocks, shift):
    counts = counts.reshape(-1)
    nblk = (counts + MOE_BLOCK - 1) // MOE_BLOCK
    ids = jnp.arange(N_EXPERTS)
    lower = ids[None, :] <= ids[:, None]
    bend = jnp.sum(jnp.where(lower, nblk[None, :], 0), axis=1)
    n_used = bend[N_EXPERTS - 1]
    pend = bend * MOE_BLOCK
    pstart = pend - nblk * MOE_BLOCK
    fill = pstart + counts
    order = jnp.sum(jnp.where(lower, (nblk > 0)[None, :], False), axis=1) - 1
    later = (ids[None, :] > ids[:, None]) & (nblk > 0)[None, :]
    nxt_e = jnp.min(jnp.where(later, ids[None, :], N_EXPERTS), axis=1)
    nxt_e = jnp.where(nxt_e == N_EXPERTS, -1, nxt_e)
    b = jnp.minimum(jnp.arange(n_blocks), n_used - 1)
    hit = bend[None, :] <= b[:, None]
    block_e = jnp.sum(hit.astype(jnp.int32), axis=1)
    of_block = lambda per_expert: jnp.sum(jnp.where(ids[None, :] == block_e[:, None], per_expert[None, :], 0), axis=1)
    first = (b == of_block(bend - nblk)).astype(jnp.int32)
    expert = lax.shift_right_logical(packed, shift)
    rank = packed & ((1 << shift) - 1)
    slots = jnp.sum(jnp.where(expert[..., None] == ids, pstart, 0), axis=-1) + rank
    i32 = lambda a: a.astype(jnp.int32)
    plan = (i32(block_e), first, i32(of_block(nxt_e)), i32(of_block(order & 1)), i32(n_used.reshape(1)))
    return i32(slots), i32(fill), i32(pend), plan


def kernel(x, mem, w_in, lb_logits, hgrn_norm_w, w_pool, pool_scale, w_out, ln1_w, ln1_b, w_xq, w_xk, w_xv, w_xo,
           ln2_w, ln2_b, w_router, b_router, w1, b1, w2, b2, ln3_w, ln3_b):
    bsz, seq, d = x.shape
    assert bsz == 1 and d == D_MODEL and seq % SEQ_BLOCK == 0 and w_in.shape[0] == 1
    xt = x.reshape(seq, d)
    x1 = _mixer(xt, w_in[0], lb_logits, hgrn_norm_w[0], w_pool[0], pool_scale[0], w_out[0], ln1_w[0], ln1_b[0])
    score_w, out_w = _kv_proj(mem[0], w_xk[0], w_xv[0], w_xq[0], w_xo[0])
    x2, packed, gates, counts = _xattn_router(x1, score_w, out_w, ln2_w[0], ln2_b[0], w_router[0], b_router[0])
    n_blocks = -(-(seq * TOP_K + N_EXPERTS * (MOE_BLOCK - 1)) // MOE_BLOCK)
    slots, fill, pend, plan = _slot_tables(counts, packed, n_blocks, _rank_bits(seq))
    nb, _, cx = slots.shape
    slots = slots.reshape(nb, TOP_K, cx // SEQ_BLOCK, SEQ_BLOCK).transpose(0, 2, 1, 3).reshape(-1, TOP_K, SEQ_BLOCK)
    xs = _dispatch(x2, slots, fill, pend, plan[-1], n_blocks)
    ys = _experts(xs, plan, w1[0], b1[0], w2[0], b2[0])
    gates_tk = gates.transpose(0, 2, 1).reshape(seq, TOP_K)
    out = _combine(ys, slots, gates_tk, x2, ln3_w[0], ln3_b[0])
    return out.reshape(bsz, seq, d)
```

```python
import functools

import numpy as np
import jax
import jax.numpy as jnp
from jax import lax
from jax.experimental import pallas as pl
from jax.experimental.pallas import tpu as pltpu

F32 = jnp.float32
BF16 = jnp.bfloat16

D_MODEL = 1024
HGRN_WIDTH = 512
HGRN_HEADS = 4
HEAD_DIM = 128
POOL_WINDOWS = (2, 4, 8, 16)
POOL_GROUP = 128
POOL_WIDTH = POOL_GROUP * len(POOL_WINDOWS)
POOL_CARRY = 24
POOL_HEAD = 8
IN_PROJ_WIDTH = 4 * HGRN_WIDTH + POOL_WIDTH
XATTN_HEADS = 4
XATTN_HEAD_DIM = 256
N_EXPERTS = 32
TOP_K = 4
D_EXPERT = 1024
SWIGLU_LIMIT = 7.0
SWIGLU_ALPHA = 1.702
MOE_BLOCK = 256
DEEPNORM_ALPHA = 2.0 ** 0.25
LN_EPS = 1e-5
RMS_EPS = 1e-6

SEQ_BLOCK = 256
EXPERT_STEP_BLOCKS = 4
XATTN_BLOCK = 1024
DISPATCH_BLOCK = 2048
N_LEVELS = 8
MXU_LEVELS = 3
ISSUE_UNROLL = 8
GATHER_UNROLL = 16
ROW_TILE = (8, 128)
VMEM_LIMIT = 48 * 1024 * 1024
EXPERT_VMEM_LIMIT = 56 * 1024 * 1024


def _rank_bits(t):
    return max(int(t - 1).bit_length(), 1)


def _layer_norm(y, w, b):
    mu = jnp.mean(y, axis=-1, keepdims=True)
    yc = y - mu
    var = jnp.mean(yc * yc, axis=-1, keepdims=True)
    return yc * lax.rsqrt(var + LN_EPS) * w + b


def _tiled(n):
    return (n * ROW_TILE[0], ROW_TILE[1])


def _tiled_spec(n, index_map):
    return pl.BlockSpec(_tiled(n), index_map)


def _tile_of(ref, r):
    return ref.at[pl.ds(pl.multiple_of(r * ROW_TILE[0], ROW_TILE[0]), ROW_TILE[0])]


def _store_row_tiles(ref, val):
    s, l = ROW_TILE
    for j in range(s):
        ref[pl.ds(j, val.shape[0], stride=s), :] = val[:, j * l:(j + 1) * l]


def _load_row_tiles(ref):
    s, _ = ROW_TILE
    return jnp.concatenate([ref[pl.ds(j, ref.shape[0] // s, stride=s), :] for j in range(s)], axis=-1)


def _dot(a, b):
    return jnp.dot(a, b, preferred_element_type=F32)


def _dot_nt(a, b):
    return lax.dot_general(a, b, (((1,), (1,)), ((), ())), preferred_element_type=F32)


def _decay_tables():
    c = SEQ_BLOCK
    r = np.arange(c)[:, None]
    u = np.arange(c)[None, :]
    mats = [(u <= r)]
    for l in range(MXU_LEVELS):
        h = 1 << l
        mid = (r // (2 * h)) * (2 * h) + h
        upper = (r >= mid) & (u >= mid) & (u <= r)
        lower = (r < mid) & (u > r) & (u < mid)
        mats.append(upper | lower)
    mall = np.concatenate(mats, axis=0).astype(np.float32)
    x = r ^ u
    lvl = np.where(u < r, np.floor(np.log2(np.maximum(x, 1))).astype(np.int32), -1).astype(np.int32)
    strict_lower = (u < r).astype(np.float32)
    return mall, lvl, strict_lower


def _level_gap(negb, l):
    half = 1 << l
    parts = []
    for base in range(0, negb.shape[0], 2 * half):
        ref_row = negb[base + half - 1:base + half, :]
        parts.append(jnp.abs(negb[base:base + 2 * half, :] - ref_row))
    return jnp.concatenate(parts, axis=0) if len(parts) > 1 else parts[0]


def _mixer_kernel(x_ref, win_ref, lbl_ref, nw_ref, wpool_ref, pscale_ref, wout_ref, lnw_ref, lnb_ref,
                  mall_ref, lvl_ref, o_ref, state_ref, carry_ref, ext_ref, mix_ref, sum_ref):
    i = pl.program_id(0)
    c = SEQ_BLOCK

    @pl.when(i == 0)
    def _():
        state_ref[...] = jnp.zeros_like(state_ref)
        carry_ref[...] = jnp.zeros_like(carry_ref)
        sum_ref[:, 0:POOL_HEAD, :] = jnp.zeros((2, POOL_HEAD, POOL_WIDTH), F32)

    x = x_ref[...]
    proj = _dot(x.astype(BF16), win_ref[...])
    w = HGRN_WIDTH
    q = proj[:, 0:w]
    z = proj[:, w:2 * w]
    v = proj[:, 2 * w:3 * w]
    g = proj[:, 3 * w:4 * w]
    p = proj[:, 4 * w:]

    ll = lbl_ref[...]
    ee = jnp.exp(ll - jnp.max(ll, axis=0, keepdims=True))
    lb = ee[0:1] / jnp.sum(ee, axis=0, keepdims=True)

    f = lb + (1.0 - lb) * (1.0 / (1.0 + jnp.exp(-z)))
    nlf = -jnp.log(f)
    key = (1.0 - lb) * (1.0 / (1.0 + jnp.exp(z)))

    hi = nlf.astype(BF16)
    lo = (nlf - hi.astype(F32)).astype(BF16)
    mall = mall_ref[...]
    cum = _dot(mall, hi) + _dot(mall, lo)
    lvl = lvl_ref[...]

    for h in range(HGRN_HEADS):
        hs = slice(h * HEAD_DIM, (h + 1) * HEAD_DIM)
        qh, kh, vh, gh = q[:, hs], key[:, hs], v[:, hs], g[:, hs]
        vb = vh.astype(BF16)
        negb = cum[0:c, hs]
        scores = jnp.zeros((c, c), F32)
        for l in range(N_LEVELS):
            gap = cum[(l + 1) * c:(l + 2) * c, hs] if l < MXU_LEVELS else _level_gap(negb, l)
            e = jnp.exp(-gap)
            pl_ = _dot_nt((qh * e).astype(BF16), (kh * e).astype(BF16))
            scores = jnp.where(lvl == l, pl_, scores)
        diag = jnp.sum(qh * kh, axis=-1, keepdims=True)
        st = state_ref[h]
        o = (_dot(scores.astype(BF16), vb) + diag * vh
             + _dot_nt((qh * jnp.exp(-negb)).astype(BF16), st.astype(BF16)))
        negb_last = negb[c - 1:c, :]
        kd = (kh * jnp.exp(negb - negb_last)).astype(BF16)
        state_ref[h] = st * jnp.exp(-negb_last) + _dot(vh.T.astype(BF16), kd)
        o = o * lax.rsqrt(jnp.mean(o * o, axis=-1, keepdims=True) + RMS_EPS) * nw_ref[:, hs]
        gate = gh * (1.0 / (1.0 + jnp.exp(-gh)))
        mix_ref[:, hs] = (o * gate).astype(BF16)

    top = POOL_CARRY + c
    ext_ref[0:POOL_CARRY, :] = carry_ref[...]
    ext_ref[POOL_CARRY:top, :] = p
    carry_ref[...] = p[c - POOL_CARRY:c, :]
    pos = i * c + lax.broadcasted_iota(jnp.int32, (c, 1), 0)
    src = ext_ref
    for gi, win in enumerate(POOL_WINDOWS):
        gs = slice(gi * POOL_GROUP, (gi + 1) * POOL_GROUP)
        rest = slice(gi * POOL_GROUP, POOL_WIDTH)
        half = win // 2
        dst = sum_ref.at[gi % 2]
        dst[POOL_HEAD:top, rest] = src[POOL_HEAD:top, rest] + src[POOL_HEAD - half:top - half, rest]
        src = dst
        pg = p[:, gs]
        tot = dst[POOL_CARRY:top, gs]
        cnt = jnp.minimum(pos + 1, win).astype(F32)
        pooled = _dot((tot / cnt - pg).astype(BF16), wpool_ref[gi])
        mix_ref[:, w + gi * POOL_GROUP:w + (gi + 1) * POOL_GROUP] = (pooled * pscale_ref[:, gs]).astype(BF16)

    y = DEEPNORM_ALPHA * x + _dot(mix_ref[...], wout_ref[...])
    o_ref[...] = _layer_norm(y, lnw_ref[...], lnb_ref[...])


def _mixer(x, w_in, lb_logits, norm_w, w_pool, pool_scale, w_out, ln_w, ln_b):
    t = x.shape[0]
    c = SEQ_BLOCK
    mall, lvl, _ = _decay_tables()
    full = lambda shape: pl.BlockSpec(shape, lambda i: (0,) * len(shape))
    return pl.pallas_call(
        _mixer_kernel,
        grid=(t // c,),
        in_specs=[
            pl.BlockSpec((c, D_MODEL), lambda i: (i, 0)),
            full((D_MODEL, IN_PROJ_WIDTH)),
            full(lb_logits.shape),
            full((1, HGRN_WIDTH)),
            full((len(POOL_WINDOWS), POOL_GROUP, POOL_GROUP)),
            full((1, POOL_WIDTH)),
            full((D_MODEL, D_MODEL)),
            full((1, D_MODEL)),
            full((1, D_MODEL)),
            full(mall.shape),
            full(lvl.shape),
        ],
        out_specs=pl.BlockSpec((c, D_MODEL), lambda i: (i, 0)),
        out_shape=jax.ShapeDtypeStruct((t, D_MODEL), F32),
        scratch_shapes=[
            pltpu.VMEM((HGRN_HEADS, HEAD_DIM, HEAD_DIM), F32),
            pltpu.VMEM((POOL_CARRY, POOL_WIDTH), F32),
            pltpu.VMEM((POOL_CARRY + c, POOL_WIDTH), F32),
            pltpu.VMEM((c, D_MODEL), BF16),
            pltpu.VMEM((2, POOL_CARRY + c, POOL_WIDTH), F32),
        ],
        compiler_params=pltpu.CompilerParams(dimension_semantics=("arbitrary",), vmem_limit_bytes=VMEM_LIMIT),
        name="mixer",
    )(x, w_in.astype(BF16), lb_logits, norm_w.reshape(1, -1), w_pool.astype(BF16), pool_scale.reshape(1, -1),
      w_out.astype(BF16), ln_w.reshape(1, -1), ln_b.reshape(1, -1), jnp.asarray(mall, BF16), jnp.asarray(lvl))


def _kv_kernel(mem_ref, wk_ref, wv_ref, wq_ref, wo_ref, a_ref, b_ref):
    n_mem = mem_ref.shape[0]
    m = mem_ref[...].astype(BF16)
    k = _dot(m, wk_ref[...]).astype(BF16)
    v = _dot(m, wv_ref[...]).astype(BF16)
    scale = XATTN_HEAD_DIM ** -0.5
    for h in range(XATTN_HEADS):
        hs = slice(h * XATTN_HEAD_DIM, (h + 1) * XATTN_HEAD_DIM)
        ms = slice(h * n_mem, (h + 1) * n_mem)
        a_ref[:, ms] = (_dot_nt(wq_ref[:, hs], k[:, hs]) * scale).astype(BF16)
        b_ref[ms, :] = _dot(v[:, hs], wo_ref[hs, :]).astype(BF16)


def _kv_proj(mem, wk, wv, wq, wo):
    n = mem.shape[0]
    return pl.pallas_call(
        _kv_kernel,
        out_shape=(jax.ShapeDtypeStruct((D_MODEL, XATTN_HEADS * n), BF16),
                   jax.ShapeDtypeStruct((XATTN_HEADS * n, D_MODEL), BF16)),
        compiler_params=pltpu.CompilerParams(vmem_limit_bytes=VMEM_LIMIT),
        name="kv_proj",
    )(mem, wk.astype(BF16), wv.astype(BF16), wq.astype(BF16), wo.astype(BF16))


def _split3(a):
    a0 = a.astype(BF16)
    r1 = a - a0.astype(F32)
    a1 = r1.astype(BF16)
    a2 = (r1 - a1.astype(F32)).astype(BF16)
    return a0, a1, a2


def _xattn_kernel(x_ref, a_ref, b_ref, lnw_ref, lnb_ref, wrt_ref, br_ref, ut_ref,
                  x2_ref, dest_ref, gate_ref, cnt_ref, att_ref, run_ref, *, shift):
    i = pl.program_id(0)
    c = x_ref.shape[0]
    n = N_EXPERTS

    @pl.when(i == 0)
    def _():
        run_ref[...] = jnp.zeros_like(run_ref)

    x = x_ref[...]
    scores = _dot(x.astype(BF16), a_ref[...])
    n_mem = a_ref.shape[1] // XATTN_HEADS
    for h in range(XATTN_HEADS):
        ms = slice(h * n_mem, (h + 1) * n_mem)
        s = scores[:, ms]
        e = jnp.exp(s - jnp.max(s, axis=-1, keepdims=True))
        att_ref[:, ms] = (e / jnp.sum(e, axis=-1, keepdims=True)).astype(BF16)
    y = DEEPNORM_ALPHA * x + _dot(att_ref[...], b_ref[...])
    x2 = _layer_norm(y, lnw_ref[...], lnb_ref[...])
    _store_row_tiles(x2_ref, x2)

    a0, a1, _ = _split3(x2)
    w0, w1, w2 = _split3(wrt_ref[...])
    pa = _dot_nt(jnp.concatenate([w0, w1, w2], axis=0), a0)
    pb = _dot_nt(jnp.concatenate([w0, w1], axis=0), a1)
    logits = ((pa[2 * n:3 * n] + pb[n:2 * n]) + (pb[0:n] + pa[n:2 * n]) + pa[0:n]) + br_ref[...]

    eid = lax.broadcasted_iota(jnp.int32, (n, c), 0)
    work = logits
    sels, vals, idxs = [], [], []
    for k in range(TOP_K):
        m = jnp.max(work, axis=0, keepdims=True)
        idx = jnp.min(jnp.where(work == m, eid, n), axis=0, keepdims=True)
        sel = eid == idx
        sels.append(sel)
        vals.append(m)
        idxs.append(idx)
        work = jnp.where(sel, -jnp.inf, work)
    es = [jnp.exp(vk - vals[0]) for vk in vals]
    denom = es[0] + es[1] + es[2] + es[3]
    gate_ref[0] = jnp.concatenate([ek / denom for ek in es], axis=0)

    onehot = jnp.zeros((n, c), F32)
    for sel in sels:
        onehot = jnp.where(sel, 1.0, onehot)
    before = _dot(onehot.astype(BF16), ut_ref[...]) + run_ref[...]
    ranks = [jnp.sum(jnp.where(sel, before, 0.0), axis=0, keepdims=True).astype(jnp.int32) for sel in sels]
    run_ref[...] = run_ref[...] + jnp.sum(onehot, axis=1, keepdims=True)

    dest_ref[0] = jnp.concatenate([idx * (1 << shift) + rk for idx, rk in zip(idxs, ranks)], axis=0)
    cnt_ref[...] = run_ref[...].astype(jnp.int32)


def _xattn_router(x1, score_w, out_w, ln_w, ln_b, w_r, b_r):
    t = x1.shape[0]
    c = XATTN_BLOCK if t % XATTN_BLOCK == 0 else SEQ_BLOCK
    idx = np.arange(c)
    strict_upper = (idx[:, None] < idx[None, :]).astype(np.float32)
    full = lambda shape: pl.BlockSpec(shape, lambda i: (0,) * len(shape))
    per_k = pl.BlockSpec((1, TOP_K, c), lambda i: (i, 0, 0))
    return pl.pallas_call(
        functools.partial(_xattn_kernel, shift=_rank_bits(t)),
        grid=(t // c,),
        in_specs=[
            pl.BlockSpec((c, D_MODEL), lambda i: (i, 0)),
            full(score_w.shape),
            full(out_w.shape),
            full((1, D_MODEL)),
            full((1, D_MODEL)),
            full((N_EXPERTS, D_MODEL)),
            full((N_EXPERTS, 1)),
            full((c, c)),
        ],
        out_specs=[_tiled_spec(c, lambda i: (i, 0)), per_k, per_k, full((N_EXPERTS, 1))],
        out_shape=[
            jax.ShapeDtypeStruct(_tiled(t), F32),
            jax.ShapeDtypeStruct((t // c, TOP_K, c), jnp.int32),
            jax.ShapeDtypeStruct((t // c, TOP_K, c), F32),
            jax.ShapeDtypeStruct((N_EXPERTS, 1), jnp.int32),
        ],
        scratch_shapes=[pltpu.VMEM((c, score_w.shape[1]), BF16), pltpu.VMEM((N_EXPERTS, 1), F32)],
        compiler_params=pltpu.CompilerParams(dimension_semantics=("arbitrary",), vmem_limit_bytes=VMEM_LIMIT),
        name="xattn_router",
    )(x1, score_w, out_w, ln_w.reshape(1, -1), ln_b.reshape(1, -1),
      w_r.T, b_r.reshape(-1, 1), jnp.asarray(strict_upper, BF16))


TAIL_RUNS = (128, 64, 32, 16, 8, 4, 2, 1)


def _dispatch_kernel(fill_ref, pend_ref, nused_ref, slot_ref, x_ref, xs_hbm, zero_ref, sem, zsem, *, n_blocks):
    i = pl.program_id(0)
    c = SEQ_BLOCK

    @pl.when(i == 0)
    def _():
        zero_ref[...] = jnp.zeros_like(zero_ref)

        sub = ROW_TILE[0]

        def rows(ref, first, n):
            return ref.at[pl.ds(pl.multiple_of(first * sub, sub), n * sub)]

        def tail(e, act):
            row = fill_ref[e]
            pad = pend_ref[e] - row
            for run in TAIL_RUNS:
                hit = (pad & run) != 0

                @pl.when(hit)
                def _():
                    act(pltpu.make_async_copy(rows(zero_ref, 0, run), rows(xs_hbm, row, run), zsem))
                row = row + jnp.where(hit, run, 0)

        def spare(b, act):
            act(pltpu.make_async_copy(zero_ref, rows(xs_hbm, b * MOE_BLOCK, MOE_BLOCK), zsem))

        for act in (lambda cp: cp.start(), lambda cp: cp.wait()):
            lax.fori_loop(0, N_EXPERTS, lambda e, carry: (tail(e, act), carry)[1], 0)
            lax.fori_loop(nused_ref[0], n_blocks, lambda b, carry: (spare(b, act), carry)[1], 0)

    for q in range(x_ref.shape[0] // (c * ROW_TILE[0])):
        def issue(j, carry, q=q):
            for u in range(ISSUE_UNROLL):
                r = j * ISSUE_UNROLL + u
                for k in range(TOP_K):
                    dst = _tile_of(xs_hbm, slot_ref[0, 0, (q * TOP_K + k) * c + r])
                    pltpu.make_async_copy(_tile_of(x_ref, q * c + r), dst, sem).start(priority=k % 2)
            return carry
        lax.fori_loop(0, c // ISSUE_UNROLL, issue, 0)
    for k in range(TOP_K):
        pltpu.make_async_copy(x_ref, xs_hbm.at[pl.ds(0, x_ref.shape[0])], sem).wait()


def _dispatch(x2, slots, fill, pend, n_used, n_blocks):
    t = x2.shape[0] // ROW_TILE[0]
    c = min(DISPATCH_BLOCK, t)
    assert t % c == 0 and c % SEQ_BLOCK == 0
    grid_spec = pltpu.PrefetchScalarGridSpec(
        num_scalar_prefetch=3,
        grid=(t // c,),
        in_specs=[
            pl.BlockSpec((1, 1, c * TOP_K), lambda i, *_: (i, 0, 0), memory_space=pltpu.SMEM),
            _tiled_spec(c, lambda i, *_: (i, 0)),
        ],
        out_specs=pl.BlockSpec(memory_space=pl.ANY),
        scratch_shapes=[pltpu.VMEM(_tiled(MOE_BLOCK), F32), pltpu.SemaphoreType.DMA, pltpu.SemaphoreType.DMA],
    )
    return pl.pallas_call(
        functools.partial(_dispatch_kernel, n_blocks=n_blocks),
        grid_spec=grid_spec,
        out_shape=jax.ShapeDtypeStruct(_tiled(n_blocks * MOE_BLOCK), F32),
        compiler_params=pltpu.CompilerParams(dimension_semantics=("arbitrary",), vmem_limit_bytes=VMEM_LIMIT),
        name="dispatch",
    )(fill, pend, n_used, slots.reshape(t // c, 1, c * TOP_K), x2)


def _expert_kernel(be_ref, first_ref, nxt_ref, par_ref, nused_ref, x_ref, w1_hbm, b1_ref, w2_hbm, b2_ref, o_ref,
                   w1f, w2f, w1b, w2b, sem1, sem2):
    step = pl.program_id(0)
    rows = MOE_BLOCK * ROW_TILE[0]

    def fetch(e, p):
        return (pltpu.make_async_copy(w1_hbm.at[e], w1f.at[p], sem1.at[p]),
                pltpu.make_async_copy(w2_hbm.at[e], w2f.at[p], sem2.at[p]))

    @pl.when(step == 0)
    def _():
        for cp in fetch(be_ref[0], par_ref[0]):
            cp.start()

    def block(sb, carry):
        b = step * EXPERT_STEP_BLOCKS + sb
        x_blk = x_ref.at[pl.ds(pl.multiple_of(sb * rows, rows), rows)]
        o_blk = o_ref.at[pl.ds(pl.multiple_of(sb * rows, rows), rows)]

        @pl.when(b >= nused_ref[0])
        def _():
            o_blk[...] = jnp.zeros(o_blk.shape, o_blk.dtype)

        @pl.when(b < nused_ref[0])
        def _():
            e = be_ref[b]

            @pl.when(first_ref[b] == 1)
            def _():
                p = par_ref[b]
                for cp in fetch(e, p):
                    cp.wait()

                @pl.when(nxt_ref[b] >= 0)
                def _():
                    for cp in fetch(nxt_ref[b], 1 - p):
                        cp.start()

                w1b[...] = w1f[p].astype(BF16)
                w2b[...] = w2f[p].astype(BF16)

            h = _dot(_load_row_tiles(x_blk).astype(BF16), w1b[...]) + b1_ref[e]
            h_glu = jnp.minimum(h[:, :D_EXPERT], SWIGLU_LIMIT)
            h_lin = jnp.clip(h[:, D_EXPERT:], -SWIGLU_LIMIT, SWIGLU_LIMIT)
            act = h_glu * (1.0 / (1.0 + jnp.exp(-SWIGLU_ALPHA * h_glu))) * (h_lin + 1.0)
            _store_row_tiles(o_blk, _dot(act.astype(BF16), w2b[...]) + b2_ref[e])
        return carry

    lax.fori_loop(0, EXPERT_STEP_BLOCKS, block, 0)


def _experts(xs, plan, w1, b1, w2, b2):
    block_e, first, nxt, par, n_used = plan
    n_blocks = block_e.shape[0]
    assert n_blocks % EXPERT_STEP_BLOCKS == 0
    rows = lambda s, *_: (s, 0)
    whole = lambda s, *_: (0, 0, 0)
    grid_spec = pltpu.PrefetchScalarGridSpec(
        num_scalar_prefetch=5,
        grid=(n_blocks // EXPERT_STEP_BLOCKS,),
        in_specs=[
            _tiled_spec(EXPERT_STEP_BLOCKS * MOE_BLOCK, rows),
            pl.BlockSpec(memory_space=pl.ANY),
            pl.BlockSpec((N_EXPERTS, 1, 2 * D_EXPERT), whole),
            pl.BlockSpec(memory_space=pl.ANY),
            pl.BlockSpec((N_EXPERTS, 1, D_MODEL), whole),
        ],
        out_specs=_tiled_spec(EXPERT_STEP_BLOCKS * MOE_BLOCK, rows),
        scratch_shapes=[
            pltpu.VMEM((2, D_MODEL, 2 * D_EXPERT), F32),
            pltpu.VMEM((2, D_EXPERT, D_MODEL), F32),
            pltpu.VMEM((D_MODEL, 2 * D_EXPERT), BF16),
            pltpu.VMEM((D_EXPERT, D_MODEL), BF16),
            pltpu.SemaphoreType.DMA((2,)),
            pltpu.SemaphoreType.DMA((2,)),
        ],
    )
    return pl.pallas_call(
        _expert_kernel,
        grid_spec=grid_spec,
        out_shape=jax.ShapeDtypeStruct(xs.shape, F32),
        compiler_params=pltpu.CompilerParams(dimension_semantics=("arbitrary",), vmem_limit_bytes=EXPERT_VMEM_LIMIT),
        name="experts",
    )(block_e, first, nxt, par, n_used, xs, w1, b1.reshape(N_EXPERTS, 1, -1), w2, b2.reshape(N_EXPERTS, 1, -1))


def _combine_kernel(slot_ref, next_slot_ref, ys_hbm, gate_ref, x_ref, lnw_ref, lnb_ref, o_ref, ybuf, sem):
    i = pl.program_id(0)
    c = SEQ_BLOCK

    def gather(slots, p):
        def issue(j, carry):
            for u in range(GATHER_UNROLL):
                r = j * GATHER_UNROLL + u
                for k in range(TOP_K):
                    src = _tile_of(ys_hbm, slots[0, 0, k * c + r])
                    pltpu.make_async_copy(src, _tile_of(ybuf.at[p, k], r), sem.at[p]).start(priority=k % 2)
            return carry
        lax.fori_loop(0, c // GATHER_UNROLL, issue, 0)

    @pl.when(i == 0)
    def _():
        gather(slot_ref, 0)

    @pl.when(i + 1 < pl.num_programs(0))
    def _():
        gather(next_slot_ref, (i + 1) % 2)

    p = i % 2
    for k in range(TOP_K):
        pltpu.make_async_copy(ys_hbm.at[pl.ds(0, ybuf.shape[2])], ybuf.at[p, k], sem.at[p]).wait()
    gates = gate_ref[...]
    ys = [_load_row_tiles(ybuf.at[p, k]) * gates[:, k:k + 1] for k in range(TOP_K)]
    y = (ys[0] + ys[1]) + (ys[2] + ys[3])
    o_ref[...] = _layer_norm(DEEPNORM_ALPHA * _load_row_tiles(x_ref) + y, lnw_ref[...], lnb_ref[...])


def _combine(ys, slots, gates, x2, ln_w, ln_b):
    t = x2.shape[0] // ROW_TILE[0]
    c = SEQ_BLOCK
    n = t // c
    slot_table = slots.reshape(n, 1, c * TOP_K)
    return pl.pallas_call(
        _combine_kernel,
        grid=(n,),
        in_specs=[
            pl.BlockSpec((1, 1, c * TOP_K), lambda i: (i, 0, 0), memory_space=pltpu.SMEM),
            pl.BlockSpec((1, 1, c * TOP_K), lambda i: (jnp.minimum(i + 1, n - 1), 0, 0), memory_space=pltpu.SMEM),
            pl.BlockSpec(memory_space=pl.ANY),
            pl.BlockSpec((c, TOP_K), lambda i: (i, 0)),
            _tiled_spec(c, lambda i: (i, 0)),
            pl.BlockSpec((1, D_MODEL), lambda i: (0, 0)),
            pl.BlockSpec((1, D_MODEL), lambda i: (0, 0)),
        ],
        out_specs=pl.BlockSpec((c, D_MODEL), lambda i: (i, 0)),
        out_shape=jax.ShapeDtypeStruct((t, D_MODEL), F32),
        scratch_shapes=[pltpu.VMEM((2, TOP_K) + _tiled(c), F32), pltpu.SemaphoreType.DMA((2,))],
        compiler_params=pltpu.CompilerParams(dimension_semantics=("arbitrary",), vmem_limit_bytes=VMEM_LIMIT),
        name="combine",
    )(slot_table, slot_table, ys, gates, x2, ln_w.reshape(1, -1), ln_b.reshape(1, -1))


def _slot_tables(counts, packed, n_blocks, shift):
    counts = counts.reshape(-1)
    nblk = (counts + MOE_BLOCK - 1) // MOE_BLOCK
    ids = jnp.arange(N_EXPERTS)
    lower = ids[None, :] <= ids[:, None]
    bend = jnp.sum(jnp.where(lower, nblk[None, :], 0), axis=1)
    n_used = bend[N_EXPERTS - 1]
    pend = bend * MOE_BLOCK
    pstart = pend - nblk * MOE_BLOCK
    fill = pstart + counts
    order = jnp.sum(jnp.where(lower, (nblk > 0)[None, :], False), axis=1) - 1
    later = (ids[None, :] > ids[:, None]) & (nblk > 0)[None, :]
    nxt_e = jnp.min(jnp.where(later, ids[None, :], N_EXPERTS), axis=1)
    nxt_e = jnp.where(nxt_e == N_EXPERTS, -1, nxt_e)
    b = jnp.minimum(jnp.arange(n_blocks), n_used - 1)
    hit = bend[None, :] <= b[:, None]
    block_e = jnp.sum(hit.astype(jnp.int32), axis=1)
    of_block = lambda per_expert: jnp.sum(jnp.where(ids[None, :] == block_e[:, None], per_expert[None, :], 0), axis=1)
    first = (b == of_block(bend - nblk)).astype(jnp.int32)
    expert = lax.shift_right_logical(packed, shift)
    rank = packed & ((1 << shift) - 1)
    slots = jnp.sum(jnp.where(expert[..., None] == ids, pstart, 0), axis=-1) + rank
    i32 = lambda a: a.astype(jnp.int32)
    plan = (i32(block_e), first, i32(of_block(nxt_e)), i32(of_block(order & 1)), i32(n_used.reshape(1)))
    return i32(slots), i32(fill), i32(pend), plan


def kernel(x, mem, w_in, lb_logits, hgrn_norm_w, w_pool, pool_scale, w_out, ln1_w, ln1_b, w_xq, w_xk, w_xv, w_xo,
           ln2_w, ln2_b, w_router, b_router, w1, b1, w2, b2, ln3_w, ln3_b):
    bsz, seq, d = x.shape
    assert bsz == 1 and d == D_MODEL and seq % SEQ_BLOCK == 0 and w_in.shape[0] == 1
    xt = x.reshape(seq, d)
    x1 = _mixer(xt, w_in[0], lb_logits, hgrn_norm_w[0], w_pool[0], pool_scale[0], w_out[0], ln1_w[0], ln1_b[0])
    score_w, out_w = _kv_proj(mem[0], w_xk[0], w_xv[0], w_xq[0], w_xo[0])
    x2, packed, gates, counts = _xattn_router(x1, score_w, out_w, ln2_w[0], ln2_b[0], w_router[0], b_router[0])
    n_blocks = -(-(seq * TOP_K + N_EXPERTS * (MOE_BLOCK - 1)) // MOE_BLOCK)
    slots, fill, pend, plan = _slot_tables(counts, packed, n_blocks, _rank_bits(seq))
    nb, _, cx = slots.shape
    slots = slots.reshape(nb, TOP_K, cx // SEQ_BLOCK, SEQ_BLOCK).transpose(0, 2, 1, 3).reshape(-1, TOP_K, SEQ_BLOCK)
    xs = _dispatch(x2, slots, fill, pend, plan[-1], n_blocks)
    ys = _experts(xs, plan, w1[0], b1[0], w2[0], b2[0])
    gates_tk = gates.transpose(0, 2, 1).reshape(seq, TOP_K)
    out = _combine(ys, slots, gates_tk, x2, ln3_w[0], ln3_b[0])
    return out.reshape(bsz, seq, d)
```

```python
import functools

import numpy as np
import jax
import jax.numpy as jnp
from jax import lax
from jax.experimental import pallas as pl
from jax.experimental.pallas import tpu as pltpu

F32 = jnp.float32
BF16 = jnp.bfloat16

D_MODEL = 1024
HGRN_WIDTH = 512
HGRN_HEADS = 4
HEAD_DIM = 128
POOL_WINDOWS = (2, 4, 8, 16)
POOL_GROUP = 128
POOL_WIDTH = POOL_GROUP * len(POOL_WINDOWS)
POOL_CARRY = 24
POOL_HEAD = 8
IN_PROJ_WIDTH = 4 * HGRN_WIDTH + POOL_WIDTH
XATTN_HEADS = 4
XATTN_HEAD_DIM = 256
N_EXPERTS = 32
TOP_K = 4
D_EXPERT = 1024
SWIGLU_LIMIT = 7.0
SWIGLU_ALPHA = 1.702
MOE_BLOCK = 256
DEEPNORM_ALPHA = 2.0 ** 0.25
LN_EPS = 1e-5
RMS_EPS = 1e-6

SEQ_BLOCK = 256
EXPERT_STEP_BLOCKS = 4
XATTN_BLOCK = 1024
DISPATCH_BLOCK = 2048
N_LEVELS = 8
MXU_LEVELS = 3
ISSUE_UNROLL = 8
GATHER_UNROLL = 16
ROW_TILE = (8, 128)
VMEM_LIMIT = 48 * 1024 * 1024
EXPERT_VMEM_LIMIT = 56 * 1024 * 1024


def _rank_bits(t):
    return max(int(t - 1).bit_length(), 1)


def _layer_norm(y, w, b):
    mu = jnp.mean(y, axis=-1, keepdims=True)
    yc = y - mu
    var = jnp.mean(yc * yc, axis=-1, keepdims=True)
    return yc * lax.rsqrt(var + LN_EPS) * w + b


def _tiled(n):
    return (n * ROW_TILE[0], ROW_TILE[1])


def _tiled_spec(n, index_map):
    return pl.BlockSpec(_tiled(n), index_map)


def _tile_of(ref, r):
    return ref.at[pl.ds(pl.multiple_of(r * ROW_TILE[0], ROW_TILE[0]), ROW_TILE[0])]


def _store_row_tiles(ref, val):
    s, l = ROW_TILE
    for j in range(s):
        ref[pl.ds(j, val.shape[0], stride=s), :] = val[:, j * l:(j + 1) * l]


def _load_row_tiles(ref):
    s, _ = ROW_TILE
    return jnp.concatenate([ref[pl.ds(j, ref.shape[0] // s, stride=s), :] for j in range(s)], axis=-1)


def _dot(a, b):
    return jnp.dot(a, b, preferred_element_type=F32)


def _dot_nt(a, b):
    return lax.dot_general(a, b, (((1,), (1,)), ((), ())), preferred_element_type=F32)


def _decay_tables():
    c = SEQ_BLOCK
    r = np.arange(c)[:, None]
    u = np.arange(c)[None, :]
    mats = [(u <= r)]
    for l in range(MXU_LEVELS):
        h = 1 << l
        mid = (r // (2 * h)) * (2 * h) + h
        upper = (r >= mid) & (u >= mid) & (u <= r)
        lower = (r < mid) & (u > r) & (u < mid)
        mats.append(upper | lower)
    mall = np.concatenate(mats, axis=0).astype(np.float32)
    x = r ^ u
    lvl = np.where(u < r, np.floor(np.log2(np.maximum(x, 1))).astype(np.int32), -1).astype(np.int32)
    strict_lower = (u < r).astype(np.float32)
    return mall, lvl, strict_lower


def _upper_chunks(l, c):
    half, sub = 1 << l, ROW_TILE[0]
    return [r // sub for base in range(0, c, 2 * half) for r in range(base + half, base + 2 * half, sub)]


def _level_gap(negb, l):
    half = 1 << l
    parts = []
    for base in range(0, negb.shape[0], 2 * half):
        ref_row = negb[base + half - 1:base + half, :]
        parts.append(jnp.abs(negb[base:base + 2 * half, :] - ref_row))
    return jnp.concatenate(parts, axis=0) if len(parts) > 1 else parts[0]


def _mixer_kernel(x_ref, win_ref, lbl_ref, nw_ref, wpool_ref, pscale_ref, wout_ref, lnw_ref, lnb_ref,
                  mall_ref, lvl_ref, o_ref, state_ref, carry_ref, ext_ref, mix_ref, sum_ref):
    i = pl.program_id(0)
    c = SEQ_BLOCK

    @pl.when(i == 0)
    def _():
        state_ref[...] = jnp.zeros_like(state_ref)
        carry_ref[...] = jnp.zeros_like(carry_ref)
        sum_ref[:, 0:POOL_HEAD, :] = jnp.zeros((2, POOL_HEAD, POOL_WIDTH), F32)

    x = x_ref[...]
    proj = _dot(x.astype(BF16), win_ref[...])
    w = HGRN_WIDTH
    q = proj[:, 0:w]
    z = proj[:, w:2 * w]
    v = proj[:, 2 * w:3 * w]
    g = proj[:, 3 * w:4 * w]
    p = proj[:, 4 * w:]

    ll = lbl_ref[...]
    ee = jnp.exp(ll - jnp.max(ll, axis=0, keepdims=True))
    lb = ee[0:1] / jnp.sum(ee, axis=0, keepdims=True)

    f = lb + (1.0 - lb) * (1.0 / (1.0 + jnp.exp(-z)))
    nlf = -jnp.log(f)
    key = (1.0 - lb) * (1.0 / (1.0 + jnp.exp(z)))

    hi = nlf.astype(BF16)
    lo = (nlf - hi.astype(F32)).astype(BF16)
    mall = mall_ref[...]
    cum = _dot(mall, hi) + _dot(mall, lo)
    sub = ROW_TILE[0]
    lvl_rows = [lvl_ref[sub * r:sub * (r + 1), :] for r in range(c // sub)]

    for h in range(HGRN_HEADS):
        hs = slice(h * HEAD_DIM, (h + 1) * HEAD_DIM)
        qh, kh, vh, gh = q[:, hs], key[:, hs], v[:, hs], g[:, hs]
        vb = vh.astype(BF16)
        negb = cum[0:c, hs]
        sc = [jnp.zeros((sub, c), F32)] * (c // sub)
        for l in range(N_LEVELS):
            gap = cum[(l + 1) * c:(l + 2) * c, hs] if l < MXU_LEVELS else _level_gap(negb, l)
            e = jnp.exp(-gap)
            ke = (kh * e).astype(BF16)
            if l < MXU_LEVELS:
                rows = list(range(c // sub))
                qe = (qh * e).astype(BF16)
            else:
                rows = _upper_chunks(l, c)
                take = lambda a: jnp.concatenate([a[sub * r:sub * (r + 1)] for r in rows], axis=0)
                qe = (take(qh) * take(e)).astype(BF16)
            pl_ = _dot_nt(qe, ke)
            for j, r in enumerate(rows):
                sc[r] = jnp.where(lvl_rows[r] == l, pl_[sub * j:sub * (j + 1)], sc[r])
        scores = jnp.concatenate(sc, axis=0)
        diag = jnp.sum(qh * kh, axis=-1, keepdims=True)
        st = state_ref[h]
        o = (_dot(scores.astype(BF16), vb) + diag * vh
             + _dot_nt((qh * jnp.exp(-negb)).astype(BF16), st.astype(BF16)))
        negb_last = negb[c - 1:c, :]
        kd = (kh * jnp.exp(negb - negb_last)).astype(BF16)
        state_ref[h] = st * jnp.exp(-negb_last) + _dot(vh.T.astype(BF16), kd)
        o = o * lax.rsqrt(jnp.mean(o * o, axis=-1, keepdims=True) + RMS_EPS) * nw_ref[:, hs]
        gate = gh * (1.0 / (1.0 + jnp.exp(-gh)))
        mix_ref[:, hs] = (o * gate).astype(BF16)

    top = POOL_CARRY + c
    ext_ref[0:POOL_CARRY, :] = carry_ref[...]
    ext_ref[POOL_CARRY:top, :] = p
    carry_ref[...] = p[c - POOL_CARRY:c, :]
    pos = i * c + lax.broadcasted_iota(jnp.int32, (c, 1), 0)
    src = ext_ref
    for gi, win in enumerate(POOL_WINDOWS):
        gs = slice(gi * POOL_GROUP, (gi + 1) * POOL_GROUP)
        rest = slice(gi * POOL_GROUP, POOL_WIDTH)
        half = win // 2
        dst = sum_ref.at[gi % 2]
        dst[POOL_HEAD:top, rest] = src[POOL_HEAD:top, rest] + src[POOL_HEAD - half:top - half, rest]
        src = dst
        pg = p[:, gs]
        tot = dst[POOL_CARRY:top, gs]
        cnt = jnp.minimum(pos + 1, win).astype(F32)
        pooled = _dot((tot / cnt - pg).astype(BF16), wpool_ref[gi])
        mix_ref[:, w + gi * POOL_GROUP:w + (gi + 1) * POOL_GROUP] = (pooled * pscale_ref[:, gs]).astype(BF16)

    y = DEEPNORM_ALPHA * x + _dot(mix_ref[...], wout_ref[...])
    o_ref[...] = _layer_norm(y, lnw_ref[...], lnb_ref[...])


def _mixer(x, w_in, lb_logits, norm_w, w_pool, pool_scale, w_out, ln_w, ln_b):
    t = x.shape[0]
    c = SEQ_BLOCK
    mall, lvl, _ = _decay_tables()
    full = lambda shape: pl.BlockSpec(shape, lambda i: (0,) * len(shape))
    return pl.pallas_call(
        _mixer_kernel,
        grid=(t // c,),
        in_specs=[
            pl.BlockSpec((c, D_MODEL), lambda i: (i, 0)),
            full((D_MODEL, IN_PROJ_WIDTH)),
            full(lb_logits.shape),
            full((1, HGRN_WIDTH)),
            full((len(POOL_WINDOWS), POOL_GROUP, POOL_GROUP)),
            full((1, POOL_WIDTH)),
            full((D_MODEL, D_MODEL)),
            full((1, D_MODEL)),
            full((1, D_MODEL)),
            full(mall.shape),
            full(lvl.shape),
        ],
        out_specs=pl.BlockSpec((c, D_MODEL), lambda i: (i, 0)),
        out_shape=jax.ShapeDtypeStruct((t, D_MODEL), F32),
        scratch_shapes=[
            pltpu.VMEM((HGRN_HEADS, HEAD_DIM, HEAD_DIM), F32),
            pltpu.VMEM((POOL_CARRY, POOL_WIDTH), F32),
            pltpu.VMEM((POOL_CARRY + c, POOL_WIDTH), F32),
            pltpu.VMEM((c, D_MODEL), BF16),
            pltpu.VMEM((2, POOL_CARRY + c, POOL_WIDTH), F32),
        ],
        compiler_params=pltpu.CompilerParams(dimension_semantics=("arbitrary",), vmem_limit_bytes=VMEM_LIMIT),
        name="mixer",
    )(x, w_in.astype(BF16), lb_logits, norm_w.reshape(1, -1), w_pool.astype(BF16), pool_scale.reshape(1, -1),
      w_out.astype(BF16), ln_w.reshape(1, -1), ln_b.reshape(1, -1), jnp.asarray(mall, BF16), jnp.asarray(lvl))


def _kv_kernel(mem_ref, wk_ref, wv_ref, wq_ref, wo_ref, a_ref, b_ref):
    n_mem = mem_ref.shape[0]
    m = mem_ref[...].astype(BF16)
    k = _dot(m, wk_ref[...]).astype(BF16)
    v = _dot(m, wv_ref[...]).astype(BF16)
    scale = XATTN_HEAD_DIM ** -0.5
    for h in range(XATTN_HEADS):
        hs = slice(h * XATTN_HEAD_DIM, (h + 1) * XATTN_HEAD_DIM)
        ms = slice(h * n_mem, (h + 1) * n_mem)
        a_ref[:, ms] = (_dot_nt(wq_ref[:, hs], k[:, hs]) * scale).astype(BF16)
        b_ref[ms, :] = _dot(v[:, hs], wo_ref[hs, :]).astype(BF16)


def _kv_proj(mem, wk, wv, wq, wo):
    n = mem.shape[0]
    return pl.pallas_call(
        _kv_kernel,
        out_shape=(jax.ShapeDtypeStruct((D_MODEL, XATTN_HEADS * n), BF16),
                   jax.ShapeDtypeStruct((XATTN_HEADS * n, D_MODEL), BF16)),
        compiler_params=pltpu.CompilerParams(vmem_limit_bytes=VMEM_LIMIT),
        name="kv_proj",
    )(mem, wk.astype(BF16), wv.astype(BF16), wq.astype(BF16), wo.astype(BF16))


def _split3(a):
    a0 = a.astype(BF16)
    r1 = a - a0.astype(F32)
    a1 = r1.astype(BF16)
    a2 = (r1 - a1.astype(F32)).astype(BF16)
    return a0, a1, a2


def _xattn_kernel(x_ref, a_ref, b_ref, lnw_ref, lnb_ref, wrt_ref, br_ref, ut_ref,
                  x2_ref, dest_ref, gate_ref, cnt_ref, att_ref, run_ref, *, shift):
    i = pl.program_id(0)
    c = x_ref.shape[0]
    n = N_EXPERTS

    @pl.when(i == 0)
    def _():
        run_ref[...] = jnp.zeros_like(run_ref)

    x = x_ref[...]
    scores = _dot(x.astype(BF16), a_ref[...])
    n_mem = a_ref.shape[1] // XATTN_HEADS
    for h in range(XATTN_HEADS):
        ms = slice(h * n_mem, (h + 1) * n_mem)
        s = scores[:, ms]
        e = jnp.exp(s - jnp.max(s, axis=-1, keepdims=True))
        att_ref[:, ms] = (e / jnp.sum(e, axis=-1, keepdims=True)).astype(BF16)
    y = DEEPNORM_ALPHA * x + _dot(att_ref[...], b_ref[...])
    x2 = _layer_norm(y, lnw_ref[...], lnb_ref[...])
    _store_row_tiles(x2_ref, x2)

    a0, a1, _ = _split3(x2)
    w0, w1, w2 = _split3(wrt_ref[...])
    pa = _dot_nt(jnp.concatenate([w0, w1, w2], axis=0), a0)
    pb = _dot_nt(jnp.concatenate([w0, w1], axis=0), a1)
    logits = ((pa[2 * n:3 * n] + pb[n:2 * n]) + (pb[0:n] + pa[n:2 * n]) + pa[0:n]) + br_ref[...]

    eid = lax.broadcasted_iota(jnp.int32, (n, c), 0)
    work = logits
    sels, vals, idxs = [], [], []
    for k in range(TOP_K):
        m = jnp.max(work, axis=0, keepdims=True)
        idx = jnp.min(jnp.where(work == m, eid, n), axis=0, keepdims=True)
        sel = eid == idx
        sels.append(sel)
        vals.append(m)
        idxs.append(idx)
        work = jnp.where(sel, -jnp.inf, work)
    es = [jnp.exp(vk - vals[0]) for vk in vals]
    denom = es[0] + es[1] + es[2] + es[3]
    gate_ref[0] = jnp.concatenate([ek / denom for ek in es], axis=0)

    onehot = jnp.zeros((n, c), F32)
    for sel in sels:
        onehot = jnp.where(sel, 1.0, onehot)
    before = _dot(onehot.astype(BF16), ut_ref[...]) + run_ref[...]
    ranks = [jnp.sum(jnp.where(sel, before, 0.0), axis=0, keepdims=True).astype(jnp.int32) for sel in sels]
    run_ref[...] = run_ref[...] + jnp.sum(onehot, axis=1, keepdims=True)

    dest_ref[0] = jnp.concatenate([idx * (1 << shift) + rk for idx, rk in zip(idxs, ranks)], axis=0)
    cnt_ref[...] = run_ref[...].astype(jnp.int32)


def _xattn_router(x1, score_w, out_w, ln_w, ln_b, w_r, b_r):
    t = x1.shape[0]
    c = XATTN_BLOCK if t % XATTN_BLOCK == 0 else SEQ_BLOCK
    idx = np.arange(c)
    strict_upper = (idx[:, None] < idx[None, :]).astype(np.float32)
    full = lambda shape: pl.BlockSpec(shape, lambda i: (0,) * len(shape))
    per_k = pl.BlockSpec((1, TOP_K, c), lambda i: (i, 0, 0))
    return pl.pallas_call(
        functools.partial(_xattn_kernel, shift=_rank_bits(t)),
        grid=(t // c,),
        in_specs=[
            pl.BlockSpec((c, D_MODEL), lambda i: (i, 0)),
            full(score_w.shape),
            full(out_w.shape),
            full((1, D_MODEL)),
            full((1, D_MODEL)),
            full((N_EXPERTS, D_MODEL)),
            full((N_EXPERTS, 1)),
            full((c, c)),
        ],
        out_specs=[_tiled_spec(c, lambda i: (i, 0)), per_k, per_k, full((N_EXPERTS, 1))],
        out_shape=[
            jax.ShapeDtypeStruct(_tiled(t), F32),
            jax.ShapeDtypeStruct((t // c, TOP_K, c), jnp.int32),
            jax.ShapeDtypeStruct((t // c, TOP_K, c), F32),
            jax.ShapeDtypeStruct((N_EXPERTS, 1), jnp.int32),
        ],
        scratch_shapes=[pltpu.VMEM((c, score_w.shape[1]), BF16), pltpu.VMEM((N_EXPERTS, 1), F32)],
        compiler_params=pltpu.CompilerParams(dimension_semantics=("arbitrary",), vmem_limit_bytes=VMEM_LIMIT),
        name="xattn_router",
    )(x1, score_w, out_w, ln_w.reshape(1, -1), ln_b.reshape(1, -1),
      w_r.T, b_r.reshape(-1, 1), jnp.asarray(strict_upper, BF16))


TAIL_RUNS = (128, 64, 32, 16, 8, 4, 2, 1)


def _dispatch_kernel(fill_ref, pend_ref, nused_ref, slot_ref, x_ref, xs_hbm, zero_ref, sem, zsem, *, n_blocks):
    i = pl.program_id(0)
    c = SEQ_BLOCK

    @pl.when(i == 0)
    def _():
        zero_ref[...] = jnp.zeros_like(zero_ref)

        sub = ROW_TILE[0]

        def rows(ref, first, n):
            return ref.at[pl.ds(pl.multiple_of(first * sub, sub), n * sub)]

        def tail(e, act):
            row = fill_ref[e]
            pad = pend_ref[e] - row
            for run in TAIL_RUNS:
                hit = (pad & run) != 0

                @pl.when(hit)
                def _():
                    act(pltpu.make_async_copy(rows(zero_ref, 0, run), rows(xs_hbm, row, run), zsem))
                row = row + jnp.where(hit, run, 0)

        def spare(b, act):
            act(pltpu.make_async_copy(zero_ref, rows(xs_hbm, b * MOE_BLOCK, MOE_BLOCK), zsem))

        for act in (lambda cp: cp.start(), lambda cp: cp.wait()):
            lax.fori_loop(0, N_EXPERTS, lambda e, carry: (tail(e, act), carry)[1], 0)
            lax.fori_loop(nused_ref[0], n_blocks, lambda b, carry: (spare(b, act), carry)[1], 0)

    for q in range(x_ref.shape[0] // (c * ROW_TILE[0])):
        def issue(j, carry, q=q):
            for u in range(ISSUE_UNROLL):
                r = j * ISSUE_UNROLL + u
                for k in range(TOP_K):
                    dst = _tile_of(xs_hbm, slot_ref[0, 0, (q * TOP_K + k) * c + r])
                    pltpu.make_async_copy(_tile_of(x_ref, q * c + r), dst, sem).start(priority=k % 2)
            return carry
        lax.fori_loop(0, c // ISSUE_UNROLL, issue, 0)
    for k in range(TOP_K):
        pltpu.make_async_copy(x_ref, xs_hbm.at[pl.ds(0, x_ref.shape[0])], sem).wait()


def _dispatch(x2, slots, fill, pend, n_used, n_blocks):
    t = x2.shape[0] // ROW_TILE[0]
    c = min(DISPATCH_BLOCK, t)
    assert t % c == 0 and c % SEQ_BLOCK == 0
    grid_spec = pltpu.PrefetchScalarGridSpec(
        num_scalar_prefetch=3,
        grid=(t // c,),
        in_specs=[
            pl.BlockSpec((1, 1, c * TOP_K), lambda i, *_: (i, 0, 0), memory_space=pltpu.SMEM),
            _tiled_spec(c, lambda i, *_: (i, 0)),
        ],
        out_specs=pl.BlockSpec(memory_space=pl.ANY),
        scratch_shapes=[pltpu.VMEM(_tiled(MOE_BLOCK), F32), pltpu.SemaphoreType.DMA, pltpu.SemaphoreType.DMA],
    )
    return pl.pallas_call(
        functools.partial(_dispatch_kernel, n_blocks=n_blocks),
        grid_spec=grid_spec,
        out_shape=jax.ShapeDtypeStruct(_tiled(n_blocks * MOE_BLOCK), F32),
        compiler_params=pltpu.CompilerParams(dimension_semantics=("arbitrary",), vmem_limit_bytes=VMEM_LIMIT),
        name="dispatch",
    )(fill, pend, n_used, slots.reshape(t // c, 1, c * TOP_K), x2)


def _expert_kernel(be_ref, first_ref, nxt_ref, par_ref, nused_ref, x_ref, w1_hbm, b1_ref, w2_hbm, b2_ref, o_ref,
                   w1f, w2f, w1b, w2b, sem1, sem2):
    step = pl.program_id(0)
    rows = MOE_BLOCK * ROW_TILE[0]

    def fetch(e, p):
        return (pltpu.make_async_copy(w1_hbm.at[e], w1f.at[p], sem1.at[p]),
                pltpu.make_async_copy(w2_hbm.at[e], w2f.at[p], sem2.at[p]))

    @pl.when(step == 0)
    def _():
        for cp in fetch(be_ref[0], par_ref[0]):
            cp.start()

    def block(sb, carry):
        b = step * EXPERT_STEP_BLOCKS + sb
        x_blk = x_ref.at[pl.ds(pl.multiple_of(sb * rows, rows), rows)]
        o_blk = o_ref.at[pl.ds(pl.multiple_of(sb * rows, rows), rows)]

        @pl.when(b >= nused_ref[0])
        def _():
            o_blk[...] = jnp.zeros(o_blk.shape, o_blk.dtype)

        @pl.when(b < nused_ref[0])
        def _():
            e = be_ref[b]

            @pl.when(first_ref[b] == 1)
            def _():
                p = par_ref[b]
                for cp in fetch(e, p):
                    cp.wait()

                @pl.when(nxt_ref[b] >= 0)
                def _():
                    for cp in fetch(nxt_ref[b], 1 - p):
                        cp.start()

                w1b[...] = w1f[p].astype(BF16)
                w2b[...] = w2f[p].astype(BF16)

            h = _dot(_load_row_tiles(x_blk).astype(BF16), w1b[...]) + b1_ref[e]
            h_glu = jnp.minimum(h[:, :D_EXPERT], SWIGLU_LIMIT)
            h_lin = jnp.clip(h[:, D_EXPERT:], -SWIGLU_LIMIT, SWIGLU_LIMIT)
            act = h_glu * (1.0 / (1.0 + jnp.exp(-SWIGLU_ALPHA * h_glu))) * (h_lin + 1.0)
            _store_row_tiles(o_blk, _dot(act.astype(BF16), w2b[...]) + b2_ref[e])
        return carry

    lax.fori_loop(0, EXPERT_STEP_BLOCKS, block, 0)


def _experts(xs, plan, w1, b1, w2, b2):
    block_e, first, nxt, par, n_used = plan
    n_blocks = block_e.shape[0]
    assert n_blocks % EXPERT_STEP_BLOCKS == 0
    rows = lambda s, *_: (s, 0)
    whole = lambda s, *_: (0, 0, 0)
    grid_spec = pltpu.PrefetchScalarGridSpec(
        num_scalar_prefetch=5,
        grid=(n_blocks // EXPERT_STEP_BLOCKS,),
        in_specs=[
            _tiled_spec(EXPERT_STEP_BLOCKS * MOE_BLOCK, rows),
            pl.BlockSpec(memory_space=pl.ANY),
            pl.BlockSpec((N_EXPERTS, 1, 2 * D_EXPERT), whole),
            pl.BlockSpec(memory_space=pl.ANY),
            pl.BlockSpec((N_EXPERTS, 1, D_MODEL), whole),
        ],
        out_specs=_tiled_spec(EXPERT_STEP_BLOCKS * MOE_BLOCK, rows),
        scratch_shapes=[
            pltpu.VMEM((2, D_MODEL, 2 * D_EXPERT), F32),
            pltpu.VMEM((2, D_EXPERT, D_MODEL), F32),
            pltpu.VMEM((D_MODEL, 2 * D_EXPERT), BF16),
            pltpu.VMEM((D_EXPERT, D_MODEL), BF16),
            pltpu.SemaphoreType.DMA((2,)),
            pltpu.SemaphoreType.DMA((2,)),
        ],
    )
    return pl.pallas_call(
        _expert_kernel,
        grid_spec=grid_spec,
        out_shape=jax.ShapeDtypeStruct(xs.shape, F32),
        compiler_params=pltpu.CompilerParams(dimension_semantics=("arbitrary",), vmem_limit_bytes=EXPERT_VMEM_LIMIT),
        name="experts",
    )(block_e, first, nxt, par, n_used, xs, w1, b1.reshape(N_EXPERTS, 1, -1), w2, b2.reshape(N_EXPERTS, 1, -1))


def _combine_kernel(slot_ref, next_slot_ref, ys_hbm, gate_ref, x_ref, lnw_ref, lnb_ref, o_ref, ybuf, sem):
    i = pl.program_id(0)
    c = SEQ_BLOCK

    def gather(slots, p):
        def issue(j, carry):
            for u in range(GATHER_UNROLL):
                r = j * GATHER_UNROLL + u
                for k in range(TOP_K):
                    src = _tile_of(ys_hbm, slots[0, 0, k * c + r])
                    pltpu.make_async_copy(src, _tile_of(ybuf.at[p, k], r), sem.at[p]).start(priority=k % 2)
            return carry
        lax.fori_loop(0, c // GATHER_UNROLL, issue, 0)

    @pl.when(i == 0)
    def _():
        gather(slot_ref, 0)

    @pl.when(i + 1 < pl.num_programs(0))
    def _():
        gather(next_slot_ref, (i + 1) % 2)

    p = i % 2
    for k in range(TOP_K):
        pltpu.make_async_copy(ys_hbm.at[pl.ds(0, ybuf.shape[2])], ybuf.at[p, k], sem.at[p]).wait()
    gates = gate_ref[...]
    ys = [_load_row_tiles(ybuf.at[p, k]) * gates[:, k:k + 1] for k in range(TOP_K)]
    y = (ys[0] + ys[1]) + (ys[2] + ys[3])
    o_ref[...] = _layer_norm(DEEPNORM_ALPHA * _load_row_tiles(x_ref) + y, lnw_ref[...], lnb_ref[...])


def _combine(ys, slots, gates, x2, ln_w, ln_b):
    t = x2.shape[0] // ROW_TILE[0]
    c = SEQ_BLOCK
    n = t // c
    slot_table = slots.reshape(n, 1, c * TOP_K)
    return pl.pallas_call(
        _combine_kernel,
        grid=(n,),
        in_specs=[
            pl.BlockSpec((1, 1, c * TOP_K), lambda i: (i, 0, 0), memory_space=pltpu.SMEM),
            pl.BlockSpec((1, 1, c * TOP_K), lambda i: (jnp.minimum(i + 1, n - 1), 0, 0), memory_space=pltpu.SMEM),
            pl.BlockSpec(memory_space=pl.ANY),
            pl.BlockSpec((c, TOP_K), lambda i: (i, 0)),
            _tiled_spec(c, lambda i: (i, 0)),
            pl.BlockSpec((1, D_MODEL), lambda i: (0, 0)),
            pl.BlockSpec((1, D_MODEL), lambda i: (0, 0)),
        ],
        out_specs=pl.BlockSpec((c, D_MODEL), lambda i: (i, 0)),
        out_shape=jax.ShapeDtypeStruct((t, D_MODEL), F32),
        scratch_shapes=[pltpu.VMEM((2, TOP_K) + _tiled(c), F32), pltpu.SemaphoreType.DMA((2,))],
        compiler_params=pltpu.CompilerParams(dimension_semantics=("arbitrary",), vmem_limit_bytes=VMEM_LIMIT),
        name="combine",
    )(slot_table, slot_table, ys, gates, x2, ln_w.reshape(1, -1), ln_b.reshape(1, -1))


def _slot_tables(counts, packed, n_blocks, shift):
    counts = counts.reshape(-1)
    nblk = (counts + MOE_BLOCK - 1) // MOE_BLOCK
    ids = jnp.arange(N_EXPERTS)
    lower = ids[None, :] <= ids[:, None]
    bend = jnp.sum(jnp.where(lower, nblk[None, :], 0), axis=1)
    n_used = bend[N_EXPERTS - 1]
    pend = bend * MOE_BLOCK
    pstart = pend - nblk * MOE_BLOCK
    fill = pstart + counts
    order = jnp.sum(jnp.where(lower, (nblk > 0)[None, :], False), axis=1) - 1
    later = (ids[None, :] > ids[:, None]) & (nblk > 0)[None, :]
    nxt_e = jnp.min(jnp.where(later, ids[None, :], N_EXPERTS), axis=1)
    nxt_e = jnp.where(nxt_e == N_EXPERTS, -1, nxt_e)
    b = jnp.minimum(jnp.arange(n_blocks), n_used - 1)
    hit = bend[None, :] <= b[:, None]
    block_e = jnp.sum(hit.astype(jnp.int32), axis=1)
    of_block = lambda per_expert: jnp.sum(jnp.where(ids[None, :] == block_e[:, None], per_expert[None, :], 0), axis=1)
    first = (b == of_block(bend - nblk)).astype(jnp.int32)
    expert = lax.shift_right_logical(packed, shift)
    rank = packed & ((1 << shift) - 1)
    slots = jnp.sum(jnp.where(expert[..., None] == ids, pstart, 0), axis=-1) + rank
    i32 = lambda a: a.astype(jnp.int32)
    plan = (i32(block_e), first, i32(of_block(nxt_e)), i32(of_block(order & 1)), i32(n_used.reshape(1)))
    return i32(slots), i32(fill), i32(pend), plan


def kernel(x, mem, w_in, lb_logits, hgrn_norm_w, w_pool, pool_scale, w_out, ln1_w, ln1_b, w_xq, w_xk, w_xv, w_xo,
           ln2_w, ln2_b, w_router, b_router, w1, b1, w2, b2, ln3_w, ln3_b):
    bsz, seq, d = x.shape
    assert bsz == 1 and d == D_MODEL and seq % SEQ_BLOCK == 0 and w_in.shape[0] == 1
    xt = x.reshape(seq, d)
    x1 = _mixer(xt, w_in[0], lb_logits, hgrn_norm_w[0], w_pool[0], pool_scale[0], w_out[0], ln1_w[0], ln1_b[0])
    score_w, out_w = _kv_proj(mem[0], w_xk[0], w_xv[0], w_xq[0], w_xo[0])
    x2, packed, gates, counts = _xattn_router(x1, score_w, out_w, ln2_w[0], ln2_b[0], w_router[0], b_router[0])
    n_blocks = -(-(seq * TOP_K + N_EXPERTS * (MOE_BLOCK - 1)) // MOE_BLOCK)
    slots, fill, pend, plan = _slot_tables(counts, packed, n_blocks, _rank_bits(seq))
    nb, _, cx = slots.shape
    slots = slots.reshape(nb, TOP_K, cx // SEQ_BLOCK, SEQ_BLOCK).transpose(0, 2, 1, 3).reshape(-1, TOP_K, SEQ_BLOCK)
    xs = _dispatch(x2, slots, fill, pend, plan[-1], n_blocks)
    ys = _experts(xs, plan, w1[0], b1[0], w2[0], b2[0])
    gates_tk = gates.transpose(0, 2, 1).reshape(seq, TOP_K)
    out = _combine(ys, slots, gates_tk, x2, ln3_w[0], ln3_b[0])
    return out.reshape(bsz, seq, d)
```

```python
import functools

import numpy as np
import jax
import jax.numpy as jnp
from jax import lax
from jax.experimental import pallas as pl
from jax.experimental.pallas import tpu as pltpu

F32 = jnp.float32
BF16 = jnp.bfloat16

D_MODEL = 1024
HGRN_WIDTH = 512
HGRN_HEADS = 4
HEAD_DIM = 128
POOL_WINDOWS = (2, 4, 8, 16)
POOL_GROUP = 128
POOL_WIDTH = POOL_GROUP * len(POOL_WINDOWS)
POOL_CARRY = 24
POOL_HEAD = 8
IN_PROJ_WIDTH = 4 * HGRN_WIDTH + POOL_WIDTH
XATTN_HEADS = 4
XATTN_HEAD_DIM = 256
N_EXPERTS = 32
TOP_K = 4
D_EXPERT = 1024
SWIGLU_LIMIT = 7.0
SWIGLU_ALPHA = 1.702
MOE_BLOCK = 256
DEEPNORM_ALPHA = 2.0 ** 0.25
LN_EPS = 1e-5
RMS_EPS = 1e-6

SEQ_BLOCK = 256
EXPERT_STEP_BLOCKS = 4
XATTN_BLOCK = 1024
DISPATCH_BLOCK = 2048
N_LEVELS = 8
MXU_LEVELS = 3
ISSUE_UNROLL = 8
GATHER_UNROLL = 16
ROW_TILE = (8, 128)
VMEM_LIMIT = 48 * 1024 * 1024
EXPERT_VMEM_LIMIT = 56 * 1024 * 1024


def _rank_bits(t):
    return max(int(t - 1).bit_length(), 1)


def _layer_norm(y, w, b):
    mu = jnp.mean(y, axis=-1, keepdims=True)
    yc = y - mu
    var = jnp.mean(yc * yc, axis=-1, keepdims=True)
    return yc * lax.rsqrt(var + LN_EPS) * w + b


def _tiled(n):
    return (n * ROW_TILE[0], ROW_TILE[1])


def _tiled_spec(n, index_map):
    return pl.BlockSpec(_tiled(n), index_map)


def _tile_of(ref, r):
    return ref.at[pl.ds(pl.multiple_of(r * ROW_TILE[0], ROW_TILE[0]), ROW_TILE[0])]


def _store_row_tiles(ref, val):
    s, l = ROW_TILE
    for j in range(s):
        ref[pl.ds(j, val.shape[0], stride=s), :] = val[:, j * l:(j + 1) * l]


def _load_row_tiles(ref):
    s, _ = ROW_TILE
    return jnp.concatenate([ref[pl.ds(j, ref.shape[0] // s, stride=s), :] for j in range(s)], axis=-1)


def _dot(a, b):
    return jnp.dot(a, b, preferred_element_type=F32)


def _dot_nt(a, b):
    return lax.dot_general(a, b, (((1,), (1,)), ((), ())), preferred_element_type=F32)


def _decay_tables():
    c = SEQ_BLOCK
    r = np.arange(c)[:, None]
    u = np.arange(c)[None, :]
    mats = [(u <= r)]
    for l in range(MXU_LEVELS):
        h = 1 << l
        mid = (r // (2 * h)) * (2 * h) + h
        upper = (r >= mid) & (u >= mid) & (u <= r)
        lower = (r < mid) & (u > r) & (u < mid)
        mats.append(upper | lower)
    mall = np.concatenate(mats, axis=0).astype(np.float32)
    x = r ^ u
    lvl = np.where(u < r, np.floor(np.log2(np.maximum(x, 1))).astype(np.int32), -1).astype(np.int32)
    strict_lower = (u < r).astype(np.float32)
    return mall, lvl, strict_lower


def _upper_chunks(l, c):
    half, sub = 1 << l, ROW_TILE[0]
    return [r // sub for base in range(0, c, 2 * half) for r in range(base + half, base + 2 * half, sub)]


def _level_gap(negb, l):
    half = 1 << l
    parts = []
    for base in range(0, negb.shape[0], 2 * half):
        ref_row = negb[base + half - 1:base + half, :]
        parts.append(jnp.abs(negb[base:base + 2 * half, :] - ref_row))
    return jnp.concatenate(parts, axis=0) if len(parts) > 1 else parts[0]


def _mixer_kernel(x_ref, win_ref, lbl_ref, nw_ref, wpool_ref, pscale_ref, wout_ref, lnw_ref, lnb_ref,
                  mall_ref, lvl_ref, o_ref, state_ref, carry_ref, ext_ref, mix_ref, sum_ref):
    i = pl.program_id(0)
    c = SEQ_BLOCK

    @pl.when(i == 0)
    def _():
        state_ref[...] = jnp.zeros_like(state_ref)
        carry_ref[...] = jnp.zeros_like(carry_ref)
        sum_ref[:, 0:POOL_HEAD, :] = jnp.zeros((2, POOL_HEAD, POOL_WIDTH), F32)

    x = x_ref[...]
    proj = _dot(x.astype(BF16), win_ref[...])
    w = HGRN_WIDTH
    q = proj[:, 0:w]
    z = proj[:, w:2 * w]
    v = proj[:, 2 * w:3 * w]
    g = proj[:, 3 * w:4 * w]
    p = proj[:, 4 * w:]

    ll = lbl_ref[...]
    ee = jnp.exp(ll - jnp.max(ll, axis=0, keepdims=True))
    lb = ee[0:1] / jnp.sum(ee, axis=0, keepdims=True)

    f = lb + (1.0 - lb) * (1.0 / (1.0 + jnp.exp(-z)))
    nlf = -jnp.log(f)
    key = (1.0 - lb) * (1.0 / (1.0 + jnp.exp(z)))

    hi = nlf.astype(BF16)
    lo = (nlf - hi.astype(F32)).astype(BF16)
    mall = mall_ref[...]
    cum = _dot(mall, hi) + _dot(mall, lo)
    sub = ROW_TILE[0]
    lvl_rows = [lvl_ref[sub * r:sub * (r + 1), :] for r in range(c // sub)]

    for h in range(HGRN_HEADS):
        hs = slice(h * HEAD_DIM, (h + 1) * HEAD_DIM)
        qh, kh, vh, gh = q[:, hs], key[:, hs], v[:, hs], g[:, hs]
        vb = vh.astype(BF16)
        negb = cum[0:c, hs]
        sc = [jnp.zeros((sub, c), F32)] * (c // sub)
        for l in range(N_LEVELS):
            gap = cum[(l + 1) * c:(l + 2) * c, hs] if l < MXU_LEVELS else _level_gap(negb, l)
            e = jnp.exp(-gap)
            ke = (kh * e).astype(BF16)
            if l < MXU_LEVELS:
                rows = list(range(c // sub))
                qe = (qh * e).astype(BF16)
            else:
                rows = _upper_chunks(l, c)
                take = lambda a: jnp.concatenate([a[sub * r:sub * (r + 1)] for r in rows], axis=0)
                qe = (take(qh) * take(e)).astype(BF16)
            pl_ = _dot_nt(qe, ke)
            for j, r in enumerate(rows):
                sc[r] = jnp.where(lvl_rows[r] == l, pl_[sub * j:sub * (j + 1)], sc[r])
        scores = jnp.concatenate(sc, axis=0)
        diag = jnp.sum(qh * kh, axis=-1, keepdims=True)
        st = state_ref[h]
        o = (_dot(scores.astype(BF16), vb) + diag * vh
             + _dot_nt((qh * jnp.exp(-negb)).astype(BF16), st.astype(BF16)))
        negb_last = negb[c - 1:c, :]
        kd = (kh * jnp.exp(negb - negb_last)).astype(BF16)
        state_ref[h] = st * jnp.exp(-negb_last) + _dot(vh.T.astype(BF16), kd)
        o = o * lax.rsqrt(jnp.mean(o * o, axis=-1, keepdims=True) + RMS_EPS) * nw_ref[:, hs]
        gate = gh * (1.0 / (1.0 + jnp.exp(-gh)))
        mix_ref[:, hs] = (o * gate).astype(BF16)

    top = POOL_CARRY + c
    ext_ref[0:POOL_CARRY, :] = carry_ref[...]
    ext_ref[POOL_CARRY:top, :] = p
    carry_ref[...] = p[c - POOL_CARRY:c, :]
    pos = i * c + lax.broadcasted_iota(jnp.int32, (c, 1), 0)
    src = ext_ref
    for gi, win in enumerate(POOL_WINDOWS):
        gs = slice(gi * POOL_GROUP, (gi + 1) * POOL_GROUP)
        rest = slice(gi * POOL_GROUP, POOL_WIDTH)
        half = win // 2
        dst = sum_ref.at[gi % 2]
        dst[POOL_HEAD:top, rest] = src[POOL_HEAD:top, rest] + src[POOL_HEAD - half:top - half, rest]
        src = dst
        pg = p[:, gs]
        tot = dst[POOL_CARRY:top, gs]
        cnt = jnp.minimum(pos + 1, win).astype(F32)
        pooled = _dot((tot / cnt - pg).astype(BF16), wpool_ref[gi])
        mix_ref[:, w + gi * POOL_GROUP:w + (gi + 1) * POOL_GROUP] = (pooled * pscale_ref[:, gs]).astype(BF16)

    y = DEEPNORM_ALPHA * x + _dot(mix_ref[...], wout_ref[...])
    o_ref[...] = _layer_norm(y, lnw_ref[...], lnb_ref[...])


def _mixer(x, w_in, lb_logits, norm_w, w_pool, pool_scale, w_out, ln_w, ln_b):
    t = x.shape[0]
    c = SEQ_BLOCK
    mall, lvl, _ = _decay_tables()
    full = lambda shape: pl.BlockSpec(shape, lambda i: (0,) * len(shape))
    return pl.pallas_call(
        _mixer_kernel,
        grid=(t // c,),
        in_specs=[
            pl.BlockSpec((c, D_MODEL), lambda i: (i, 0)),
            full((D_MODEL, IN_PROJ_WIDTH)),
            full(lb_logits.shape),
            full((1, HGRN_WIDTH)),
            full((len(POOL_WINDOWS), POOL_GROUP, POOL_GROUP)),
            full((1, POOL_WIDTH)),
            full((D_MODEL, D_MODEL)),
            full((1, D_MODEL)),
            full((1, D_MODEL)),
            full(mall.shape),
            full(lvl.shape),
        ],
        out_specs=pl.BlockSpec((c, D_MODEL), lambda i: (i, 0)),
        out_shape=jax.ShapeDtypeStruct((t, D_MODEL), F32),
        scratch_shapes=[
            pltpu.VMEM((HGRN_HEADS, HEAD_DIM, HEAD_DIM), F32),
            pltpu.VMEM((POOL_CARRY, POOL_WIDTH), F32),
            pltpu.VMEM((POOL_CARRY + c, POOL_WIDTH), F32),
            pltpu.VMEM((c, D_MODEL), BF16),
            pltpu.VMEM((2, POOL_CARRY + c, POOL_WIDTH), F32),
        ],
        compiler_params=pltpu.CompilerParams(dimension_semantics=("arbitrary",), vmem_limit_bytes=VMEM_LIMIT),
        name="mixer",
    )(x, w_in.astype(BF16), lb_logits, norm_w.reshape(1, -1), w_pool.astype(BF16), pool_scale.reshape(1, -1),
      w_out.astype(BF16), ln_w.reshape(1, -1), ln_b.reshape(1, -1), jnp.asarray(mall, BF16), jnp.asarray(lvl))


def _kv_kernel(mem_ref, wk_ref, wv_ref, wq_ref, wo_ref, a_ref, b_ref):
    n_mem = mem_ref.shape[0]
    m = mem_ref[...].astype(BF16)
    k = _dot(m, wk_ref[...]).astype(BF16)
    v = _dot(m, wv_ref[...]).astype(BF16)
    scale = XATTN_HEAD_DIM ** -0.5
    for h in range(XATTN_HEADS):
        hs = slice(h * XATTN_HEAD_DIM, (h + 1) * XATTN_HEAD_DIM)
        ms = slice(h * n_mem, (h + 1) * n_mem)
        a_ref[:, ms] = (_dot_nt(wq_ref[:, hs], k[:, hs]) * scale).astype(BF16)
        b_ref[ms, :] = _dot(v[:, hs], wo_ref[hs, :]).astype(BF16)


def _kv_proj(mem, wk, wv, wq, wo):
    n = mem.shape[0]
    return pl.pallas_call(
        _kv_kernel,
        out_shape=(jax.ShapeDtypeStruct((D_MODEL, XATTN_HEADS * n), BF16),
                   jax.ShapeDtypeStruct((XATTN_HEADS * n, D_MODEL), BF16)),
        compiler_params=pltpu.CompilerParams(vmem_limit_bytes=VMEM_LIMIT),
        name="kv_proj",
    )(mem, wk.astype(BF16), wv.astype(BF16), wq.astype(BF16), wo.astype(BF16))


def _split3(a):
    a0 = a.astype(BF16)
    r1 = a - a0.astype(F32)
    a1 = r1.astype(BF16)
    a2 = (r1 - a1.astype(F32)).astype(BF16)
    return a0, a1, a2


def _xattn_kernel(x_ref, a_ref, b_ref, lnw_ref, lnb_ref, wrt_ref, br_ref, ut_ref,
                  x2_ref, dest_ref, gate_ref, cnt_ref, att_ref, run_ref, *, shift):
    i = pl.program_id(0)
    c = x_ref.shape[0]
    n = N_EXPERTS

    @pl.when(i == 0)
    def _():
        run_ref[...] = jnp.zeros_like(run_ref)

    x = x_ref[...]
    scores = _dot(x.astype(BF16), a_ref[...])
    n_mem = a_ref.shape[1] // XATTN_HEADS
    for h in range(XATTN_HEADS):
        ms = slice(h * n_mem, (h + 1) * n_mem)
        s = scores[:, ms]
        e = jnp.exp(s - jnp.max(s, axis=-1, keepdims=True))
        att_ref[:, ms] = (e / jnp.sum(e, axis=-1, keepdims=True)).astype(BF16)
    y = DEEPNORM_ALPHA * x + _dot(att_ref[...], b_ref[...])
    x2 = _layer_norm(y, lnw_ref[...], lnb_ref[...])
    _store_row_tiles(x2_ref, x2)

    a0, a1, _ = _split3(x2)
    w0, w1, w2 = _split3(wrt_ref[...])
    pa = _dot_nt(jnp.concatenate([w0, w1, w2], axis=0), a0)
    pb = _dot_nt(jnp.concatenate([w0, w1], axis=0), a1)
    logits = ((pa[2 * n:3 * n] + pb[n:2 * n]) + (pb[0:n] + pa[n:2 * n]) + pa[0:n]) + br_ref[...]

    eid = lax.broadcasted_iota(jnp.int32, (n, c), 0)
    work = logits
    sels, vals, idxs = [], [], []
    for k in range(TOP_K):
        m = jnp.max(work, axis=0, keepdims=True)
        idx = jnp.min(jnp.where(work == m, eid, n), axis=0, keepdims=True)
        sel = eid == idx
        sels.append(sel)
        vals.append(m)
        idxs.append(idx)
        work = jnp.where(sel, -jnp.inf, work)
    es = [jnp.exp(vk - vals[0]) for vk in vals]
    denom = es[0] + es[1] + es[2] + es[3]
    gate_ref[0] = jnp.concatenate([ek / denom for ek in es], axis=0)

    onehot = jnp.zeros((n, c), F32)
    for sel in sels:
        onehot = jnp.where(sel, 1.0, onehot)
    before = _dot(onehot.astype(BF16), ut_ref[...]) + run_ref[...]
    ranks = [jnp.sum(jnp.where(sel, before, 0.0), axis=0, keepdims=True).astype(jnp.int32) for sel in sels]
    run_ref[...] = run_ref[...] + jnp.sum(onehot, axis=1, keepdims=True)

    dest_ref[0] = jnp.concatenate([idx * (1 << shift) + rk for idx, rk in zip(idxs, ranks)], axis=0)
    cnt_ref[...] = run_ref[...].astype(jnp.int32)


def _xattn_router(x1, score_w, out_w, ln_w, ln_b, w_r, b_r):
    t = x1.shape[0]
    c = XATTN_BLOCK if t % XATTN_BLOCK == 0 else SEQ_BLOCK
    idx = np.arange(c)
    strict_upper = (idx[:, None] < idx[None, :]).astype(np.float32)
    full = lambda shape: pl.BlockSpec(shape, lambda i: (0,) * len(shape))
    per_k = pl.BlockSpec((1, TOP_K, c), lambda i: (i, 0, 0))
    return pl.pallas_call(
        functools.partial(_xattn_kernel, shift=_rank_bits(t)),
        grid=(t // c,),
        in_specs=[
            pl.BlockSpec((c, D_MODEL), lambda i: (i, 0)),
            full(score_w.shape),
            full(out_w.shape),
            full((1, D_MODEL)),
            full((1, D_MODEL)),
            full((N_EXPERTS, D_MODEL)),
            full((N_EXPERTS, 1)),
            full((c, c)),
        ],
        out_specs=[_tiled_spec(c, lambda i: (i, 0)), per_k, per_k, full((N_EXPERTS, 1))],
        out_shape=[
            jax.ShapeDtypeStruct(_tiled(t), F32),
            jax.ShapeDtypeStruct((t // c, TOP_K, c), jnp.int32),
            jax.ShapeDtypeStruct((t // c, TOP_K, c), F32),
            jax.ShapeDtypeStruct((N_EXPERTS, 1), jnp.int32),
        ],
        scratch_shapes=[pltpu.VMEM((c, score_w.shape[1]), BF16), pltpu.VMEM((N_EXPERTS, 1), F32)],
        compiler_params=pltpu.CompilerParams(dimension_semantics=("arbitrary",), vmem_limit_bytes=VMEM_LIMIT),
        name="xattn_router",
    )(x1, score_w, out_w, ln_w.reshape(1, -1), ln_b.reshape(1, -1),
      w_r.T, b_r.reshape(-1, 1), jnp.asarray(strict_upper, BF16))


TAIL_RUNS = (128, 64, 32, 16, 8, 4, 2, 1)


def _dispatch_kernel(fill_ref, pend_ref, nused_ref, slot_ref, x_ref, xs_hbm, zero_ref, sem, zsem, *, n_blocks):
    i = pl.program_id(0)
    c = SEQ_BLOCK

    @pl.when(i == 0)
    def _():
        zero_ref[...] = jnp.zeros_like(zero_ref)

        sub = ROW_TILE[0]

        def rows(ref, first, n):
            return ref.at[pl.ds(pl.multiple_of(first * sub, sub), n * sub)]

        def tail(e, act):
            row = fill_ref[e]
            pad = pend_ref[e] - row
            for run in TAIL_RUNS:
                hit = (pad & run) != 0

                @pl.when(hit)
                def _():
                    act(pltpu.make_async_copy(rows(zero_ref, 0, run), rows(xs_hbm, row, run), zsem))
                row = row + jnp.where(hit, run, 0)

        def spare(b, act):
            act(pltpu.make_async_copy(zero_ref, rows(xs_hbm, b * MOE_BLOCK, MOE_BLOCK), zsem))

        for act in (lambda cp: cp.start(), lambda cp: cp.wait()):
            lax.fori_loop(0, N_EXPERTS, lambda e, carry: (tail(e, act), carry)[1], 0)
            lax.fori_loop(nused_ref[0], n_blocks, lambda b, carry: (spare(b, act), carry)[1], 0)

    for q in range(x_ref.shape[0] // (c * ROW_TILE[0])):
        def issue(j, carry, q=q):
            for u in range(ISSUE_UNROLL):
                r = j * ISSUE_UNROLL + u
                for k in range(TOP_K):
                    dst = _tile_of(xs_hbm, slot_ref[0, 0, (q * TOP_K + k) * c + r])
                    pltpu.make_async_copy(_tile_of(x_ref, q * c + r), dst, sem).start(priority=k % 2)
            return carry
        lax.fori_loop(0, c // ISSUE_UNROLL, issue, 0)
    for k in range(TOP_K):
        pltpu.make_async_copy(x_ref, xs_hbm.at[pl.ds(0, x_ref.shape[0])], sem).wait()


def _dispatch(x2, slots, fill, pend, n_used, n_blocks):
    t = x2.shape[0] // ROW_TILE[0]
    c = min(DISPATCH_BLOCK, t)
    assert t % c == 0 and c % SEQ_BLOCK == 0
    grid_spec = pltpu.PrefetchScalarGridSpec(
        num_scalar_prefetch=3,
        grid=(t // c,),
        in_specs=[
            pl.BlockSpec((1, 1, c * TOP_K), lambda i, *_: (i, 0, 0), memory_space=pltpu.SMEM),
            _tiled_spec(c, lambda i, *_: (i, 0)),
        ],
        out_specs=pl.BlockSpec(memory_space=pl.ANY),
        scratch_shapes=[pltpu.VMEM(_tiled(MOE_BLOCK), F32), pltpu.SemaphoreType.DMA, pltpu.SemaphoreType.DMA],
    )
    return pl.pallas_call(
        functools.partial(_dispatch_kernel, n_blocks=n_blocks),
        grid_spec=grid_spec,
        out_shape=jax.ShapeDtypeStruct(_tiled(n_blocks * MOE_BLOCK), F32),
        compiler_params=pltpu.CompilerParams(dimension_semantics=("arbitrary",), vmem_limit_bytes=VMEM_LIMIT),
        name="dispatch",
    )(fill, pend, n_used, slots.reshape(t // c, 1, c * TOP_K), x2)


def _expert_kernel(be_ref, first_ref, nxt_ref, par_ref, half_ref, nused_ref, x_ref, w1_hbm, b1_ref, w2_hbm, b2_ref, o_ref,
                   w1f, w2f, w1b, w2b, sem1, sem2):
    step = pl.program_id(0)
    rows = MOE_BLOCK * ROW_TILE[0]

    def fetch(e, p):
        return (pltpu.make_async_copy(w1_hbm.at[e], w1f.at[p], sem1.at[p]),
                pltpu.make_async_copy(w2_hbm.at[e], w2f.at[p], sem2.at[p]))

    @pl.when(step == 0)
    def _():
        for cp in fetch(be_ref[0], par_ref[0]):
            cp.start()

    def block(sb, carry):
        b = step * EXPERT_STEP_BLOCKS + sb
        x_blk = x_ref.at[pl.ds(pl.multiple_of(sb * rows, rows), rows)]
        o_blk = o_ref.at[pl.ds(pl.multiple_of(sb * rows, rows), rows)]

        @pl.when(b >= nused_ref[0])
        def _():
            o_blk[...] = jnp.zeros(o_blk.shape, o_blk.dtype)

        @pl.when(b < nused_ref[0])
        def _():
            e = be_ref[b]

            @pl.when(first_ref[b] == 1)
            def _():
                p = par_ref[b]
                for cp in fetch(e, p):
                    cp.wait()

                @pl.when(nxt_ref[b] >= 0)
                def _():
                    for cp in fetch(nxt_ref[b], 1 - p):
                        cp.start()

                w1b[...] = w1f[p].astype(BF16)
                w2b[...] = w2f[p].astype(BF16)

            def mlp(x_rows, o_rows):
                h = _dot(_load_row_tiles(x_rows).astype(BF16), w1b[...]) + b1_ref[e]
                h_glu = jnp.minimum(h[:, :D_EXPERT], SWIGLU_LIMIT)
                h_lin = jnp.clip(h[:, D_EXPERT:], -SWIGLU_LIMIT, SWIGLU_LIMIT)
                act = h_glu * (1.0 / (1.0 + jnp.exp(-SWIGLU_ALPHA * h_glu))) * (h_lin + 1.0)
                _store_row_tiles(o_rows, _dot(act.astype(BF16), w2b[...]) + b2_ref[e])

            @pl.when(half_ref[b] == 0)
            def _():
                mlp(x_blk, o_blk)

            @pl.when(half_ref[b] == 1)
            def _():
                head = pl.ds(0, rows // 2)
                mlp(x_blk.at[head], o_blk.at[head])
                o_blk[pl.ds(rows // 2, rows // 2), :] = jnp.zeros((rows // 2, ROW_TILE[1]), F32)
        return carry

    lax.fori_loop(0, EXPERT_STEP_BLOCKS, block, 0)


def _experts(xs, plan, w1, b1, w2, b2):
    block_e, first, nxt, par, half, n_used = plan
    n_blocks = block_e.shape[0]
    assert n_blocks % EXPERT_STEP_BLOCKS == 0
    rows = lambda s, *_: (s, 0)
    whole = lambda s, *_: (0, 0, 0)
    grid_spec = pltpu.PrefetchScalarGridSpec(
        num_scalar_prefetch=6,
        grid=(n_blocks // EXPERT_STEP_BLOCKS,),
        in_specs=[
            _tiled_spec(EXPERT_STEP_BLOCKS * MOE_BLOCK, rows),
            pl.BlockSpec(memory_space=pl.ANY),
            pl.BlockSpec((N_EXPERTS, 1, 2 * D_EXPERT), whole),
            pl.BlockSpec(memory_space=pl.ANY),
            pl.BlockSpec((N_EXPERTS, 1, D_MODEL), whole),
        ],
        out_specs=_tiled_spec(EXPERT_STEP_BLOCKS * MOE_BLOCK, rows),
        scratch_shapes=[
            pltpu.VMEM((2, D_MODEL, 2 * D_EXPERT), F32),
            pltpu.VMEM((2, D_EXPERT, D_MODEL), F32),
            pltpu.VMEM((D_MODEL, 2 * D_EXPERT), BF16),
            pltpu.VMEM((D_EXPERT, D_MODEL), BF16),
            pltpu.SemaphoreType.DMA((2,)),
            pltpu.SemaphoreType.DMA((2,)),
        ],
    )
    return pl.pallas_call(
        _expert_kernel,
        grid_spec=grid_spec,
        out_shape=jax.ShapeDtypeStruct(xs.shape, F32),
        compiler_params=pltpu.CompilerParams(dimension_semantics=("arbitrary",), vmem_limit_bytes=EXPERT_VMEM_LIMIT),
        name="experts",
    )(block_e, first, nxt, par, half, n_used, xs, w1, b1.reshape(N_EXPERTS, 1, -1), w2, b2.reshape(N_EXPERTS, 1, -1))


def _combine_kernel(slot_ref, next_slot_ref, ys_hbm, gate_ref, x_ref, lnw_ref, lnb_ref, o_ref, ybuf, sem):
    i = pl.program_id(0)
    c = SEQ_BLOCK

    def gather(slots, p):
        def issue(j, carry):
            for u in range(GATHER_UNROLL):
                r = j * GATHER_UNROLL + u
                for k in range(TOP_K):
                    src = _tile_of(ys_hbm, slots[0, 0, k * c + r])
                    pltpu.make_async_copy(src, _tile_of(ybuf.at[p, k], r), sem.at[p]).start(priority=k % 2)
            return carry
        lax.fori_loop(0, c // GATHER_UNROLL, issue, 0)

    @pl.when(i == 0)
    def _():
        gather(slot_ref, 0)

    @pl.when(i + 1 < pl.num_programs(0))
    def _():
        gather(next_slot_ref, (i + 1) % 2)

    p = i % 2
    for k in range(TOP_K):
        pltpu.make_async_copy(ys_hbm.at[pl.ds(0, ybuf.shape[2])], ybuf.at[p, k], sem.at[p]).wait()
    gates = gate_ref[...]
    ys = [_load_row_tiles(ybuf.at[p, k]) * gates[:, k:k + 1] for k in range(TOP_K)]
    y = (ys[0] + ys[1]) + (ys[2] + ys[3])
    o_ref[...] = _layer_norm(DEEPNORM_ALPHA * _load_row_tiles(x_ref) + y, lnw_ref[...], lnb_ref[...])


def _combine(ys, slots, gates, x2, ln_w, ln_b):
    t = x2.shape[0] // ROW_TILE[0]
    c = SEQ_BLOCK
    n = t // c
    slot_table = slots.reshape(n, 1, c * TOP_K)
    return pl.pallas_call(
        _combine_kernel,
        grid=(n,),
        in_specs=[
            pl.BlockSpec((1, 1, c * TOP_K), lambda i: (i, 0, 0), memory_space=pltpu.SMEM),
            pl.BlockSpec((1, 1, c * TOP_K), lambda i: (jnp.minimum(i + 1, n - 1), 0, 0), memory_space=pltpu.SMEM),
            pl.BlockSpec(memory_space=pl.ANY),
            pl.BlockSpec((c, TOP_K), lambda i: (i, 0)),
            _tiled_spec(c, lambda i: (i, 0)),
            pl.BlockSpec((1, D_MODEL), lambda i: (0, 0)),
            pl.BlockSpec((1, D_MODEL), lambda i: (0, 0)),
        ],
        out_specs=pl.BlockSpec((c, D_MODEL), lambda i: (i, 0)),
        out_shape=jax.ShapeDtypeStruct((t, D_MODEL), F32),
        scratch_shapes=[pltpu.VMEM((2, TOP_K) + _tiled(c), F32), pltpu.SemaphoreType.DMA((2,))],
        compiler_params=pltpu.CompilerParams(dimension_semantics=("arbitrary",), vmem_limit_bytes=VMEM_LIMIT),
        name="combine",
    )(slot_table, slot_table, ys, gates, x2, ln_w.reshape(1, -1), ln_b.reshape(1, -1))


def _slot_tables(counts, packed, n_blocks, shift):
    counts = counts.reshape(-1)
    nblk = (counts + MOE_BLOCK - 1) // MOE_BLOCK
    ids = jnp.arange(N_EXPERTS)
    lower = ids[None, :] <= ids[:, None]
    bend = jnp.sum(jnp.where(lower, nblk[None, :], 0), axis=1)
    n_used = bend[N_EXPERTS - 1]
    pend = bend * MOE_BLOCK
    pstart = pend - nblk * MOE_BLOCK
    fill = pstart + counts
    order = jnp.sum(jnp.where(lower, (nblk > 0)[None, :], False), axis=1) - 1
    later = (ids[None, :] > ids[:, None]) & (nblk > 0)[None, :]
    nxt_e = jnp.min(jnp.where(later, ids[None, :], N_EXPERTS), axis=1)
    nxt_e = jnp.where(nxt_e == N_EXPERTS, -1, nxt_e)
    b = jnp.minimum(jnp.arange(n_blocks), n_used - 1)
    hit = bend[None, :] <= b[:, None]
    block_e = jnp.sum(hit.astype(jnp.int32), axis=1)
    of_block = lambda per_expert: jnp.sum(jnp.where(ids[None, :] == block_e[:, None], per_expert[None, :], 0), axis=1)
    first = (b == of_block(bend - nblk)).astype(jnp.int32)
    expert = lax.shift_right_logical(packed, shift)
    rank = packed & ((1 << shift) - 1)
    slots = jnp.sum(jnp.where(expert[..., None] == ids, pstart, 0), axis=-1) + rank
    i32 = lambda a: a.astype(jnp.int32)
    assigned = of_block(counts) - (b - of_block(bend - nblk)) * MOE_BLOCK
    half = (assigned <= MOE_BLOCK // 2).astype(jnp.int32)
    plan = (i32(block_e), first, i32(of_block(nxt_e)), i32(of_block(order & 1)), half, i32(n_used.reshape(1)))
    return i32(slots), i32(fill), i32(pend), plan


def kernel(x, mem, w_in, lb_logits, hgrn_norm_w, w_pool, pool_scale, w_out, ln1_w, ln1_b, w_xq, w_xk, w_xv, w_xo,
           ln2_w, ln2_b, w_router, b_router, w1, b1, w2, b2, ln3_w, ln3_b):
    bsz, seq, d = x.shape
    assert bsz == 1 and d == D_MODEL and seq % SEQ_BLOCK == 0 and w_in.shape[0] == 1
    xt = x.reshape(seq, d)
    x1 = _mixer(xt, w_in[0], lb_logits, hgrn_norm_w[0], w_pool[0], pool_scale[0], w_out[0], ln1_w[0], ln1_b[0])
    score_w, out_w = _kv_proj(mem[0], w_xk[0], w_xv[0], w_xq[0], w_xo[0])
    x2, packed, gates, counts = _xattn_router(x1, score_w, out_w, ln2_w[0], ln2_b[0], w_router[0], b_router[0])
    n_blocks = -(-(seq * TOP_K + N_EXPERTS * (MOE_BLOCK - 1)) // MOE_BLOCK)
    slots, fill, pend, plan = _slot_tables(counts, packed, n_blocks, _rank_bits(seq))
    nb, _, cx = slots.shape
    slots = slots.reshape(nb, TOP_K, cx // SEQ_BLOCK, SEQ_BLOCK).transpose(0, 2, 1, 3).reshape(-1, TOP_K, SEQ_BLOCK)
    xs = _dispatch(x2, slots, fill, pend, plan[-1], n_blocks)
    ys = _experts(xs, plan, w1[0], b1[0], w2[0], b2[0])
    gates_tk = gates.transpose(0, 2, 1).reshape(seq, TOP_K)
    out = _combine(ys, slots, gates_tk, x2, ln3_w[0], ln3_b[0])
    return out.reshape(bsz, seq, d)
```

```python
import functools

import numpy as np
import jax
import jax.numpy as jnp
from jax import lax
from jax.experimental import pallas as pl
from jax.experimental.pallas import tpu as pltpu

F32 = jnp.float32
BF16 = jnp.bfloat16

D_MODEL = 1024
HGRN_WIDTH = 512
HGRN_HEADS = 4
HEAD_DIM = 128
POOL_WINDOWS = (2, 4, 8, 16)
POOL_GROUP = 128
POOL_WIDTH = POOL_GROUP * len(POOL_WINDOWS)
POOL_CARRY = 24
POOL_HEAD = 8
IN_PROJ_WIDTH = 4 * HGRN_WIDTH + POOL_WIDTH
XATTN_HEADS = 4
XATTN_HEAD_DIM = 256
N_EXPERTS = 32
TOP_K = 4
D_EXPERT = 1024
SWIGLU_LIMIT = 7.0
SWIGLU_ALPHA = 1.702
MOE_BLOCK = 512
DEEPNORM_ALPHA = 2.0 ** 0.25
LN_EPS = 1e-5
RMS_EPS = 1e-6

SEQ_BLOCK = 256
EXPERT_STEP_BLOCKS = 2
XATTN_BLOCK = 1024
DISPATCH_BLOCK = 2048
N_LEVELS = 8
MXU_LEVELS = 3
ISSUE_UNROLL = 8
GATHER_UNROLL = 16
ROW_TILE = (8, 128)
VMEM_LIMIT = 48 * 1024 * 1024
EXPERT_VMEM_LIMIT = 56 * 1024 * 1024


def _rank_bits(t):
    return max(int(t - 1).bit_length(), 1)


def _layer_norm(y, w, b):
    mu = jnp.mean(y, axis=-1, keepdims=True)
    yc = y - mu
    var = jnp.mean(yc * yc, axis=-1, keepdims=True)
    return yc * lax.rsqrt(var + LN_EPS) * w + b


def _tiled(n):
    return (n * ROW_TILE[0], ROW_TILE[1])


def _tiled_spec(n, index_map):
    return pl.BlockSpec(_tiled(n), index_map)


def _tile_of(ref, r):
    return ref.at[pl.ds(pl.multiple_of(r * ROW_TILE[0], ROW_TILE[0]), ROW_TILE[0])]


def _store_row_tiles(ref, val):
    s, l = ROW_TILE
    for j in range(s):
        ref[pl.ds(j, val.shape[0], stride=s), :] = val[:, j * l:(j + 1) * l]


def _load_row_tiles(ref):
    s, _ = ROW_TILE
    return jnp.concatenate([ref[pl.ds(j, ref.shape[0] // s, stride=s), :] for j in range(s)], axis=-1)


def _dot(a, b):
    return jnp.dot(a, b, preferred_element_type=F32)


def _dot_nt(a, b):
    return lax.dot_general(a, b, (((1,), (1,)), ((), ())), preferred_element_type=F32)


def _decay_tables():
    c = SEQ_BLOCK
    r = np.arange(c)[:, None]
    u = np.arange(c)[None, :]
    mats = [(u <= r)]
    for l in range(MXU_LEVELS):
        h = 1 << l
        mid = (r // (2 * h)) * (2 * h) + h
        upper = (r >= mid) & (u >= mid) & (u <= r)
        lower = (r < mid) & (u > r) & (u < mid)
        mats.append(upper | lower)
    mall = np.concatenate(mats, axis=0).astype(np.float32)
    x = r ^ u
    lvl = np.where(u < r, np.floor(np.log2(np.maximum(x, 1))).astype(np.int32), -1).astype(np.int32)
    strict_lower = (u < r).astype(np.float32)
    return mall, lvl, strict_lower


def _upper_chunks(l, c):
    half, sub = 1 << l, ROW_TILE[0]
    return [r // sub for base in range(0, c, 2 * half) for r in range(base + half, base + 2 * half, sub)]


def _level_gap(negb, l):
    half = 1 << l
    parts = []
    for base in range(0, negb.shape[0], 2 * half):
        ref_row = negb[base + half - 1:base + half, :]
        parts.append(jnp.abs(negb[base:base + 2 * half, :] - ref_row))
    return jnp.concatenate(parts, axis=0) if len(parts) > 1 else parts[0]


def _mixer_kernel(x_ref, win_ref, lbl_ref, nw_ref, wpool_ref, pscale_ref, wout_ref, lnw_ref, lnb_ref,
                  mall_ref, lvl_ref, o_ref, state_ref, carry_ref, ext_ref, mix_ref, sum_ref):
    i = pl.program_id(0)
    c = SEQ_BLOCK

    @pl.when(i == 0)
    def _():
        state_ref[...] = jnp.zeros_like(state_ref)
        carry_ref[...] = jnp.zeros_like(carry_ref)
        sum_ref[:, 0:POOL_HEAD, :] = jnp.zeros((2, POOL_HEAD, POOL_WIDTH), F32)

    x = x_ref[...]
    proj = _dot(x.astype(BF16), win_ref[...])
    w = HGRN_WIDTH
    q = proj[:, 0:w]
    z = proj[:, w:2 * w]
    v = proj[:, 2 * w:3 * w]
    g = proj[:, 3 * w:4 * w]
    p = proj[:, 4 * w:]

    ll = lbl_ref[...]
    ee = jnp.exp(ll - jnp.max(ll, axis=0, keepdims=True))
    lb = ee[0:1] / jnp.sum(ee, axis=0, keepdims=True)

    f = lb + (1.0 - lb) * (1.0 / (1.0 + jnp.exp(-z)))
    nlf = -jnp.log(f)
    key = (1.0 - lb) * (1.0 / (1.0 + jnp.exp(z)))

    hi = nlf.astype(BF16)
    lo = (nlf - hi.astype(F32)).astype(BF16)
    mall = mall_ref[...]
    cum = _dot(mall, hi) + _dot(mall, lo)
    sub = ROW_TILE[0]
    lvl_rows = [lvl_ref[sub * r:sub * (r + 1), :] for r in range(c // sub)]

    for h in range(HGRN_HEADS):
        hs = slice(h * HEAD_DIM, (h + 1) * HEAD_DIM)
        qh, kh, vh, gh = q[:, hs], key[:, hs], v[:, hs], g[:, hs]
        vb = vh.astype(BF16)
        negb = cum[0:c, hs]
        sc = [jnp.zeros((sub, c), F32)] * (c // sub)
        for l in range(N_LEVELS):
            gap = cum[(l + 1) * c:(l + 2) * c, hs] if l < MXU_LEVELS else _level_gap(negb, l)
            e = jnp.exp(-gap)
            ke = (kh * e).astype(BF16)
            if l < MXU_LEVELS:
                rows = list(range(c // sub))
                qe = (qh * e).astype(BF16)
            else:
                rows = _upper_chunks(l, c)
                take = lambda a: jnp.concatenate([a[sub * r:sub * (r + 1)] for r in rows], axis=0)
                qe = (take(qh) * take(e)).astype(BF16)
            pl_ = _dot_nt(qe, ke)
            for j, r in enumerate(rows):
                sc[r] = jnp.where(lvl_rows[r] == l, pl_[sub * j:sub * (j + 1)], sc[r])
        scores = jnp.concatenate(sc, axis=0)
        diag = jnp.sum(qh * kh, axis=-1, keepdims=True)
        st = state_ref[h]
        o = (_dot(scores.astype(BF16), vb) + diag * vh
             + _dot_nt((qh * jnp.exp(-negb)).astype(BF16), st.astype(BF16)))
        negb_last = negb[c - 1:c, :]
        kd = (kh * jnp.exp(negb - negb_last)).astype(BF16)
        state_ref[h] = st * jnp.exp(-negb_last) + _dot(vh.T.astype(BF16), kd)
        o = o * lax.rsqrt(jnp.mean(o * o, axis=-1, keepdims=True) + RMS_EPS) * nw_ref[:, hs]
        gate = gh * (1.0 / (1.0 + jnp.exp(-gh)))
        mix_ref[:, hs] = (o * gate).astype(BF16)

    top = POOL_CARRY + c
    ext_ref[0:POOL_CARRY, :] = carry_ref[...]
    ext_ref[POOL_CARRY:top, :] = p
    carry_ref[...] = p[c - POOL_CARRY:c, :]
    pos = i * c + lax.broadcasted_iota(jnp.int32, (c, 1), 0)
    src = ext_ref
    for gi, win in enumerate(POOL_WINDOWS):
        gs = slice(gi * POOL_GROUP, (gi + 1) * POOL_GROUP)
        rest = slice(gi * POOL_GROUP, POOL_WIDTH)
        half = win // 2
        dst = sum_ref.at[gi % 2]
        dst[POOL_HEAD:top, rest] = src[POOL_HEAD:top, rest] + src[POOL_HEAD - half:top - half, rest]
        src = dst
        pg = p[:, gs]
        tot = dst[POOL_CARRY:top, gs]
        cnt = jnp.minimum(pos + 1, win).astype(F32)
        pooled = _dot((tot / cnt - pg).astype(BF16), wpool_ref[gi])
        mix_ref[:, w + gi * POOL_GROUP:w + (gi + 1) * POOL_GROUP] = (pooled * pscale_ref[:, gs]).astype(BF16)

    y = DEEPNORM_ALPHA * x + _dot(mix_ref[...], wout_ref[...])
    o_ref[...] = _layer_norm(y, lnw_ref[...], lnb_ref[...])


def _mixer(x, w_in, lb_logits, norm_w, w_pool, pool_scale, w_out, ln_w, ln_b):
    t = x.shape[0]
    c = SEQ_BLOCK
    mall, lvl, _ = _decay_tables()
    full = lambda shape: pl.BlockSpec(shape, lambda i: (0,) * len(shape))
    return pl.pallas_call(
        _mixer_kernel,
        grid=(t // c,),
        in_specs=[
            pl.BlockSpec((c, D_MODEL), lambda i: (i, 0)),
            full((D_MODEL, IN_PROJ_WIDTH)),
            full(lb_logits.shape),
            full((1, HGRN_WIDTH)),
            full((len(POOL_WINDOWS), POOL_GROUP, POOL_GROUP)),
            full((1, POOL_WIDTH)),
            full((D_MODEL, D_MODEL)),
            full((1, D_MODEL)),
            full((1, D_MODEL)),
            full(mall.shape),
            full(lvl.shape),
        ],
        out_specs=pl.BlockSpec((c, D_MODEL), lambda i: (i, 0)),
        out_shape=jax.ShapeDtypeStruct((t, D_MODEL), F32),
        scratch_shapes=[
            pltpu.VMEM((HGRN_HEADS, HEAD_DIM, HEAD_DIM), F32),
            pltpu.VMEM((POOL_CARRY, POOL_WIDTH), F32),
            pltpu.VMEM((POOL_CARRY + c, POOL_WIDTH), F32),
            pltpu.VMEM((c, D_MODEL), BF16),
            pltpu.VMEM((2, POOL_CARRY + c, POOL_WIDTH), F32),
        ],
        compiler_params=pltpu.CompilerParams(dimension_semantics=("arbitrary",), vmem_limit_bytes=VMEM_LIMIT),
        name="mixer",
    )(x, w_in.astype(BF16), lb_logits, norm_w.reshape(1, -1), w_pool.astype(BF16), pool_scale.reshape(1, -1),
      w_out.astype(BF16), ln_w.reshape(1, -1), ln_b.reshape(1, -1), jnp.asarray(mall, BF16), jnp.asarray(lvl))


def _kv_kernel(mem_ref, wk_ref, wv_ref, wq_ref, wo_ref, a_ref, b_ref):
    n_mem = mem_ref.shape[0]
    m = mem_ref[...].astype(BF16)
    k = _dot(m, wk_ref[...]).astype(BF16)
    v = _dot(m, wv_ref[...]).astype(BF16)
    scale = XATTN_HEAD_DIM ** -0.5
    for h in range(XATTN_HEADS):
        hs = slice(h * XATTN_HEAD_DIM, (h + 1) * XATTN_HEAD_DIM)
        ms = slice(h * n_mem, (h + 1) * n_mem)
        a_ref[:, ms] = (_dot_nt(wq_ref[:, hs], k[:, hs]) * scale).astype(BF16)
        b_ref[ms, :] = _dot(v[:, hs], wo_ref[hs, :]).astype(BF16)


def _kv_proj(mem, wk, wv, wq, wo):
    n = mem.shape[0]
    return pl.pallas_call(
        _kv_kernel,
        out_shape=(jax.ShapeDtypeStruct((D_MODEL, XATTN_HEADS * n), BF16),
                   jax.ShapeDtypeStruct((XATTN_HEADS * n, D_MODEL), BF16)),
        compiler_params=pltpu.CompilerParams(vmem_limit_bytes=VMEM_LIMIT),
        name="kv_proj",
    )(mem, wk.astype(BF16), wv.astype(BF16), wq.astype(BF16), wo.astype(BF16))


def _split3(a):
    a0 = a.astype(BF16)
    r1 = a - a0.astype(F32)
    a1 = r1.astype(BF16)
    a2 = (r1 - a1.astype(F32)).astype(BF16)
    return a0, a1, a2


def _xattn_kernel(x_ref, a_ref, b_ref, lnw_ref, lnb_ref, wrt_ref, br_ref, ut_ref,
                  x2_ref, dest_ref, gate_ref, cnt_ref, att_ref, run_ref, *, shift):
    i = pl.program_id(0)
    c = x_ref.shape[0]
    n = N_EXPERTS

    @pl.when(i == 0)
    def _():
        run_ref[...] = jnp.zeros_like(run_ref)

    x = x_ref[...]
    scores = _dot(x.astype(BF16), a_ref[...])
    n_mem = a_ref.shape[1] // XATTN_HEADS
    for h in range(XATTN_HEADS):
        ms = slice(h * n_mem, (h + 1) * n_mem)
        s = scores[:, ms]
        e = jnp.exp(s - jnp.max(s, axis=-1, keepdims=True))
        att_ref[:, ms] = (e / jnp.sum(e, axis=-1, keepdims=True)).astype(BF16)
    y = DEEPNORM_ALPHA * x + _dot(att_ref[...], b_ref[...])
    x2 = _layer_norm(y, lnw_ref[...], lnb_ref[...])
    _store_row_tiles(x2_ref, x2)

    a0, a1, _ = _split3(x2)
    w0, w1, w2 = _split3(wrt_ref[...])
    pa = _dot_nt(jnp.concatenate([w0, w1, w2], axis=0), a0)
    pb = _dot_nt(jnp.concatenate([w0, w1], axis=0), a1)
    logits = ((pa[2 * n:3 * n] + pb[n:2 * n]) + (pb[0:n] + pa[n:2 * n]) + pa[0:n]) + br_ref[...]

    eid = lax.broadcasted_iota(jnp.int32, (n, c), 0)
    work = logits
    sels, vals, idxs = [], [], []
    for k in range(TOP_K):
        m = jnp.max(work, axis=0, keepdims=True)
        idx = jnp.min(jnp.where(work == m, eid, n), axis=0, keepdims=True)
        sel = eid == idx
        sels.append(sel)
        vals.append(m)
        idxs.append(idx)
        work = jnp.where(sel, -jnp.inf, work)
    es = [jnp.exp(vk - vals[0]) for vk in vals]
    denom = es[0] + es[1] + es[2] + es[3]
    gate_ref[0] = jnp.concatenate([ek / denom for ek in es], axis=0)

    onehot = jnp.zeros((n, c), F32)
    for sel in sels:
        onehot = jnp.where(sel, 1.0, onehot)
    before = _dot(onehot.astype(BF16), ut_ref[...]) + run_ref[...]
    ranks = [jnp.sum(jnp.where(sel, before, 0.0), axis=0, keepdims=True).astype(jnp.int32) for sel in sels]
    run_ref[...] = run_ref[...] + jnp.sum(onehot, axis=1, keepdims=True)

    dest_ref[0] = jnp.concatenate([idx * (1 << shift) + rk for idx, rk in zip(idxs, ranks)], axis=0)
    cnt_ref[...] = run_ref[...].astype(jnp.int32)


def _xattn_router(x1, score_w, out_w, ln_w, ln_b, w_r, b_r):
    t = x1.shape[0]
    c = XATTN_BLOCK if t % XATTN_BLOCK == 0 else SEQ_BLOCK
    idx = np.arange(c)
    strict_upper = (idx[:, None] < idx[None, :]).astype(np.float32)
    full = lambda shape: pl.BlockSpec(shape, lambda i: (0,) * len(shape))
    per_k = pl.BlockSpec((1, TOP_K, c), lambda i: (i, 0, 0))
    return pl.pallas_call(
        functools.partial(_xattn_kernel, shift=_rank_bits(t)),
        grid=(t // c,),
        in_specs=[
            pl.BlockSpec((c, D_MODEL), lambda i: (i, 0)),
            full(score_w.shape),
            full(out_w.shape),
            full((1, D_MODEL)),
            full((1, D_MODEL)),
            full((N_EXPERTS, D_MODEL)),
            full((N_EXPERTS, 1)),
            full((c, c)),
        ],
        out_specs=[_tiled_spec(c, lambda i: (i, 0)), per_k, per_k, full((N_EXPERTS, 1))],
        out_shape=[
            jax.ShapeDtypeStruct(_tiled(t), F32),
            jax.ShapeDtypeStruct((t // c, TOP_K, c), jnp.int32),
            jax.ShapeDtypeStruct((t // c, TOP_K, c), F32),
            jax.ShapeDtypeStruct((N_EXPERTS, 1), jnp.int32),
        ],
        scratch_shapes=[pltpu.VMEM((c, score_w.shape[1]), BF16), pltpu.VMEM((N_EXPERTS, 1), F32)],
        compiler_params=pltpu.CompilerParams(dimension_semantics=("arbitrary",), vmem_limit_bytes=VMEM_LIMIT),
        name="xattn_router",
    )(x1, score_w, out_w, ln_w.reshape(1, -1), ln_b.reshape(1, -1),
      w_r.T, b_r.reshape(-1, 1), jnp.asarray(strict_upper, BF16))


TAIL_RUNS = (256, 128, 64, 32, 16, 8, 4, 2, 1)


def _dispatch_kernel(fill_ref, pend_ref, nused_ref, slot_ref, x_ref, xs_hbm, zero_ref, sem, zsem, *, n_blocks):
    i = pl.program_id(0)
    c = SEQ_BLOCK

    @pl.when(i == 0)
    def _():
        zero_ref[...] = jnp.zeros_like(zero_ref)

        sub = ROW_TILE[0]

        def rows(ref, first, n):
            return ref.at[pl.ds(pl.multiple_of(first * sub, sub), n * sub)]

        def tail(e, act):
            row = fill_ref[e]
            pad = pend_ref[e] - row
            for run in TAIL_RUNS:
                hit = (pad & run) != 0

                @pl.when(hit)
                def _():
                    act(pltpu.make_async_copy(rows(zero_ref, 0, run), rows(xs_hbm, row, run), zsem))
                row = row + jnp.where(hit, run, 0)

        def spare(b, act):
            act(pltpu.make_async_copy(zero_ref, rows(xs_hbm, b * MOE_BLOCK, MOE_BLOCK), zsem))

        for act in (lambda cp: cp.start(), lambda cp: cp.wait()):
            lax.fori_loop(0, N_EXPERTS, lambda e, carry: (tail(e, act), carry)[1], 0)
            lax.fori_loop(nused_ref[0], n_blocks, lambda b, carry: (spare(b, act), carry)[1], 0)

    for q in range(x_ref.shape[0] // (c * ROW_TILE[0])):
        def issue(j, carry, q=q):
            for u in range(ISSUE_UNROLL):
                r = j * ISSUE_UNROLL + u
                for k in range(TOP_K):
                    dst = _tile_of(xs_hbm, slot_ref[0, 0, (q * TOP_K + k) * c + r])
                    pltpu.make_async_copy(_tile_of(x_ref, q * c + r), dst, sem).start(priority=k % 2)
            return carry
        lax.fori_loop(0, c // ISSUE_UNROLL, issue, 0)
    for k in range(TOP_K):
        pltpu.make_async_copy(x_ref, xs_hbm.at[pl.ds(0, x_ref.shape[0])], sem).wait()


def _dispatch(x2, slots, fill, pend, n_used, n_blocks):
    t = x2.shape[0] // ROW_TILE[0]
    c = min(DISPATCH_BLOCK, t)
    assert t % c == 0 and c % SEQ_BLOCK == 0
    grid_spec = pltpu.PrefetchScalarGridSpec(
        num_scalar_prefetch=3,
        grid=(t // c,),
        in_specs=[
            pl.BlockSpec((1, 1, c * TOP_K), lambda i, *_: (i, 0, 0), memory_space=pltpu.SMEM),
            _tiled_spec(c, lambda i, *_: (i, 0)),
        ],
        out_specs=pl.BlockSpec(memory_space=pl.ANY),
        scratch_shapes=[pltpu.VMEM(_tiled(MOE_BLOCK), F32), pltpu.SemaphoreType.DMA, pltpu.SemaphoreType.DMA],
    )
    return pl.pallas_call(
        functools.partial(_dispatch_kernel, n_blocks=n_blocks),
        grid_spec=grid_spec,
        out_shape=jax.ShapeDtypeStruct(_tiled(n_blocks * MOE_BLOCK), F32),
        compiler_params=pltpu.CompilerParams(dimension_semantics=("arbitrary",), vmem_limit_bytes=VMEM_LIMIT),
        name="dispatch",
    )(fill, pend, n_used, slots.reshape(t // c, 1, c * TOP_K), x2)


def _expert_kernel(be_ref, first_ref, nxt_ref, par_ref, nused_ref, x_ref, w1_hbm, b1_ref, w2_hbm, b2_ref, o_ref,
                   w1f, w2f, w1b, w2b, sem1, sem2):
    step = pl.program_id(0)
    rows = MOE_BLOCK * ROW_TILE[0]

    def fetch(e, p):
        return (pltpu.make_async_copy(w1_hbm.at[e], w1f.at[p], sem1.at[p]),
                pltpu.make_async_copy(w2_hbm.at[e], w2f.at[p], sem2.at[p]))

    @pl.when(step == 0)
    def _():
        for cp in fetch(be_ref[0], par_ref[0]):
            cp.start()

    def block(sb, carry):
        b = step * EXPERT_STEP_BLOCKS + sb
        x_blk = x_ref.at[pl.ds(pl.multiple_of(sb * rows, rows), rows)]
        o_blk = o_ref.at[pl.ds(pl.multiple_of(sb * rows, rows), rows)]

        @pl.when(b >= nused_ref[0])
        def _():
            o_blk[...] = jnp.zeros(o_blk.shape, o_blk.dtype)

        @pl.when(b < nused_ref[0])
        def _():
            e = be_ref[b]

            @pl.when(first_ref[b] == 1)
            def _():
                p = par_ref[b]
                for cp in fetch(e, p):
                    cp.wait()

                @pl.when(nxt_ref[b] >= 0)
                def _():
                    for cp in fetch(nxt_ref[b], 1 - p):
                        cp.start()

                w1b[...] = w1f[p].astype(BF16)
                w2b[...] = w2f[p].astype(BF16)

            h = _dot(_load_row_tiles(x_blk).astype(BF16), w1b[...]) + b1_ref[e]
            h_glu = jnp.minimum(h[:, :D_EXPERT], SWIGLU_LIMIT)
            h_lin = jnp.clip(h[:, D_EXPERT:], -SWIGLU_LIMIT, SWIGLU_LIMIT)
            act = h_glu * (1.0 / (1.0 + jnp.exp(-SWIGLU_ALPHA * h_glu))) * (h_lin + 1.0)
            _store_row_tiles(o_blk, _dot(act.astype(BF16), w2b[...]) + b2_ref[e])
        return carry

    lax.fori_loop(0, EXPERT_STEP_BLOCKS, block, 0)


def _experts(xs, plan, w1, b1, w2, b2):
    block_e, first, nxt, par, n_used = plan
    n_blocks = block_e.shape[0]
    assert n_blocks % EXPERT_STEP_BLOCKS == 0
    rows = lambda s, *_: (s, 0)
    whole = lambda s, *_: (0, 0, 0)
    grid_spec = pltpu.PrefetchScalarGridSpec(
        num_scalar_prefetch=5,
        grid=(n_blocks // EXPERT_STEP_BLOCKS,),
        in_specs=[
            _tiled_spec(EXPERT_STEP_BLOCKS * MOE_BLOCK, rows),
            pl.BlockSpec(memory_space=pl.ANY),
            pl.BlockSpec((N_EXPERTS, 1, 2 * D_EXPERT), whole),
            pl.BlockSpec(memory_space=pl.ANY),
            pl.BlockSpec((N_EXPERTS, 1, D_MODEL), whole),
        ],
        out_specs=_tiled_spec(EXPERT_STEP_BLOCKS * MOE_BLOCK, rows),
        scratch_shapes=[
            pltpu.VMEM((2, D_MODEL, 2 * D_EXPERT), F32),
            pltpu.VMEM((2, D_EXPERT, D_MODEL), F32),
            pltpu.VMEM((D_MODEL, 2 * D_EXPERT), BF16),
            pltpu.VMEM((D_EXPERT, D_MODEL), BF16),
            pltpu.SemaphoreType.DMA((2,)),
            pltpu.SemaphoreType.DMA((2,)),
        ],
    )
    return pl.pallas_call(
        _expert_kernel,
        grid_spec=grid_spec,
        out_shape=jax.ShapeDtypeStruct(xs.shape, F32),
        compiler_params=pltpu.CompilerParams(dimension_semantics=("arbitrary",), vmem_limit_bytes=EXPERT_VMEM_LIMIT),
        name="experts",
    )(block_e, first, nxt, par, n_used, xs, w1, b1.reshape(N_EXPERTS, 1, -1), w2, b2.reshape(N_EXPERTS, 1, -1))


def _combine_kernel(slot_ref, next_slot_ref, ys_hbm, gate_ref, x_ref, lnw_ref, lnb_ref, o_ref, ybuf, sem):
    i = pl.program_id(0)
    c = SEQ_BLOCK

    def gather(slots, p):
        def issue(j, carry):
            for u in range(GATHER_UNROLL):
                r = j * GATHER_UNROLL + u
                for k in range(TOP_K):
                    src = _tile_of(ys_hbm, slots[0, 0, k * c + r])
                    pltpu.make_async_copy(src, _tile_of(ybuf.at[p, k], r), sem.at[p]).start(priority=k % 2)
            return carry
        lax.fori_loop(0, c // GATHER_UNROLL, issue, 0)

    @pl.when(i == 0)
    def _():
        gather(slot_ref, 0)

    @pl.when(i + 1 < pl.num_programs(0))
    def _():
        gather(next_slot_ref, (i + 1) % 2)

    p = i % 2
    for k in range(TOP_K):
        pltpu.make_async_copy(ys_hbm.at[pl.ds(0, ybuf.shape[2])], ybuf.at[p, k], sem.at[p]).wait()
    gates = gate_ref[...]
    ys = [_load_row_tiles(ybuf.at[p, k]) * gates[:, k:k + 1] for k in range(TOP_K)]
    y = (ys[0] + ys[1]) + (ys[2] + ys[3])
    o_ref[...] = _layer_norm(DEEPNORM_ALPHA * _load_row_tiles(x_ref) + y, lnw_ref[...], lnb_ref[...])


def _combine(ys, slots, gates, x2, ln_w, ln_b):
    t = x2.shape[0] // ROW_TILE[0]
    c = SEQ_BLOCK
    n = t // c
    slot_table = slots.reshape(n, 1, c * TOP_K)
    return pl.pallas_call(
        _combine_kernel,
        grid=(n,),
        in_specs=[
            pl.BlockSpec((1, 1, c * TOP_K), lambda i: (i, 0, 0), memory_space=pltpu.SMEM),
            pl.BlockSpec((1, 1, c * TOP_K), lambda i: (jnp.minimum(i + 1, n - 1), 0, 0), memory_space=pltpu.SMEM),
            pl.BlockSpec(memory_space=pl.ANY),
            pl.BlockSpec((c, TOP_K), lambda i: (i, 0)),
            _tiled_spec(c, lambda i: (i, 0)),
            pl.BlockSpec((1, D_MODEL), lambda i: (0, 0)),
            pl.BlockSpec((1, D_MODEL), lambda i: (0, 0)),
        ],
        out_specs=pl.BlockSpec((c, D_MODEL), lambda i: (i, 0)),
        out_shape=jax.ShapeDtypeStruct((t, D_MODEL), F32),
        scratch_shapes=[pltpu.VMEM((2, TOP_K) + _tiled(c), F32), pltpu.SemaphoreType.DMA((2,))],
        compiler_params=pltpu.CompilerParams(dimension_semantics=("arbitrary",), vmem_limit_bytes=VMEM_LIMIT),
        name="combine",
    )(slot_table, slot_table, ys, gates, x2, ln_w.reshape(1, -1), ln_b.reshape(1, -1))


def _slot_tables(counts, packed, n_blocks, shift):
    counts = counts.reshape(-1)
    nblk = (counts + MOE_BLOCK - 1) // MOE_BLOCK
    ids = jnp.arange(N_EXPERTS)
    lower = ids[None, :] <= ids[:, None]
    bend = jnp.sum(jnp.where(lower, nblk[None, :], 0), axis=1)
    n_used = bend[N_EXPERTS - 1]
    pend = bend * MOE_BLOCK
    pstart = pend - nblk * MOE_BLOCK
    fill = pstart + counts
    order = jnp.sum(jnp.where(lower, (nblk > 0)[None, :], False), axis=1) - 1
    later = (ids[None, :] > ids[:, None]) & (nblk > 0)[None, :]
    nxt_e = jnp.min(jnp.where(later, ids[None, :], N_EXPERTS), axis=1)
    nxt_e = jnp.where(nxt_e == N_EXPERTS, -1, nxt_e)
    b = jnp.minimum(jnp.arange(n_blocks), n_used - 1)
    hit = bend[None, :] <= b[:, None]
    block_e = jnp.sum(hit.astype(jnp.int32), axis=1)
    of_block = lambda per_expert: jnp.sum(jnp.where(ids[None, :] == block_e[:, None], per_expert[None, :], 0), axis=1)
    first = (b == of_block(bend - nblk)).astype(jnp.int32)
    expert = lax.shift_right_logical(packed, shift)
    rank = packed & ((1 << shift) - 1)
    slots = jnp.sum(jnp.where(expert[..., None] == ids, pstart, 0), axis=-1) + rank
    i32 = lambda a: a.astype(jnp.int32)
    plan = (i32(block_e), first, i32(of_block(nxt_e)), i32(of_block(order & 1)), i32(n_used.reshape(1)))
    return i32(slots), i32(fill), i32(pend), plan


def kernel(x, mem, w_in, lb_logits, hgrn_norm_w, w_pool, pool_scale, w_out, ln1_w, ln1_b, w_xq, w_xk, w_xv, w_xo,
           ln2_w, ln2_b, w_router, b_router, w1, b1, w2, b2, ln3_w, ln3_b):
    bsz, seq, d = x.shape
    assert bsz == 1 and d == D_MODEL and seq % SEQ_BLOCK == 0 and w_in.shape[0] == 1
    xt = x.reshape(seq, d)
    x1 = _mixer(xt, w_in[0], lb_logits, hgrn_norm_w[0], w_pool[0], pool_scale[0], w_out[0], ln1_w[0], ln1_b[0])
    score_w, out_w = _kv_proj(mem[0], w_xk[0], w_xv[0], w_xq[0], w_xo[0])
    x2, packed, gates, counts = _xattn_router(x1, score_w, out_w, ln2_w[0], ln2_b[0], w_router[0], b_router[0])
    n_blocks = -(-(seq * TOP_K + N_EXPERTS * (MOE_BLOCK - 1)) // MOE_BLOCK)
    slots, fill, pend, plan = _slot_tables(counts, packed, n_blocks, _rank_bits(seq))
    nb, _, cx = slots.shape
    slots = slots.reshape(nb, TOP_K, cx // SEQ_BLOCK, SEQ_BLOCK).transpose(0, 2, 1, 3).reshape(-1, TOP_K, SEQ_BLOCK)
    xs = _dispatch(x2, slots, fill, pend, plan[-1], n_blocks)
    ys = _experts(xs, plan, w1[0], b1[0], w2[0], b2[0])
    gates_tk = gates.transpose(0, 2, 1).reshape(seq, TOP_K)
    out = _combine(ys, slots, gates_tk, x2, ln3_w[0], ln3_b[0])
    return out.reshape(bsz, seq, d)
```

```python
import functools

import numpy as np
import jax
import jax.numpy as jnp
from jax import lax
from jax.experimental import pallas as pl
from jax.experimental.pallas import tpu as pltpu

F32 = jnp.float32
BF16 = jnp.bfloat16

D_MODEL = 1024
HGRN_WIDTH = 512
HGRN_HEADS = 4
HEAD_DIM = 128
POOL_WINDOWS = (2, 4, 8, 16)
POOL_GROUP = 128
POOL_WIDTH = POOL_GROUP * len(POOL_WINDOWS)
POOL_CARRY = 24
POOL_HEAD = 8
IN_PROJ_WIDTH = 4 * HGRN_WIDTH + POOL_WIDTH
XATTN_HEADS = 4
XATTN_HEAD_DIM = 256
N_EXPERTS = 32
TOP_K = 4
D_EXPERT = 1024
SWIGLU_LIMIT = 7.0
SWIGLU_ALPHA = 1.702
MOE_BLOCK = 256
DEEPNORM_ALPHA = 2.0 ** 0.25
LN_EPS = 1e-5
RMS_EPS = 1e-6

SEQ_BLOCK = 256
EXPERT_STEP_BLOCKS = 4
XATTN_BLOCK = 1024
DISPATCH_BLOCK = 2048
N_LEVELS = 8
MXU_LEVELS = 3
ISSUE_UNROLL = 8
GATHER_UNROLL = 16
ROW_TILE = (8, 128)
VMEM_LIMIT = 48 * 1024 * 1024
EXPERT_VMEM_LIMIT = 56 * 1024 * 1024


def _rank_bits(t):
    return max(int(t - 1).bit_length(), 1)


def _layer_norm(y, w, b):
    mu = jnp.mean(y, axis=-1, keepdims=True)
    yc = y - mu
    var = jnp.mean(yc * yc, axis=-1, keepdims=True)
    return yc * lax.rsqrt(var + LN_EPS) * w + b


def _tiled(n):
    return (n * ROW_TILE[0], ROW_TILE[1])


def _tiled_spec(n, index_map):
    return pl.BlockSpec(_tiled(n), index_map)


def _tile_of(ref, r):
    return ref.at[pl.ds(pl.multiple_of(r * ROW_TILE[0], ROW_TILE[0]), ROW_TILE[0])]


def _store_row_tiles(ref, val):
    s, l = ROW_TILE
    for j in range(s):
        ref[pl.ds(j, val.shape[0], stride=s), :] = val[:, j * l:(j + 1) * l]


def _load_row_tiles(ref):
    s, _ = ROW_TILE
    return jnp.concatenate([ref[pl.ds(j, ref.shape[0] // s, stride=s), :] for j in range(s)], axis=-1)


def _dot(a, b):
    return jnp.dot(a, b, preferred_element_type=F32)


def _dot_nt(a, b):
    return lax.dot_general(a, b, (((1,), (1,)), ((), ())), preferred_element_type=F32)


def _decay_tables():
    c = SEQ_BLOCK
    r = np.arange(c)[:, None]
    u = np.arange(c)[None, :]
    mats = [(u <= r)]
    for l in range(MXU_LEVELS):
        h = 1 << l
        mid = (r // (2 * h)) * (2 * h) + h
        upper = (r >= mid) & (u >= mid) & (u <= r)
        lower = (r < mid) & (u > r) & (u < mid)
        mats.append(upper | lower)
    mall = np.concatenate(mats, axis=0).astype(np.float32)
    x = r ^ u
    lvl = np.where(u < r, np.floor(np.log2(np.maximum(x, 1))).astype(np.int32), -1).astype(np.int32)
    strict_lower = (u < r).astype(np.float32)
    return mall, lvl, strict_lower


def _upper_chunks(l, c):
    half, sub = 1 << l, ROW_TILE[0]
    return [r // sub for base in range(0, c, 2 * half) for r in range(base + half, base + 2 * half, sub)]


def _level_gap(negb, l):
    half = 1 << l
    parts = []
    for base in range(0, negb.shape[0], 2 * half):
        ref_row = negb[base + half - 1:base + half, :]
        parts.append(jnp.abs(negb[base:base + 2 * half, :] - ref_row))
    return jnp.concatenate(parts, axis=0) if len(parts) > 1 else parts[0]


def _mixer_kernel(x_ref, win_ref, lbl_ref, nw_ref, wpool_ref, pscale_ref, wout_ref, lnw_ref, lnb_ref,
                  mall_ref, lvl_ref, o_ref, state_ref, carry_ref, ext_ref, mix_ref, sum_ref):
    i = pl.program_id(0)
    c = SEQ_BLOCK

    @pl.when(i == 0)
    def _():
        state_ref[...] = jnp.zeros_like(state_ref)
        carry_ref[...] = jnp.zeros_like(carry_ref)
        sum_ref[:, 0:POOL_HEAD, :] = jnp.zeros((2, POOL_HEAD, POOL_WIDTH), F32)

    x = x_ref[...]
    proj = _dot(x.astype(BF16), win_ref[...])
    w = HGRN_WIDTH
    q = proj[:, 0:w]
    z = proj[:, w:2 * w]
    v = proj[:, 2 * w:3 * w]
    g = proj[:, 3 * w:4 * w]
    p = proj[:, 4 * w:]

    ll = lbl_ref[...]
    ee = jnp.exp(ll - jnp.max(ll, axis=0, keepdims=True))
    lb = ee[0:1] / jnp.sum(ee, axis=0, keepdims=True)

    f = lb + (1.0 - lb) * (1.0 / (1.0 + jnp.exp(-z)))
    nlf = -jnp.log(f)
    key = (1.0 - lb) * (1.0 / (1.0 + jnp.exp(z)))

    hi = nlf.astype(BF16)
    lo = (nlf - hi.astype(F32)).astype(BF16)
    mall = mall_ref[...]
    cum = _dot(mall, hi) + _dot(mall, lo)
    sub = ROW_TILE[0]
    lvl_rows = [lvl_ref[sub * r:sub * (r + 1), :] for r in range(c // sub)]

    for h in range(HGRN_HEADS):
        hs = slice(h * HEAD_DIM, (h + 1) * HEAD_DIM)
        qh, kh, vh, gh = q[:, hs], key[:, hs], v[:, hs], g[:, hs]
        vb = vh.astype(BF16)
        negb = cum[0:c, hs]
        sc = [jnp.zeros((sub, c), F32)] * (c // sub)
        for l in range(N_LEVELS):
            gap = cum[(l + 1) * c:(l + 2) * c, hs] if l < MXU_LEVELS else _level_gap(negb, l)
            e = jnp.exp(-gap)
            ke = (kh * e).astype(BF16)
            if l < MXU_LEVELS:
                rows = list(range(c // sub))
                qe = (qh * e).astype(BF16)
            else:
                rows = _upper_chunks(l, c)
                take = lambda a: jnp.concatenate([a[sub * r:sub * (r + 1)] for r in rows], axis=0)
                qe = (take(qh) * take(e)).astype(BF16)
            pl_ = _dot_nt(qe, ke)
            for j, r in enumerate(rows):
                sc[r] = jnp.where(lvl_rows[r] == l, pl_[sub * j:sub * (j + 1)], sc[r])
        scores = jnp.concatenate(sc, axis=0)
        diag = jnp.sum(qh * kh, axis=-1, keepdims=True)
        st = state_ref[h]
        o = (_dot(scores.astype(BF16), vb) + diag * vh
             + _dot_nt((qh * jnp.exp(-negb)).astype(BF16), st.astype(BF16)))
        negb_last = negb[c - 1:c, :]
        kd = (kh * jnp.exp(negb - negb_last)).astype(BF16)
        state_ref[h] = st * jnp.exp(-negb_last) + _dot(vh.T.astype(BF16), kd)
        o = o * lax.rsqrt(jnp.mean(o * o, axis=-1, keepdims=True) + RMS_EPS) * nw_ref[:, hs]
        gate = gh * (1.0 / (1.0 + jnp.exp(-gh)))
        mix_ref[:, hs] = (o * gate).astype(BF16)

    top = POOL_CARRY + c
    ext_ref[0:POOL_CARRY, :] = carry_ref[...]
    ext_ref[POOL_CARRY:top, :] = p
    carry_ref[...] = p[c - POOL_CARRY:c, :]
    pos = i * c + lax.broadcasted_iota(jnp.int32, (c, 1), 0)
    src = ext_ref
    for gi, win in enumerate(POOL_WINDOWS):
        gs = slice(gi * POOL_GROUP, (gi + 1) * POOL_GROUP)
        rest = slice(gi * POOL_GROUP, POOL_WIDTH)
        half = win // 2
        dst = sum_ref.at[gi % 2]
        dst[POOL_HEAD:top, rest] = src[POOL_HEAD:top, rest] + src[POOL_HEAD - half:top - half, rest]
        src = dst
        pg = p[:, gs]
        tot = dst[POOL_CARRY:top, gs]
        cnt = jnp.minimum(pos + 1, win).astype(F32)
        pooled = _dot((tot / cnt - pg).astype(BF16), wpool_ref[gi])
        mix_ref[:, w + gi * POOL_GROUP:w + (gi + 1) * POOL_GROUP] = (pooled * pscale_ref[:, gs]).astype(BF16)

    y = DEEPNORM_ALPHA * x + _dot(mix_ref[...], wout_ref[...])
    o_ref[...] = _layer_norm(y, lnw_ref[...], lnb_ref[...])


def _mixer(x, w_in, lb_logits, norm_w, w_pool, pool_scale, w_out, ln_w, ln_b):
    t = x.shape[0]
    c = SEQ_BLOCK
    mall, lvl, _ = _decay_tables()
    full = lambda shape: pl.BlockSpec(shape, lambda i: (0,) * len(shape))
    return pl.pallas_call(
        _mixer_kernel,
        grid=(t // c,),
        in_specs=[
            pl.BlockSpec((c, D_MODEL), lambda i: (i, 0)),
            full((D_MODEL, IN_PROJ_WIDTH)),
            full(lb_logits.shape),
            full((1, HGRN_WIDTH)),
            full((len(POOL_WINDOWS), POOL_GROUP, POOL_GROUP)),
            full((1, POOL_WIDTH)),
            full((D_MODEL, D_MODEL)),
            full((1, D_MODEL)),
            full((1, D_MODEL)),
            full(mall.shape),
            full(lvl.shape),
        ],
        out_specs=pl.BlockSpec((c, D_MODEL), lambda i: (i, 0)),
        out_shape=jax.ShapeDtypeStruct((t, D_MODEL), F32),
        scratch_shapes=[
            pltpu.VMEM((HGRN_HEADS, HEAD_DIM, HEAD_DIM), F32),
            pltpu.VMEM((POOL_CARRY, POOL_WIDTH), F32),
            pltpu.VMEM((POOL_CARRY + c, POOL_WIDTH), F32),
            pltpu.VMEM((c, D_MODEL), BF16),
            pltpu.VMEM((2, POOL_CARRY + c, POOL_WIDTH), F32),
        ],
        compiler_params=pltpu.CompilerParams(dimension_semantics=("arbitrary",), vmem_limit_bytes=VMEM_LIMIT),
        name="mixer",
    )(x, w_in.astype(BF16), lb_logits, norm_w.reshape(1, -1), w_pool.astype(BF16), pool_scale.reshape(1, -1),
      w_out.astype(BF16), ln_w.reshape(1, -1), ln_b.reshape(1, -1), jnp.asarray(mall, BF16), jnp.asarray(lvl))


def _kv_kernel(mem_ref, wk_ref, wv_ref, wq_ref, wo_ref, a_ref, b_ref):
    n_mem = mem_ref.shape[0]
    m = mem_ref[...].astype(BF16)
    k = _dot(m, wk_ref[...]).astype(BF16)
    v = _dot(m, wv_ref[...]).astype(BF16)
    scale = XATTN_HEAD_DIM ** -0.5
    for h in range(XATTN_HEADS):
        hs = slice(h * XATTN_HEAD_DIM, (h + 1) * XATTN_HEAD_DIM)
        ms = slice(h * n_mem, (h + 1) * n_mem)
        a_ref[:, ms] = (_dot_nt(wq_ref[:, hs], k[:, hs]) * scale).astype(BF16)
        b_ref[ms, :] = _dot(v[:, hs], wo_ref[hs, :]).astype(BF16)


def _kv_proj(mem, wk, wv, wq, wo):
    n = mem.shape[0]
    return pl.pallas_call(
        _kv_kernel,
        out_shape=(jax.ShapeDtypeStruct((D_MODEL, XATTN_HEADS * n), BF16),
                   jax.ShapeDtypeStruct((XATTN_HEADS * n, D_MODEL), BF16)),
        compiler_params=pltpu.CompilerParams(vmem_limit_bytes=VMEM_LIMIT),
        name="kv_proj",
    )(mem, wk.astype(BF16), wv.astype(BF16), wq.astype(BF16), wo.astype(BF16))


def _split3(a):
    a0 = a.astype(BF16)
    r1 = a - a0.astype(F32)
    a1 = r1.astype(BF16)
    a2 = (r1 - a1.astype(F32)).astype(BF16)
    return a0, a1, a2


def _xattn_kernel(x_ref, a_ref, b_ref, lnw_ref, lnb_ref, wrt_ref, br_ref, ut_ref,
                  x2_ref, dest_ref, gate_ref, cnt_ref, att_ref, run_ref, *, shift):
    i = pl.program_id(0)
    c = x_ref.shape[0]
    n = N_EXPERTS

    @pl.when(i == 0)
    def _():
        run_ref[...] = jnp.zeros_like(run_ref)

    x = x_ref[...]
    scores = _dot(x.astype(BF16), a_ref[...])
    n_mem = a_ref.shape[1] // XATTN_HEADS
    for h in range(XATTN_HEADS):
        ms = slice(h * n_mem, (h + 1) * n_mem)
        s = scores[:, ms]
        e = jnp.exp(s - jnp.max(s, axis=-1, keepdims=True))
        att_ref[:, ms] = (e / jnp.sum(e, axis=-1, keepdims=True)).astype(BF16)
    y = DEEPNORM_ALPHA * x + _dot(att_ref[...], b_ref[...])
    x2 = _layer_norm(y, lnw_ref[...], lnb_ref[...])
    _store_row_tiles(x2_ref, x2)

    a0, a1, _ = _split3(x2)
    w0, w1, w2 = _split3(wrt_ref[...])
    pa = _dot_nt(jnp.concatenate([w0, w1, w2], axis=0), a0)
    pb = _dot_nt(jnp.concatenate([w0, w1], axis=0), a1)
    logits = ((pa[2 * n:3 * n] + pb[n:2 * n]) + (pb[0:n] + pa[n:2 * n]) + pa[0:n]) + br_ref[...]

    eid = lax.broadcasted_iota(jnp.int32, (n, c), 0)
    work = logits
    sels, vals, idxs = [], [], []
    for k in range(TOP_K):
        m = jnp.max(work, axis=0, keepdims=True)
        idx = jnp.min(jnp.where(work == m, eid, n), axis=0, keepdims=True)
        sel = eid == idx
        sels.append(sel)
        vals.append(m)
        idxs.append(idx)
        work = jnp.where(sel, -jnp.inf, work)
    es = [jnp.exp(vk - vals[0]) for vk in vals]
    denom = es[0] + es[1] + es[2] + es[3]
    gate_ref[0] = jnp.concatenate([ek / denom for ek in es], axis=0)

    onehot = jnp.zeros((n, c), F32)
    for sel in sels:
        onehot = jnp.where(sel, 1.0, onehot)
    before = _dot(onehot.astype(BF16), ut_ref[...]) + run_ref[...]
    ranks = [jnp.sum(jnp.where(sel, before, 0.0), axis=0, keepdims=True).astype(jnp.int32) for sel in sels]
    run_ref[...] = run_ref[...] + jnp.sum(onehot, axis=1, keepdims=True)

    dest_ref[0] = jnp.concatenate([idx * (1 << shift) + rk for idx, rk in zip(idxs, ranks)], axis=0)
    cnt_ref[...] = run_ref[...].astype(jnp.int32)


def _xattn_router(x1, score_w, out_w, ln_w, ln_b, w_r, b_r):
    t = x1.shape[0]
    c = XATTN_BLOCK if t % XATTN_BLOCK == 0 else SEQ_BLOCK
    idx = np.arange(c)
    strict_upper = (idx[:, None] < idx[None, :]).astype(np.float32)
    full = lambda shape: pl.BlockSpec(shape, lambda i: (0,) * len(shape))
    per_k = pl.BlockSpec((1, TOP_K, c), lambda i: (i, 0, 0))
    return pl.pallas_call(
        functools.partial(_xattn_kernel, shift=_rank_bits(t)),
        grid=(t // c,),
        in_specs=[
            pl.BlockSpec((c, D_MODEL), lambda i: (i, 0)),
            full(score_w.shape),
            full(out_w.shape),
            full((1, D_MODEL)),
            full((1, D_MODEL)),
            full((N_EXPERTS, D_MODEL)),
            full((N_EXPERTS, 1)),
            full((c, c)),
        ],
        out_specs=[_tiled_spec(c, lambda i: (i, 0)), per_k, per_k, full((N_EXPERTS, 1))],
        out_shape=[
            jax.ShapeDtypeStruct(_tiled(t), F32),
            jax.ShapeDtypeStruct((t // c, TOP_K, c), jnp.int32),
            jax.ShapeDtypeStruct((t // c, TOP_K, c), F32),
            jax.ShapeDtypeStruct((N_EXPERTS, 1), jnp.int32),
        ],
        scratch_shapes=[pltpu.VMEM((c, score_w.shape[1]), BF16), pltpu.VMEM((N_EXPERTS, 1), F32)],
        compiler_params=pltpu.CompilerParams(dimension_semantics=("arbitrary",), vmem_limit_bytes=VMEM_LIMIT),
        name="xattn_router",
    )(x1, score_w, out_w, ln_w.reshape(1, -1), ln_b.reshape(1, -1),
      w_r.T, b_r.reshape(-1, 1), jnp.asarray(strict_upper, BF16))


TAIL_RUNS = (128, 64, 32, 16, 8, 4, 2, 1)


def _dispatch_kernel(fill_ref, pend_ref, nused_ref, slot_ref, x_ref, xs_hbm, zero_ref, sem, zsem, *, n_blocks):
    i = pl.program_id(0)
    c = SEQ_BLOCK

    @pl.when(i == 0)
    def _():
        zero_ref[...] = jnp.zeros_like(zero_ref)

        sub = ROW_TILE[0]

        def rows(ref, first, n):
            return ref.at[pl.ds(pl.multiple_of(first * sub, sub), n * sub)]

        def tail(e, act):
            row = fill_ref[e]
            pad = pend_ref[e] - row
            for run in TAIL_RUNS:
                hit = (pad & run) != 0

                @pl.when(hit)
                def _():
                    act(pltpu.make_async_copy(rows(zero_ref, 0, run), rows(xs_hbm, row, run), zsem))
                row = row + jnp.where(hit, run, 0)

        def spare(b, act):
            act(pltpu.make_async_copy(zero_ref, rows(xs_hbm, b * MOE_BLOCK, MOE_BLOCK), zsem))

        for act in (lambda cp: cp.start(), lambda cp: cp.wait()):
            lax.fori_loop(0, N_EXPERTS, lambda e, carry: (tail(e, act), carry)[1], 0)
            lax.fori_loop(nused_ref[0], n_blocks, lambda b, carry: (spare(b, act), carry)[1], 0)

    for q in range(x_ref.shape[0] // (c * ROW_TILE[0])):
        def issue(j, carry, q=q):
            for u in range(ISSUE_UNROLL):
                r = j * ISSUE_UNROLL + u
                for k in range(TOP_K):
                    dst = _tile_of(xs_hbm, slot_ref[0, 0, (q * TOP_K + k) * c + r])
                    pltpu.make_async_copy(_tile_of(x_ref, q * c + r), dst, sem).start(priority=k % 2)
            return carry
        lax.fori_loop(0, c // ISSUE_UNROLL, issue, 0)
    for k in range(TOP_K):
        pltpu.make_async_copy(x_ref, xs_hbm.at[pl.ds(0, x_ref.shape[0])], sem).wait()


def _dispatch(x2, slots, fill, pend, n_used, n_blocks):
    t = x2.shape[0] // ROW_TILE[0]
    c = min(DISPATCH_BLOCK, t)
    assert t % c == 0 and c % SEQ_BLOCK == 0
    grid_spec = pltpu.PrefetchScalarGridSpec(
        num_scalar_prefetch=3,
        grid=(t // c,),
        in_specs=[
            pl.BlockSpec((1, 1, c * TOP_K), lambda i, *_: (i, 0, 0), memory_space=pltpu.SMEM),
            _tiled_spec(c, lambda i, *_: (i, 0)),
        ],
        out_specs=pl.BlockSpec(memory_space=pl.ANY),
        scratch_shapes=[pltpu.VMEM(_tiled(MOE_BLOCK), F32), pltpu.SemaphoreType.DMA, pltpu.SemaphoreType.DMA],
    )
    return pl.pallas_call(
        functools.partial(_dispatch_kernel, n_blocks=n_blocks),
        grid_spec=grid_spec,
        out_shape=jax.ShapeDtypeStruct(_tiled(n_blocks * MOE_BLOCK), F32),
        compiler_params=pltpu.CompilerParams(dimension_semantics=("arbitrary",), vmem_limit_bytes=VMEM_LIMIT),
        name="dispatch",
    )(fill, pend, n_used, slots.reshape(t // c, 1, c * TOP_K), x2)


def _expert_kernel(be_ref, first_ref, nxt_ref, par_ref, fused_ref, nused_ref, x_ref, w1_hbm, b1_ref, w2_hbm, b2_ref, o_ref,
                   w1f, w2f, w1b, w2b, sem1, sem2):
    step = pl.program_id(0)
    rows = MOE_BLOCK * ROW_TILE[0]

    def fetch(e, p):
        return (pltpu.make_async_copy(w1_hbm.at[e], w1f.at[p], sem1.at[p]),
                pltpu.make_async_copy(w2_hbm.at[e], w2f.at[p], sem2.at[p]))

    @pl.when(step == 0)
    def _():
        for cp in fetch(be_ref[0], par_ref[0]):
            cp.start()

    def window(first_block, n):
        at = pl.ds(pl.multiple_of(first_block * rows, rows), n * rows)
        return x_ref.at[at], o_ref.at[at]

    def load_weights(b):
        @pl.when(first_ref[b] == 1)
        def _():
            p = par_ref[b]
            for cp in fetch(be_ref[b], p):
                cp.wait()

            @pl.when(nxt_ref[b] >= 0)
            def _():
                for cp in fetch(nxt_ref[b], 1 - p):
                    cp.start()

            w1b[...] = w1f[p].astype(BF16)
            w2b[...] = w2f[p].astype(BF16)

    def mlp(x_rows, o_rows, e):
        h = _dot(_load_row_tiles(x_rows).astype(BF16), w1b[...]) + b1_ref[e]
        h_glu = jnp.minimum(h[:, :D_EXPERT], SWIGLU_LIMIT)
        h_lin = jnp.clip(h[:, D_EXPERT:], -SWIGLU_LIMIT, SWIGLU_LIMIT)
        act = h_glu * (1.0 / (1.0 + jnp.exp(-SWIGLU_ALPHA * h_glu))) * (h_lin + 1.0)
        _store_row_tiles(o_rows, _dot(act.astype(BF16), w2b[...]) + b2_ref[e])

    def single(sb, carry):
        b = step * EXPERT_STEP_BLOCKS + sb
        x_blk, o_blk = window(sb, 1)

        @pl.when(b >= nused_ref[0])
        def _():
            o_blk[...] = jnp.zeros(o_blk.shape, o_blk.dtype)

        @pl.when(b < nused_ref[0])
        def _():
            load_weights(b)
            mlp(x_blk, o_blk, be_ref[b])
        return carry

    def pair(jp, carry):
        b = step * EXPERT_STEP_BLOCKS + 2 * jp

        @pl.when(fused_ref[lax.shift_right_logical(b, 1)] == 1)
        def _():
            load_weights(b)
            x_two, o_two = window(2 * jp, 2)
            mlp(x_two, o_two, be_ref[b])

        @pl.when(fused_ref[lax.shift_right_logical(b, 1)] == 0)
        def _():
            lax.fori_loop(2 * jp, 2 * jp + 2, single, 0)
        return carry

    lax.fori_loop(0, EXPERT_STEP_BLOCKS // 2, pair, 0)


def _experts(xs, plan, w1, b1, w2, b2):
    block_e, first, nxt, par, fused, n_used = plan
    n_blocks = block_e.shape[0]
    assert n_blocks % EXPERT_STEP_BLOCKS == 0 and EXPERT_STEP_BLOCKS % 2 == 0
    rows = lambda s, *_: (s, 0)
    whole = lambda s, *_: (0, 0, 0)
    grid_spec = pltpu.PrefetchScalarGridSpec(
        num_scalar_prefetch=6,
        grid=(n_blocks // EXPERT_STEP_BLOCKS,),
        in_specs=[
            _tiled_spec(EXPERT_STEP_BLOCKS * MOE_BLOCK, rows),
            pl.BlockSpec(memory_space=pl.ANY),
            pl.BlockSpec((N_EXPERTS, 1, 2 * D_EXPERT), whole),
            pl.BlockSpec(memory_space=pl.ANY),
            pl.BlockSpec((N_EXPERTS, 1, D_MODEL), whole),
        ],
        out_specs=_tiled_spec(EXPERT_STEP_BLOCKS * MOE_BLOCK, rows),
        scratch_shapes=[
            pltpu.VMEM((2, D_MODEL, 2 * D_EXPERT), F32),
            pltpu.VMEM((2, D_EXPERT, D_MODEL), F32),
            pltpu.VMEM((D_MODEL, 2 * D_EXPERT), BF16),
            pltpu.VMEM((D_EXPERT, D_MODEL), BF16),
            pltpu.SemaphoreType.DMA((2,)),
            pltpu.SemaphoreType.DMA((2,)),
        ],
    )
    return pl.pallas_call(
        _expert_kernel,
        grid_spec=grid_spec,
        out_shape=jax.ShapeDtypeStruct(xs.shape, F32),
        compiler_params=pltpu.CompilerParams(dimension_semantics=("arbitrary",), vmem_limit_bytes=EXPERT_VMEM_LIMIT),
        name="experts",
    )(block_e, first, nxt, par, fused, n_used, xs, w1, b1.reshape(N_EXPERTS, 1, -1), w2, b2.reshape(N_EXPERTS, 1, -1))


def _combine_kernel(slot_ref, next_slot_ref, ys_hbm, gate_ref, x_ref, lnw_ref, lnb_ref, o_ref, ybuf, sem):
    i = pl.program_id(0)
    c = SEQ_BLOCK

    def gather(slots, p):
        def issue(j, carry):
            for u in range(GATHER_UNROLL):
                r = j * GATHER_UNROLL + u
                for k in range(TOP_K):
                    src = _tile_of(ys_hbm, slots[0, 0, k * c + r])
                    pltpu.make_async_copy(src, _tile_of(ybuf.at[p, k], r), sem.at[p]).start(priority=k % 2)
            return carry
        lax.fori_loop(0, c // GATHER_UNROLL, issue, 0)

    @pl.when(i == 0)
    def _():
        gather(slot_ref, 0)

    @pl.when(i + 1 < pl.num_programs(0))
    def _():
        gather(next_slot_ref, (i + 1) % 2)

    p = i % 2
    for k in range(TOP_K):
        pltpu.make_async_copy(ys_hbm.at[pl.ds(0, ybuf.shape[2])], ybuf.at[p, k], sem.at[p]).wait()
    gates = gate_ref[...]
    ys = [_load_row_tiles(ybuf.at[p, k]) * gates[:, k:k + 1] for k in range(TOP_K)]
    y = (ys[0] + ys[1]) + (ys[2] + ys[3])
    o_ref[...] = _layer_norm(DEEPNORM_ALPHA * _load_row_tiles(x_ref) + y, lnw_ref[...], lnb_ref[...])


def _combine(ys, slots, gates, x2, ln_w, ln_b):
    t = x2.shape[0] // ROW_TILE[0]
    c = SEQ_BLOCK
    n = t // c
    slot_table = slots.reshape(n, 1, c * TOP_K)
    return pl.pallas_call(
        _combine_kernel,
        grid=(n,),
        in_specs=[
            pl.BlockSpec((1, 1, c * TOP_K), lambda i: (i, 0, 0), memory_space=pltpu.SMEM),
            pl.BlockSpec((1, 1, c * TOP_K), lambda i: (jnp.minimum(i + 1, n - 1), 0, 0), memory_space=pltpu.SMEM),
            pl.BlockSpec(memory_space=pl.ANY),
            pl.BlockSpec((c, TOP_K), lambda i: (i, 0)),
            _tiled_spec(c, lambda i: (i, 0)),
            pl.BlockSpec((1, D_MODEL), lambda i: (0, 0)),
            pl.BlockSpec((1, D_MODEL), lambda i: (0, 0)),
        ],
        out_specs=pl.BlockSpec((c, D_MODEL), lambda i: (i, 0)),
        out_shape=jax.ShapeDtypeStruct((t, D_MODEL), F32),
        scratch_shapes=[pltpu.VMEM((2, TOP_K) + _tiled(c), F32), pltpu.SemaphoreType.DMA((2,))],
        compiler_params=pltpu.CompilerParams(dimension_semantics=("arbitrary",), vmem_limit_bytes=VMEM_LIMIT),
        name="combine",
    )(slot_table, slot_table, ys, gates, x2, ln_w.reshape(1, -1), ln_b.reshape(1, -1))


def _slot_tables(counts, packed, n_blocks, shift):
    counts = counts.reshape(-1)
    nblk = (counts + MOE_BLOCK - 1) // MOE_BLOCK
    ids = jnp.arange(N_EXPERTS)
    lower = ids[None, :] <= ids[:, None]
    bend = jnp.sum(jnp.where(lower, nblk[None, :], 0), axis=1)
    n_used = bend[N_EXPERTS - 1]
    pend = bend * MOE_BLOCK
    pstart = pend - nblk * MOE_BLOCK
    fill = pstart + counts
    order = jnp.sum(jnp.where(lower, (nblk > 0)[None, :], False), axis=1) - 1
    later = (ids[None, :] > ids[:, None]) & (nblk > 0)[None, :]
    nxt_e = jnp.min(jnp.where(later, ids[None, :], N_EXPERTS), axis=1)
    nxt_e = jnp.where(nxt_e == N_EXPERTS, -1, nxt_e)
    b = jnp.minimum(jnp.arange(n_blocks), n_used - 1)
    hit = bend[None, :] <= b[:, None]
    block_e = jnp.sum(hit.astype(jnp.int32), axis=1)
    of_block = lambda per_expert: jnp.sum(jnp.where(ids[None, :] == block_e[:, None], per_expert[None, :], 0), axis=1)
    first = (b == of_block(bend - nblk)).astype(jnp.int32)
    expert = lax.shift_right_logical(packed, shift)
    rank = packed & ((1 << shift) - 1)
    slots = jnp.sum(jnp.where(expert[..., None] == ids, pstart, 0), axis=-1) + rank
    i32 = lambda a: a.astype(jnp.int32)
    odd = 2 * jnp.arange(n_blocks // 2) + 1
    fused = (odd < n_used) & (block_e[0::2] == block_e[1::2])
    plan = (i32(block_e), first, i32(of_block(nxt_e)), i32(of_block(order & 1)), i32(fused), i32(n_used.reshape(1)))
    return i32(slots), i32(fill), i32(pend), plan


def kernel(x, mem, w_in, lb_logits, hgrn_norm_w, w_pool, pool_scale, w_out, ln1_w, ln1_b, w_xq, w_xk, w_xv, w_xo,
           ln2_w, ln2_b, w_router, b_router, w1, b1, w2, b2, ln3_w, ln3_b):
    bsz, seq, d = x.shape
    assert bsz == 1 and d == D_MODEL and seq % SEQ_BLOCK == 0 and w_in.shape[0] == 1
    xt = x.reshape(seq, d)
    x1 = _mixer(xt, w_in[0], lb_logits, hgrn_norm_w[0], w_pool[0], pool_scale[0], w_out[0], ln1_w[0], ln1_b[0])
    score_w, out_w = _kv_proj(mem[0], w_xk[0], w_xv[0], w_xq[0], w_xo[0])
    x2, packed, gates, counts = _xattn_router(x1, score_w, out_w, ln2_w[0], ln2_b[0], w_router[0], b_router[0])
    n_blocks = -(-(seq * TOP_K + N_EXPERTS * (MOE_BLOCK - 1)) // MOE_BLOCK)
    slots, fill, pend, plan = _slot_tables(counts, packed, n_blocks, _rank_bits(seq))
    nb, _, cx = slots.shape
    slots = slots.reshape(nb, TOP_K, cx // SEQ_BLOCK, SEQ_BLOCK).transpose(0, 2, 1, 3).reshape(-1, TOP_K, SEQ_BLOCK)
    xs = _dispatch(x2, slots, fill, pend, plan[-1], n_blocks)
    ys = _experts(xs, plan, w1[0], b1[0], w2[0], b2[0])
    gates_tk = gates.transpose(0, 2, 1).reshape(seq, TOP_K)
    out = _combine(ys, slots, gates_tk, x2, ln3_w[0], ln3_b[0])
    return out.reshape(bsz, seq, d)
```

```python
import functools

import numpy as np
import jax
import jax.numpy as jnp
from jax import lax
from jax.experimental import pallas as pl
from jax.experimental.pallas import tpu as pltpu

F32 = jnp.float32
BF16 = jnp.bfloat16

D_MODEL = 1024
HGRN_WIDTH = 512
HGRN_HEADS = 4
HEAD_DIM = 128
POOL_WINDOWS = (2, 4, 8, 16)
POOL_GROUP = 128
POOL_WIDTH = POOL_GROUP * len(POOL_WINDOWS)
POOL_CARRY = 24
POOL_HEAD = 8
IN_PROJ_WIDTH = 4 * HGRN_WIDTH + POOL_WIDTH
XATTN_HEADS = 4
XATTN_HEAD_DIM = 256
N_EXPERTS = 32
TOP_K = 4
D_EXPERT = 1024
SWIGLU_LIMIT = 7.0
SWIGLU_ALPHA = 1.702
MOE_BLOCK = 256
DEEPNORM_ALPHA = 2.0 ** 0.25
LN_EPS = 1e-5
RMS_EPS = 1e-6

SEQ_BLOCK = 256
EXPERT_STEP_BLOCKS = 4
XATTN_BLOCK = 1024
DISPATCH_BLOCK = 2048
N_LEVELS = 8
MXU_LEVELS = 3
ISSUE_UNROLL = 8
GATHER_UNROLL = 16
ROW_TILE = (8, 128)
VMEM_LIMIT = 48 * 1024 * 1024
EXPERT_VMEM_LIMIT = 56 * 1024 * 1024


def _rank_bits(t):
    return max(int(t - 1).bit_length(), 1)


def _layer_norm(y, w, b):
    mu = jnp.mean(y, axis=-1, keepdims=True)
    yc = y - mu
    var = jnp.mean(yc * yc, axis=-1, keepdims=True)
    return yc * lax.rsqrt(var + LN_EPS) * w + b


def _tiled(n):
    return (n * ROW_TILE[0], ROW_TILE[1])


def _tiled_spec(n, index_map):
    return pl.BlockSpec(_tiled(n), index_map)


def _tile_of(ref, r):
    return ref.at[pl.ds(pl.multiple_of(r * ROW_TILE[0], ROW_TILE[0]), ROW_TILE[0])]


def _store_row_tiles(ref, val):
    s, l = ROW_TILE
    for j in range(s):
        ref[pl.ds(j, val.shape[0], stride=s), :] = val[:, j * l:(j + 1) * l]


def _load_row_tiles(ref):
    s, _ = ROW_TILE
    return jnp.concatenate([ref[pl.ds(j, ref.shape[0] // s, stride=s), :] for j in range(s)], axis=-1)


def _dot(a, b):
    return jnp.dot(a, b, preferred_element_type=F32)


def _dot_nt(a, b):
    return lax.dot_general(a, b, (((1,), (1,)), ((), ())), preferred_element_type=F32)


def _decay_tables():
    c = SEQ_BLOCK
    r = np.arange(c)[:, None]
    u = np.arange(c)[None, :]
    mats = [(u <= r)]
    for l in range(MXU_LEVELS):
        h = 1 << l
        mid = (r // (2 * h)) * (2 * h) + h
        upper = (r >= mid) & (u >= mid) & (u <= r)
        lower = (r < mid) & (u > r) & (u < mid)
        mats.append(upper | lower)
    mall = np.concatenate(mats, axis=0).astype(np.float32)
    x = r ^ u
    lvl = np.where(u < r, np.floor(np.log2(np.maximum(x, 1))).astype(np.int32), -1).astype(np.int32)
    strict_lower = (u < r).astype(np.float32)
    return mall, lvl, strict_lower


def _upper_chunks(l, c):
    half, sub = 1 << l, ROW_TILE[0]
    return [r // sub for base in range(0, c, 2 * half) for r in range(base + half, base + 2 * half, sub)]


def _level_gap(negb, l):
    half = 1 << l
    parts = []
    for base in range(0, negb.shape[0], 2 * half):
        ref_row = negb[base + half - 1:base + half, :]
        parts.append(jnp.abs(negb[base:base + 2 * half, :] - ref_row))
    return jnp.concatenate(parts, axis=0) if len(parts) > 1 else parts[0]


def _mixer_kernel(x_ref, win_ref, lbl_ref, nw_ref, wpool_ref, pscale_ref, wout_ref, lnw_ref, lnb_ref,
                  mall_ref, lvl_ref, o_ref, state_ref, carry_ref, ext_ref, mix_ref, sum_ref):
    i = pl.program_id(0)
    c = SEQ_BLOCK

    @pl.when(i == 0)
    def _():
        state_ref[...] = jnp.zeros_like(state_ref)
        carry_ref[...] = jnp.zeros_like(carry_ref)
        sum_ref[:, 0:POOL_HEAD, :] = jnp.zeros((2, POOL_HEAD, POOL_WIDTH), F32)

    x = x_ref[...]
    proj = _dot(x.astype(BF16), win_ref[...])
    w = HGRN_WIDTH
    q = proj[:, 0:w]
    z = proj[:, w:2 * w]
    v = proj[:, 2 * w:3 * w]
    g = proj[:, 3 * w:4 * w]
    p = proj[:, 4 * w:]

    ll = lbl_ref[...]
    ee = jnp.exp(ll - jnp.max(ll, axis=0, keepdims=True))
    lb = ee[0:1] / jnp.sum(ee, axis=0, keepdims=True)

    f = lb + (1.0 - lb) * (1.0 / (1.0 + jnp.exp(-z)))
    nlf = -jnp.log(f)
    key = (1.0 - lb) * (1.0 / (1.0 + jnp.exp(z)))

    hi = nlf.astype(BF16)
    lo = (nlf - hi.astype(F32)).astype(BF16)
    mall = mall_ref[...]
    cum = _dot(mall, hi) + _dot(mall, lo)
    sub = ROW_TILE[0]
    lvl_rows = [lvl_ref[sub * r:sub * (r + 1), :] for r in range(c // sub)]

    for h in range(HGRN_HEADS):
        hs = slice(h * HEAD_DIM, (h + 1) * HEAD_DIM)
        qh, kh, vh, gh = q[:, hs], key[:, hs], v[:, hs], g[:, hs]
        vb = vh.astype(BF16)
        negb = cum[0:c, hs]
        sc = [jnp.zeros((sub, c), F32)] * (c // sub)
        for l in range(N_LEVELS):
            gap = cum[(l + 1) * c:(l + 2) * c, hs] if l < MXU_LEVELS else _level_gap(negb, l)
            e = jnp.exp(-gap)
            ke = (kh * e).astype(BF16)
            if l < MXU_LEVELS:
                rows = list(range(c // sub))
                qe = (qh * e).astype(BF16)
            else:
                rows = _upper_chunks(l, c)
                take = lambda a: jnp.concatenate([a[sub * r:sub * (r + 1)] for r in rows], axis=0)
                qe = (take(qh) * take(e)).astype(BF16)
            pl_ = _dot_nt(qe, ke)
            for j, r in enumerate(rows):
                sc[r] = jnp.where(lvl_rows[r] == l, pl_[sub * j:sub * (j + 1)], sc[r])
        scores = jnp.concatenate(sc, axis=0)
        diag = jnp.sum(qh * kh, axis=-1, keepdims=True)
        st = state_ref[h]
        o = (_dot(scores.astype(BF16), vb) + diag * vh
             + _dot_nt((qh * jnp.exp(-negb)).astype(BF16), st.astype(BF16)))
        negb_last = negb[c - 1:c, :]
        kd = (kh * jnp.exp(negb - negb_last)).astype(BF16)
        state_ref[h] = st * jnp.exp(-negb_last) + _dot(vh.T.astype(BF16), kd)
        o = o * lax.rsqrt(jnp.mean(o * o, axis=-1, keepdims=True) + RMS_EPS) * nw_ref[:, hs]
        gate = gh * (1.0 / (1.0 + jnp.exp(-gh)))
        mix_ref[:, hs] = (o * gate).astype(BF16)

    top = POOL_CARRY + c
    ext_ref[0:POOL_CARRY, :] = carry_ref[...]
    ext_ref[POOL_CARRY:top, :] = p
    carry_ref[...] = p[c - POOL_CARRY:c, :]
    pos = i * c + lax.broadcasted_iota(jnp.int32, (c, 1), 0)
    src = ext_ref
    for gi, win in enumerate(POOL_WINDOWS):
        gs = slice(gi * POOL_GROUP, (gi + 1) * POOL_GROUP)
        rest = slice(gi * POOL_GROUP, POOL_WIDTH)
        half = win // 2
        dst = sum_ref.at[gi % 2]
        dst[POOL_HEAD:top, rest] = src[POOL_HEAD:top, rest] + src[POOL_HEAD - half:top - half, rest]
        src = dst
        pg = p[:, gs]
        tot = dst[POOL_CARRY:top, gs]
        cnt = jnp.minimum(pos + 1, win).astype(F32)
        pooled = _dot((tot / cnt - pg).astype(BF16), wpool_ref[gi])
        mix_ref[:, w + gi * POOL_GROUP:w + (gi + 1) * POOL_GROUP] = (pooled * pscale_ref[:, gs]).astype(BF16)

    y = DEEPNORM_ALPHA * x + _dot(mix_ref[...], wout_ref[...])
    o_ref[...] = _layer_norm(y, lnw_ref[...], lnb_ref[...])


def _mixer(x, w_in, lb_logits, norm_w, w_pool, pool_scale, w_out, ln_w, ln_b):
    t = x.shape[0]
    c = SEQ_BLOCK
    mall, lvl, _ = _decay_tables()
    full = lambda shape: pl.BlockSpec(shape, lambda i: (0,) * len(shape))
    return pl.pallas_call(
        _mixer_kernel,
        grid=(t // c,),
        in_specs=[
            pl.BlockSpec((c, D_MODEL), lambda i: (i, 0)),
            full((D_MODEL, IN_PROJ_WIDTH)),
            full(lb_logits.shape),
            full((1, HGRN_WIDTH)),
            full((len(POOL_WINDOWS), POOL_GROUP, POOL_GROUP)),
            full((1, POOL_WIDTH)),
            full((D_MODEL, D_MODEL)),
            full((1, D_MODEL)),
            full((1, D_MODEL)),
            full(mall.shape),
            full(lvl.shape),
        ],
        out_specs=pl.BlockSpec((c, D_MODEL), lambda i: (i, 0)),
        out_shape=jax.ShapeDtypeStruct((t, D_MODEL), F32),
        scratch_shapes=[
            pltpu.VMEM((HGRN_HEADS, HEAD_DIM, HEAD_DIM), F32),
            pltpu.VMEM((POOL_CARRY, POOL_WIDTH), F32),
            pltpu.VMEM((POOL_CARRY + c, POOL_WIDTH), F32),
            pltpu.VMEM((c, D_MODEL), BF16),
            pltpu.VMEM((2, POOL_CARRY + c, POOL_WIDTH), F32),
        ],
        compiler_params=pltpu.CompilerParams(dimension_semantics=("arbitrary",), vmem_limit_bytes=VMEM_LIMIT),
        name="mixer",
    )(x, w_in.astype(BF16), lb_logits, norm_w.reshape(1, -1), w_pool.astype(BF16), pool_scale.reshape(1, -1),
      w_out.astype(BF16), ln_w.reshape(1, -1), ln_b.reshape(1, -1), jnp.asarray(mall, BF16), jnp.asarray(lvl))


def _kv_kernel(mem_ref, wk_ref, wv_ref, wq_ref, wo_ref, a_ref, b_ref):
    n_mem = mem_ref.shape[0]
    m = mem_ref[...].astype(BF16)
    k = _dot(m, wk_ref[...]).astype(BF16)
    v = _dot(m, wv_ref[...]).astype(BF16)
    scale = XATTN_HEAD_DIM ** -0.5
    for h in range(XATTN_HEADS):
        hs = slice(h * XATTN_HEAD_DIM, (h + 1) * XATTN_HEAD_DIM)
        ms = slice(h * n_mem, (h + 1) * n_mem)
        a_ref[:, ms] = (_dot_nt(wq_ref[:, hs], k[:, hs]) * scale).astype(BF16)
        b_ref[ms, :] = _dot(v[:, hs], wo_ref[hs, :]).astype(BF16)


def _kv_proj(mem, wk, wv, wq, wo):
    n = mem.shape[0]
    return pl.pallas_call(
        _kv_kernel,
        out_shape=(jax.ShapeDtypeStruct((D_MODEL, XATTN_HEADS * n), BF16),
                   jax.ShapeDtypeStruct((XATTN_HEADS * n, D_MODEL), BF16)),
        compiler_params=pltpu.CompilerParams(vmem_limit_bytes=VMEM_LIMIT),
        name="kv_proj",
    )(mem, wk.astype(BF16), wv.astype(BF16), wq.astype(BF16), wo.astype(BF16))


def _split3(a):
    a0 = a.astype(BF16)
    r1 = a - a0.astype(F32)
    a1 = r1.astype(BF16)
    a2 = (r1 - a1.astype(F32)).astype(BF16)
    return a0, a1, a2


def _xattn_kernel(x_ref, a_ref, b_ref, lnw_ref, lnb_ref, wrt_ref, br_ref, ut_ref,
                  x2_ref, dest_ref, gate_ref, cnt_ref, att_ref, run_ref, *, shift):
    i = pl.program_id(0)
    c = x_ref.shape[0]
    n = N_EXPERTS

    @pl.when(i == 0)
    def _():
        run_ref[...] = jnp.zeros_like(run_ref)

    x = x_ref[...]
    scores = _dot(x.astype(BF16), a_ref[...])
    n_mem = a_ref.shape[1] // XATTN_HEADS
    for h in range(XATTN_HEADS):
        ms = slice(h * n_mem, (h + 1) * n_mem)
        s = scores[:, ms]
        e = jnp.exp(s - jnp.max(s, axis=-1, keepdims=True))
        att_ref[:, ms] = (e / jnp.sum(e, axis=-1, keepdims=True)).astype(BF16)
    y = DEEPNORM_ALPHA * x + _dot(att_ref[...], b_ref[...])
    x2 = _layer_norm(y, lnw_ref[...], lnb_ref[...])
    _store_row_tiles(x2_ref, x2)

    a0, a1, _ = _split3(x2)
    w0, w1, w2 = _split3(wrt_ref[...])
    pa = _dot_nt(jnp.concatenate([w0, w1, w2], axis=0), a0)
    pb = _dot_nt(jnp.concatenate([w0, w1], axis=0), a1)
    logits = ((pa[2 * n:3 * n] + pb[n:2 * n]) + (pb[0:n] + pa[n:2 * n]) + pa[0:n]) + br_ref[...]

    eid = lax.broadcasted_iota(jnp.int32, (n, c), 0)
    work = logits
    sels, vals, idxs = [], [], []
    for k in range(TOP_K):
        m = jnp.max(work, axis=0, keepdims=True)
        idx = jnp.min(jnp.where(work == m, eid, n), axis=0, keepdims=True)
        sel = eid == idx
        sels.append(sel)
        vals.append(m)
        idxs.append(idx)
        work = jnp.where(sel, -jnp.inf, work)
    es = [jnp.exp(vk - vals[0]) for vk in vals]
    denom = es[0] + es[1] + es[2] + es[3]
    gate_ref[0] = jnp.concatenate([ek / denom for ek in es], axis=0)

    onehot = jnp.zeros((n, c), F32)
    for sel in sels:
        onehot = jnp.where(sel, 1.0, onehot)
    before = _dot(onehot.astype(BF16), ut_ref[...]) + run_ref[...]
    ranks = [jnp.sum(jnp.where(sel, before, 0.0), axis=0, keepdims=True).astype(jnp.int32) for sel in sels]
    run_ref[...] = run_ref[...] + jnp.sum(onehot, axis=1, keepdims=True)

    dest_ref[0] = jnp.concatenate([idx * (1 << shift) + rk for idx, rk in zip(idxs, ranks)], axis=0)
    cnt_ref[...] = run_ref[...].astype(jnp.int32)


def _xattn_router(x1, score_w, out_w, ln_w, ln_b, w_r, b_r):
    t = x1.shape[0]
    c = XATTN_BLOCK if t % XATTN_BLOCK == 0 else SEQ_BLOCK
    idx = np.arange(c)
    strict_upper = (idx[:, None] < idx[None, :]).astype(np.float32)
    full = lambda shape: pl.BlockSpec(shape, lambda i: (0,) * len(shape))
    per_k = pl.BlockSpec((1, TOP_K, c), lambda i: (i, 0, 0))
    return pl.pallas_call(
        functools.partial(_xattn_kernel, shift=_rank_bits(t)),
        grid=(t // c,),
        in_specs=[
            pl.BlockSpec((c, D_MODEL), lambda i: (i, 0)),
            full(score_w.shape),
            full(out_w.shape),
            full((1, D_MODEL)),
            full((1, D_MODEL)),
            full((N_EXPERTS, D_MODEL)),
            full((N_EXPERTS, 1)),
            full((c, c)),
        ],
        out_specs=[_tiled_spec(c, lambda i: (i, 0)), per_k, per_k, full((N_EXPERTS, 1))],
        out_shape=[
            jax.ShapeDtypeStruct(_tiled(t), F32),
            jax.ShapeDtypeStruct((t // c, TOP_K, c), jnp.int32),
            jax.ShapeDtypeStruct((t // c, TOP_K, c), F32),
            jax.ShapeDtypeStruct((N_EXPERTS, 1), jnp.int32),
        ],
        scratch_shapes=[pltpu.VMEM((c, score_w.shape[1]), BF16), pltpu.VMEM((N_EXPERTS, 1), F32)],
        compiler_params=pltpu.CompilerParams(dimension_semantics=("arbitrary",), vmem_limit_bytes=VMEM_LIMIT),
        name="xattn_router",
    )(x1, score_w, out_w, ln_w.reshape(1, -1), ln_b.reshape(1, -1),
      w_r.T, b_r.reshape(-1, 1), jnp.asarray(strict_upper, BF16))


TAIL_RUNS = (128, 64, 32, 16, 8, 4, 2, 1)


def _dispatch_kernel(fill_ref, pend_ref, nused_ref, slot_ref, x_ref, xs_hbm, zero_ref, sem, zsem, *, n_blocks):
    i = pl.program_id(0)
    c = SEQ_BLOCK

    @pl.when(i == 0)
    def _():
        zero_ref[...] = jnp.zeros_like(zero_ref)

        sub = ROW_TILE[0]

        def rows(ref, first, n):
            return ref.at[pl.ds(pl.multiple_of(first * sub, sub), n * sub)]

        def tail(e, act):
            row = fill_ref[e]
            pad = pend_ref[e] - row
            for run in TAIL_RUNS:
                hit = (pad & run) != 0

                @pl.when(hit)
                def _():
                    act(pltpu.make_async_copy(rows(zero_ref, 0, run), rows(xs_hbm, row, run), zsem))
                row = row + jnp.where(hit, run, 0)

        def spare(b, act):
            act(pltpu.make_async_copy(zero_ref, rows(xs_hbm, b * MOE_BLOCK, MOE_BLOCK), zsem))

        for act in (lambda cp: cp.start(), lambda cp: cp.wait()):
            lax.fori_loop(0, N_EXPERTS, lambda e, carry: (tail(e, act), carry)[1], 0)
            lax.fori_loop(nused_ref[0], n_blocks, lambda b, carry: (spare(b, act), carry)[1], 0)

    for q in range(x_ref.shape[0] // (c * ROW_TILE[0])):
        def issue(j, carry, q=q):
            for u in range(ISSUE_UNROLL):
                r = j * ISSUE_UNROLL + u
                for k in range(TOP_K):
                    dst = _tile_of(xs_hbm, slot_ref[0, 0, (q * TOP_K + k) * c + r])
                    pltpu.make_async_copy(_tile_of(x_ref, q * c + r), dst, sem).start(priority=k % 2)
            return carry
        lax.fori_loop(0, c // ISSUE_UNROLL, issue, 0)
    for k in range(TOP_K):
        pltpu.make_async_copy(x_ref, xs_hbm.at[pl.ds(0, x_ref.shape[0])], sem).wait()


def _dispatch(x2, slots, fill, pend, n_used, n_blocks):
    t = x2.shape[0] // ROW_TILE[0]
    c = min(DISPATCH_BLOCK, t)
    assert t % c == 0 and c % SEQ_BLOCK == 0
    grid_spec = pltpu.PrefetchScalarGridSpec(
        num_scalar_prefetch=3,
        grid=(t // c,),
        in_specs=[
            pl.BlockSpec((1, 1, c * TOP_K), lambda i, *_: (i, 0, 0), memory_space=pltpu.SMEM),
            _tiled_spec(c, lambda i, *_: (i, 0)),
        ],
        out_specs=pl.BlockSpec(memory_space=pl.ANY),
        scratch_shapes=[pltpu.VMEM(_tiled(MOE_BLOCK), F32), pltpu.SemaphoreType.DMA, pltpu.SemaphoreType.DMA],
    )
    return pl.pallas_call(
        functools.partial(_dispatch_kernel, n_blocks=n_blocks),
        grid_spec=grid_spec,
        out_shape=jax.ShapeDtypeStruct(_tiled(n_blocks * MOE_BLOCK), F32),
        compiler_params=pltpu.CompilerParams(dimension_semantics=("arbitrary",), vmem_limit_bytes=VMEM_LIMIT),
        name="dispatch",
    )(fill, pend, n_used, slots.reshape(t // c, 1, c * TOP_K), x2)


def _expert_kernel(be_ref, first_ref, nxt_ref, par_ref, fused_ref, nused_ref, x_ref, w1_hbm, b1_ref, w2_hbm, b2_ref, o_ref,
                   w1f, w2f, w1b, w2b, sem1, sem2):
    step = pl.program_id(0)
    rows = MOE_BLOCK * ROW_TILE[0]

    def fetch(e, p):
        return (pltpu.make_async_copy(w1_hbm.at[e], w1f.at[p], sem1.at[p]),
                pltpu.make_async_copy(w2_hbm.at[e], w2f.at[p], sem2.at[p]))

    @pl.when(step == 0)
    def _():
        for cp in fetch(be_ref[0], par_ref[0]):
            cp.start()

    def window(first_block, n):
        at = pl.ds(pl.multiple_of(first_block * rows, rows), n * rows)
        return x_ref.at[at], o_ref.at[at]

    def load_weights(b):
        @pl.when(first_ref[b] == 1)
        def _():
            p = par_ref[b]
            for cp in fetch(be_ref[b], p):
                cp.wait()

            @pl.when(nxt_ref[b] >= 0)
            def _():
                for cp in fetch(nxt_ref[b], 1 - p):
                    cp.start()

            w1b[...] = w1f[p].astype(BF16)
            w2b[...] = w2f[p].astype(BF16)

    def mlp(x_rows, o_rows, e):
        h = _dot(_load_row_tiles(x_rows).astype(BF16), w1b[...]) + b1_ref[e]
        h_glu = jnp.minimum(h[:, :D_EXPERT], SWIGLU_LIMIT)
        h_lin = jnp.clip(h[:, D_EXPERT:], -SWIGLU_LIMIT, SWIGLU_LIMIT)
        act = h_glu * (1.0 / (1.0 + jnp.exp(-SWIGLU_ALPHA * h_glu))) * (h_lin + 1.0)
        _store_row_tiles(o_rows, _dot(act.astype(BF16), w2b[...]) + b2_ref[e])

    def single(sb, carry):
        b = step * EXPERT_STEP_BLOCKS + sb
        x_blk, o_blk = window(sb, 1)

        @pl.when(b >= nused_ref[0])
        def _():
            o_blk[...] = jnp.zeros(o_blk.shape, o_blk.dtype)

        @pl.when(b < nused_ref[0])
        def _():
            load_weights(b)
            mlp(x_blk, o_blk, be_ref[b])
        return carry

    def pair(jp, carry):
        b = step * EXPERT_STEP_BLOCKS + 2 * jp

        @pl.when(fused_ref[lax.shift_right_logical(b, 1)] == 1)
        def _():
            load_weights(b)
            x_two, o_two = window(2 * jp, 2)
            mlp(x_two, o_two, be_ref[b])

        @pl.when(fused_ref[lax.shift_right_logical(b, 1)] == 0)
        def _():
            lax.fori_loop(2 * jp, 2 * jp + 2, single, 0)
        return carry

    first_pair = step * (EXPERT_STEP_BLOCKS // 2)
    whole = (fused_ref[first_pair] == 1) & (fused_ref[first_pair + 1] == 1) & \
        (be_ref[step * EXPERT_STEP_BLOCKS] == be_ref[step * EXPERT_STEP_BLOCKS + EXPERT_STEP_BLOCKS - 1])

    @pl.when(whole)
    def _():
        load_weights(step * EXPERT_STEP_BLOCKS)
        mlp(x_ref, o_ref, be_ref[step * EXPERT_STEP_BLOCKS])

    @pl.when(jnp.logical_not(whole))
    def _():
        lax.fori_loop(0, EXPERT_STEP_BLOCKS // 2, pair, 0)


def _experts(xs, plan, w1, b1, w2, b2):
    block_e, first, nxt, par, fused, n_used = plan
    n_blocks = block_e.shape[0]
    assert n_blocks % EXPERT_STEP_BLOCKS == 0 and EXPERT_STEP_BLOCKS % 2 == 0
    rows = lambda s, *_: (s, 0)
    whole = lambda s, *_: (0, 0, 0)
    grid_spec = pltpu.PrefetchScalarGridSpec(
        num_scalar_prefetch=6,
        grid=(n_blocks // EXPERT_STEP_BLOCKS,),
        in_specs=[
            _tiled_spec(EXPERT_STEP_BLOCKS * MOE_BLOCK, rows),
            pl.BlockSpec(memory_space=pl.ANY),
            pl.BlockSpec((N_EXPERTS, 1, 2 * D_EXPERT), whole),
            pl.BlockSpec(memory_space=pl.ANY),
            pl.BlockSpec((N_EXPERTS, 1, D_MODEL), whole),
        ],
        out_specs=_tiled_spec(EXPERT_STEP_BLOCKS * MOE_BLOCK, rows),
        scratch_shapes=[
            pltpu.VMEM((2, D_MODEL, 2 * D_EXPERT), F32),
            pltpu.VMEM((2, D_EXPERT, D_MODEL), F32),
            pltpu.VMEM((D_MODEL, 2 * D_EXPERT), BF16),
            pltpu.VMEM((D_EXPERT, D_MODEL), BF16),
            pltpu.SemaphoreType.DMA((2,)),
            pltpu.SemaphoreType.DMA((2,)),
        ],
    )
    return pl.pallas_call(
        _expert_kernel,
        grid_spec=grid_spec,
        out_shape=jax.ShapeDtypeStruct(xs.shape, F32),
        compiler_params=pltpu.CompilerParams(dimension_semantics=("arbitrary",), vmem_limit_bytes=EXPERT_VMEM_LIMIT),
        name="experts",
    )(block_e, first, nxt, par, fused, n_used, xs, w1, b1.reshape(N_EXPERTS, 1, -1), w2, b2.reshape(N_EXPERTS, 1, -1))


def _combine_kernel(slot_ref, next_slot_ref, ys_hbm, gate_ref, x_ref, lnw_ref, lnb_ref, o_ref, ybuf, sem):
    i = pl.program_id(0)
    c = SEQ_BLOCK

    def gather(slots, p):
        def issue(j, carry):
            for u in range(GATHER_UNROLL):
                r = j * GATHER_UNROLL + u
                for k in range(TOP_K):
                    src = _tile_of(ys_hbm, slots[0, 0, k * c + r])
                    pltpu.make_async_copy(src, _tile_of(ybuf.at[p, k], r), sem.at[p]).start(priority=k % 2)
            return carry
        lax.fori_loop(0, c // GATHER_UNROLL, issue, 0)

    @pl.when(i == 0)
    def _():
        gather(slot_ref, 0)

    @pl.when(i + 1 < pl.num_programs(0))
    def _():
        gather(next_slot_ref, (i + 1) % 2)

    p = i % 2
    for k in range(TOP_K):
        pltpu.make_async_copy(ys_hbm.at[pl.ds(0, ybuf.shape[2])], ybuf.at[p, k], sem.at[p]).wait()
    gates = gate_ref[...]
    ys = [_load_row_tiles(ybuf.at[p, k]) * gates[:, k:k + 1] for k in range(TOP_K)]
    y = (ys[0] + ys[1]) + (ys[2] + ys[3])
    o_ref[...] = _layer_norm(DEEPNORM_ALPHA * _load_row_tiles(x_ref) + y, lnw_ref[...], lnb_ref[...])


def _combine(ys, slots, gates, x2, ln_w, ln_b):
    t = x2.shape[0] // ROW_TILE[0]
    c = SEQ_BLOCK
    n = t // c
    slot_table = slots.reshape(n, 1, c * TOP_K)
    return pl.pallas_call(
        _combine_kernel,
        grid=(n,),
        in_specs=[
            pl.BlockSpec((1, 1, c * TOP_K), lambda i: (i, 0, 0), memory_space=pltpu.SMEM),
            pl.BlockSpec((1, 1, c * TOP_K), lambda i: (jnp.minimum(i + 1, n - 1), 0, 0), memory_space=pltpu.SMEM),
            pl.BlockSpec(memory_space=pl.ANY),
            pl.BlockSpec((c, TOP_K), lambda i: (i, 0)),
            _tiled_spec(c, lambda i: (i, 0)),
            pl.BlockSpec((1, D_MODEL), lambda i: (0, 0)),
            pl.BlockSpec((1, D_MODEL), lambda i: (0, 0)),
        ],
        out_specs=pl.BlockSpec((c, D_MODEL), lambda i: (i, 0)),
        out_shape=jax.ShapeDtypeStruct((t, D_MODEL), F32),
        scratch_shapes=[pltpu.VMEM((2, TOP_K) + _tiled(c), F32), pltpu.SemaphoreType.DMA((2,))],
        compiler_params=pltpu.CompilerParams(dimension_semantics=("arbitrary",), vmem_limit_bytes=VMEM_LIMIT),
        name="combine",
    )(slot_table, slot_table, ys, gates, x2, ln_w.reshape(1, -1), ln_b.reshape(1, -1))


def _slot_tables(counts, packed, n_blocks, shift):
    counts = counts.reshape(-1)
    nblk = (counts + MOE_BLOCK - 1) // MOE_BLOCK
    ids = jnp.arange(N_EXPERTS)
    lower = ids[None, :] <= ids[:, None]
    bend = jnp.sum(jnp.where(lower, nblk[None, :], 0), axis=1)
    n_used = bend[N_EXPERTS - 1]
    pend = bend * MOE_BLOCK
    pstart = pend - nblk * MOE_BLOCK
    fill = pstart + counts
    order = jnp.sum(jnp.where(lower, (nblk > 0)[None, :], False), axis=1) - 1
    later = (ids[None, :] > ids[:, None]) & (nblk > 0)[None, :]
    nxt_e = jnp.min(jnp.where(later, ids[None, :], N_EXPERTS), axis=1)
    nxt_e = jnp.where(nxt_e == N_EXPERTS, -1, nxt_e)
    b = jnp.minimum(jnp.arange(n_blocks), n_used - 1)
    hit = bend[None, :] <= b[:, None]
    block_e = jnp.sum(hit.astype(jnp.int32), axis=1)
    of_block = lambda per_expert: jnp.sum(jnp.where(ids[None, :] == block_e[:, None], per_expert[None, :], 0), axis=1)
    first = (b == of_block(bend - nblk)).astype(jnp.int32)
    expert = lax.shift_right_logical(packed, shift)
    rank = packed & ((1 << shift) - 1)
    slots = jnp.sum(jnp.where(expert[..., None] == ids, pstart, 0), axis=-1) + rank
    i32 = lambda a: a.astype(jnp.int32)
    odd = 2 * jnp.arange(n_blocks // 2) + 1
    fused = (odd < n_used) & (block_e[0::2] == block_e[1::2])
    plan = (i32(block_e), first, i32(of_block(nxt_e)), i32(of_block(order & 1)), i32(fused), i32(n_used.reshape(1)))
    return i32(slots), i32(fill), i32(pend), plan


def kernel(x, mem, w_in, lb_logits, hgrn_norm_w, w_pool, pool_scale, w_out, ln1_w, ln1_b, w_xq, w_xk, w_xv, w_xo,
           ln2_w, ln2_b, w_router, b_router, w1, b1, w2, b2, ln3_w, ln3_b):
    bsz, seq, d = x.shape
    assert bsz == 1 and d == D_MODEL and seq % SEQ_BLOCK == 0 and w_in.shape[0] == 1
    xt = x.reshape(seq, d)
    x1 = _mixer(xt, w_in[0], lb_logits, hgrn_norm_w[0], w_pool[0], pool_scale[0], w_out[0], ln1_w[0], ln1_b[0])
    score_w, out_w = _kv_proj(mem[0], w_xk[0], w_xv[0], w_xq[0], w_xo[0])
    x2, packed, gates, counts = _xattn_router(x1, score_w, out_w, ln2_w[0], ln2_b[0], w_router[0], b_router[0])
    n_blocks = -(-(seq * TOP_K + N_EXPERTS * (MOE_BLOCK - 1)) // MOE_BLOCK)
    slots, fill, pend, plan = _slot_tables(counts, packed, n_blocks, _rank_bits(seq))
    nb, _, cx = slots.shape
    slots = slots.reshape(nb, TOP_K, cx // SEQ_BLOCK, SEQ_BLOCK).transpose(0, 2, 1, 3).reshape(-1, TOP_K, SEQ_BLOCK)
    xs = _dispatch(x2, slots, fill, pend, plan[-1], n_blocks)
    ys = _experts(xs, plan, w1[0], b1[0], w2[0], b2[0])
    gates_tk = gates.transpose(0, 2, 1).reshape(seq, TOP_K)
    out = _combine(ys, slots, gates_tk, x2, ln3_w[0], ln3_b[0])
    return out.reshape(bsz, seq, d)
```

```python
import functools

import numpy as np
import jax
import jax.numpy as jnp
from jax import lax
from jax.experimental import pallas as pl
from jax.experimental.pallas import tpu as pltpu

F32 = jnp.float32
BF16 = jnp.bfloat16

D_MODEL = 1024
HGRN_WIDTH = 512
HGRN_HEADS = 4
HEAD_DIM = 128
POOL_WINDOWS = (2, 4, 8, 16)
POOL_GROUP = 128
POOL_WIDTH = POOL_GROUP * len(POOL_WINDOWS)
POOL_CARRY = 24
POOL_HEAD = 8
IN_PROJ_WIDTH = 4 * HGRN_WIDTH + POOL_WIDTH
XATTN_HEADS = 4
XATTN_HEAD_DIM = 256
N_EXPERTS = 32
TOP_K = 4
D_EXPERT = 1024
SWIGLU_LIMIT = 7.0
SWIGLU_ALPHA = 1.702
MOE_BLOCK = 256
DEEPNORM_ALPHA = 2.0 ** 0.25
LN_EPS = 1e-5
RMS_EPS = 1e-6

SEQ_BLOCK = 256
EXPERT_STEP_BLOCKS = 4
MIXER_BLOCK = 512
XATTN_BLOCK = 1024
DISPATCH_BLOCK = 2048
N_LEVELS = 8
MXU_LEVELS = 3
ISSUE_UNROLL = 8
GATHER_UNROLL = 16
ROW_TILE = (8, 128)
VMEM_LIMIT = 48 * 1024 * 1024
EXPERT_VMEM_LIMIT = 56 * 1024 * 1024


def _rank_bits(t):
    return max(int(t - 1).bit_length(), 1)


def _layer_norm(y, w, b):
    mu = jnp.mean(y, axis=-1, keepdims=True)
    yc = y - mu
    var = jnp.mean(yc * yc, axis=-1, keepdims=True)
    return yc * lax.rsqrt(var + LN_EPS) * w + b


def _tiled(n):
    return (n * ROW_TILE[0], ROW_TILE[1])


def _tiled_spec(n, index_map):
    return pl.BlockSpec(_tiled(n), index_map)


def _tile_of(ref, r):
    return ref.at[pl.ds(pl.multiple_of(r * ROW_TILE[0], ROW_TILE[0]), ROW_TILE[0])]


def _store_row_tiles(ref, val):
    s, l = ROW_TILE
    for j in range(s):
        ref[pl.ds(j, val.shape[0], stride=s), :] = val[:, j * l:(j + 1) * l]


def _load_row_tiles(ref):
    s, _ = ROW_TILE
    return jnp.concatenate([ref[pl.ds(j, ref.shape[0] // s, stride=s), :] for j in range(s)], axis=-1)


def _dot(a, b):
    return jnp.dot(a, b, preferred_element_type=F32)


def _dot_nt(a, b):
    return lax.dot_general(a, b, (((1,), (1,)), ((), ())), preferred_element_type=F32)


def _decay_tables():
    c = SEQ_BLOCK
    r = np.arange(c)[:, None]
    u = np.arange(c)[None, :]
    mats = [(u <= r)]
    for l in range(MXU_LEVELS):
        h = 1 << l
        mid = (r // (2 * h)) * (2 * h) + h
        upper = (r >= mid) & (u >= mid) & (u <= r)
        lower = (r < mid) & (u > r) & (u < mid)
        mats.append(upper | lower)
    mall = np.concatenate(mats, axis=0).astype(np.float32)
    x = r ^ u
    lvl = np.where(u < r, np.floor(np.log2(np.maximum(x, 1))).astype(np.int32), -1).astype(np.int32)
    strict_lower = (u < r).astype(np.float32)
    return mall, lvl, strict_lower


def _upper_chunks(l, c):
    half, sub = 1 << l, ROW_TILE[0]
    return [r // sub for base in range(0, c, 2 * half) for r in range(base + half, base + 2 * half, sub)]


def _level_gap(negb, l):
    half = 1 << l
    parts = []
    for base in range(0, negb.shape[0], 2 * half):
        ref_row = negb[base + half - 1:base + half, :]
        parts.append(jnp.abs(negb[base:base + 2 * half, :] - ref_row))
    return jnp.concatenate(parts, axis=0) if len(parts) > 1 else parts[0]


def _mixer_kernel(x_ref, win_ref, lbl_ref, nw_ref, wpool_ref, pscale_ref, wout_ref, lnw_ref, lnb_ref,
                  mall_ref, lvl_ref, o_ref, state_ref, carry_ref, ext_ref, mix_ref, sum_ref):
    i = pl.program_id(0)
    c = x_ref.shape[0]
    n = SEQ_BLOCK

    @pl.when(i == 0)
    def _():
        state_ref[...] = jnp.zeros_like(state_ref)
        carry_ref[...] = jnp.zeros_like(carry_ref)
        sum_ref[:, 0:POOL_HEAD, :] = jnp.zeros((2, POOL_HEAD, POOL_WIDTH), F32)

    x = x_ref[...]
    proj = _dot(x.astype(BF16), win_ref[...])
    w = HGRN_WIDTH
    q = proj[:, 0:w]
    z = proj[:, w:2 * w]
    v = proj[:, 2 * w:3 * w]
    g = proj[:, 3 * w:4 * w]
    p = proj[:, 4 * w:]

    ll = lbl_ref[...]
    ee = jnp.exp(ll - jnp.max(ll, axis=0, keepdims=True))
    lb = ee[0:1] / jnp.sum(ee, axis=0, keepdims=True)

    f = lb + (1.0 - lb) * (1.0 / (1.0 + jnp.exp(-z)))
    nlf = -jnp.log(f)
    key = (1.0 - lb) * (1.0 / (1.0 + jnp.exp(z)))

    hi = nlf.astype(BF16)
    lo = (nlf - hi.astype(F32)).astype(BF16)
    mall = mall_ref[...]
    chunks = [slice(j * n, (j + 1) * n) for j in range(c // n)]
    cums = [_dot(mall, hi[rs]) + _dot(mall, lo[rs]) for rs in chunks]
    sub = ROW_TILE[0]
    lvl_rows = [lvl_ref[sub * r:sub * (r + 1), :] for r in range(n // sub)]

    for h in range(HGRN_HEADS):
        hs = slice(h * HEAD_DIM, (h + 1) * HEAD_DIM)
        st = state_ref[h]
        for rs, cum in zip(chunks, cums):
            qh, kh, vh, gh = q[rs, hs], key[rs, hs], v[rs, hs], g[rs, hs]
            negb = cum[0:n, hs]
            sc = [jnp.zeros((sub, n), F32)] * (n // sub)
            for l in range(N_LEVELS):
                gap = cum[(l + 1) * n:(l + 2) * n, hs] if l < MXU_LEVELS else _level_gap(negb, l)
                e = jnp.exp(-gap)
                ke = (kh * e).astype(BF16)
                if l < MXU_LEVELS:
                    rows = list(range(n // sub))
                    qe = (qh * e).astype(BF16)
                else:
                    rows = _upper_chunks(l, n)
                    take = lambda a: jnp.concatenate([a[sub * r:sub * (r + 1)] for r in rows], axis=0)
                    qe = (take(qh) * take(e)).astype(BF16)
                pl_ = _dot_nt(qe, ke)
                for j, r in enumerate(rows):
                    sc[r] = jnp.where(lvl_rows[r] == l, pl_[sub * j:sub * (j + 1)], sc[r])
            scores = jnp.concatenate(sc, axis=0)
            diag = jnp.sum(qh * kh, axis=-1, keepdims=True)
            o = (_dot(scores.astype(BF16), vh.astype(BF16)) + diag * vh
                 + _dot_nt((qh * jnp.exp(-negb)).astype(BF16), st.astype(BF16)))
            negb_last = negb[n - 1:n, :]
            kd = (kh * jnp.exp(negb - negb_last)).astype(BF16)
            st = st * jnp.exp(-negb_last) + _dot(vh.T.astype(BF16), kd)
            o = o * lax.rsqrt(jnp.mean(o * o, axis=-1, keepdims=True) + RMS_EPS) * nw_ref[:, hs]
            gate = gh * (1.0 / (1.0 + jnp.exp(-gh)))
            mix_ref[rs, hs] = (o * gate).astype(BF16)
        state_ref[h] = st

    top = POOL_CARRY + c
    ext_ref[0:POOL_CARRY, :] = carry_ref[...]
    ext_ref[POOL_CARRY:top, :] = p
    carry_ref[...] = p[c - POOL_CARRY:c, :]
    pos = i * c + lax.broadcasted_iota(jnp.int32, (c, 1), 0)
    src = ext_ref
    for gi, win in enumerate(POOL_WINDOWS):
        gs = slice(gi * POOL_GROUP, (gi + 1) * POOL_GROUP)
        rest = slice(gi * POOL_GROUP, POOL_WIDTH)
        half = win // 2
        dst = sum_ref.at[gi % 2]
        dst[POOL_HEAD:top, rest] = src[POOL_HEAD:top, rest] + src[POOL_HEAD - half:top - half, rest]
        src = dst
        pg = p[:, gs]
        tot = dst[POOL_CARRY:top, gs]
        cnt = jnp.minimum(pos + 1, win).astype(F32)
        pooled = _dot((tot / cnt - pg).astype(BF16), wpool_ref[gi])
        mix_ref[:, w + gi * POOL_GROUP:w + (gi + 1) * POOL_GROUP] = (pooled * pscale_ref[:, gs]).astype(BF16)

    y = DEEPNORM_ALPHA * x + _dot(mix_ref[...], wout_ref[...])
    o_ref[...] = _layer_norm(y, lnw_ref[...], lnb_ref[...])


def _mixer(x, w_in, lb_logits, norm_w, w_pool, pool_scale, w_out, ln_w, ln_b):
    t = x.shape[0]
    c = MIXER_BLOCK if t % MIXER_BLOCK == 0 else SEQ_BLOCK
    mall, lvl, _ = _decay_tables()
    full = lambda shape: pl.BlockSpec(shape, lambda i: (0,) * len(shape))
    return pl.pallas_call(
        _mixer_kernel,
        grid=(t // c,),
        in_specs=[
            pl.BlockSpec((c, D_MODEL), lambda i: (i, 0)),
            full((D_MODEL, IN_PROJ_WIDTH)),
            full(lb_logits.shape),
            full((1, HGRN_WIDTH)),
            full((len(POOL_WINDOWS), POOL_GROUP, POOL_GROUP)),
            full((1, POOL_WIDTH)),
            full((D_MODEL, D_MODEL)),
            full((1, D_MODEL)),
            full((1, D_MODEL)),
            full(mall.shape),
            full(lvl.shape),
        ],
        out_specs=pl.BlockSpec((c, D_MODEL), lambda i: (i, 0)),
        out_shape=jax.ShapeDtypeStruct((t, D_MODEL), F32),
        scratch_shapes=[
            pltpu.VMEM((HGRN_HEADS, HEAD_DIM, HEAD_DIM), F32),
            pltpu.VMEM((POOL_CARRY, POOL_WIDTH), F32),
            pltpu.VMEM((POOL_CARRY + c, POOL_WIDTH), F32),
            pltpu.VMEM((c, D_MODEL), BF16),
            pltpu.VMEM((2, POOL_CARRY + c, POOL_WIDTH), F32),
        ],
        compiler_params=pltpu.CompilerParams(dimension_semantics=("arbitrary",), vmem_limit_bytes=VMEM_LIMIT),
        name="mixer",
    )(x, w_in.astype(BF16), lb_logits, norm_w.reshape(1, -1), w_pool.astype(BF16), pool_scale.reshape(1, -1),
      w_out.astype(BF16), ln_w.reshape(1, -1), ln_b.reshape(1, -1), jnp.asarray(mall, BF16), jnp.asarray(lvl))


def _kv_kernel(mem_ref, wk_ref, wv_ref, wq_ref, wo_ref, a_ref, b_ref):
    n_mem = mem_ref.shape[0]
    m = mem_ref[...].astype(BF16)
    k = _dot(m, wk_ref[...]).astype(BF16)
    v = _dot(m, wv_ref[...]).astype(BF16)
    scale = XATTN_HEAD_DIM ** -0.5
    for h in range(XATTN_HEADS):
        hs = slice(h * XATTN_HEAD_DIM, (h + 1) * XATTN_HEAD_DIM)
        ms = slice(h * n_mem, (h + 1) * n_mem)
        a_ref[:, ms] = (_dot_nt(wq_ref[:, hs], k[:, hs]) * scale).astype(BF16)
        b_ref[ms, :] = _dot(v[:, hs], wo_ref[hs, :]).astype(BF16)


def _kv_proj(mem, wk, wv, wq, wo):
    n = mem.shape[0]
    return pl.pallas_call(
        _kv_kernel,
        out_shape=(jax.ShapeDtypeStruct((D_MODEL, XATTN_HEADS * n), BF16),
                   jax.ShapeDtypeStruct((XATTN_HEADS * n, D_MODEL), BF16)),
        compiler_params=pltpu.CompilerParams(vmem_limit_bytes=VMEM_LIMIT),
        name="kv_proj",
    )(mem, wk.astype(BF16), wv.astype(BF16), wq.astype(BF16), wo.astype(BF16))


def _split3(a):
    a0 = a.astype(BF16)
    r1 = a - a0.astype(F32)
    a1 = r1.astype(BF16)
    a2 = (r1 - a1.astype(F32)).astype(BF16)
    return a0, a1, a2


def _xattn_kernel(x_ref, a_ref, b_ref, lnw_ref, lnb_ref, wrt_ref, br_ref, ut_ref,
                  x2_ref, dest_ref, gate_ref, cnt_ref, att_ref, run_ref, *, shift):
    i = pl.program_id(0)
    c = x_ref.shape[0]
    n = N_EXPERTS

    @pl.when(i == 0)
    def _():
        run_ref[...] = jnp.zeros_like(run_ref)

    x = x_ref[...]
    scores = _dot(x.astype(BF16), a_ref[...])
    n_mem = a_ref.shape[1] // XATTN_HEADS
    for h in range(XATTN_HEADS):
        ms = slice(h * n_mem, (h + 1) * n_mem)
        s = scores[:, ms]
        e = jnp.exp(s - jnp.max(s, axis=-1, keepdims=True))
        att_ref[:, ms] = (e / jnp.sum(e, axis=-1, keepdims=True)).astype(BF16)
    y = DEEPNORM_ALPHA * x + _dot(att_ref[...], b_ref[...])
    x2 = _layer_norm(y, lnw_ref[...], lnb_ref[...])
    _store_row_tiles(x2_ref, x2)

    a0, a1, _ = _split3(x2)
    w0, w1, w2 = _split3(wrt_ref[...])
    pa = _dot_nt(jnp.concatenate([w0, w1, w2], axis=0), a0)
    pb = _dot_nt(jnp.concatenate([w0, w1], axis=0), a1)
    logits = ((pa[2 * n:3 * n] + pb[n:2 * n]) + (pb[0:n] + pa[n:2 * n]) + pa[0:n]) + br_ref[...]

    eid = lax.broadcasted_iota(jnp.int32, (n, c), 0)
    work = logits
    sels, vals, idxs = [], [], []
    for k in range(TOP_K):
        m = jnp.max(work, axis=0, keepdims=True)
        idx = jnp.min(jnp.where(work == m, eid, n), axis=0, keepdims=True)
        sel = eid == idx
        sels.append(sel)
        vals.append(m)
        idxs.append(idx)
        work = jnp.where(sel, -jnp.inf, work)
    es = [jnp.exp(vk - vals[0]) for vk in vals]
    denom = es[0] + es[1] + es[2] + es[3]
    gate_ref[0] = jnp.concatenate([ek / denom for ek in es], axis=0)

    onehot = jnp.zeros((n, c), F32)
    for sel in sels:
        onehot = jnp.where(sel, 1.0, onehot)
    before = _dot(onehot.astype(BF16), ut_ref[...]) + run_ref[...]
    ranks = [jnp.sum(jnp.where(sel, before, 0.0), axis=0, keepdims=True).astype(jnp.int32) for sel in sels]
    run_ref[...] = run_ref[...] + jnp.sum(onehot, axis=1, keepdims=True)

    dest_ref[0] = jnp.concatenate([idx * (1 << shift) + rk for idx, rk in zip(idxs, ranks)], axis=0)
    cnt_ref[...] = run_ref[...].astype(jnp.int32)


def _xattn_router(x1, score_w, out_w, ln_w, ln_b, w_r, b_r):
    t = x1.shape[0]
    c = XATTN_BLOCK if t % XATTN_BLOCK == 0 else SEQ_BLOCK
    idx = np.arange(c)
    strict_upper = (idx[:, None] < idx[None, :]).astype(np.float32)
    full = lambda shape: pl.BlockSpec(shape, lambda i: (0,) * len(shape))
    per_k = pl.BlockSpec((1, TOP_K, c), lambda i: (i, 0, 0))
    return pl.pallas_call(
        functools.partial(_xattn_kernel, shift=_rank_bits(t)),
        grid=(t // c,),
        in_specs=[
            pl.BlockSpec((c, D_MODEL), lambda i: (i, 0)),
            full(score_w.shape),
            full(out_w.shape),
            full((1, D_MODEL)),
            full((1, D_MODEL)),
            full((N_EXPERTS, D_MODEL)),
            full((N_EXPERTS, 1)),
            full((c, c)),
        ],
        out_specs=[_tiled_spec(c, lambda i: (i, 0)), per_k, per_k, full((N_EXPERTS, 1))],
        out_shape=[
            jax.ShapeDtypeStruct(_tiled(t), F32),
            jax.ShapeDtypeStruct((t // c, TOP_K, c), jnp.int32),
            jax.ShapeDtypeStruct((t // c, TOP_K, c), F32),
            jax.ShapeDtypeStruct((N_EXPERTS, 1), jnp.int32),
        ],
        scratch_shapes=[pltpu.VMEM((c, score_w.shape[1]), BF16), pltpu.VMEM((N_EXPERTS, 1), F32)],
        compiler_params=pltpu.CompilerParams(dimension_semantics=("arbitrary",), vmem_limit_bytes=VMEM_LIMIT),
        name="xattn_router",
    )(x1, score_w, out_w, ln_w.reshape(1, -1), ln_b.reshape(1, -1),
      w_r.T, b_r.reshape(-1, 1), jnp.asarray(strict_upper, BF16))


TAIL_RUNS = (128, 64, 32, 16, 8, 4, 2, 1)


def _dispatch_kernel(fill_ref, pend_ref, nused_ref, slot_ref, x_ref, xs_hbm, zero_ref, sem, zsem, *, n_blocks):
    i = pl.program_id(0)
    c = SEQ_BLOCK

    @pl.when(i == 0)
    def _():
        zero_ref[...] = jnp.zeros_like(zero_ref)

        sub = ROW_TILE[0]

        def rows(ref, first, n):
            return ref.at[pl.ds(pl.multiple_of(first * sub, sub), n * sub)]

        def tail(e, act):
            row = fill_ref[e]
            pad = pend_ref[e] - row
            for run in TAIL_RUNS:
                hit = (pad & run) != 0

                @pl.when(hit)
                def _():
                    act(pltpu.make_async_copy(rows(zero_ref, 0, run), rows(xs_hbm, row, run), zsem))
                row = row + jnp.where(hit, run, 0)

        def spare(b, act):
            act(pltpu.make_async_copy(zero_ref, rows(xs_hbm, b * MOE_BLOCK, MOE_BLOCK), zsem))

        for act in (lambda cp: cp.start(), lambda cp: cp.wait()):
            lax.fori_loop(0, N_EXPERTS, lambda e, carry: (tail(e, act), carry)[1], 0)
            lax.fori_loop(nused_ref[0], n_blocks, lambda b, carry: (spare(b, act), carry)[1], 0)

    for q in range(x_ref.shape[0] // (c * ROW_TILE[0])):
        def issue(j, carry, q=q):
            for u in range(ISSUE_UNROLL):
                r = j * ISSUE_UNROLL + u
                for k in range(TOP_K):
                    dst = _tile_of(xs_hbm, slot_ref[0, 0, (q * TOP_K + k) * c + r])
                    pltpu.make_async_copy(_tile_of(x_ref, q * c + r), dst, sem).start(priority=k % 2)
            return carry
        lax.fori_loop(0, c // ISSUE_UNROLL, issue, 0)
    for k in range(TOP_K):
        pltpu.make_async_copy(x_ref, xs_hbm.at[pl.ds(0, x_ref.shape[0])], sem).wait()


def _dispatch(x2, slots, fill, pend, n_used, n_blocks):
    t = x2.shape[0] // ROW_TILE[0]
    c = min(DISPATCH_BLOCK, t)
    assert t % c == 0 and c % SEQ_BLOCK == 0
    grid_spec = pltpu.PrefetchScalarGridSpec(
        num_scalar_prefetch=3,
        grid=(t // c,),
        in_specs=[
            pl.BlockSpec((1, 1, c * TOP_K), lambda i, *_: (i, 0, 0), memory_space=pltpu.SMEM),
            _tiled_spec(c, lambda i, *_: (i, 0)),
        ],
        out_specs=pl.BlockSpec(memory_space=pl.ANY),
        scratch_shapes=[pltpu.VMEM(_tiled(MOE_BLOCK), F32), pltpu.SemaphoreType.DMA, pltpu.SemaphoreType.DMA],
    )
    return pl.pallas_call(
        functools.partial(_dispatch_kernel, n_blocks=n_blocks),
        grid_spec=grid_spec,
        out_shape=jax.ShapeDtypeStruct(_tiled(n_blocks * MOE_BLOCK), F32),
        compiler_params=pltpu.CompilerParams(dimension_semantics=("arbitrary",), vmem_limit_bytes=VMEM_LIMIT),
        name="dispatch",
    )(fill, pend, n_used, slots.reshape(t // c, 1, c * TOP_K), x2)


def _expert_kernel(be_ref, first_ref, nxt_ref, par_ref, fused_ref, nused_ref, x_ref, w1_hbm, b1_ref, w2_hbm, b2_ref, o_ref,
                   w1f, w2f, w1b, w2b, sem1, sem2):
    step = pl.program_id(0)
    rows = MOE_BLOCK * ROW_TILE[0]

    def fetch(e, p):
        return (pltpu.make_async_copy(w1_hbm.at[e], w1f.at[p], sem1.at[p]),
                pltpu.make_async_copy(w2_hbm.at[e], w2f.at[p], sem2.at[p]))

    @pl.when(step == 0)
    def _():
        for cp in fetch(be_ref[0], par_ref[0]):
            cp.start()

    def window(first_block, n):
        at = pl.ds(pl.multiple_of(first_block * rows, rows), n * rows)
        return x_ref.at[at], o_ref.at[at]

    def load_weights(b):
        @pl.when(first_ref[b] == 1)
        def _():
            p = par_ref[b]
            for cp in fetch(be_ref[b], p):
                cp.wait()

            @pl.when(nxt_ref[b] >= 0)
            def _():
                for cp in fetch(nxt_ref[b], 1 - p):
                    cp.start()

            w1b[...] = w1f[p].astype(BF16)
            w2b[...] = w2f[p].astype(BF16)

    def mlp(x_rows, o_rows, e):
        h = _dot(_load_row_tiles(x_rows).astype(BF16), w1b[...]) + b1_ref[e]
        h_glu = jnp.minimum(h[:, :D_EXPERT], SWIGLU_LIMIT)
        h_lin = jnp.clip(h[:, D_EXPERT:], -SWIGLU_LIMIT, SWIGLU_LIMIT)
        act = h_glu * (1.0 / (1.0 + jnp.exp(-SWIGLU_ALPHA * h_glu))) * (h_lin + 1.0)
        _store_row_tiles(o_rows, _dot(act.astype(BF16), w2b[...]) + b2_ref[e])

    def single(sb, carry):
        b = step * EXPERT_STEP_BLOCKS + sb
        x_blk, o_blk = window(sb, 1)

        @pl.when(b >= nused_ref[0])
        def _():
            o_blk[...] = jnp.zeros(o_blk.shape, o_blk.dtype)

        @pl.when(b < nused_ref[0])
        def _():
            load_weights(b)
            mlp(x_blk, o_blk, be_ref[b])
        return carry

    def pair(jp, carry):
        b = step * EXPERT_STEP_BLOCKS + 2 * jp

        @pl.when(fused_ref[lax.shift_right_logical(b, 1)] == 1)
        def _():
            load_weights(b)
            x_two, o_two = window(2 * jp, 2)
            mlp(x_two, o_two, be_ref[b])

        @pl.when(fused_ref[lax.shift_right_logical(b, 1)] == 0)
        def _():
            lax.fori_loop(2 * jp, 2 * jp + 2, single, 0)
        return carry

    first_pair = step * (EXPERT_STEP_BLOCKS // 2)
    whole = (fused_ref[first_pair] == 1) & (fused_ref[first_pair + 1] == 1) & \
        (be_ref[step * EXPERT_STEP_BLOCKS] == be_ref[step * EXPERT_STEP_BLOCKS + EXPERT_STEP_BLOCKS - 1])

    @pl.when(whole)
    def _():
        load_weights(step * EXPERT_STEP_BLOCKS)
        mlp(x_ref, o_ref, be_ref[step * EXPERT_STEP_BLOCKS])

    @pl.when(jnp.logical_not(whole))
    def _():
        lax.fori_loop(0, EXPERT_STEP_BLOCKS // 2, pair, 0)


def _experts(xs, plan, w1, b1, w2, b2):
    block_e, first, nxt, par, fused, n_used = plan
    n_blocks = block_e.shape[0]
    assert n_blocks % EXPERT_STEP_BLOCKS == 0 and EXPERT_STEP_BLOCKS % 2 == 0
    rows = lambda s, *_: (s, 0)
    whole = lambda s, *_: (0, 0, 0)
    grid_spec = pltpu.PrefetchScalarGridSpec(
        num_scalar_prefetch=6,
        grid=(n_blocks // EXPERT_STEP_BLOCKS,),
        in_specs=[
            _tiled_spec(EXPERT_STEP_BLOCKS * MOE_BLOCK, rows),
            pl.BlockSpec(memory_space=pl.ANY),
            pl.BlockSpec((N_EXPERTS, 1, 2 * D_EXPERT), whole),
            pl.BlockSpec(memory_space=pl.ANY),
            pl.BlockSpec((N_EXPERTS, 1, D_MODEL), whole),
        ],
        out_specs=_tiled_spec(EXPERT_STEP_BLOCKS * MOE_BLOCK, rows),
        scratch_shapes=[
            pltpu.VMEM((2, D_MODEL, 2 * D_EXPERT), F32),
            pltpu.VMEM((2, D_EXPERT, D_MODEL), F32),
            pltpu.VMEM((D_MODEL, 2 * D_EXPERT), BF16),
            pltpu.VMEM((D_EXPERT, D_MODEL), BF16),
            pltpu.SemaphoreType.DMA((2,)),
            pltpu.SemaphoreType.DMA((2,)),
        ],
    )
    return pl.pallas_call(
        _expert_kernel,
        grid_spec=grid_spec,
        out_shape=jax.ShapeDtypeStruct(xs.shape, F32),
        compiler_params=pltpu.CompilerParams(dimension_semantics=("arbitrary",), vmem_limit_bytes=EXPERT_VMEM_LIMIT),
        name="experts",
    )(block_e, first, nxt, par, fused, n_used, xs, w1, b1.reshape(N_EXPERTS, 1, -1), w2, b2.reshape(N_EXPERTS, 1, -1))


def _combine_kernel(slot_ref, next_slot_ref, ys_hbm, gate_ref, x_ref, lnw_ref, lnb_ref, o_ref, ybuf, sem):
    i = pl.program_id(0)
    c = SEQ_BLOCK

    def gather(slots, p):
        def issue(j, carry):
            for u in range(GATHER_UNROLL):
                r = j * GATHER_UNROLL + u
                for k in range(TOP_K):
                    src = _tile_of(ys_hbm, slots[0, 0, k * c + r])
                    pltpu.make_async_copy(src, _tile_of(ybuf.at[p, k], r), sem.at[p]).start(priority=k % 2)
            return carry
        lax.fori_loop(0, c // GATHER_UNROLL, issue, 0)

    @pl.when(i == 0)
    def _():
        gather(slot_ref, 0)

    @pl.when(i + 1 < pl.num_programs(0))
    def _():
        gather(next_slot_ref, (i + 1) % 2)

    p = i % 2
    for k in range(TOP_K):
        pltpu.make_async_copy(ys_hbm.at[pl.ds(0, ybuf.shape[2])], ybuf.at[p, k], sem.at[p]).wait()
    gates = gate_ref[...]
    ys = [_load_row_tiles(ybuf.at[p, k]) * gates[:, k:k + 1] for k in range(TOP_K)]
    y = (ys[0] + ys[1]) + (ys[2] + ys[3])
    o_ref[...] = _layer_norm(DEEPNORM_ALPHA * _load_row_tiles(x_ref) + y, lnw_ref[...], lnb_ref[...])


def _combine(ys, slots, gates, x2, ln_w, ln_b):
    t = x2.shape[0] // ROW_TILE[0]
    c = SEQ_BLOCK
    n = t // c
    slot_table = slots.reshape(n, 1, c * TOP_K)
    return pl.pallas_call(
        _combine_kernel,
        grid=(n,),
        in_specs=[
            pl.BlockSpec((1, 1, c * TOP_K), lambda i: (i, 0, 0), memory_space=pltpu.SMEM),
            pl.BlockSpec((1, 1, c * TOP_K), lambda i: (jnp.minimum(i + 1, n - 1), 0, 0), memory_space=pltpu.SMEM),
            pl.BlockSpec(memory_space=pl.ANY),
            pl.BlockSpec((c, TOP_K), lambda i: (i, 0)),
            _tiled_spec(c, lambda i: (i, 0)),
            pl.BlockSpec((1, D_MODEL), lambda i: (0, 0)),
            pl.BlockSpec((1, D_MODEL), lambda i: (0, 0)),
        ],
        out_specs=pl.BlockSpec((c, D_MODEL), lambda i: (i, 0)),
        out_shape=jax.ShapeDtypeStruct((t, D_MODEL), F32),
        scratch_shapes=[pltpu.VMEM((2, TOP_K) + _tiled(c), F32), pltpu.SemaphoreType.DMA((2,))],
        compiler_params=pltpu.CompilerParams(dimension_semantics=("arbitrary",), vmem_limit_bytes=VMEM_LIMIT),
        name="combine",
    )(slot_table, slot_table, ys, gates, x2, ln_w.reshape(1, -1), ln_b.reshape(1, -1))


def _slot_tables(counts, packed, n_blocks, shift):
    counts = counts.reshape(-1)
    nblk = (counts + MOE_BLOCK - 1) // MOE_BLOCK
    ids = jnp.arange(N_EXPERTS)
    lower = ids[None, :] <= ids[:, None]
    bend = jnp.sum(jnp.where(lower, nblk[None, :], 0), axis=1)
    n_used = bend[N_EXPERTS - 1]
    pend = bend * MOE_BLOCK
    pstart = pend - nblk * MOE_BLOCK
    fill = pstart + counts
    order = jnp.sum(jnp.where(lower, (nblk > 0)[None, :], False), axis=1) - 1
    later = (ids[None, :] > ids[:, None]) & (nblk > 0)[None, :]
    nxt_e = jnp.min(jnp.where(later, ids[None, :], N_EXPERTS), axis=1)
    nxt_e = jnp.where(nxt_e == N_EXPERTS, -1, nxt_e)
    b = jnp.minimum(jnp.arange(n_blocks), n_used - 1)
    hit = bend[None, :] <= b[:, None]
    block_e = jnp.sum(hit.astype(jnp.int32), axis=1)
    of_block = lambda per_expert: jnp.sum(jnp.where(ids[None, :] == block_e[:, None], per_expert[None, :], 0), axis=1)
    first = (b == of_block(bend - nblk)).astype(jnp.int32)
    expert = lax.shift_right_logical(packed, shift)
    rank = packed & ((1 << shift) - 1)
    slots = jnp.sum(jnp.where(expert[..., None] == ids, pstart, 0), axis=-1) + rank
    i32 = lambda a: a.astype(jnp.int32)
    odd = 2 * jnp.arange(n_blocks // 2) + 1
    fused = (odd < n_used) & (block_e[0::2] == block_e[1::2])
    plan = (i32(block_e), first, i32(of_block(nxt_e)), i32(of_block(order & 1)), i32(fused), i32(n_used.reshape(1)))
    return i32(slots), i32(fill), i32(pend), plan


def kernel(x, mem, w_in, lb_logits, hgrn_norm_w, w_pool, pool_scale, w_out, ln1_w, ln1_b, w_xq, w_xk, w_xv, w_xo,
           ln2_w, ln2_b, w_router, b_router, w1, b1, w2, b2, ln3_w, ln3_b):
    bsz, seq, d = x.shape
    assert bsz == 1 and d == D_MODEL and seq % SEQ_BLOCK == 0 and w_in.shape[0] == 1
    xt = x.reshape(seq, d)
    x1 = _mixer(xt, w_in[0], lb_logits, hgrn_norm_w[0], w_pool[0], pool_scale[0], w_out[0], ln1_w[0], ln1_b[0])
    score_w, out_w = _kv_proj(mem[0], w_xk[0], w_xv[0], w_xq[0], w_xo[0])
    x2, packed, gates, counts = _xattn_router(x1, score_w, out_w, ln2_w[0], ln2_b[0], w_router[0], b_router[0])
    n_blocks = -(-(seq * TOP_K + N_EXPERTS * (MOE_BLOCK - 1)) // MOE_BLOCK)
    slots, fill, pend, plan = _slot_tables(counts, packed, n_blocks, _rank_bits(seq))
    nb, _, cx = slots.shape
    slots = slots.reshape(nb, TOP_K, cx // SEQ_BLOCK, SEQ_BLOCK).transpose(0, 2, 1, 3).reshape(-1, TOP_K, SEQ_BLOCK)
    xs = _dispatch(x2, slots, fill, pend, plan[-1], n_blocks)
    ys = _experts(xs, plan, w1[0], b1[0], w2[0], b2[0])
    gates_tk = gates.transpose(0, 2, 1).reshape(seq, TOP_K)
    out = _combine(ys, slots, gates_tk, x2, ln3_w[0], ln3_b[0])
    return out.reshape(bsz, seq, d)
```

```python
import functools

import numpy as np
import jax
import jax.numpy as jnp
from jax import lax
from jax.experimental import pallas as pl
from jax.experimental.pallas import tpu as pltpu

F32 = jnp.float32
BF16 = jnp.bfloat16

D_MODEL = 1024
HGRN_WIDTH = 512
HGRN_HEADS = 4
HEAD_DIM = 128
POOL_WINDOWS = (2, 4, 8, 16)
POOL_GROUP = 128
POOL_WIDTH = POOL_GROUP * len(POOL_WINDOWS)
POOL_CARRY = 24
POOL_HEAD = 8
IN_PROJ_WIDTH = 4 * HGRN_WIDTH + POOL_WIDTH
XATTN_HEADS = 4
XATTN_HEAD_DIM = 256
N_EXPERTS = 32
TOP_K = 4
D_EXPERT = 1024
SWIGLU_LIMIT = 7.0
SWIGLU_ALPHA = 1.702
MOE_BLOCK = 256
DEEPNORM_ALPHA = 2.0 ** 0.25
LN_EPS = 1e-5
RMS_EPS = 1e-6

SEQ_BLOCK = 256
EXPERT_STEP_BLOCKS = 4
XATTN_BLOCK = 1024
DISPATCH_BLOCK = 2048
N_LEVELS = 8
MXU_LEVELS = 3
ISSUE_UNROLL = 8
GATHER_UNROLL = 16
ROW_TILE = (8, 128)
VMEM_LIMIT = 48 * 1024 * 1024
EXPERT_VMEM_LIMIT = 56 * 1024 * 1024


def _rank_bits(t):
    return max(int(t - 1).bit_length(), 1)


def _layer_norm(y, w, b):
    mu = jnp.mean(y, axis=-1, keepdims=True)
    yc = y - mu
    var = jnp.mean(yc * yc, axis=-1, keepdims=True)
    return yc * lax.rsqrt(var + LN_EPS) * w + b


def _tiled(n):
    return (n * ROW_TILE[0], ROW_TILE[1])


def _tiled_spec(n, index_map):
    return pl.BlockSpec(_tiled(n), index_map)


def _tile_of(ref, r):
    return ref.at[pl.ds(pl.multiple_of(r * ROW_TILE[0], ROW_TILE[0]), ROW_TILE[0])]


def _store_row_tiles(ref, val):
    s, l = ROW_TILE
    for j in range(s):
        ref[pl.ds(j, val.shape[0], stride=s), :] = val[:, j * l:(j + 1) * l]


def _load_row_tiles(ref):
    s, _ = ROW_TILE
    return jnp.concatenate([ref[pl.ds(j, ref.shape[0] // s, stride=s), :] for j in range(s)], axis=-1)


def _dot(a, b):
    return jnp.dot(a, b, preferred_element_type=F32)


def _dot_nt(a, b):
    return lax.dot_general(a, b, (((1,), (1,)), ((), ())), preferred_element_type=F32)


def _decay_tables():
    c = SEQ_BLOCK
    r = np.arange(c)[:, None]
    u = np.arange(c)[None, :]
    mats = [(u <= r)]
    for l in range(MXU_LEVELS):
        h = 1 << l
        mid = (r // (2 * h)) * (2 * h) + h
        upper = (r >= mid) & (u >= mid) & (u <= r)
        lower = (r < mid) & (u > r) & (u < mid)
        mats.append(upper | lower)
    mall = np.concatenate(mats, axis=0).astype(np.float32)
    x = r ^ u
    lvl = np.where(u < r, np.floor(np.log2(np.maximum(x, 1))).astype(np.int32), -1).astype(np.int32)
    strict_lower = (u < r).astype(np.float32)
    return mall, lvl, strict_lower


def _upper_chunks(l, c):
    half, sub = 1 << l, ROW_TILE[0]
    return [r // sub for base in range(0, c, 2 * half) for r in range(base + half, base + 2 * half, sub)]


def _level_gap(negb, l):
    half = 1 << l
    parts = []
    for base in range(0, negb.shape[0], 2 * half):
        ref_row = negb[base + half - 1:base + half, :]
        parts.append(jnp.abs(negb[base:base + 2 * half, :] - ref_row))
    return jnp.concatenate(parts, axis=0) if len(parts) > 1 else parts[0]


def _mixer_kernel(x_ref, win_ref, lbl_ref, nw_ref, wpool_ref, pscale_ref, wout_ref, lnw_ref, lnb_ref,
                  mall_ref, lvl_ref, o_ref, state_ref, carry_ref, ext_ref, mix_ref, sum_ref, winb_ref, woutb_ref):
    i = pl.program_id(0)
    c = SEQ_BLOCK

    @pl.when(i == 0)
    def _():
        state_ref[...] = jnp.zeros_like(state_ref)
        carry_ref[...] = jnp.zeros_like(carry_ref)
        sum_ref[:, 0:POOL_HEAD, :] = jnp.zeros((2, POOL_HEAD, POOL_WIDTH), F32)
        for r in range(0, D_MODEL, c):
            winb_ref[r:r + c, :] = win_ref[r:r + c, :].astype(BF16)
            woutb_ref[r:r + c, :] = wout_ref[r:r + c, :].astype(BF16)

    x = x_ref[...]
    proj = _dot(x.astype(BF16), winb_ref[...])
    w = HGRN_WIDTH
    q = proj[:, 0:w]
    z = proj[:, w:2 * w]
    v = proj[:, 2 * w:3 * w]
    g = proj[:, 3 * w:4 * w]
    p = proj[:, 4 * w:]

    ll = lbl_ref[...]
    ee = jnp.exp(ll - jnp.max(ll, axis=0, keepdims=True))
    lb = ee[0:1] / jnp.sum(ee, axis=0, keepdims=True)

    f = lb + (1.0 - lb) * (1.0 / (1.0 + jnp.exp(-z)))
    nlf = -jnp.log(f)
    key = (1.0 - lb) * (1.0 / (1.0 + jnp.exp(z)))

    hi = nlf.astype(BF16)
    lo = (nlf - hi.astype(F32)).astype(BF16)
    mall = mall_ref[...]
    cum = _dot(mall, hi) + _dot(mall, lo)
    sub = ROW_TILE[0]
    lvl_rows = [lvl_ref[sub * r:sub * (r + 1), :] for r in range(c // sub)]

    for h in range(HGRN_HEADS):
        hs = slice(h * HEAD_DIM, (h + 1) * HEAD_DIM)
        qh, kh, vh, gh = q[:, hs], key[:, hs], v[:, hs], g[:, hs]
        vb = vh.astype(BF16)
        negb = cum[0:c, hs]
        sc = [jnp.zeros((sub, c), F32)] * (c // sub)
        for l in range(N_LEVELS):
            gap = cum[(l + 1) * c:(l + 2) * c, hs] if l < MXU_LEVELS else _level_gap(negb, l)
            e = jnp.exp(-gap)
            ke = (kh * e).astype(BF16)
            if l < MXU_LEVELS:
                rows = list(range(c // sub))
                qe = (qh * e).astype(BF16)
            else:
                rows = _upper_chunks(l, c)
                take = lambda a: jnp.concatenate([a[sub * r:sub * (r + 1)] for r in rows], axis=0)
                qe = (take(qh) * take(e)).astype(BF16)
            pl_ = _dot_nt(qe, ke)
            for j, r in enumerate(rows):
                sc[r] = jnp.where(lvl_rows[r] == l, pl_[sub * j:sub * (j + 1)], sc[r])
        scores = jnp.concatenate(sc, axis=0)
        diag = jnp.sum(qh * kh, axis=-1, keepdims=True)
        st = state_ref[h]
        o = (_dot(scores.astype(BF16), vb) + diag * vh
             + _dot_nt((qh * jnp.exp(-negb)).astype(BF16), st.astype(BF16)))
        negb_last = negb[c - 1:c, :]
        kd = (kh * jnp.exp(negb - negb_last)).astype(BF16)
        state_ref[h] = st * jnp.exp(-negb_last) + _dot(vh.T.astype(BF16), kd)
        o = o * lax.rsqrt(jnp.mean(o * o, axis=-1, keepdims=True) + RMS_EPS) * nw_ref[:, hs]
        gate = gh * (1.0 / (1.0 + jnp.exp(-gh)))
        mix_ref[:, hs] = (o * gate).astype(BF16)

    top = POOL_CARRY + c
    ext_ref[0:POOL_CARRY, :] = carry_ref[...]
    ext_ref[POOL_CARRY:top, :] = p
    carry_ref[...] = p[c - POOL_CARRY:c, :]
    pos = i * c + lax.broadcasted_iota(jnp.int32, (c, 1), 0)
    src = ext_ref
    for gi, win in enumerate(POOL_WINDOWS):
        gs = slice(gi * POOL_GROUP, (gi + 1) * POOL_GROUP)
        rest = slice(gi * POOL_GROUP, POOL_WIDTH)
        half = win // 2
        dst = sum_ref.at[gi % 2]
        dst[POOL_HEAD:top, rest] = src[POOL_HEAD:top, rest] + src[POOL_HEAD - half:top - half, rest]
        src = dst
        pg = p[:, gs]
        tot = dst[POOL_CARRY:top, gs]
        cnt = jnp.minimum(pos + 1, win).astype(F32)
        pooled = _dot((tot / cnt - pg).astype(BF16), wpool_ref[gi])
        mix_ref[:, w + gi * POOL_GROUP:w + (gi + 1) * POOL_GROUP] = (pooled * pscale_ref[:, gs]).astype(BF16)

    y = DEEPNORM_ALPHA * x + _dot(mix_ref[...], woutb_ref[...])
    o_ref[...] = _layer_norm(y, lnw_ref[...], lnb_ref[...])


def _mixer(x, w_in, lb_logits, norm_w, w_pool, pool_scale, w_out, ln_w, ln_b):
    t = x.shape[0]
    c = SEQ_BLOCK
    mall, lvl, _ = _decay_tables()
    full = lambda shape: pl.BlockSpec(shape, lambda i: (0,) * len(shape))
    return pl.pallas_call(
        _mixer_kernel,
        grid=(t // c,),
        in_specs=[
            pl.BlockSpec((c, D_MODEL), lambda i: (i, 0)),
            full((D_MODEL, IN_PROJ_WIDTH)),
            full(lb_logits.shape),
            full((1, HGRN_WIDTH)),
            full((len(POOL_WINDOWS), POOL_GROUP, POOL_GROUP)),
            full((1, POOL_WIDTH)),
            full((D_MODEL, D_MODEL)),
            full((1, D_MODEL)),
            full((1, D_MODEL)),
            full(mall.shape),
            full(lvl.shape),
        ],
        out_specs=pl.BlockSpec((c, D_MODEL), lambda i: (i, 0)),
        out_shape=jax.ShapeDtypeStruct((t, D_MODEL), F32),
        scratch_shapes=[
            pltpu.VMEM((HGRN_HEADS, HEAD_DIM, HEAD_DIM), F32),
            pltpu.VMEM((POOL_CARRY, POOL_WIDTH), F32),
            pltpu.VMEM((POOL_CARRY + c, POOL_WIDTH), F32),
            pltpu.VMEM((c, D_MODEL), BF16),
            pltpu.VMEM((2, POOL_CARRY + c, POOL_WIDTH), F32),
            pltpu.VMEM((D_MODEL, IN_PROJ_WIDTH), BF16),
            pltpu.VMEM((D_MODEL, D_MODEL), BF16),
        ],
        compiler_params=pltpu.CompilerParams(dimension_semantics=("arbitrary",), vmem_limit_bytes=EXPERT_VMEM_LIMIT),
        name="mixer",
    )(x, w_in, lb_logits, norm_w.reshape(1, -1), w_pool.astype(BF16), pool_scale.reshape(1, -1),
      w_out, ln_w.reshape(1, -1), ln_b.reshape(1, -1), jnp.asarray(mall, BF16), jnp.asarray(lvl))


def _kv_kernel(mem_ref, wk_ref, wv_ref, wq_ref, wo_ref, a_ref, b_ref):
    n_mem = mem_ref.shape[0]
    m = mem_ref[...].astype(BF16)
    k = _dot(m, wk_ref[...].astype(BF16)).astype(BF16)
    v = _dot(m, wv_ref[...].astype(BF16)).astype(BF16)
    scale = XATTN_HEAD_DIM ** -0.5
    for h in range(XATTN_HEADS):
        hs = slice(h * XATTN_HEAD_DIM, (h + 1) * XATTN_HEAD_DIM)
        ms = slice(h * n_mem, (h + 1) * n_mem)
        a_ref[:, ms] = (_dot_nt(wq_ref[:, hs].astype(BF16), k[:, hs]) * scale).astype(BF16)
        b_ref[ms, :] = _dot(v[:, hs], wo_ref[hs, :].astype(BF16)).astype(BF16)


def _kv_proj(mem, wk, wv, wq, wo):
    n = mem.shape[0]
    return pl.pallas_call(
        _kv_kernel,
        out_shape=(jax.ShapeDtypeStruct((D_MODEL, XATTN_HEADS * n), BF16),
                   jax.ShapeDtypeStruct((XATTN_HEADS * n, D_MODEL), BF16)),
        compiler_params=pltpu.CompilerParams(vmem_limit_bytes=VMEM_LIMIT),
        name="kv_proj",
    )(mem, wk, wv, wq, wo)


def _split3(a):
    a0 = a.astype(BF16)
    r1 = a - a0.astype(F32)
    a1 = r1.astype(BF16)
    a2 = (r1 - a1.astype(F32)).astype(BF16)
    return a0, a1, a2


def _xattn_kernel(x_ref, a_ref, b_ref, lnw_ref, lnb_ref, wrt_ref, br_ref, ut_ref,
                  x2_ref, dest_ref, gate_ref, cnt_ref, att_ref, run_ref, *, shift):
    i = pl.program_id(0)
    c = x_ref.shape[0]
    n = N_EXPERTS

    @pl.when(i == 0)
    def _():
        run_ref[...] = jnp.zeros_like(run_ref)

    x = x_ref[...]
    scores = _dot(x.astype(BF16), a_ref[...])
    n_mem = a_ref.shape[1] // XATTN_HEADS
    for h in range(XATTN_HEADS):
        ms = slice(h * n_mem, (h + 1) * n_mem)
        s = scores[:, ms]
        e = jnp.exp(s - jnp.max(s, axis=-1, keepdims=True))
        att_ref[:, ms] = (e / jnp.sum(e, axis=-1, keepdims=True)).astype(BF16)
    y = DEEPNORM_ALPHA * x + _dot(att_ref[...], b_ref[...])
    x2 = _layer_norm(y, lnw_ref[...], lnb_ref[...])
    _store_row_tiles(x2_ref, x2)

    a0, a1, _ = _split3(x2)
    w0, w1, w2 = _split3(wrt_ref[...])
    pa = _dot_nt(jnp.concatenate([w0, w1, w2], axis=0), a0)
    pb = _dot_nt(jnp.concatenate([w0, w1], axis=0), a1)
    logits = ((pa[2 * n:3 * n] + pb[n:2 * n]) + (pb[0:n] + pa[n:2 * n]) + pa[0:n]) + br_ref[...]

    eid = lax.broadcasted_iota(jnp.int32, (n, c), 0)
    work = logits
    sels, vals, idxs = [], [], []
    for k in range(TOP_K):
        m = jnp.max(work, axis=0, keepdims=True)
        idx = jnp.min(jnp.where(work == m, eid, n), axis=0, keepdims=True)
        sel = eid == idx
        sels.append(sel)
        vals.append(m)
        idxs.append(idx)
        work = jnp.where(sel, -jnp.inf, work)
    es = [jnp.exp(vk - vals[0]) for vk in vals]
    denom = es[0] + es[1] + es[2] + es[3]
    gate_ref[0] = jnp.concatenate([ek / denom for ek in es], axis=0)

    onehot = jnp.zeros((n, c), F32)
    for sel in sels:
        onehot = jnp.where(sel, 1.0, onehot)
    before = _dot(onehot.astype(BF16), ut_ref[...]) + run_ref[...]
    ranks = [jnp.sum(jnp.where(sel, before, 0.0), axis=0, keepdims=True).astype(jnp.int32) for sel in sels]
    run_ref[...] = run_ref[...] + jnp.sum(onehot, axis=1, keepdims=True)

    dest_ref[0] = jnp.concatenate([idx * (1 << shift) + rk for idx, rk in zip(idxs, ranks)], axis=0)
    cnt_ref[...] = run_ref[...].astype(jnp.int32)


def _xattn_router(x1, score_w, out_w, ln_w, ln_b, w_r, b_r):
    t = x1.shape[0]
    c = XATTN_BLOCK if t % XATTN_BLOCK == 0 else SEQ_BLOCK
    idx = np.arange(c)
    strict_upper = (idx[:, None] < idx[None, :]).astype(np.float32)
    full = lambda shape: pl.BlockSpec(shape, lambda i: (0,) * len(shape))
    per_k = pl.BlockSpec((1, TOP_K, c), lambda i: (i, 0, 0))
    return pl.pallas_call(
        functools.partial(_xattn_kernel, shift=_rank_bits(t)),
        grid=(t // c,),
        in_specs=[
            pl.BlockSpec((c, D_MODEL), lambda i: (i, 0)),
            full(score_w.shape),
            full(out_w.shape),
            full((1, D_MODEL)),
            full((1, D_MODEL)),
            full((N_EXPERTS, D_MODEL)),
            full((N_EXPERTS, 1)),
            full((c, c)),
        ],
        out_specs=[_tiled_spec(c, lambda i: (i, 0)), per_k, per_k, full((N_EXPERTS, 1))],
        out_shape=[
            jax.ShapeDtypeStruct(_tiled(t), F32),
            jax.ShapeDtypeStruct((t // c, TOP_K, c), jnp.int32),
            jax.ShapeDtypeStruct((t // c, TOP_K, c), F32),
            jax.ShapeDtypeStruct((N_EXPERTS, 1), jnp.int32),
        ],
        scratch_shapes=[pltpu.VMEM((c, score_w.shape[1]), BF16), pltpu.VMEM((N_EXPERTS, 1), F32)],
        compiler_params=pltpu.CompilerParams(dimension_semantics=("arbitrary",), vmem_limit_bytes=VMEM_LIMIT),
        name="xattn_router",
    )(x1, score_w, out_w, ln_w.reshape(1, -1), ln_b.reshape(1, -1),
      w_r.T, b_r.reshape(-1, 1), jnp.asarray(strict_upper, BF16))


TAIL_RUNS = (128, 64, 32, 16, 8, 4, 2, 1)


def _dispatch_kernel(fill_ref, pend_ref, nused_ref, slot_ref, x_ref, xs_hbm, zero_ref, sem, zsem, *, n_blocks):
    i = pl.program_id(0)
    c = SEQ_BLOCK

    @pl.when(i == 0)
    def _():
        zero_ref[...] = jnp.zeros_like(zero_ref)

        sub = ROW_TILE[0]

        def rows(ref, first, n):
            return ref.at[pl.ds(pl.multiple_of(first * sub, sub), n * sub)]

        def tail(e, act):
            row = fill_ref[e]
            pad = pend_ref[e] - row
            for run in TAIL_RUNS:
                hit = (pad & run) != 0

                @pl.when(hit)
                def _():
                    act(pltpu.make_async_copy(rows(zero_ref, 0, run), rows(xs_hbm, row, run), zsem))
                row = row + jnp.where(hit, run, 0)

        def spare(b, act):
            act(pltpu.make_async_copy(zero_ref, rows(xs_hbm, b * MOE_BLOCK, MOE_BLOCK), zsem))

        for act in (lambda cp: cp.start(), lambda cp: cp.wait()):
            lax.fori_loop(0, N_EXPERTS, lambda e, carry: (tail(e, act), carry)[1], 0)
            lax.fori_loop(nused_ref[0], n_blocks, lambda b, carry: (spare(b, act), carry)[1], 0)

    for q in range(x_ref.shape[0] // (c * ROW_TILE[0])):
        def issue(j, carry, q=q):
            for u in range(ISSUE_UNROLL):
                r = j * ISSUE_UNROLL + u
                for k in range(TOP_K):
                    dst = _tile_of(xs_hbm, slot_ref[0, 0, (q * TOP_K + k) * c + r])
                    pltpu.make_async_copy(_tile_of(x_ref, q * c + r), dst, sem).start(priority=k % 2)
            return carry
        lax.fori_loop(0, c // ISSUE_UNROLL, issue, 0)
    for k in range(TOP_K):
        pltpu.make_async_copy(x_ref, xs_hbm.at[pl.ds(0, x_ref.shape[0])], sem).wait()


def _dispatch(x2, slots, fill, pend, n_used, n_blocks):
    t = x2.shape[0] // ROW_TILE[0]
    c = min(DISPATCH_BLOCK, t)
    assert t % c == 0 and c % SEQ_BLOCK == 0
    grid_spec = pltpu.PrefetchScalarGridSpec(
        num_scalar_prefetch=3,
        grid=(t // c,),
        in_specs=[
            pl.BlockSpec((1, 1, c * TOP_K), lambda i, *_: (i, 0, 0), memory_space=pltpu.SMEM),
            _tiled_spec(c, lambda i, *_: (i, 0)),
        ],
        out_specs=pl.BlockSpec(memory_space=pl.ANY),
        scratch_shapes=[pltpu.VMEM(_tiled(MOE_BLOCK), F32), pltpu.SemaphoreType.DMA, pltpu.SemaphoreType.DMA],
    )
    return pl.pallas_call(
        functools.partial(_dispatch_kernel, n_blocks=n_blocks),
        grid_spec=grid_spec,
        out_shape=jax.ShapeDtypeStruct(_tiled(n_blocks * MOE_BLOCK), F32),
        compiler_params=pltpu.CompilerParams(dimension_semantics=("arbitrary",), vmem_limit_bytes=VMEM_LIMIT),
        name="dispatch",
    )(fill, pend, n_used, slots.reshape(t // c, 1, c * TOP_K), x2)


def _expert_kernel(be_ref, first_ref, nxt_ref, par_ref, fused_ref, nused_ref, x_ref, w1_hbm, b1_ref, w2_hbm, b2_ref, o_ref,
                   w1f, w2f, w1b, w2b, sem1, sem2):
    step = pl.program_id(0)
    rows = MOE_BLOCK * ROW_TILE[0]

    def fetch(e, p):
        return (pltpu.make_async_copy(w1_hbm.at[e], w1f.at[p], sem1.at[p]),
                pltpu.make_async_copy(w2_hbm.at[e], w2f.at[p], sem2.at[p]))

    @pl.when(step == 0)
    def _():
        for cp in fetch(be_ref[0], par_ref[0]):
            cp.start()

    def window(first_block, n):
        at = pl.ds(pl.multiple_of(first_block * rows, rows), n * rows)
        return x_ref.at[at], o_ref.at[at]

    def load_weights(b):
        @pl.when(first_ref[b] == 1)
        def _():
            p = par_ref[b]
            for cp in fetch(be_ref[b], p):
                cp.wait()

            @pl.when(nxt_ref[b] >= 0)
            def _():
                for cp in fetch(nxt_ref[b], 1 - p):
                    cp.start()

            w1b[...] = w1f[p].astype(BF16)
            w2b[...] = w2f[p].astype(BF16)

    def mlp(x_rows, o_rows, e):
        h = _dot(_load_row_tiles(x_rows).astype(BF16), w1b[...]) + b1_ref[e]
        h_glu = jnp.minimum(h[:, :D_EXPERT], SWIGLU_LIMIT)
        h_lin = jnp.clip(h[:, D_EXPERT:], -SWIGLU_LIMIT, SWIGLU_LIMIT)
        act = h_glu * (1.0 / (1.0 + jnp.exp(-SWIGLU_ALPHA * h_glu))) * (h_lin + 1.0)
        _store_row_tiles(o_rows, _dot(act.astype(BF16), w2b[...]) + b2_ref[e])

    def single(sb, carry):
        b = step * EXPERT_STEP_BLOCKS + sb
        x_blk, o_blk = window(sb, 1)

        @pl.when(b >= nused_ref[0])
        def _():
            o_blk[...] = jnp.zeros(o_blk.shape, o_blk.dtype)

        @pl.when(b < nused_ref[0])
        def _():
            load_weights(b)
            mlp(x_blk, o_blk, be_ref[b])
        return carry

    def pair(jp, carry):
        b = step * EXPERT_STEP_BLOCKS + 2 * jp

        @pl.when(fused_ref[lax.shift_right_logical(b, 1)] == 1)
        def _():
            load_weights(b)
            x_two, o_two = window(2 * jp, 2)
            mlp(x_two, o_two, be_ref[b])

        @pl.when(fused_ref[lax.shift_right_logical(b, 1)] == 0)
        def _():
            lax.fori_loop(2 * jp, 2 * jp + 2, single, 0)
        return carry

    first_pair = step * (EXPERT_STEP_BLOCKS // 2)
    whole = (fused_ref[first_pair] == 1) & (fused_ref[first_pair + 1] == 1) & \
        (be_ref[step * EXPERT_STEP_BLOCKS] == be_ref[step * EXPERT_STEP_BLOCKS + EXPERT_STEP_BLOCKS - 1])

    @pl.when(whole)
    def _():
        load_weights(step * EXPERT_STEP_BLOCKS)
        mlp(x_ref, o_ref, be_ref[step * EXPERT_STEP_BLOCKS])

    @pl.when(jnp.logical_not(whole))
    def _():
        lax.fori_loop(0, EXPERT_STEP_BLOCKS // 2, pair, 0)


def _experts(xs, plan, w1, b1, w2, b2):
    block_e, first, nxt, par, fused, n_used = plan
    n_blocks = block_e.shape[0]
    assert n_blocks % EXPERT_STEP_BLOCKS == 0 and EXPERT_STEP_BLOCKS % 2 == 0
    rows = lambda s, *_: (s, 0)
    whole = lambda s, *_: (0, 0, 0)
    grid_spec = pltpu.PrefetchScalarGridSpec(
        num_scalar_prefetch=6,
        grid=(n_blocks // EXPERT_STEP_BLOCKS,),
        in_specs=[
            _tiled_spec(EXPERT_STEP_BLOCKS * MOE_BLOCK, rows),
            pl.BlockSpec(memory_space=pl.ANY),
            pl.BlockSpec((N_EXPERTS, 1, 2 * D_EXPERT), whole),
            pl.BlockSpec(memory_space=pl.ANY),
            pl.BlockSpec((N_EXPERTS, 1, D_MODEL), whole),
        ],
        out_specs=_tiled_spec(EXPERT_STEP_BLOCKS * MOE_BLOCK, rows),
        scratch_shapes=[
            pltpu.VMEM((2, D_MODEL, 2 * D_EXPERT), F32),
            pltpu.VMEM((2, D_EXPERT, D_MODEL), F32),
            pltpu.VMEM((D_MODEL, 2 * D_EXPERT), BF16),
            pltpu.VMEM((D_EXPERT, D_MODEL), BF16),
            pltpu.SemaphoreType.DMA((2,)),
            pltpu.SemaphoreType.DMA((2,)),
        ],
    )
    return pl.pallas_call(
        _expert_kernel,
        grid_spec=grid_spec,
        out_shape=jax.ShapeDtypeStruct(xs.shape, F32),
        compiler_params=pltpu.CompilerParams(dimension_semantics=("arbitrary",), vmem_limit_bytes=EXPERT_VMEM_LIMIT),
        name="experts",
    )(block_e, first, nxt, par, fused, n_used, xs, w1, b1.reshape(N_EXPERTS, 1, -1), w2, b2.reshape(N_EXPERTS, 1, -1))


def _combine_kernel(slot_ref, next_slot_ref, ys_hbm, gate_ref, x_ref, lnw_ref, lnb_ref, o_ref, ybuf, sem):
    i = pl.program_id(0)
    c = SEQ_BLOCK

    def gather(slots, p):
        def issue(j, carry):
            for u in range(GATHER_UNROLL):
                r = j * GATHER_UNROLL + u
                for k in range(TOP_K):
                    src = _tile_of(ys_hbm, slots[0, 0, k * c + r])
                    pltpu.make_async_copy(src, _tile_of(ybuf.at[p, k], r), sem.at[p]).start(priority=k % 2)
            return carry
        lax.fori_loop(0, c // GATHER_UNROLL, issue, 0)

    @pl.when(i == 0)
    def _():
        gather(slot_ref, 0)

    @pl.when(i + 1 < pl.num_programs(0))
    def _():
        gather(next_slot_ref, (i + 1) % 2)

    p = i % 2
    for k in range(TOP_K):
        pltpu.make_async_copy(ys_hbm.at[pl.ds(0, ybuf.shape[2])], ybuf.at[p, k], sem.at[p]).wait()
    gates = gate_ref[...]
    ys = [_load_row_tiles(ybuf.at[p, k]) * gates[:, k:k + 1] for k in range(TOP_K)]
    y = (ys[0] + ys[1]) + (ys[2] + ys[3])
    o_ref[...] = _layer_norm(DEEPNORM_ALPHA * _load_row_tiles(x_ref) + y, lnw_ref[...], lnb_ref[...])


def _combine(ys, slots, gates, x2, ln_w, ln_b):
    t = x2.shape[0] // ROW_TILE[0]
    c = SEQ_BLOCK
    n = t // c
    slot_table = slots.reshape(n, 1, c * TOP_K)
    return pl.pallas_call(
        _combine_kernel,
        grid=(n,),
        in_specs=[
            pl.BlockSpec((1, 1, c * TOP_K), lambda i: (i, 0, 0), memory_space=pltpu.SMEM),
            pl.BlockSpec((1, 1, c * TOP_K), lambda i: (jnp.minimum(i + 1, n - 1), 0, 0), memory_space=pltpu.SMEM),
            pl.BlockSpec(memory_space=pl.ANY),
            pl.BlockSpec((c, TOP_K), lambda i: (i, 0)),
            _tiled_spec(c, lambda i: (i, 0)),
            pl.BlockSpec((1, D_MODEL), lambda i: (0, 0)),
            pl.BlockSpec((1, D_MODEL), lambda i: (0, 0)),
        ],
        out_specs=pl.BlockSpec((c, D_MODEL), lambda i: (i, 0)),
        out_shape=jax.ShapeDtypeStruct((t, D_MODEL), F32),
        scratch_shapes=[pltpu.VMEM((2, TOP_K) + _tiled(c), F32), pltpu.SemaphoreType.DMA((2,))],
        compiler_params=pltpu.CompilerParams(dimension_semantics=("arbitrary",), vmem_limit_bytes=VMEM_LIMIT),
        name="combine",
    )(slot_table, slot_table, ys, gates, x2, ln_w.reshape(1, -1), ln_b.reshape(1, -1))


def _slot_tables(counts, packed, n_blocks, shift):
    counts = counts.reshape(-1)
    nblk = (counts + MOE_BLOCK - 1) // MOE_BLOCK
    ids = jnp.arange(N_EXPERTS)
    lower = ids[None, :] <= ids[:, None]
    bend = jnp.sum(jnp.where(lower, nblk[None, :], 0), axis=1)
    n_used = bend[N_EXPERTS - 1]
    pend = bend * MOE_BLOCK
    pstart = pend - nblk * MOE_BLOCK
    fill = pstart + counts
    order = jnp.sum(jnp.where(lower, (nblk > 0)[None, :], False), axis=1) - 1
    later = (ids[None, :] > ids[:, None]) & (nblk > 0)[None, :]
    nxt_e = jnp.min(jnp.where(later, ids[None, :], N_EXPERTS), axis=1)
    nxt_e = jnp.where(nxt_e == N_EXPERTS, -1, nxt_e)
    b = jnp.minimum(jnp.arange(n_blocks), n_used - 1)
    hit = bend[None, :] <= b[:, None]
    block_e = jnp.sum(hit.astype(jnp.int32), axis=1)
    of_block = lambda per_expert: jnp.sum(jnp.where(ids[None, :] == block_e[:, None], per_expert[None, :], 0), axis=1)
    first = (b == of_block(bend - nblk)).astype(jnp.int32)
    expert = lax.shift_right_logical(packed, shift)
    rank = packed & ((1 << shift) - 1)
    slots = jnp.sum(jnp.where(expert[..., None] == ids, pstart, 0), axis=-1) + rank
    i32 = lambda a: a.astype(jnp.int32)
    odd = 2 * jnp.arange(n_blocks // 2) + 1
    fused = (odd < n_used) & (block_e[0::2] == block_e[1::2])
    plan = (i32(block_e), first, i32(of_block(nxt_e)), i32(of_block(order & 1)), i32(fused), i32(n_used.reshape(1)))
    return i32(slots), i32(fill), i32(pend), plan


def kernel(x, mem, w_in, lb_logits, hgrn_norm_w, w_pool, pool_scale, w_out, ln1_w, ln1_b, w_xq, w_xk, w_xv, w_xo,
           ln2_w, ln2_b, w_router, b_router, w1, b1, w2, b2, ln3_w, ln3_b):
    bsz, seq, d = x.shape
    assert bsz == 1 and d == D_MODEL and seq % SEQ_BLOCK == 0 and w_in.shape[0] == 1
    xt = x.reshape(seq, d)
    x1 = _mixer(xt, w_in[0], lb_logits, hgrn_norm_w[0], w_pool[0], pool_scale[0], w_out[0], ln1_w[0], ln1_b[0])
    score_w, out_w = _kv_proj(mem[0], w_xk[0], w_xv[0], w_xq[0], w_xo[0])
    x2, packed, gates, counts = _xattn_router(x1, score_w, out_w, ln2_w[0], ln2_b[0], w_router[0], b_router[0])
    n_blocks = -(-(seq * TOP_K + N_EXPERTS * (MOE_BLOCK - 1)) // MOE_BLOCK)
    slots, fill, pend, plan = _slot_tables(counts, packed, n_blocks, _rank_bits(seq))
    nb, _, cx = slots.shape
    slots = slots.reshape(nb, TOP_K, cx // SEQ_BLOCK, SEQ_BLOCK).transpose(0, 2, 1, 3).reshape(-1, TOP_K, SEQ_BLOCK)
    xs = _dispatch(x2, slots, fill, pend, plan[-1], n_blocks)
    ys = _experts(xs, plan, w1[0], b1[0], w2[0], b2[0])
    gates_tk = gates.transpose(0, 2, 1).reshape(seq, TOP_K)
    out = _combine(ys, slots, gates_tk, x2, ln3_w[0], ln3_b[0])
    return out.reshape(bsz, seq, d)
```

```python
import functools

import numpy as np
import jax
import jax.numpy as jnp
from jax import lax
from jax.experimental import pallas as pl
from jax.experimental.pallas import tpu as pltpu

F32 = jnp.float32
BF16 = jnp.bfloat16

D_MODEL = 1024
HGRN_WIDTH = 512
HGRN_HEADS = 4
HEAD_DIM = 128
POOL_WINDOWS = (2, 4, 8, 16)
POOL_GROUP = 128
POOL_WIDTH = POOL_GROUP * len(POOL_WINDOWS)
POOL_CARRY = 24
POOL_HEAD = 8
IN_PROJ_WIDTH = 4 * HGRN_WIDTH + POOL_WIDTH
XATTN_HEADS = 4
XATTN_HEAD_DIM = 256
N_EXPERTS = 32
TOP_K = 4
D_EXPERT = 1024
SWIGLU_LIMIT = 7.0
SWIGLU_ALPHA = 1.702
MOE_BLOCK = 256
DEEPNORM_ALPHA = 2.0 ** 0.25
LN_EPS = 1e-5
RMS_EPS = 1e-6

SEQ_BLOCK = 256
EXPERT_STEP_BLOCKS = 4
XATTN_BLOCK = 1024
DISPATCH_BLOCK = 2048
N_LEVELS = 8
MXU_LEVELS = 3
ISSUE_UNROLL = 8
GATHER_UNROLL = 16
ROW_TILE = (8, 128)
VMEM_LIMIT = 48 * 1024 * 1024
EXPERT_VMEM_LIMIT = 56 * 1024 * 1024


def _rank_bits(t):
    return max(int(t - 1).bit_length(), 1)


def _layer_norm(y, w, b):
    mu = jnp.mean(y, axis=-1, keepdims=True)
    yc = y - mu
    var = jnp.mean(yc * yc, axis=-1, keepdims=True)
    return yc * lax.rsqrt(var + LN_EPS) * w + b


def _tiled(n):
    return (n * ROW_TILE[0], ROW_TILE[1])


def _tiled_spec(n, index_map):
    return pl.BlockSpec(_tiled(n), index_map)


def _tile_of(ref, r):
    return ref.at[pl.ds(pl.multiple_of(r * ROW_TILE[0], ROW_TILE[0]), ROW_TILE[0])]


def _store_row_tiles(ref, val):
    s, l = ROW_TILE
    for j in range(s):
        ref[pl.ds(j, val.shape[0], stride=s), :] = val[:, j * l:(j + 1) * l]


def _load_row_tiles(ref):
    s, _ = ROW_TILE
    return jnp.concatenate([ref[pl.ds(j, ref.shape[0] // s, stride=s), :] for j in range(s)], axis=-1)


def _dot(a, b):
    return jnp.dot(a, b, preferred_element_type=F32)


def _dot_nt(a, b):
    return lax.dot_general(a, b, (((1,), (1,)), ((), ())), preferred_element_type=F32)


def _decay_tables():
    c = SEQ_BLOCK
    r = np.arange(c)[:, None]
    u = np.arange(c)[None, :]
    mats = [(u <= r)]
    for l in range(MXU_LEVELS):
        h = 1 << l
        mid = (r // (2 * h)) * (2 * h) + h
        upper = (r >= mid) & (u >= mid) & (u <= r)
        lower = (r < mid) & (u > r) & (u < mid)
        mats.append(upper | lower)
    mall = np.concatenate(mats, axis=0).astype(np.float32)
    x = r ^ u
    lvl = np.where(u < r, np.floor(np.log2(np.maximum(x, 1))).astype(np.int32), -1).astype(np.int32)
    strict_lower = (u < r).astype(np.float32)
    return mall, lvl, strict_lower


def _upper_chunks(l, c):
    half, sub = 1 << l, ROW_TILE[0]
    return [r // sub for base in range(0, c, 2 * half) for r in range(base + half, base + 2 * half, sub)]


def _level_gap(negb, l):
    half = 1 << l
    parts = []
    for base in range(0, negb.shape[0], 2 * half):
        ref_row = negb[base + half - 1:base + half, :]
        parts.append(jnp.abs(negb[base:base + 2 * half, :] - ref_row))
    return jnp.concatenate(parts, axis=0) if len(parts) > 1 else parts[0]


def _mixer_kernel(x_ref, win_ref, lbl_ref, nw_ref, wpool_ref, pscale_ref, wout_ref, lnw_ref, lnb_ref,
                  mall_ref, lvl_ref, o_ref, state_ref, carry_ref, ext_ref, mix_ref, sum_ref, winb_ref, woutb_ref):
    i = pl.program_id(0)
    c = SEQ_BLOCK

    @pl.when(i == 0)
    def _():
        state_ref[...] = jnp.zeros_like(state_ref)
        carry_ref[...] = jnp.zeros_like(carry_ref)
        sum_ref[:, 0:POOL_HEAD, :] = jnp.zeros((2, POOL_HEAD, POOL_WIDTH), F32)
        for r in range(0, D_MODEL, c):
            winb_ref[r:r + c, :] = win_ref[r:r + c, :].astype(BF16)
            woutb_ref[r:r + c, :] = wout_ref[r:r + c, :].astype(BF16)

    x = x_ref[...]
    proj = _dot(x.astype(BF16), winb_ref[...])
    w = HGRN_WIDTH
    q = proj[:, 0:w]
    z = proj[:, w:2 * w]
    v = proj[:, 2 * w:3 * w]
    g = proj[:, 3 * w:4 * w]
    p = proj[:, 4 * w:]

    ll = lbl_ref[...]
    ee = jnp.exp(ll - jnp.max(ll, axis=0, keepdims=True))
    lb = ee[0:1] / jnp.sum(ee, axis=0, keepdims=True)

    f = lb + (1.0 - lb) * (1.0 / (1.0 + jnp.exp(-z)))
    nlf = -jnp.log(f)
    key = (1.0 - lb) * (1.0 / (1.0 + jnp.exp(z)))

    hi = nlf.astype(BF16)
    lo = (nlf - hi.astype(F32)).astype(BF16)
    mall = mall_ref[...]
    cum = _dot(mall, hi) + _dot(mall, lo)
    sub = ROW_TILE[0]
    lvl_rows = [lvl_ref[sub * r:sub * (r + 1), :] for r in range(c // sub)]

    for h in range(HGRN_HEADS):
        hs = slice(h * HEAD_DIM, (h + 1) * HEAD_DIM)
        qh, kh, vh, gh = q[:, hs], key[:, hs], v[:, hs], g[:, hs]
        vb = vh.astype(BF16)
        negb = cum[0:c, hs]
        sc = [jnp.zeros((sub, c), F32)] * (c // sub)
        for l in range(N_LEVELS):
            gap = cum[(l + 1) * c:(l + 2) * c, hs] if l < MXU_LEVELS else _level_gap(negb, l)
            e = jnp.exp(-gap)
            ke = (kh * e).astype(BF16)
            if l < MXU_LEVELS:
                rows = list(range(c // sub))
                qe = (qh * e).astype(BF16)
            else:
                rows = _upper_chunks(l, c)
                take = lambda a: jnp.concatenate([a[sub * r:sub * (r + 1)] for r in rows], axis=0)
                qe = (take(qh) * take(e)).astype(BF16)
            pl_ = _dot_nt(qe, ke)
            for j, r in enumerate(rows):
                sc[r] = jnp.where(lvl_rows[r] == l, pl_[sub * j:sub * (j + 1)], sc[r])
        scores = jnp.concatenate(sc, axis=0)
        diag = jnp.sum(qh * kh, axis=-1, keepdims=True)
        st = state_ref[h]
        o = (_dot(scores.astype(BF16), vb) + diag * vh
             + _dot_nt((qh * jnp.exp(-negb)).astype(BF16), st.astype(BF16)))
        negb_last = negb[c - 1:c, :]
        kd = (kh * jnp.exp(negb - negb_last)).astype(BF16)
        state_ref[h] = st * jnp.exp(-negb_last) + _dot(vh.T.astype(BF16), kd)
        o = o * lax.rsqrt(jnp.mean(o * o, axis=-1, keepdims=True) + RMS_EPS) * nw_ref[:, hs]
        gate = gh * (1.0 / (1.0 + jnp.exp(-gh)))
        mix_ref[:, hs] = (o * gate).astype(BF16)

    top = POOL_CARRY + c
    ext_ref[0:POOL_CARRY, :] = carry_ref[...]
    ext_ref[POOL_CARRY:top, :] = p
    carry_ref[...] = p[c - POOL_CARRY:c, :]
    pos = i * c + lax.broadcasted_iota(jnp.int32, (c, 1), 0)
    src = ext_ref
    for gi, win in enumerate(POOL_WINDOWS):
        gs = slice(gi * POOL_GROUP, (gi + 1) * POOL_GROUP)
        rest = slice(gi * POOL_GROUP, POOL_WIDTH)
        half = win // 2
        dst = sum_ref.at[gi % 2]
        dst[POOL_HEAD:top, rest] = src[POOL_HEAD:top, rest] + src[POOL_HEAD - half:top - half, rest]
        src = dst
        pg = p[:, gs]
        tot = dst[POOL_CARRY:top, gs]
        cnt = jnp.minimum(pos + 1, win).astype(F32)
        pooled = _dot((tot / cnt - pg).astype(BF16), wpool_ref[gi])
        mix_ref[:, w + gi * POOL_GROUP:w + (gi + 1) * POOL_GROUP] = (pooled * pscale_ref[:, gs]).astype(BF16)

    y = DEEPNORM_ALPHA * x + _dot(mix_ref[...], woutb_ref[...])
    o_ref[...] = _layer_norm(y, lnw_ref[...], lnb_ref[...])


def _mixer(x, w_in, lb_logits, norm_w, w_pool, pool_scale, w_out, ln_w, ln_b):
    t = x.shape[0]
    c = SEQ_BLOCK
    mall, lvl, _ = _decay_tables()
    full = lambda shape: pl.BlockSpec(shape, lambda i: (0,) * len(shape))
    return pl.pallas_call(
        _mixer_kernel,
        grid=(t // c,),
        in_specs=[
            pl.BlockSpec((c, D_MODEL), lambda i: (i, 0)),
            full((D_MODEL, IN_PROJ_WIDTH)),
            full(lb_logits.shape),
            full((1, HGRN_WIDTH)),
            full((len(POOL_WINDOWS), POOL_GROUP, POOL_GROUP)),
            full((1, POOL_WIDTH)),
            full((D_MODEL, D_MODEL)),
            full((1, D_MODEL)),
            full((1, D_MODEL)),
            full(mall.shape),
            full(lvl.shape),
        ],
        out_specs=pl.BlockSpec((c, D_MODEL), lambda i: (i, 0)),
        out_shape=jax.ShapeDtypeStruct((t, D_MODEL), F32),
        scratch_shapes=[
            pltpu.VMEM((HGRN_HEADS, HEAD_DIM, HEAD_DIM), F32),
            pltpu.VMEM((POOL_CARRY, POOL_WIDTH), F32),
            pltpu.VMEM((POOL_CARRY + c, POOL_WIDTH), F32),
            pltpu.VMEM((c, D_MODEL), BF16),
            pltpu.VMEM((2, POOL_CARRY + c, POOL_WIDTH), F32),
            pltpu.VMEM((D_MODEL, IN_PROJ_WIDTH), BF16),
            pltpu.VMEM((D_MODEL, D_MODEL), BF16),
        ],
        compiler_params=pltpu.CompilerParams(dimension_semantics=("arbitrary",), vmem_limit_bytes=EXPERT_VMEM_LIMIT),
        name="mixer",
    )(x, w_in, lb_logits, norm_w.reshape(1, -1), w_pool.astype(BF16), pool_scale.reshape(1, -1),
      w_out, ln_w.reshape(1, -1), ln_b.reshape(1, -1), jnp.asarray(mall, BF16), jnp.asarray(lvl))


def _kv_kernel(mem_ref, wk_ref, wv_ref, wq_ref, wo_ref, a_ref, b_ref):
    n_mem = mem_ref.shape[0]
    m = mem_ref[...].astype(BF16)
    k = _dot(m, wk_ref[...].astype(BF16)).astype(BF16)
    v = _dot(m, wv_ref[...].astype(BF16)).astype(BF16)
    scale = XATTN_HEAD_DIM ** -0.5
    for h in range(XATTN_HEADS):
        hs = slice(h * XATTN_HEAD_DIM, (h + 1) * XATTN_HEAD_DIM)
        ms = slice(h * n_mem, (h + 1) * n_mem)
        a_ref[:, ms] = (_dot_nt(wq_ref[:, hs].astype(BF16), k[:, hs]) * scale).astype(BF16)
        b_ref[ms, :] = _dot(v[:, hs], wo_ref[hs, :].astype(BF16)).astype(BF16)


def _kv_proj(mem, wk, wv, wq, wo):
    n = mem.shape[0]
    return pl.pallas_call(
        _kv_kernel,
        out_shape=(jax.ShapeDtypeStruct((D_MODEL, XATTN_HEADS * n), BF16),
                   jax.ShapeDtypeStruct((XATTN_HEADS * n, D_MODEL), BF16)),
        compiler_params=pltpu.CompilerParams(vmem_limit_bytes=VMEM_LIMIT),
        name="kv_proj",
    )(mem, wk, wv, wq, wo)


def _split3(a):
    a0 = a.astype(BF16)
    r1 = a - a0.astype(F32)
    a1 = r1.astype(BF16)
    a2 = (r1 - a1.astype(F32)).astype(BF16)
    return a0, a1, a2


def _xattn_kernel(x_ref, a_ref, b_ref, lnw_ref, lnb_ref, wrt_ref, br_ref, ut_ref,
                  x2_ref, dest_ref, gate_ref, cnt_ref, att_ref, run_ref, *, shift):
    i = pl.program_id(0)
    c = x_ref.shape[0]
    n = N_EXPERTS

    @pl.when(i == 0)
    def _():
        run_ref[...] = jnp.zeros_like(run_ref)

    x = x_ref[...]
    scores = _dot(x.astype(BF16), a_ref[...])
    n_mem = a_ref.shape[1] // XATTN_HEADS
    for h in range(XATTN_HEADS):
        ms = slice(h * n_mem, (h + 1) * n_mem)
        s = scores[:, ms]
        e = jnp.exp(s - jnp.max(s, axis=-1, keepdims=True))
        att_ref[:, ms] = (e / jnp.sum(e, axis=-1, keepdims=True)).astype(BF16)
    y = DEEPNORM_ALPHA * x + _dot(att_ref[...], b_ref[...])
    x2 = _layer_norm(y, lnw_ref[...], lnb_ref[...])
    _store_row_tiles(x2_ref, x2)

    a0, a1, _ = _split3(x2)
    w0, w1, w2 = _split3(wrt_ref[...])
    pa = _dot_nt(jnp.concatenate([w0, w1, w2], axis=0), a0)
    pb = _dot_nt(jnp.concatenate([w0, w1], axis=0), a1)
    logits = ((pa[2 * n:3 * n] + pb[n:2 * n]) + (pb[0:n] + pa[n:2 * n]) + pa[0:n]) + br_ref[...]

    eid = lax.broadcasted_iota(jnp.int32, (n, c), 0)
    work = logits
    sels, vals, idxs = [], [], []
    for k in range(TOP_K):
        m = jnp.max(work, axis=0, keepdims=True)
        idx = jnp.min(jnp.where(work == m, eid, n), axis=0, keepdims=True)
        sel = eid == idx
        sels.append(sel)
        vals.append(m)
        idxs.append(idx)
        work = jnp.where(sel, -jnp.inf, work)
    es = [jnp.exp(vk - vals[0]) for vk in vals]
    denom = es[0] + es[1] + es[2] + es[3]
    gate_ref[0] = jnp.concatenate([ek / denom for ek in es], axis=0)

    onehot = jnp.zeros((n, c), F32)
    for sel in sels:
        onehot = jnp.where(sel, 1.0, onehot)
    before = _dot(onehot.astype(BF16), ut_ref[...]) + run_ref[...]
    ranks = [jnp.sum(jnp.where(sel, before, 0.0), axis=0, keepdims=True).astype(jnp.int32) for sel in sels]
    run_ref[...] = run_ref[...] + jnp.sum(onehot, axis=1, keepdims=True)

    dest_ref[0] = jnp.concatenate([idx * (1 << shift) + rk for idx, rk in zip(idxs, ranks)], axis=0)
    cnt_ref[...] = run_ref[...].astype(jnp.int32)


def _xattn_router(x1, score_w, out_w, ln_w, ln_b, w_r, b_r):
    t = x1.shape[0]
    c = XATTN_BLOCK if t % XATTN_BLOCK == 0 else SEQ_BLOCK
    idx = np.arange(c)
    strict_upper = (idx[:, None] < idx[None, :]).astype(np.float32)
    full = lambda shape: pl.BlockSpec(shape, lambda i: (0,) * len(shape))
    per_k = pl.BlockSpec((1, TOP_K, c), lambda i: (i, 0, 0))
    return pl.pallas_call(
        functools.partial(_xattn_kernel, shift=_rank_bits(t)),
        grid=(t // c,),
        in_specs=[
            pl.BlockSpec((c, D_MODEL), lambda i: (i, 0)),
            full(score_w.shape),
            full(out_w.shape),
            full((1, D_MODEL)),
            full((1, D_MODEL)),
            full((N_EXPERTS, D_MODEL)),
            full((N_EXPERTS, 1)),
            full((c, c)),
        ],
        out_specs=[_tiled_spec(c, lambda i: (i, 0)), per_k, per_k, full((N_EXPERTS, 1))],
        out_shape=[
            jax.ShapeDtypeStruct(_tiled(t), F32),
            jax.ShapeDtypeStruct((t // c, TOP_K, c), jnp.int32),
            jax.ShapeDtypeStruct((t // c, TOP_K, c), F32),
            jax.ShapeDtypeStruct((N_EXPERTS, 1), jnp.int32),
        ],
        scratch_shapes=[pltpu.VMEM((c, score_w.shape[1]), BF16), pltpu.VMEM((N_EXPERTS, 1), F32)],
        compiler_params=pltpu.CompilerParams(dimension_semantics=("arbitrary",), vmem_limit_bytes=VMEM_LIMIT),
        name="xattn_router",
    )(x1, score_w, out_w, ln_w.reshape(1, -1), ln_b.reshape(1, -1),
      w_r.T, b_r.reshape(-1, 1), jnp.asarray(strict_upper, BF16))


TAIL_RUNS = (128, 64, 32, 16, 8, 4, 2, 1)


def _dispatch_kernel(fill_ref, pend_ref, nused_ref, slot_ref, x_ref, xs_hbm, zero_ref, sem, zsem, *, n_blocks):
    i = pl.program_id(0)
    c = SEQ_BLOCK

    @pl.when(i == 0)
    def _():
        zero_ref[...] = jnp.zeros_like(zero_ref)

        sub = ROW_TILE[0]

        def rows(ref, first, n):
            return ref.at[pl.ds(pl.multiple_of(first * sub, sub), n * sub)]

        def tail(e, act):
            row = fill_ref[e]
            pad = pend_ref[e] - row
            for run in TAIL_RUNS:
                hit = (pad & run) != 0

                @pl.when(hit)
                def _():
                    act(pltpu.make_async_copy(rows(zero_ref, 0, run), rows(xs_hbm, row, run), zsem))
                row = row + jnp.where(hit, run, 0)

        def spare(b, act):
            act(pltpu.make_async_copy(zero_ref, rows(xs_hbm, b * MOE_BLOCK, MOE_BLOCK), zsem))

        for act in (lambda cp: cp.start(), lambda cp: cp.wait()):
            lax.fori_loop(0, N_EXPERTS, lambda e, carry: (tail(e, act), carry)[1], 0)
            lax.fori_loop(nused_ref[0], n_blocks, lambda b, carry: (spare(b, act), carry)[1], 0)

    for q in range(x_ref.shape[0] // (c * ROW_TILE[0])):
        def issue(j, carry, q=q):
            for u in range(ISSUE_UNROLL):
                r = j * ISSUE_UNROLL + u
                for k in range(TOP_K):
                    dst = _tile_of(xs_hbm, slot_ref[0, 0, (q * TOP_K + k) * c + r])
                    pltpu.make_async_copy(_tile_of(x_ref, q * c + r), dst, sem).start(priority=k % 2)
            return carry
        lax.fori_loop(0, c // ISSUE_UNROLL, issue, 0)
    for k in range(TOP_K):
        pltpu.make_async_copy(x_ref, xs_hbm.at[pl.ds(0, x_ref.shape[0])], sem).wait()


def _dispatch(x2, slots, fill, pend, n_used, n_blocks):
    t = x2.shape[0] // ROW_TILE[0]
    c = min(DISPATCH_BLOCK, t)
    assert t % c == 0 and c % SEQ_BLOCK == 0
    grid_spec = pltpu.PrefetchScalarGridSpec(
        num_scalar_prefetch=3,
        grid=(t // c,),
        in_specs=[
            pl.BlockSpec((1, 1, c * TOP_K), lambda i, *_: (i, 0, 0), memory_space=pltpu.SMEM),
            _tiled_spec(c, lambda i, *_: (i, 0)),
        ],
        out_specs=pl.BlockSpec(memory_space=pl.ANY),
        scratch_shapes=[pltpu.VMEM(_tiled(MOE_BLOCK), F32), pltpu.SemaphoreType.DMA, pltpu.SemaphoreType.DMA],
    )
    return pl.pallas_call(
        functools.partial(_dispatch_kernel, n_blocks=n_blocks),
        grid_spec=grid_spec,
        out_shape=jax.ShapeDtypeStruct(_tiled(n_blocks * MOE_BLOCK), F32),
        compiler_params=pltpu.CompilerParams(dimension_semantics=("arbitrary",), vmem_limit_bytes=VMEM_LIMIT),
        name="dispatch",
    )(fill, pend, n_used, slots.reshape(t // c, 1, c * TOP_K), x2)


def _expert_kernel(be_ref, first_ref, nxt_ref, par_ref, fused_ref, nused_ref, x_ref, w1_hbm, b1_ref, w2_hbm, b2_ref, o_ref,
                   w1f, w2f, w1b, w2b, sem1, sem2):
    step = pl.program_id(0)
    rows = MOE_BLOCK * ROW_TILE[0]

    def fetch(e, p):
        return (pltpu.make_async_copy(w1_hbm.at[e], w1f.at[p], sem1.at[p]),
                pltpu.make_async_copy(w2_hbm.at[e], w2f.at[p], sem2.at[p]))

    @pl.when(step == 0)
    def _():
        for cp in fetch(be_ref[0], par_ref[0]):
            cp.start()

    def window(first_block, n):
        at = pl.ds(pl.multiple_of(first_block * rows, rows), n * rows)
        return x_ref.at[at], o_ref.at[at]

    def load_weights(b):
        @pl.when(first_ref[b] == 1)
        def _():
            p = par_ref[b]
            for cp in fetch(be_ref[b], p):
                cp.wait()

            @pl.when(nxt_ref[b] >= 0)
            def _():
                for cp in fetch(nxt_ref[b], 1 - p):
                    cp.start()

            w1b[...] = w1f[p].astype(BF16)
            w2b[...] = w2f[p].astype(BF16)

    def mlp(x_rows, o_rows, e):
        h = _dot(_load_row_tiles(x_rows).astype(BF16), w1b[...]) + b1_ref[e]
        h_glu = jnp.minimum(h[:, :D_EXPERT], SWIGLU_LIMIT)
        h_lin = jnp.clip(h[:, D_EXPERT:], -SWIGLU_LIMIT, SWIGLU_LIMIT)
        act = h_glu * (1.0 / (1.0 + jnp.exp(-SWIGLU_ALPHA * h_glu))) * (h_lin + 1.0)
        _store_row_tiles(o_rows, _dot(act.astype(BF16), w2b[...]) + b2_ref[e])

    def single(sb, carry):
        b = step * EXPERT_STEP_BLOCKS + sb
        x_blk, o_blk = window(sb, 1)

        @pl.when(b >= nused_ref[0])
        def _():
            o_blk[...] = jnp.zeros(o_blk.shape, o_blk.dtype)

        @pl.when(b < nused_ref[0])
        def _():
            load_weights(b)
            mlp(x_blk, o_blk, be_ref[b])
        return carry

    def pair(jp, carry):
        b = step * EXPERT_STEP_BLOCKS + 2 * jp

        @pl.when(fused_ref[lax.shift_right_logical(b, 1)] == 1)
        def _():
            load_weights(b)
            x_two, o_two = window(2 * jp, 2)
            mlp(x_two, o_two, be_ref[b])

        @pl.when(fused_ref[lax.shift_right_logical(b, 1)] == 0)
        def _():
            lax.fori_loop(2 * jp, 2 * jp + 2, single, 0)
        return carry

    first_pair = step * (EXPERT_STEP_BLOCKS // 2)
    whole = (fused_ref[first_pair] == 1) & (fused_ref[first_pair + 1] == 1) & \
        (be_ref[step * EXPERT_STEP_BLOCKS] == be_ref[step * EXPERT_STEP_BLOCKS + EXPERT_STEP_BLOCKS - 1])

    @pl.when(whole)
    def _():
        load_weights(step * EXPERT_STEP_BLOCKS)
        mlp(x_ref, o_ref, be_ref[step * EXPERT_STEP_BLOCKS])

    @pl.when(jnp.logical_not(whole))
    def _():
        lax.fori_loop(0, EXPERT_STEP_BLOCKS // 2, pair, 0)


def _experts(xs, plan, w1, b1, w2, b2):
    block_e, first, nxt, par, fused, n_used = plan
    n_blocks = block_e.shape[0]
    assert n_blocks % EXPERT_STEP_BLOCKS == 0 and EXPERT_STEP_BLOCKS % 2 == 0
    rows = lambda s, *_: (s, 0)
    whole = lambda s, *_: (0, 0, 0)
    grid_spec = pltpu.PrefetchScalarGridSpec(
        num_scalar_prefetch=6,
        grid=(n_blocks // EXPERT_STEP_BLOCKS,),
        in_specs=[
            _tiled_spec(EXPERT_STEP_BLOCKS * MOE_BLOCK, rows),
            pl.BlockSpec(memory_space=pl.ANY),
            pl.BlockSpec((N_EXPERTS, 1, 2 * D_EXPERT), whole),
            pl.BlockSpec(memory_space=pl.ANY),
            pl.BlockSpec((N_EXPERTS, 1, D_MODEL), whole),
        ],
        out_specs=_tiled_spec(EXPERT_STEP_BLOCKS * MOE_BLOCK, rows),
        scratch_shapes=[
            pltpu.VMEM((2, D_MODEL, 2 * D_EXPERT), F32),
            pltpu.VMEM((2, D_EXPERT, D_MODEL), F32),
            pltpu.VMEM((D_MODEL, 2 * D_EXPERT), BF16),
            pltpu.VMEM((D_EXPERT, D_MODEL), BF16),
            pltpu.SemaphoreType.DMA((2,)),
            pltpu.SemaphoreType.DMA((2,)),
        ],
    )
    return pl.pallas_call(
        _expert_kernel,
        grid_spec=grid_spec,
        out_shape=jax.ShapeDtypeStruct(xs.shape, F32),
        compiler_params=pltpu.CompilerParams(dimension_semantics=("arbitrary",), vmem_limit_bytes=EXPERT_VMEM_LIMIT),
        name="experts",
    )(block_e, first, nxt, par, fused, n_used, xs, w1, b1.reshape(N_EXPERTS, 1, -1), w2, b2.reshape(N_EXPERTS, 1, -1))


def _combine_kernel(slot_ref, next_slot_ref, ys_hbm, gate_ref, x_ref, lnw_ref, lnb_ref, o_ref, ybuf, sem):
    i = pl.program_id(0)
    c = SEQ_BLOCK

    def gather(slots, p):
        def issue(j, carry):
            for u in range(GATHER_UNROLL):
                r = j * GATHER_UNROLL + u
                for k in range(TOP_K):
                    src = _tile_of(ys_hbm, slots[0, 0, k * c + r])
                    pltpu.make_async_copy(src, _tile_of(ybuf.at[p, k], r), sem.at[p]).start(priority=k % 2)
            return carry
        lax.fori_loop(0, c // GATHER_UNROLL, issue, 0)

    @pl.when(i == 0)
    def _():
        gather(slot_ref, 0)

    @pl.when(i + 1 < pl.num_programs(0))
    def _():
        gather(next_slot_ref, (i + 1) % 2)

    p = i % 2
    for k in range(TOP_K):
        pltpu.make_async_copy(ys_hbm.at[pl.ds(0, ybuf.shape[2])], ybuf.at[p, k], sem.at[p]).wait()
    gk = gate_ref[0]
    gates = jnp.concatenate([gk, jnp.zeros((ROW_TILE[0] - TOP_K, c), F32)], axis=0).T
    ys = [_load_row_tiles(ybuf.at[p, k]) * gates[:, k:k + 1] for k in range(TOP_K)]
    y = (ys[0] + ys[1]) + (ys[2] + ys[3])
    o_ref[...] = _layer_norm(DEEPNORM_ALPHA * _load_row_tiles(x_ref) + y, lnw_ref[...], lnb_ref[...])


def _combine(ys, slots, gates, x2, ln_w, ln_b):
    t = x2.shape[0] // ROW_TILE[0]
    c = SEQ_BLOCK
    n = t // c
    per = gates.shape[2] // c
    slot_table = slots.reshape(n, 1, c * TOP_K)
    return pl.pallas_call(
        _combine_kernel,
        grid=(n,),
        in_specs=[
            pl.BlockSpec((1, 1, c * TOP_K), lambda i: (i, 0, 0), memory_space=pltpu.SMEM),
            pl.BlockSpec((1, 1, c * TOP_K), lambda i: (jnp.minimum(i + 1, n - 1), 0, 0), memory_space=pltpu.SMEM),
            pl.BlockSpec(memory_space=pl.ANY),
            pl.BlockSpec((1, TOP_K, c), lambda i: (i // per, 0, i % per)),
            _tiled_spec(c, lambda i: (i, 0)),
            pl.BlockSpec((1, D_MODEL), lambda i: (0, 0)),
            pl.BlockSpec((1, D_MODEL), lambda i: (0, 0)),
        ],
        out_specs=pl.BlockSpec((c, D_MODEL), lambda i: (i, 0)),
        out_shape=jax.ShapeDtypeStruct((t, D_MODEL), F32),
        scratch_shapes=[pltpu.VMEM((2, TOP_K) + _tiled(c), F32), pltpu.SemaphoreType.DMA((2,))],
        compiler_params=pltpu.CompilerParams(dimension_semantics=("arbitrary",), vmem_limit_bytes=VMEM_LIMIT),
        name="combine",
    )(slot_table, slot_table, ys, gates, x2, ln_w.reshape(1, -1), ln_b.reshape(1, -1))


def _slot_tables(counts, packed, n_blocks, shift):
    counts = counts.reshape(-1)
    nblk = (counts + MOE_BLOCK - 1) // MOE_BLOCK
    ids = jnp.arange(N_EXPERTS)
    lower = ids[None, :] <= ids[:, None]
    bend = jnp.sum(jnp.where(lower, nblk[None, :], 0), axis=1)
    n_used = bend[N_EXPERTS - 1]
    pend = bend * MOE_BLOCK
    pstart = pend - nblk * MOE_BLOCK
    fill = pstart + counts
    order = jnp.sum(jnp.where(lower, (nblk > 0)[None, :], False), axis=1) - 1
    later = (ids[None, :] > ids[:, None]) & (nblk > 0)[None, :]
    nxt_e = jnp.min(jnp.where(later, ids[None, :], N_EXPERTS), axis=1)
    nxt_e = jnp.where(nxt_e == N_EXPERTS, -1, nxt_e)
    b = jnp.minimum(jnp.arange(n_blocks), n_used - 1)
    hit = bend[None, :] <= b[:, None]
    block_e = jnp.sum(hit.astype(jnp.int32), axis=1)
    of_block = lambda per_expert: jnp.sum(jnp.where(ids[None, :] == block_e[:, None], per_expert[None, :], 0), axis=1)
    first = (b == of_block(bend - nblk)).astype(jnp.int32)
    expert = lax.shift_right_logical(packed, shift)
    rank = packed & ((1 << shift) - 1)
    slots = jnp.sum(jnp.where(expert[..., None] == ids, pstart, 0), axis=-1) + rank
    i32 = lambda a: a.astype(jnp.int32)
    odd = 2 * jnp.arange(n_blocks // 2) + 1
    fused = (odd < n_used) & (block_e[0::2] == block_e[1::2])
    plan = (i32(block_e), first, i32(of_block(nxt_e)), i32(of_block(order & 1)), i32(fused), i32(n_used.reshape(1)))
    return i32(slots), i32(fill), i32(pend), plan


def kernel(x, mem, w_in, lb_logits, hgrn_norm_w, w_pool, pool_scale, w_out, ln1_w, ln1_b, w_xq, w_xk, w_xv, w_xo,
           ln2_w, ln2_b, w_router, b_router, w1, b1, w2, b2, ln3_w, ln3_b):
    bsz, seq, d = x.shape
    assert bsz == 1 and d == D_MODEL and seq % SEQ_BLOCK == 0 and w_in.shape[0] == 1
    xt = x.reshape(seq, d)
    x1 = _mixer(xt, w_in[0], lb_logits, hgrn_norm_w[0], w_pool[0], pool_scale[0], w_out[0], ln1_w[0], ln1_b[0])
    score_w, out_w = _kv_proj(mem[0], w_xk[0], w_xv[0], w_xq[0], w_xo[0])
    x2, packed, gates, counts = _xattn_router(x1, score_w, out_w, ln2_w[0], ln2_b[0], w_router[0], b_router[0])
    n_blocks = -(-(seq * TOP_K + N_EXPERTS * (MOE_BLOCK - 1)) // MOE_BLOCK)
    slots, fill, pend, plan = _slot_tables(counts, packed, n_blocks, _rank_bits(seq))
    nb, _, cx = slots.shape
    slots = slots.reshape(nb, TOP_K, cx // SEQ_BLOCK, SEQ_BLOCK).transpose(0, 2, 1, 3).reshape(-1, TOP_K, SEQ_BLOCK)
    xs = _dispatch(x2, slots, fill, pend, plan[-1], n_blocks)
    ys = _experts(xs, plan, w1[0], b1[0], w2[0], b2[0])
    out = _combine(ys, slots, gates, x2, ln3_w[0], ln3_b[0])
    return out.reshape(bsz, seq, d)
```
